```python
import math
import jax, jax.numpy as jnp
from jax import lax
import numpy as np

D_MODEL = 2048
BATCH = 2
SEQ = 8192
DEPTH = 1

CTX_LEN = 256
GRID_W = 64
MIX_WIDTH = D_MODEL
POOL_WIDTH = MIX_WIDTH // 2
SSM_WIDTH = MIX_WIDTH - POOL_WIDTH
POOL_WINDOWS = (2, 4, 8, 16)
POOL_GROUPS = len(POOL_WINDOWS)
POOL_GROUP_W = POOL_WIDTH // POOL_GROUPS
SSM_GROUP_CH = 16
SSM_GROUPS = SSM_WIDTH // SSM_GROUP_CH
SSM_STATE = 64
DT_MIN = 1e-3
DT_MAX = 1e-1
EPS = 1e-6

kernel_name = "hybrid_pool_s5_prefix_dit_block"


def _rmsnorm(v, g):
    v32 = v.astype(jnp.float32)
    out = v32 * lax.rsqrt(jnp.mean(v32 * v32, axis=-1, keepdims=True) + EPS)
    return (out * g.astype(jnp.float32)).astype(v.dtype)


def _centred_window_mean(v, window, axis):
    n = v.shape[axis]
    pad = [(0, 0)] * v.ndim
    pad[axis] = (1, 0)
    cs = jnp.pad(jnp.cumsum(v.astype(jnp.float32), axis=axis), pad)
    t = jnp.arange(n)
    lo = jnp.clip(t - window // 2, 0, n)
    hi = jnp.clip(t + window - window // 2, 0, n)
    s = jnp.take(cs, hi, axis=axis) - jnp.take(cs, lo, axis=axis)
    shape = [1] * v.ndim
    shape[axis] = n
    cnt = (hi - lo).astype(jnp.float32).reshape(shape)
    return (s / cnt).astype(v.dtype)


def _pool_mix(u, pool_w, pool_scale, axes):
    outs = []
    for g, w in enumerate(POOL_WINDOWS):
        ug = u[..., g * POOL_GROUP_W:(g + 1) * POOL_GROUP_W]
        m = ug
        for ax in axes:
            m = _centred_window_mean(m, w, ax)
        outs.append(jnp.einsum('...c,cd->...d', m - ug, pool_w[g]))
    return jnp.concatenate(outs, axis=-1) * pool_scale


def _linear_combine(left, right):
    a_l, b_l = left
    a_r, b_r = right
    return a_l * a_r, a_r * b_l + b_r


def _s5_scan(u, lam_re, lam_im, log_dt, b_re, b_im, h0, reverse):
    bsz, n = u.shape[0], u.shape[1]
    lam = lax.complex(lam_re.astype(jnp.float32), lam_im.astype(jnp.float32))
    dt = jnp.exp(log_dt.astype(jnp.float32))[:, None]
    a_bar = jnp.exp(lam * dt)
    b = lax.complex(b_re.astype(jnp.float32), b_im.astype(jnp.float32))
    b_bar = ((a_bar - 1.0) / lam)[..., None] * b
    ug = u.astype(jnp.float32).reshape(bsz, n, SSM_GROUPS, SSM_GROUP_CH)
    bu = lax.complex(jnp.einsum('blgh,gnh->blgn', ug, b_bar.real),
                     jnp.einsum('blgh,gnh->blgn', ug, b_bar.imag))
    first = n - 1 if reverse else 0
    bu = bu.at[:, first].add(a_bar * h0)
    a = jnp.broadcast_to(a_bar, bu.shape)
    _, hs = lax.associative_scan(_linear_combine, (a, bu), reverse=reverse, axis=1)
    return hs


def _s5_readout(hs, c_re, c_im):
    y = (jnp.einsum('blgn,ghn->blgh', hs.real, c_re.astype(jnp.float32))
         - jnp.einsum('blgn,ghn->blgh', hs.imag, c_im.astype(jnp.float32)))
    return y.reshape(hs.shape[0], hs.shape[1], SSM_WIDTH)


def _merge_branches(pool_out, u_ssm, hs_f, hs_b, z, c_re, c_im, d_skip, glu_w, glu_b, out_w):
    y = (_s5_readout(hs_f, c_re[0], c_im[0]) + _s5_readout(hs_b, c_re[1], c_im[1])).astype(u_ssm.dtype)
    y = jax.nn.gelu(y + d_skip * u_ssm)
    y1, y2 = jnp.split(y @ glu_w + glu_b, 2, axis=-1)
    ssm_out = y1 * jax.nn.sigmoid(y2)
    branch = jnp.concatenate([pool_out, ssm_out], axis=-1) * jax.nn.silu(z)
    return branch @ out_w


def _layer(x, ctx, c, c_ctx, ada_w, ada_b, norm_g, in_w, pool_w, pool_scale,
           lam_re, lam_im, log_dt, b_re, b_im, c_re, c_im, d_skip, glu_w, glu_b, out_w, update_ctx):
    bsz, seq = x.shape[0], x.shape[1]
    rows = seq // GRID_W
    shift, scale, gate = jnp.split(jax.nn.silu(c) @ ada_w + ada_b, 3, axis=-1)
    shift_c, scale_c, gate_c = jnp.split(jax.nn.silu(c_ctx) @ ada_w + ada_b, 3, axis=-1)
    h = _rmsnorm(x, norm_g) * (1.0 + scale[:, None]) + shift[:, None]
    hc = _rmsnorm(ctx, norm_g) * (1.0 + scale_c) + shift_c

    uc_ssm = hc @ in_w[:, POOL_WIDTH:MIX_WIDTH]
    zeros_h = jnp.zeros((bsz, SSM_GROUPS, SSM_STATE), jnp.complex64)
    hs_cf = _s5_scan(uc_ssm, lam_re[0], lam_im[0], log_dt[0], b_re[0], b_im[0], zeros_h, False)
    hs_cb = _s5_scan(uc_ssm, lam_re[1], lam_im[1], log_dt[1], b_re[1], b_im[1], zeros_h, True)
    if update_ctx:
        uc_pool = hc @ in_w[:, :POOL_WIDTH]
        zc = hc @ in_w[:, MIX_WIDTH:]
        pool_c = _pool_mix(uc_pool, pool_w, pool_scale, (1,))
        mix_c = _merge_branches(pool_c, uc_ssm, hs_cf, hs_cb, zc, c_re, c_im, d_skip, glu_w, glu_b, out_w)
        ctx = ctx + gate_c * mix_c

    proj = h @ in_w
    u_pool = proj[..., :POOL_WIDTH]
    u_ssm = proj[..., POOL_WIDTH:MIX_WIDTH]
    z = proj[..., MIX_WIDTH:]
    grid = u_pool.reshape(bsz, rows, GRID_W, POOL_WIDTH)
    pool_out = _pool_mix(grid, pool_w, pool_scale, (1, 2)).reshape(bsz, seq, POOL_WIDTH)
    hs_f = _s5_scan(u_ssm, lam_re[0], lam_im[0], log_dt[0], b_re[0], b_im[0], hs_cf[:, -1], False)
    hs_b = _s5_scan(u_ssm, lam_re[1], lam_im[1], log_dt[1], b_re[1], b_im[1], hs_cb[:, 0], True)
    mix = _merge_branches(pool_out, u_ssm, hs_f, hs_b, z, c_re, c_im, d_skip, glu_w, glu_b, out_w)
    x = x + gate[:, None] * mix
    return x, ctx


def setup_inputs(seed: int = 0) -> dict:
    key = jax.random.key(seed)
    k = jax.random.split(key, 24)
    nrm = jax.random.normal
    D, G, N, H = D_MODEL, SSM_GROUPS, SSM_STATE, SSM_GROUP_CH
    n_idx = jnp.arange(N, dtype=jnp.float32)
    return {
        "x": nrm(k[0], (BATCH, SEQ, D), jnp.float32),
        "c": nrm(k[1], (BATCH, D), jnp.float32),
        "ctx": nrm(k[2], (BATCH, CTX_LEN, D), jnp.float32),
        "c_ctx": nrm(k[3], (D,), jnp.float32),
        "ada_w": nrm(k[4], (DEPTH, D, 3 * D), jnp.float32) * (0.5 * D ** -0.5),
        "ada_b": 0.01 * nrm(k[5], (DEPTH, 3 * D), jnp.float32),
        "norm_g": 1.0 + 0.01 * nrm(k[6], (DEPTH, D), jnp.float32),
        "in_w": nrm(k[7], (DEPTH, D, 2 * MIX_WIDTH), jnp.float32) * D ** -0.5,
        "pool_w": nrm(k[8], (DEPTH, POOL_GROUPS, POOL_GROUP_W, POOL_GROUP_W), jnp.float32) * POOL_GROUP_W ** -0.5,
        "pool_scale": 1.0 + 0.1 * nrm(k[9], (DEPTH, POOL_WIDTH), jnp.float32),
        "s5_lam_re": -0.5 + 0.01 * nrm(k[10], (DEPTH, 2, G, N), jnp.float32),
        "s5_lam_im": jnp.pi * n_idx + 0.01 * nrm(k[11], (DEPTH, 2, G, N), jnp.float32),
        "s5_log_dt": jax.random.uniform(k[12], (DEPTH, 2, G), jnp.float32,
                                        minval=math.log(DT_MIN), maxval=math.log(DT_MAX)),
        "s5_b_re": nrm(k[13], (DEPTH, 2, G, N, H), jnp.float32) * (2 * H) ** -0.5,
        "s5_b_im": nrm(k[14], (DEPTH, 2, G, N, H), jnp.float32) * (2 * H) ** -0.5,
        "s5_c_re": nrm(k[15], (DEPTH, 2, G, H, N), jnp.float32) * N ** -0.5,
        "s5_c_im": nrm(k[16], (DEPTH, 2, G, H, N), jnp.float32) * N ** -0.5,
        "s5_d": nrm(k[17], (DEPTH, SSM_WIDTH), jnp.float32),
        "glu_w": nrm(k[18], (DEPTH, SSM_WIDTH, 2 * SSM_WIDTH), jnp.float32) * SSM_WIDTH ** -0.5,
        "glu_b": 0.01 * nrm(k[19], (DEPTH, 2 * SSM_WIDTH), jnp.float32),
        "out_w": nrm(k[20], (DEPTH, MIX_WIDTH, D), jnp.float32) * MIX_WIDTH ** -0.5,
        "final_g": 1.0 + 0.01 * nrm(k[21], (D,), jnp.float32),
    }


def reference(x, c, ctx, c_ctx, ada_w, ada_b, norm_g, in_w, pool_w, pool_scale,
              s5_lam_re, s5_lam_im, s5_log_dt, s5_b_re, s5_b_im, s5_c_re, s5_c_im, s5_d,
              glu_w, glu_b, out_w, final_g):
    for i in range(DEPTH):
        x, ctx = _layer(x, ctx, c, c_ctx, ada_w[i], ada_b[i], norm_g[i], in_w[i], pool_w[i], pool_scale[i],
                        s5_lam_re[i], s5_lam_im[i], s5_log_dt[i], s5_b_re[i], s5_b_im[i],
                        s5_c_re[i], s5_c_im[i], s5_d[i], glu_w[i], glu_b[i], out_w[i],
                        update_ctx=(i < DEPTH - 1))
    return _rmsnorm(x, final_g)
```

```python
import functools
import math

import jax
import jax.numpy as jnp
from jax import lax
from jax.experimental import pallas as pl
from jax.experimental.pallas import tpu as pltpu

GRID_W = 64
POOL_WINDOWS = (2, 4, 8, 16)
EPS = 1e-6
CHUNK_T = 32
LANES = 128
SUBLANES = 8
VMEM_LIMIT = 56 * 1024 * 1024

F32 = jnp.float32
BF16 = jnp.bfloat16


def _cparams(sem):
    return pltpu.CompilerParams(dimension_semantics=sem, vmem_limit_bytes=VMEM_LIMIT)


def _ada_kernel(ct_ref, w_ref, b_ref, o_ref, *, nrows):
    d = w_ref.shape[0]
    tn = w_ref.shape[1]

    def body(i, accs):
        k0 = pl.multiple_of(i * SUBLANES, SUBLANES)
        w = w_ref[pl.ds(k0, SUBLANES), :]
        cv = ct_ref[pl.ds(k0, SUBLANES), :]
        s = cv / (1.0 + jnp.exp(-cv))
        return tuple(acc + s[:, r:r + 1] * w for r, acc in enumerate(accs))

    init = tuple(jnp.zeros((SUBLANES, tn), F32) for _ in range(nrows))
    accs = lax.fori_loop(0, d // SUBLANES, body, init, unroll=4)
    o_ref[...] = jnp.zeros(o_ref.shape, F32)
    for r, acc in enumerate(accs):
        o_ref[r:r + 1, :] = jnp.sum(acc, axis=0, keepdims=True) + b_ref[...]


def _ada_call(cvecs, ada_w, ada_b):
    nrows, d = cvecs.shape
    n = ada_w.shape[1]
    tn = 768 if n % 768 == 0 else n
    ct = jnp.zeros((d, SUBLANES), F32).at[:, :nrows].set(cvecs.T)
    out = pl.pallas_call(
        functools.partial(_ada_kernel, nrows=nrows),
        out_shape=jax.ShapeDtypeStruct((SUBLANES, n), F32),
        grid=(n // tn,),
        in_specs=[
            pl.BlockSpec((d, SUBLANES), lambda j: (0, 0)),
            pl.BlockSpec((d, tn), lambda j: (0, j)),
            pl.BlockSpec((1, tn), lambda j: (0, j)),
        ],
        out_specs=pl.BlockSpec((SUBLANES, tn), lambda j: (0, j)),
        compiler_params=_cparams(("arbitrary",)),
        name="ada",
    )(ct, ada_w, ada_b.reshape(1, n))
    return out[:nrows]


def _inproj_kernel(x_ref, sc_ref, sh_ref, g_ref, w_ref, *o_refs, nchunk):
    x = x_ref[0]
    ms = jnp.mean(x * x, axis=-1, keepdims=True)
    xn = x * lax.rsqrt(ms + EPS) * g_ref[...]
    h = (xn * (1.0 + sc_ref[0]) + sh_ref[0]).astype(BF16)
    col = 0
    for o_ref in o_refs:
        width = o_ref.shape[-1]
        for n0 in range(0, width, nchunk):
            acc = jnp.dot(h, w_ref[:, col + n0:col + n0 + nchunk], preferred_element_type=F32)
            o_ref[0, :, n0:n0 + nchunk] = acc.astype(o_ref.dtype)
        col += width


def _inproj_call(x, scale, shift, g, w, widths, tm, name):
    b, l, d = x.shape
    nout = w.shape[1]
    assert sum(widths) == nout and l % tm == 0
    nchunk = min(512, min(widths))
    assert all(wd % nchunk == 0 for wd in widths)
    return pl.pallas_call(
        functools.partial(_inproj_kernel, nchunk=nchunk),
        out_shape=[jax.ShapeDtypeStruct((b, l, wd), BF16) for wd in widths],
        grid=(b, l // tm),
        in_specs=[
            pl.BlockSpec((1, tm, d), lambda i, j: (i, j, 0)),
            pl.BlockSpec((1, 1, d), lambda i, j: (i, 0, 0)),
            pl.BlockSpec((1, 1, d), lambda i, j: (i, 0, 0)),
            pl.BlockSpec((1, d), lambda i, j: (0, 0)),
            pl.BlockSpec((d, nout), lambda i, j: (0, 0), pipeline_mode=pl.Buffered(1)),
        ],
        out_specs=[pl.BlockSpec((1, tm, wd), lambda i, j: (i, j, 0)) for wd in widths],
        compiler_params=_cparams(("arbitrary", "arbitrary")),
        name=name,
    )(x, scale, shift, g, w)


def _pool_kernel(u_ref, band_ref, pw_ref, ps_ref, o_ref, pad_ref, *, rows, tb, pad):
    grp = pl.program_id(1)
    l = u_ref.shape[1]
    c = u_ref.shape[2]
    nblk = l // tb
    pad_ref[0:pad, :] = jnp.zeros((pad, c), F32)
    pad_ref[pad + l:pad + l + pad, :] = jnp.zeros((pad, c), F32)

    def colsum(i, carry):
        r0 = pl.multiple_of(i * tb, tb)
        cs = jnp.dot(band_ref[0], u_ref[0, pl.ds(r0, tb), :], preferred_element_type=F32)
        pad_ref[pl.ds(pl.multiple_of(pad + r0, GRID_W), tb), :] = cs
        return carry

    lax.fori_loop(0, nblk, colsum, 0)

    for gi, w in enumerate(POOL_WINDOWS):
        half = w // 2

        @pl.when(grp == gi)
        def _(w=w, half=half):
            def blk(i, carry):
                r0 = pl.multiple_of(i * tb, tb)
                acc = pad_ref[pl.ds(pl.multiple_of(pad + r0 - half * GRID_W, GRID_W), tb), :]
                for k in range(1, w):
                    acc = acc + pad_ref[pl.ds(pl.multiple_of(pad + r0 + (k - half) * GRID_W, GRID_W), tb), :]
                t = r0 + lax.broadcasted_iota(jnp.int32, (tb, 1), 0)
                r = t // GRID_W
                cc = t % GRID_W
                cnt_r = jnp.minimum(r + half, rows) - jnp.maximum(r - half, 0)
                cnt_c = jnp.minimum(cc + half, GRID_W) - jnp.maximum(cc - half, 0)
                m = acc / (cnt_r * cnt_c).astype(F32)
                dlt = (m - u_ref[0, pl.ds(r0, tb), :].astype(F32)).astype(BF16)
                o = jnp.dot(dlt, pw_ref[0], preferred_element_type=F32) * ps_ref[...]
                o_ref[0, pl.ds(r0, tb), :] = o.astype(o_ref.dtype)
                return carry

            lax.fori_loop(0, nblk, blk, 0)


def _pool_call(u_pool, pool_w, pool_scale):
    b, l, p = u_pool.shape
    ng = len(POOL_WINDOWS)
    c = p // ng
    rows = l // GRID_W
    tb = 4 * GRID_W
    pad = (max(POOL_WINDOWS) // 2) * GRID_W
    assert l % tb == 0 and c % LANES == 0
    i = jnp.arange(tb)
    same_row = (i[:, None] // GRID_W) == (i[None, :] // GRID_W)
    dcol = (i[None, :] % GRID_W) - (i[:, None] % GRID_W)
    band = jnp.stack([(same_row & (dcol >= -(w // 2)) & (dcol < w - w // 2)) for w in POOL_WINDOWS]).astype(BF16)
    return pl.pallas_call(
        functools.partial(_pool_kernel, rows=rows, tb=tb, pad=pad),
        out_shape=jax.ShapeDtypeStruct((b, l, p), BF16),
        grid=(b, ng),
        in_specs=[
            pl.BlockSpec((1, l, c), lambda i, j: (i, 0, j)),
            pl.BlockSpec((1, tb, tb), lambda i, j: (j, 0, 0)),
            pl.BlockSpec((1, c, c), lambda i, j: (j, 0, 0)),
            pl.BlockSpec((1, c), lambda i, j: (0, j)),
        ],
        out_specs=pl.BlockSpec((1, l, c), lambda i, j: (i, 0, j)),
        scratch_shapes=[pltpu.VMEM((pad + l + pad, c), F32)],
        compiler_params=_cparams(("arbitrary", "arbitrary")),
        name="pool",
    )(u_pool, band, pool_w.astype(BF16), pool_scale.reshape(1, p))


def _s5_kernel(u_ref, win_ref, mt_ref, wout_ref, ap_ref, y_ref, buf_ref, hin_ref, *, nb, rb, pad, nsteps, nstate):
    two_n = 2 * nstate

    @pl.when(pl.program_id(0) == 0)
    def _():
        buf_ref[...] = jnp.zeros(buf_ref.shape, F32)

    u = u_ref[0]
    s = jnp.dot(u, win_ref[0], preferred_element_type=F32)
    for b in range(nb):
        buf_ref[b, pad:pad + rb, :] = s[b * rb:(b + 1) * rb]
    is_fwd = lax.broadcasted_iota(jnp.int32, (rb, two_n), 1) < nstate

    def shifted(b, st, lo):
        return jnp.where(is_fwd, buf_ref[b, pad - st:pad - st + rb, lo:lo + two_n],
                         buf_ref[b, pad + st:pad + st + rb, lo:lo + two_n])

    for k in range(nsteps):
        st = 1 << k
        ar = ap_ref[0, 2 * k:2 * k + 1, :]
        ai = ap_ref[0, 2 * k + 1:2 * k + 2, :]
        for b in range(nb):
            xr = buf_ref[b, pad:pad + rb, 0:two_n]
            xi = buf_ref[b, pad:pad + rb, two_n:2 * two_n]
            pr = shifted(b, st, 0)
            pi = shifted(b, st, two_n)
            buf_ref[b, pad:pad + rb, 0:two_n] = xr + (ar * pr - ai * pi)
            buf_ref[b, pad:pad + rb, two_n:2 * two_n] = xi + (ar * pi + ai * pr)
    for b in range(nb):
        hin_ref[b * rb:(b + 1) * rb, 0:two_n] = shifted(b, 1, 0).astype(BF16)
        hin_ref[b * rb:(b + 1) * rb, two_n:2 * two_n] = shifted(b, 1, two_n).astype(BF16)
    y = jnp.dot(u, mt_ref[0], preferred_element_type=F32)
    y = y + jnp.dot(hin_ref[...], wout_ref[0], preferred_element_type=F32)
    y_ref[0] = y.astype(y_ref.dtype)


def _toeplitz(cseq, t):
    g, _, h, _ = cseq.shape
    crev = jnp.pad(cseq[:, ::-1], ((0, 0), (0, 1), (0, 0), (0, 0)))
    skew = jnp.tile(crev, (1, t, 1, 1))[:, :t * (2 * t - 1)].reshape(g, t, 2 * t - 1, h, h)[:, :, t - 1:]
    return skew.transpose(0, 1, 3, 2, 4).reshape(g, t * h, t * h)


def _s5_tables(lam_re, lam_im, log_dt, b_re, b_im, c_re, c_im, t, nsteps):
    hp = lax.Precision.HIGHEST
    _, g, n = lam_re.shape
    h = b_re.shape[-1]
    lam = lax.complex(lam_re.astype(F32), lam_im.astype(F32))
    ldt = lam * jnp.exp(log_dt.astype(F32))[..., None]
    a_bar = jnp.exp(ldt)
    b_bar = ((a_bar - 1.0) / lam)[..., None] * lax.complex(b_re.astype(F32), b_im.astype(F32))
    cc = lax.complex(c_re.astype(F32), c_im.astype(F32))
    ell = jnp.arange(t + 1, dtype=F32)
    pw = jnp.exp(ldt[:, :, None, :] * ell[None, None, :, None])
    bp = b_bar[:, :, None, :, :] * pw[:, :, :t, :, None]
    kk = (jnp.einsum('dglnp,dghn->dglph', bp.real, cc.real, precision=hp)
          - jnp.einsum('dglnp,dghn->dglph', bp.imag, cc.imag, precision=hp))
    cseq = jnp.concatenate([kk[0, :, :0:-1], (kk[0, :, 0] + kk[1, :, 0])[:, None], kk[1, :, 1:]], axis=1)
    mt = _toeplitz(cseq, t)
    e_f = bp[0, :, ::-1]
    e_b = bp[1]
    win = jnp.concatenate([e_f.real, e_b.real, e_f.imag, e_b.imag], axis=2)
    win = win.transpose(0, 1, 3, 2).reshape(g, t * h, 4 * n)
    cn = cc.transpose(0, 1, 3, 2)
    f_f = cn[0][:, None] * pw[0, :, 1:, :, None]
    f_b = cn[1][:, None] * pw[1, :, :0:-1, :, None]
    wout = jnp.concatenate([f_f.real, f_b.real, -f_f.imag, -f_b.imag], axis=2)
    wout = wout.transpose(0, 2, 1, 3).reshape(g, 4 * n, t * h)
    strides = (t * (2.0 ** jnp.arange(nsteps, dtype=F32)))
    ap = jnp.exp(ldt[:, :, None, :] * strides[None, None, :, None])
    ap = jnp.stack([jnp.concatenate([ap[0].real, ap[1].real], -1),
                    jnp.concatenate([ap[0].imag, ap[1].imag], -1)], axis=2)
    ap = ap.reshape(g, 2 * nsteps, 2 * n)
    rows = -(-2 * nsteps // SUBLANES) * SUBLANES
    ap = jnp.pad(ap, ((0, 0), (0, rows - 2 * nsteps), (0, 0)))
    return win.astype(BF16), mt.astype(BF16), wout.astype(BF16), ap


def _s5_call(u_ssm, uc_ssm, tables, t, nsteps, n_state, h):
    win, mt, wout, ap = tables
    b, l, s = u_ssm.shape
    cl = uc_ssm.shape[1]
    g = s // h
    ncc, ncl = cl // t, l // t
    rb = ncc + ncl + ncc
    pad = 1 << (nsteps - 1)
    th = t * h
    seq = jnp.concatenate([uc_ssm, u_ssm, uc_ssm], axis=1)
    ug = seq.reshape(b, rb, t, g, h).transpose(3, 0, 1, 2, 4).reshape(g, b * rb, th)
    y = pl.pallas_call(
        functools.partial(_s5_kernel, nb=b, rb=rb, pad=pad, nsteps=nsteps, nstate=n_state),
        out_shape=jax.ShapeDtypeStruct((g, b * rb, th), BF16),
        grid=(g,),
        in_specs=[
            pl.BlockSpec((1, b * rb, th), lambda i: (i, 0, 0)),
            pl.BlockSpec((1, th, 4 * n_state), lambda i: (i, 0, 0)),
            pl.BlockSpec((1, th, th), lambda i: (i, 0, 0)),
            pl.BlockSpec((1, 4 * n_state, th), lambda i: (i, 0, 0)),
            pl.BlockSpec((1, ap.shape[1], 2 * n_state), lambda i: (i, 0, 0)),
        ],
        out_specs=pl.BlockSpec((1, b * rb, th), lambda i: (i, 0, 0)),
        scratch_shapes=[pltpu.VMEM((b, pad + rb + pad, 4 * n_state), F32),
                        pltpu.VMEM((b * rb, 4 * n_state), BF16)],
        compiler_params=_cparams(("arbitrary",)),
        name="s5",
    )(ug, win, mt, wout, ap)
    y = y.reshape(g, b, rb, t, h)[:, :, ncc:ncc + ncl]
    return y.transpose(1, 2, 3, 0, 4).reshape(b, l, s)


def _gelu_tanh(v):
    return 0.5 * v * (1.0 + jnp.tanh(math.sqrt(2.0 / math.pi) * (v + 0.044715 * (v * v * v))))


def _sigmoid(v):
    return 1.0 / (1.0 + jnp.exp(-v))


def _merge_kernel(y_ref, u_ref, p_ref, z_ref, x_ref, gate_ref, d_ref, gw_ref, gb_ref, ow_ref, fg_ref, o_ref):
    sw = y_ref.shape[-1]
    pw = p_ref.shape[-1]
    y = _gelu_tanh(y_ref[0].astype(F32) + d_ref[...] * u_ref[0].astype(F32)).astype(BF16)
    yy = jnp.dot(y, gw_ref[...], preferred_element_type=F32) + gb_ref[...]
    ssm_out = yy[:, :sw] * _sigmoid(yy[:, sw:])
    z = z_ref[0].astype(F32)
    zs = z * _sigmoid(z)
    br_pool = (p_ref[0].astype(F32) * zs[:, :pw]).astype(BF16)
    br_ssm = (ssm_out * zs[:, pw:]).astype(BF16)
    mix = jnp.dot(br_pool, ow_ref[0:pw, :], preferred_element_type=F32)
    mix = mix + jnp.dot(br_ssm, ow_ref[pw:pw + sw, :], preferred_element_type=F32)
    xo = x_ref[0] + gate_ref[0] * mix
    ms = jnp.mean(xo * xo, axis=-1, keepdims=True)
    o_ref[0] = (xo * lax.rsqrt(ms + EPS) * fg_ref[...]).astype(o_ref.dtype)


def _merge_call(y_ssm, u_ssm, pool_out, z, x, gate, d_skip, glu_w, glu_b, out_w, final_g, tm):
    b, l, d = x.shape
    sw = y_ssm.shape[-1]
    pw = pool_out.shape[-1]
    mixw = z.shape[-1]
    tok = lambda wd: pl.BlockSpec((1, tm, wd), lambda i, j: (i, j, 0))
    const = lambda shp: pl.BlockSpec(shp, lambda i, j: (0,) * len(shp), pipeline_mode=pl.Buffered(1))
    return pl.pallas_call(
        _merge_kernel,
        out_shape=jax.ShapeDtypeStruct((b, l, d), x.dtype),
        grid=(b, l // tm),
        in_specs=[
            tok(sw), tok(sw), tok(pw), tok(mixw), tok(d),
            pl.BlockSpec((1, 1, d), lambda i, j: (i, 0, 0)),
            const((1, sw)), const((sw, 2 * sw)), const((1, 2 * sw)), const((mixw, d)), const((1, d)),
        ],
        out_specs=tok(d),
        compiler_params=_cparams(("arbitrary", "arbitrary")),
        name="merge",
    )(y_ssm, u_ssm, pool_out, z, x, gate, d_skip.reshape(1, sw), glu_w.astype(BF16), glu_b.reshape(1, 2 * sw),
      out_w.astype(BF16), final_g.reshape(1, d))


def kernel(x, c, ctx, c_ctx, ada_w, ada_b, norm_g, in_w, pool_w, pool_scale, s5_lam_re, s5_lam_im, s5_log_dt,
           s5_b_re, s5_b_im, s5_c_re, s5_c_im, s5_d, glu_w, glu_b, out_w, final_g):
    assert ada_w.shape[0] == 1, "single-layer block"
    bsz, seq, d = x.shape
    cl = ctx.shape[1]
    mixw = in_w.shape[2] // 2
    poolw = pool_scale.shape[-1]
    ssmw = s5_d.shape[-1]
    n_grp, n_state = s5_lam_re.shape[2], s5_lam_re.shape[3]
    h = ssmw // n_grp
    t = CHUNK_T
    assert poolw + ssmw == mixw and seq % t == 0 and cl % t == 0 and 2 * n_state == LANES

    mod = _ada_call(jnp.concatenate([c, c_ctx[None]], axis=0), ada_w[0], ada_b[0])
    shift, scale, gate = mod[:, :d], mod[:, d:2 * d], mod[:, 2 * d:]
    g1 = norm_g[0].reshape(1, d)
    in_w16 = in_w[0].astype(BF16)

    tm = min(512, seq)
    u_pool, u_ssm, z = _inproj_call(x, scale[:bsz, None], shift[:bsz, None], g1, in_w16,
                                    (poolw, ssmw, mixw), tm, "inproj")
    sc_c = jnp.broadcast_to(scale[bsz][None, None], (bsz, 1, d))
    sh_c = jnp.broadcast_to(shift[bsz][None, None], (bsz, 1, d))
    (uc_ssm,) = _inproj_call(ctx, sc_c, sh_c, g1, in_w16[:, poolw:mixw], (ssmw,), min(256, cl), "inproj_ctx")

    pool_out = _pool_call(u_pool, pool_w[0], pool_scale[0])

    rb = (seq + 2 * cl) // t
    nsteps = max(1, (rb - 1).bit_length())
    tables = _s5_tables(s5_lam_re[0], s5_lam_im[0], s5_log_dt[0], s5_b_re[0], s5_b_im[0], s5_c_re[0], s5_c_im[0],
                        t, nsteps)
    y_ssm = _s5_call(u_ssm, uc_ssm, tables, t, nsteps, n_state, h)

    return _merge_call(y_ssm, u_ssm, pool_out, z, x, gate[:bsz, None], s5_d[0], glu_w[0], glu_b[0], out_w[0],
                       final_g, tm)
```

```python
import functools
import math

import jax
import jax.numpy as jnp
from jax import lax
from jax.experimental import pallas as pl
from jax.experimental.pallas import tpu as pltpu

GRID_W = 64
POOL_WINDOWS = (2, 4, 8, 16)
EPS = 1e-6
CHUNK_T = 32
LANES = 128
SUBLANES = 8
VMEM_LIMIT = 56 * 1024 * 1024

F32 = jnp.float32
BF16 = jnp.bfloat16
NT_DIMS = (((1,), (1,)), ((), ()))
TN_DIMS = (((0,), (0,)), ((), ()))


def _cparams(sem):
    return pltpu.CompilerParams(dimension_semantics=sem, vmem_limit_bytes=VMEM_LIMIT)


def _const_spec(shape):
    return pl.BlockSpec(shape, lambda *_: (0,) * len(shape), pipeline_mode=pl.Buffered(1))


def _sigmoid(v):
    return 1.0 / (1.0 + jnp.exp(-v))


def _gelu_tanh(v):
    return 0.5 * v * (1.0 + jnp.tanh(math.sqrt(2.0 / math.pi) * (v + 0.044715 * (v * v * v))))


def _modulated_norm(x, g, scale, shift):
    ms = jnp.mean(x * x, axis=-1, keepdims=True)
    return (x * lax.rsqrt(ms + EPS) * g) * (1.0 + scale) + shift


def _ada_kernel(ct_ref, w_ref, b_ref, o_ref, *, nrows):
    d = w_ref.shape[0]
    tn = w_ref.shape[1]

    def body(i, accs):
        k0 = pl.multiple_of(i * SUBLANES, SUBLANES)
        w = w_ref[pl.ds(k0, SUBLANES), :]
        cv = ct_ref[pl.ds(k0, SUBLANES), :]
        s = cv * _sigmoid(cv)
        return tuple(acc + s[:, r:r + 1] * w for r, acc in enumerate(accs))

    init = tuple(jnp.zeros((SUBLANES, tn), F32) for _ in range(nrows))
    accs = lax.fori_loop(0, d // SUBLANES, body, init, unroll=4)
    o_ref[...] = jnp.zeros(o_ref.shape, F32)
    for r, acc in enumerate(accs):
        o_ref[r:r + 1, :] = jnp.sum(acc, axis=0, keepdims=True) + b_ref[...]


def _ada_call(cvecs, ada_w, ada_b):
    nrows, d = cvecs.shape
    n = ada_w.shape[1]
    tn = 768 if n % 768 == 0 else n
    ct = jnp.zeros((d, SUBLANES), F32).at[:, :nrows].set(cvecs.T)
    out = pl.pallas_call(
        functools.partial(_ada_kernel, nrows=nrows),
        out_shape=jax.ShapeDtypeStruct((SUBLANES, n), F32),
        grid=(n // tn,),
        in_specs=[
            pl.BlockSpec((d, SUBLANES), lambda j: (0, 0)),
            pl.BlockSpec((d, tn), lambda j: (0, j)),
            pl.BlockSpec((1, tn), lambda j: (0, j)),
        ],
        out_specs=pl.BlockSpec((SUBLANES, tn), lambda j: (0, j)),
        compiler_params=_cparams(("arbitrary",)),
        name="ada",
    )(ct, ada_w, ada_b.reshape(1, n))
    return out[:nrows]


def _inproj_kernel(x_ref, sc_ref, sh_ref, g_ref, w_ref, *o_refs, nchunk):
    h = _modulated_norm(x_ref[0], g_ref[...], sc_ref[0], sh_ref[0]).astype(BF16)
    col = 0
    for o_ref in o_refs:
        width = o_ref.shape[-1]
        for n0 in range(0, width, nchunk):
            acc = jnp.dot(h, w_ref[:, col + n0:col + n0 + nchunk], preferred_element_type=F32)
            o_ref[0, :, n0:n0 + nchunk] = acc.astype(o_ref.dtype)
        col += width


def _inproj_call(x, scale, shift, g, w, widths, tm):
    b, l, d = x.shape
    nout = w.shape[1]
    assert sum(widths) == nout and l % tm == 0
    nchunk = min(512, min(widths))
    assert all(wd % nchunk == 0 for wd in widths)
    return pl.pallas_call(
        functools.partial(_inproj_kernel, nchunk=nchunk),
        out_shape=[jax.ShapeDtypeStruct((b, l, wd), BF16) for wd in widths],
        grid=(b, l // tm),
        in_specs=[
            pl.BlockSpec((1, tm, d), lambda i, j: (i, j, 0)),
            pl.BlockSpec((1, 1, d), lambda i, j: (i, 0, 0)),
            pl.BlockSpec((1, 1, d), lambda i, j: (i, 0, 0)),
            _const_spec((1, d)),
            _const_spec((d, nout)),
        ],
        out_specs=[pl.BlockSpec((1, tm, wd), lambda i, j: (i, j, 0)) for wd in widths],
        compiler_params=_cparams(("arbitrary", "arbitrary")),
        name="inproj",
    )(x, scale, shift, g, w)


def _inproj_t_kernel(x_ref, c_ref, sc_ref, sh_ref, g_ref, wt_ref, ut_ref, zt_ref, h_ref, *, nb, nl, ncx, sw, rchunk):
    g = g_ref[...]
    per = nl // nb
    for b in range(nb):
        xb = x_ref[b * per:(b + 1) * per, :]
        h_ref[b * per:(b + 1) * per, :] = _modulated_norm(xb, g, sc_ref[b:b + 1, :], sh_ref[b:b + 1, :]).astype(BF16)
    hc = _modulated_norm(c_ref[...], g, sc_ref[nb:nb + 1, :], sh_ref[nb:nb + 1, :]).astype(BF16)
    h_ref[nl:nl + LANES, :] = jnp.zeros((LANES, h_ref.shape[1]), BF16)
    h_ref[nl:nl + ncx, :] = hc
    for r0 in range(0, sw, rchunk):
        acc = lax.dot_general(wt_ref[r0:r0 + rchunk, :], h_ref[...], NT_DIMS, preferred_element_type=F32)
        ut_ref[0, r0:r0 + rchunk, :] = acc.astype(BF16)
    for r0 in range(0, sw, rchunk):
        acc = lax.dot_general(wt_ref[sw + r0:sw + r0 + rchunk, :], h_ref[0:nl, :], NT_DIMS, preferred_element_type=F32)
        zt_ref[0, r0:r0 + rchunk, :] = acc.astype(BF16)


def _inproj_t_call(x, ctx, scale, shift, g, wt, t):
    b, l, d = x.shape
    cl = ctx.shape[1]
    sw = wt.shape[0] // 2
    nl = b * (l // t)
    ncx = b * (cl // t)
    assert nl % LANES == 0 and ncx <= LANES and ncx % SUBLANES == 0
    x2 = x.reshape(nl, t * d)
    c2 = ctx.reshape(ncx, t * d)
    return pl.pallas_call(
        functools.partial(_inproj_t_kernel, nb=b, nl=nl, ncx=ncx, sw=sw, rchunk=min(512, sw)),
        out_shape=[jax.ShapeDtypeStruct((t, sw, nl + LANES), BF16), jax.ShapeDtypeStruct((t, sw, nl), BF16)],
        grid=(t,),
        in_specs=[
            pl.BlockSpec((nl, d), lambda i: (0, i)),
            pl.BlockSpec((ncx, d), lambda i: (0, i)),
            _const_spec((b + 1, d)),
            _const_spec((b + 1, d)),
            _const_spec((1, d)),
            _const_spec((2 * sw, d)),
        ],
        out_specs=[pl.BlockSpec((1, sw, nl + LANES), lambda i: (i, 0, 0)),
                   pl.BlockSpec((1, sw, nl), lambda i: (i, 0, 0))],
        scratch_shapes=[pltpu.VMEM((nl + LANES, d), BF16)],
        compiler_params=_cparams(("arbitrary",)),
        name="inproj_t",
    )(x2, c2, scale, shift, g, wt)


def _pool_kernel(u_ref, band_ref, pw_ref, ps_ref, o_ref, pad_ref, *, rows, tb, pad):
    grp = pl.program_id(1)
    l = u_ref.shape[1]
    c = u_ref.shape[2]
    nblk = l // tb
    pad_ref[0:pad, :] = jnp.zeros((pad, c), F32)
    pad_ref[pad + l:pad + l + pad, :] = jnp.zeros((pad, c), F32)

    def colsum(i, carry):
        r0 = pl.multiple_of(i * tb, tb)
        cs = jnp.dot(band_ref[0], u_ref[0, pl.ds(r0, tb), :], preferred_element_type=F32)
        pad_ref[pl.ds(pl.multiple_of(pad + r0, GRID_W), tb), :] = cs
        return carry

    lax.fori_loop(0, nblk, colsum, 0)

    for gi, w in enumerate(POOL_WINDOWS):
        half = w // 2

        @pl.when(grp == gi)
        def _(w=w, half=half):
            def blk(i, carry):
                r0 = pl.multiple_of(i * tb, tb)
                acc = pad_ref[pl.ds(pl.multiple_of(pad + r0 - half * GRID_W, GRID_W), tb), :]
                for k in range(1, w):
                    acc = acc + pad_ref[pl.ds(pl.multiple_of(pad + r0 + (k - half) * GRID_W, GRID_W), tb), :]
                t = r0 + lax.broadcasted_iota(jnp.int32, (tb, 1), 0)
                r = t // GRID_W
                cc = t % GRID_W
                cnt_r = jnp.minimum(r + half, rows) - jnp.maximum(r - half, 0)
                cnt_c = jnp.minimum(cc + half, GRID_W) - jnp.maximum(cc - half, 0)
                m = acc / (cnt_r * cnt_c).astype(F32)
                dlt = (m - u_ref[0, pl.ds(r0, tb), :].astype(F32)).astype(BF16)
                o = jnp.dot(dlt, pw_ref[0], preferred_element_type=F32) * ps_ref[...]
                o_ref[0, pl.ds(r0, tb), :] = o.astype(o_ref.dtype)
                return carry

            lax.fori_loop(0, nblk, blk, 0)


def _pool_call(u_pool, pool_w, pool_scale):
    b, l, p = u_pool.shape
    ng = len(POOL_WINDOWS)
    c = p // ng
    rows = l // GRID_W
    tb = 4 * GRID_W
    pad = (max(POOL_WINDOWS) // 2) * GRID_W
    assert l % tb == 0 and c % LANES == 0
    i = jnp.arange(tb)
    same_row = (i[:, None] // GRID_W) == (i[None, :] // GRID_W)
    dcol = (i[None, :] % GRID_W) - (i[:, None] % GRID_W)
    band = jnp.stack([(same_row & (dcol >= -(w // 2)) & (dcol < w - w // 2)) for w in POOL_WINDOWS]).astype(BF16)
    return pl.pallas_call(
        functools.partial(_pool_kernel, rows=rows, tb=tb, pad=pad),
        out_shape=jax.ShapeDtypeStruct((b, l, p), BF16),
        grid=(b, ng),
        in_specs=[
            pl.BlockSpec((1, l, c), lambda i, j: (i, 0, j)),
            pl.BlockSpec((1, tb, tb), lambda i, j: (j, 0, 0)),
            pl.BlockSpec((1, c, c), lambda i, j: (j, 0, 0)),
            pl.BlockSpec((1, c), lambda i, j: (0, j)),
        ],
        out_specs=pl.BlockSpec((1, l, c), lambda i, j: (i, 0, j)),
        scratch_shapes=[pltpu.VMEM((pad + l + pad, c), F32)],
        compiler_params=_cparams(("arbitrary", "arbitrary")),
        name="pool",
    )(u_pool, band, pool_w.astype(BF16), pool_scale.reshape(1, p))


def _split3(v):
    hi = v.astype(BF16)
    r1 = v - hi.astype(F32)
    mid = r1.astype(BF16)
    lo = (r1 - mid.astype(F32)).astype(BF16)
    return hi, mid, lo


def _dot_hi(a, b):
    a_hi = a.astype(BF16)
    a_lo = (a - a_hi.astype(F32)).astype(BF16)
    b_hi = b.astype(BF16)
    b_lo = (b - b_hi.astype(F32)).astype(BF16)
    return (jnp.dot(a_hi, b_hi, preferred_element_type=F32) + jnp.dot(a_hi, b_lo, preferred_element_type=F32)
            + jnp.dot(a_lo, b_hi, preferred_element_type=F32))


def _s5_kernel(u_ref, pw_ref, bb_ref, cc_ref, ca_ref, cb_ref, rowa_ref, ap_ref, rep_ref, y_ref,
               win_ref, mt_ref, wout_ref, buf_ref, hin_ref, *, t, hc, nb, ncl, ncc, pad, nsteps, nstate):
    n = nstate
    th = t * hc
    nl = nb * ncl
    rb = ncc + ncl + ncc
    ntile = th // LANES

    @pl.when(pl.program_id(0) == 0)
    def _():
        buf_ref[...] = jnp.zeros(buf_ref.shape, F32)

    rep = rep_ref[...]
    prep = sum(jnp.dot(part, rep, preferred_element_type=F32) for part in _split3(pw_ref[0]))
    bt = jnp.concatenate([bb_ref[0]] * ntile, axis=1)
    e = []
    for d in range(2):
        pr, pi = prep[2 * d * n:(2 * d + 1) * n], prep[(2 * d + 1) * n:(2 * d + 2) * n]
        br, bi = bt[2 * d * n:(2 * d + 1) * n], bt[(2 * d + 1) * n:(2 * d + 2) * n]
        e.append((pr * br - pi * bi, pr * bi + pi * br))
    win_ref[0:n, :] = e[0][0].astype(BF16)
    win_ref[n:2 * n, :] = e[1][0].astype(BF16)
    win_ref[2 * n:3 * n, :] = e[0][1].astype(BF16)
    win_ref[3 * n:4 * n, :] = e[1][1].astype(BF16)
    cc = cc_ref[0]
    q_f = _dot_hi(cc[0:hc], jnp.concatenate(e[0], axis=0))
    q_b = _dot_hi(cc[hc:2 * hc], jnp.concatenate(e[1], axis=0))
    zeros = jnp.zeros((hc, th), F32)
    line_b = jnp.concatenate([zeros, q_b], axis=1)
    line_f = jnp.concatenate([q_f, zeros], axis=1)
    per_tile = LANES // hc
    rolled_b = [line_b if r == 0 else pltpu.roll(line_b, hc * r, axis=1) for r in range(per_tile)]
    rolled_f = [line_f if r == per_tile - 1 else pltpu.roll(line_f, 2 * th - hc * (per_tile - 1 - r), axis=1)
                for r in range(per_tile)]
    for ti in range(t):
        a, r = divmod(ti, per_tile)
        row_b = rolled_b[r][:, th - LANES * a:2 * th - LANES * a]
        off_f = LANES * (ntile - 1 - a)
        row_f = rolled_f[r][:, off_f:off_f + th]
        mt_ref[ti * hc:(ti + 1) * hc, :] = (row_b + row_f).astype(BF16)
    ca = ca_ref[0]
    cb = cb_ref[0]
    for ti in range(t):
        ra = rowa_ref[0, ti:ti + 1, :]
        rbw = jnp.concatenate([ra[:, 2 * n:4 * n], ra[:, 0:2 * n]], axis=1)
        wout_ref[ti * hc:(ti + 1) * hc, :] = (ca * ra + cb * rbw).astype(BF16)

    u = jnp.concatenate([u_ref[ti] for ti in range(t)], axis=0)
    s = jnp.dot(win_ref[...], u, preferred_element_type=F32).T
    for b in range(nb):
        ctx_rows = s[nl + b * ncc:nl + (b + 1) * ncc]
        buf_ref[b, pad:pad + ncc, :] = ctx_rows
        buf_ref[b, pad + ncc:pad + ncc + ncl, :] = s[b * ncl:(b + 1) * ncl]
        buf_ref[b, pad + ncc + ncl:pad + rb, :] = ctx_rows
    two_n = 2 * n
    is_fwd = lax.broadcasted_iota(jnp.int32, (rb, two_n), 1) < n

    def shifted(b, st, lo):
        return jnp.where(is_fwd, buf_ref[b, pad - st:pad - st + rb, lo:lo + two_n],
                         buf_ref[b, pad + st:pad + st + rb, lo:lo + two_n])

    for k in range(nsteps):
        st = 1 << k
        ar = ap_ref[0, 2 * k:2 * k + 1, :]
        ai = ap_ref[0, 2 * k + 1:2 * k + 2, :]
        for b in range(nb):
            xr = buf_ref[b, pad:pad + rb, 0:two_n]
            xi = buf_ref[b, pad:pad + rb, two_n:2 * two_n]
            pr = shifted(b, st, 0)
            pi = shifted(b, st, two_n)
            buf_ref[b, pad:pad + rb, 0:two_n] = xr + (ar * pr - ai * pi)
            buf_ref[b, pad:pad + rb, two_n:2 * two_n] = xi + (ar * pi + ai * pr)
    for b in range(nb):
        hin_ref[b * ncl:(b + 1) * ncl, 0:two_n] = shifted(b, 1, 0)[ncc:ncc + ncl]
        hin_ref[b * ncl:(b + 1) * ncl, two_n:2 * two_n] = shifted(b, 1, two_n)[ncc:ncc + ncl]

    hin_t = hin_ref[...].T.astype(BF16)
    y = jnp.dot(mt_ref[...], u[:, 0:nl], preferred_element_type=F32)
    y = y + jnp.dot(wout_ref[...], hin_t, preferred_element_type=F32)
    for ti in range(t):
        y_ref[ti] = y[ti * hc:(ti + 1) * hc].astype(y_ref.dtype)


def _s5_params(lam_re, lam_im, log_dt, b_re, b_im, c_re, c_im, t, nsteps):
    _, g, n = lam_re.shape
    h = b_re.shape[-1]
    lam_re, lam_im = lam_re.astype(F32), lam_im.astype(F32)
    dt = jnp.exp(log_dt.astype(F32))[..., None]
    lr, li = lam_re * dt, lam_im * dt

    def power(k):
        mag = jnp.exp(lr[..., None] * k)
        return mag * jnp.cos(li[..., None] * k), mag * jnp.sin(li[..., None] * k)

    a_re, a_im = jnp.exp(lr) * jnp.cos(li), jnp.exp(lr) * jnp.sin(li)
    den = lam_re * lam_re + lam_im * lam_im
    f_re = ((a_re - 1.0) * lam_re + a_im * lam_im) / den
    f_im = (a_im * lam_re - (a_re - 1.0) * lam_im) / den
    bb_re = f_re[..., None] * b_re - f_im[..., None] * b_im
    bb_im = f_re[..., None] * b_im + f_im[..., None] * b_re

    ell = jnp.arange(t, dtype=F32)
    pf_re, pf_im = power((t - 1.0) - ell)
    pb_re, pb_im = power(ell)
    pw = jnp.concatenate([pf_re[0], pf_im[0], pb_re[1], pb_im[1]], axis=1)
    pw = jnp.pad(pw, ((0, 0), (0, 0), (0, LANES - t)))
    bb = jnp.concatenate([bb_re[0], bb_im[0], bb_re[1], bb_im[1]], axis=1)
    bb = jnp.tile(bb, (1, 1, LANES // h))
    cc = jnp.concatenate([jnp.concatenate([c_re[0], -c_im[0]], axis=-1),
                          jnp.concatenate([c_re[1], -c_im[1]], axis=-1)], axis=1)
    ca = jnp.concatenate([c_re[0], c_re[1], -c_re[0], -c_re[1]], axis=-1)
    cb = jnp.concatenate([-c_im[0], -c_im[1], -c_im[0], -c_im[1]], axis=-1)
    tau = jnp.arange(t, dtype=F32)
    rf_re, rf_im = power(tau + 1.0)
    rb_re, rb_im = power(t - tau)
    rowa = jnp.concatenate([rf_re[0], rb_re[1], rf_im[0], rb_im[1]], axis=1)
    rowa = rowa.transpose(0, 2, 1)
    strides = t * (2.0 ** jnp.arange(nsteps, dtype=F32))
    ap_re, ap_im = power(strides)
    ap = jnp.stack([jnp.concatenate([ap_re[0], ap_re[1]], axis=1),
                    jnp.concatenate([ap_im[0], ap_im[1]], axis=1)], axis=-1)
    ap = ap.reshape(g, 2 * n, 2 * nsteps).transpose(0, 2, 1)
    rows = -(-2 * nsteps // SUBLANES) * SUBLANES
    ap = jnp.pad(ap, ((0, 0), (0, rows - 2 * nsteps), (0, 0)))
    lane = jnp.arange(t * h)
    rep = (jnp.arange(LANES)[:, None] == (lane // h)[None, :]).astype(BF16)
    return pw, bb, cc.astype(F32), ca.astype(F32), cb.astype(F32), rowa, ap, rep


def _s5_call(ut, params, t, nb, ncl, ncc, nsteps, n_state, hc):
    pw, bb, cc, ca, cb, rowa, ap, rep = params
    _, sw, width = ut.shape
    nl = nb * ncl
    g = sw // hc
    th = t * hc
    rb = ncc + ncl + ncc
    pad = max(SUBLANES, 1 << (nsteps - 1))
    grp = lambda shp: pl.BlockSpec((1,) + shp, lambda i: (i, 0, 0))
    return pl.pallas_call(
        functools.partial(_s5_kernel, t=t, hc=hc, nb=nb, ncl=ncl, ncc=ncc, pad=pad, nsteps=nsteps, nstate=n_state),
        out_shape=jax.ShapeDtypeStruct((t, sw, nl), BF16),
        grid=(g,),
        in_specs=[
            pl.BlockSpec((t, hc, width), lambda i: (0, i, 0)),
            grp((4 * n_state, LANES)), grp((4 * n_state, LANES)), grp((2 * hc, 2 * n_state)),
            grp((hc, 4 * n_state)), grp((hc, 4 * n_state)), grp((t, 4 * n_state)),
            grp((ap.shape[1], 2 * n_state)),
            _const_spec((LANES, th)),
        ],
        out_specs=pl.BlockSpec((t, hc, nl), lambda i: (0, i, 0)),
        scratch_shapes=[pltpu.VMEM((4 * n_state, th), BF16), pltpu.VMEM((th, th), BF16),
                        pltpu.VMEM((th, 4 * n_state), BF16),
                        pltpu.VMEM((nb, pad + rb + pad, 4 * n_state), F32),
                        pltpu.VMEM((nl, 4 * n_state), F32)],
        compiler_params=_cparams(("arbitrary",)),
        name="s5",
    )(ut, pw, bb, cc, ca, cb, rowa, ap, rep)


def _merge_kernel(yt_ref, ut_ref, zt_ref, p_ref, zp_ref, x_ref, gate_ref, d_ref, gwt_ref, gb_ref, ow_ref, fg_ref,
                  o_ref, *, nb):
    sw = yt_ref.shape[1]
    nl = yt_ref.shape[2]
    pw = p_ref.shape[-1]
    tile = lambda col: jnp.concatenate([col] * (nl // LANES), axis=1)
    v = yt_ref[0].astype(F32) + tile(d_ref[...]) * ut_ref[0, :, 0:nl].astype(F32)
    yy = jnp.dot(gwt_ref[...], _gelu_tanh(v).astype(BF16), preferred_element_type=F32) + tile(gb_ref[...])
    zs = zt_ref[0].astype(F32)
    br_ssm_t = (yy[0:sw] * _sigmoid(yy[sw:2 * sw]) * (zs * _sigmoid(zs))).astype(BF16)
    zp = zp_ref[...].astype(F32)
    br_pool = (p_ref[...].astype(F32) * (zp * _sigmoid(zp))).astype(BF16)
    mix = jnp.dot(br_pool, ow_ref[0:pw, :], preferred_element_type=F32)
    mix = mix + lax.dot_general(br_ssm_t, ow_ref[pw:pw + sw, :], TN_DIMS, preferred_element_type=F32)
    per = nl // nb
    for b in range(nb):
        xo = x_ref[b * per:(b + 1) * per, :] + gate_ref[b:b + 1, :] * mix[b * per:(b + 1) * per]
        ms = jnp.mean(xo * xo, axis=-1, keepdims=True)
        o_ref[b * per:(b + 1) * per, :] = (xo * lax.rsqrt(ms + EPS) * fg_ref[...]).astype(o_ref.dtype)


def _merge_call(yt, ut, zt, pool_out, z_pool, x, gate, d_skip, glu_w, glu_b, out_w, final_g, t):
    b, l, d = x.shape
    _, sw, nl = yt.shape
    pw = pool_out.shape[-1]
    mixw = out_w.shape[0]
    col = lambda v: jnp.broadcast_to(v[:, None], (v.shape[0], LANES))
    slab = lambda width: pl.BlockSpec((1, sw, width), lambda i: (i, 0, 0))
    rows = lambda wd: pl.BlockSpec((nl, wd), lambda i: (0, i))
    out = pl.pallas_call(
        functools.partial(_merge_kernel, nb=b),
        out_shape=jax.ShapeDtypeStruct((nl, t * d), x.dtype),
        grid=(t,),
        in_specs=[
            slab(nl), slab(ut.shape[2]), slab(nl), rows(pw), rows(pw), rows(d),
            _const_spec((b, d)), _const_spec((sw, LANES)), _const_spec((2 * sw, sw)), _const_spec((2 * sw, LANES)),
            _const_spec((mixw, d)), _const_spec((1, d)),
        ],
        out_specs=rows(d),
        compiler_params=_cparams(("arbitrary",)),
        name="merge",
    )(yt, ut, zt, pool_out.reshape(nl, t * pw), z_pool.reshape(nl, t * pw), x.reshape(nl, t * d), gate,
      col(d_skip), glu_w.T.astype(BF16), col(glu_b), out_w.astype(BF16), final_g.reshape(1, d))
    return out.reshape(b, l, d)


def kernel(x, c, ctx, c_ctx, ada_w, ada_b, norm_g, in_w, pool_w, pool_scale, s5_lam_re, s5_lam_im, s5_log_dt,
           s5_b_re, s5_b_im, s5_c_re, s5_c_im, s5_d, glu_w, glu_b, out_w, final_g):
    assert ada_w.shape[0] == 1, "single-layer block"
    bsz, seq, d = x.shape
    cl = ctx.shape[1]
    mixw = in_w.shape[2] // 2
    poolw = pool_scale.shape[-1]
    ssmw = s5_d.shape[-1]
    n_grp, n_state = s5_lam_re.shape[2], s5_lam_re.shape[3]
    hc = ssmw // n_grp
    t = CHUNK_T
    assert poolw + ssmw == mixw and seq % t == 0 and cl % t == 0 and 2 * n_state == LANES and LANES % hc == 0

    mod = _ada_call(jnp.concatenate([c, c_ctx[None]], axis=0), ada_w[0], ada_b[0])
    shift, scale, gate = mod[:, :d], mod[:, d:2 * d], mod[:, 2 * d:]
    g1 = norm_g[0].reshape(1, d)
    w = in_w[0]
    w_main = jnp.concatenate([w[:, :poolw], w[:, mixw:mixw + poolw]], axis=1).astype(BF16)
    w_t = jnp.concatenate([w[:, poolw:mixw], w[:, mixw + poolw:]], axis=1).T.astype(BF16)

    u_pool, z_pool = _inproj_call(x, scale[:bsz, None], shift[:bsz, None], g1, w_main, (poolw, poolw), min(512, seq))
    ut, zt = _inproj_t_call(x, ctx, scale, shift, g1, w_t, t)
    pool_out = _pool_call(u_pool, pool_w[0], pool_scale[0])

    ncl, ncc = seq // t, cl // t
    nsteps = max(1, (ncc + ncl + ncc - 1).bit_length())
    params = _s5_params(s5_lam_re[0], s5_lam_im[0], s5_log_dt[0], s5_b_re[0], s5_b_im[0], s5_c_re[0], s5_c_im[0],
                        t, nsteps)
    yt = _s5_call(ut, params, t, bsz, ncl, ncc, nsteps, n_state, hc)

    return _merge_call(yt, ut, zt, pool_out, z_pool, x, gate[:bsz], s5_d[0], glu_w[0], glu_b[0], out_w[0],
                       final_g, t)
```

```python
import functools
import math

import jax
import jax.numpy as jnp
from jax import lax
from jax.experimental import pallas as pl
from jax.experimental.pallas import tpu as pltpu

GRID_W = 64
POOL_WINDOWS = (2, 4, 8, 16)
EPS = 1e-6
CHUNK_T = 32
LANES = 128
SUBLANES = 8
VMEM_LIMIT = 56 * 1024 * 1024

F32 = jnp.float32
BF16 = jnp.bfloat16


def _cparams(sem):
    return pltpu.CompilerParams(dimension_semantics=sem, vmem_limit_bytes=VMEM_LIMIT)


def _const_spec(shape):
    return pl.BlockSpec(shape, lambda *_: (0,) * len(shape), pipeline_mode=pl.Buffered(1))


def _sigmoid(v):
    return 1.0 / (1.0 + jnp.exp(-v))


def _gelu_tanh(v):
    return 0.5 * v * (1.0 + jnp.tanh(math.sqrt(2.0 / math.pi) * (v + 0.044715 * (v * v * v))))


def _modulated_norm(x, g, scale, shift):
    ms = jnp.mean(x * x, axis=-1, keepdims=True)
    return (x * lax.rsqrt(ms + EPS) * g) * (1.0 + scale) + shift


def _ada_kernel(ct_ref, w_ref, b_ref, o_ref, *, nrows):
    d = w_ref.shape[0]
    tn = w_ref.shape[1]

    def body(i, accs):
        k0 = pl.multiple_of(i * SUBLANES, SUBLANES)
        w = w_ref[pl.ds(k0, SUBLANES), :]
        cv = ct_ref[pl.ds(k0, SUBLANES), :]
        s = cv * _sigmoid(cv)
        return tuple(acc + s[:, r:r + 1] * w for r, acc in enumerate(accs))

    init = tuple(jnp.zeros((SUBLANES, tn), F32) for _ in range(nrows))
    accs = lax.fori_loop(0, d // SUBLANES, body, init, unroll=4)
    o_ref[...] = jnp.zeros(o_ref.shape, F32)
    for r, acc in enumerate(accs):
        o_ref[r:r + 1, :] = jnp.sum(acc, axis=0, keepdims=True) + b_ref[...]


def _ada_call(cvecs, ada_w, ada_b):
    nrows, d = cvecs.shape
    n = ada_w.shape[1]
    tn = 768 if n % 768 == 0 else n
    ct = jnp.zeros((d, SUBLANES), F32).at[:, :nrows].set(cvecs.T)
    out = pl.pallas_call(
        functools.partial(_ada_kernel, nrows=nrows),
        out_shape=jax.ShapeDtypeStruct((SUBLANES, n), F32),
        grid=(n // tn,),
        in_specs=[
            pl.BlockSpec((d, SUBLANES), lambda j: (0, 0)),
            pl.BlockSpec((d, tn), lambda j: (0, j)),
            pl.BlockSpec((1, tn), lambda j: (0, j)),
        ],
        out_specs=pl.BlockSpec((SUBLANES, tn), lambda j: (0, j)),
        compiler_params=_cparams(("arbitrary",)),
        name="ada",
    )(ct, ada_w, ada_b.reshape(1, n))
    return out[:nrows]


def _inproj_kernel(x_ref, sc_ref, sh_ref, g_ref, w_ref, *o_refs, nchunk):
    h = _modulated_norm(x_ref[0], g_ref[...], sc_ref[0], sh_ref[0]).astype(BF16)
    col = 0
    for o_ref in o_refs:
        width = o_ref.shape[-1]
        for n0 in range(0, width, nchunk):
            acc = jnp.dot(h, w_ref[:, col + n0:col + n0 + nchunk], preferred_element_type=F32)
            o_ref[0, :, n0:n0 + nchunk] = acc.astype(o_ref.dtype)
        col += width


def _inproj_call(x, scale, shift, g, w, widths, tm, name):
    b, l, d = x.shape
    nout = w.shape[1]
    assert sum(widths) == nout and l % tm == 0
    nchunk = min(512, min(widths))
    assert all(wd % nchunk == 0 for wd in widths)
    return pl.pallas_call(
        functools.partial(_inproj_kernel, nchunk=nchunk),
        out_shape=[jax.ShapeDtypeStruct((b, l, wd), BF16) for wd in widths],
        grid=(b, l // tm),
        in_specs=[
            pl.BlockSpec((1, tm, d), lambda i, j: (i, j, 0)),
            pl.BlockSpec((1, 1, d), lambda i, j: (i, 0, 0)),
            pl.BlockSpec((1, 1, d), lambda i, j: (i, 0, 0)),
            _const_spec((1, d)),
            _const_spec((d, nout)),
        ],
        out_specs=[pl.BlockSpec((1, tm, wd), lambda i, j: (i, j, 0)) for wd in widths],
        compiler_params=_cparams(("arbitrary", "arbitrary")),
        name=name,
    )(x, scale, shift, g, w)


def _pack_kernel(u_ref, uc_ref, ut_ref, scr_ref, *, t, nblk):
    j = pl.program_id(0)
    nk, tok, _ = scr_ref.shape
    ctok = uc_ref.shape[0]

    @pl.when(j < nblk)
    def _():
        for k in range(nk):
            scr_ref[k] = u_ref[:, LANES * k:LANES * (k + 1)].astype(F32)

    @pl.when(j == nblk)
    def _():
        for k in range(nk):
            scr_ref[k, 0:ctok, :] = uc_ref[:, LANES * k:LANES * (k + 1)].astype(F32)
            scr_ref[k, ctok:tok, :] = jnp.zeros((tok - ctok, LANES), F32)

    def body(ti, carry):
        for k in range(nk):
            rows = scr_ref[k, pl.ds(ti, LANES, stride=t), :]
            ut_ref[ti, LANES * k:LANES * (k + 1), :] = rows.T.astype(BF16)
        return carry

    lax.fori_loop(0, t, body, 0)


def _pack_call(u_ssm, uc_ssm, t):
    b, l, s = u_ssm.shape
    tok = LANES * t
    ntok = b * l
    ctok = b * uc_ssm.shape[1]
    assert ntok % tok == 0 and ctok <= tok and s % LANES == 0
    nblk = ntok // tok
    return pl.pallas_call(
        functools.partial(_pack_kernel, t=t, nblk=nblk),
        out_shape=jax.ShapeDtypeStruct((t, s, ntok // t + LANES), BF16),
        grid=(nblk + 1,),
        in_specs=[pl.BlockSpec((tok, s), lambda j: (jnp.minimum(j, nblk - 1), 0)),
                  _const_spec((ctok, s))],
        out_specs=pl.BlockSpec((t, s, LANES), lambda j: (0, 0, j)),
        scratch_shapes=[pltpu.VMEM((s // LANES, tok, LANES), F32)],
        compiler_params=_cparams(("arbitrary",)),
        name="pack",
    )(u_ssm.reshape(ntok, s), uc_ssm.reshape(ctok, s))


def _unpack_kernel(yt_ref, y_ref, scr_ref, *, t):
    nk = scr_ref.shape[0]

    def body(ti, carry):
        for k in range(nk):
            scr_ref[k, pl.ds(ti, LANES, stride=t), :] = yt_ref[ti, LANES * k:LANES * (k + 1), :].astype(F32).T
        return carry

    lax.fori_loop(0, t, body, 0)
    for k in range(nk):
        y_ref[:, LANES * k:LANES * (k + 1)] = scr_ref[k].astype(y_ref.dtype)


def _unpack_call(yt, b, l):
    t, s, nl = yt.shape
    tok = LANES * t
    y = pl.pallas_call(
        functools.partial(_unpack_kernel, t=t),
        out_shape=jax.ShapeDtypeStruct((nl * t, s), BF16),
        grid=(nl // LANES,),
        in_specs=[pl.BlockSpec((t, s, LANES), lambda j: (0, 0, j))],
        out_specs=pl.BlockSpec((tok, s), lambda j: (j, 0)),
        scratch_shapes=[pltpu.VMEM((s // LANES, tok, LANES), F32)],
        compiler_params=_cparams(("arbitrary",)),
        name="unpack",
    )(yt)
    return y.reshape(b, l, s)


def _pool_kernel(u_ref, band_ref, pw_ref, ps_ref, o_ref, pad_ref, *, rows, tb, pad):
    grp = pl.program_id(1)
    l = u_ref.shape[1]
    c = u_ref.shape[2]
    nblk = l // tb
    pad_ref[0:pad, :] = jnp.zeros((pad, c), F32)
    pad_ref[pad + l:pad + l + pad, :] = jnp.zeros((pad, c), F32)

    def colsum(i, carry):
        r0 = pl.multiple_of(i * tb, tb)
        cs = jnp.dot(band_ref[0], u_ref[0, pl.ds(r0, tb), :], preferred_element_type=F32)
        pad_ref[pl.ds(pl.multiple_of(pad + r0, GRID_W), tb), :] = cs
        return carry

    lax.fori_loop(0, nblk, colsum, 0)

    for gi, w in enumerate(POOL_WINDOWS):
        half = w // 2

        @pl.when(grp == gi)
        def _(w=w, half=half):
            def blk(i, carry):
                r0 = pl.multiple_of(i * tb, tb)
                acc = pad_ref[pl.ds(pl.multiple_of(pad + r0 - half * GRID_W, GRID_W), tb), :]
                for k in range(1, w):
                    acc = acc + pad_ref[pl.ds(pl.multiple_of(pad + r0 + (k - half) * GRID_W, GRID_W), tb), :]
                t = r0 + lax.broadcasted_iota(jnp.int32, (tb, 1), 0)
                r = t // GRID_W
                cc = t % GRID_W
                cnt_r = jnp.minimum(r + half, rows) - jnp.maximum(r - half, 0)
                cnt_c = jnp.minimum(cc + half, GRID_W) - jnp.maximum(cc - half, 0)
                m = acc / (cnt_r * cnt_c).astype(F32)
                dlt = (m - u_ref[0, pl.ds(r0, tb), :].astype(F32)).astype(BF16)
                o = jnp.dot(dlt, pw_ref[0], preferred_element_type=F32) * ps_ref[...]
                o_ref[0, pl.ds(r0, tb), :] = o.astype(o_ref.dtype)
                return carry

            lax.fori_loop(0, nblk, blk, 0)


def _pool_call(u_pool, pool_w, pool_scale):
    b, l, p = u_pool.shape
    ng = len(POOL_WINDOWS)
    c = p // ng
    rows = l // GRID_W
    tb = 4 * GRID_W
    pad = (max(POOL_WINDOWS) // 2) * GRID_W
    assert l % tb == 0 and c % LANES == 0
    i = jnp.arange(tb)
    same_row = (i[:, None] // GRID_W) == (i[None, :] // GRID_W)
    dcol = (i[None, :] % GRID_W) - (i[:, None] % GRID_W)
    band = jnp.stack([(same_row & (dcol >= -(w // 2)) & (dcol < w - w // 2)) for w in POOL_WINDOWS]).astype(BF16)
    return pl.pallas_call(
        functools.partial(_pool_kernel, rows=rows, tb=tb, pad=pad),
        out_shape=jax.ShapeDtypeStruct((b, l, p), BF16),
        grid=(b, ng),
        in_specs=[
            pl.BlockSpec((1, l, c), lambda i, j: (i, 0, j)),
            pl.BlockSpec((1, tb, tb), lambda i, j: (j, 0, 0)),
            pl.BlockSpec((1, c, c), lambda i, j: (j, 0, 0)),
            pl.BlockSpec((1, c), lambda i, j: (0, j)),
        ],
        out_specs=pl.BlockSpec((1, l, c), lambda i, j: (i, 0, j)),
        scratch_shapes=[pltpu.VMEM((pad + l + pad, c), F32)],
        compiler_params=_cparams(("arbitrary", "arbitrary")),
        name="pool",
    )(u_pool, band, pool_w.astype(BF16), pool_scale.reshape(1, p))


def _split3(v):
    hi = v.astype(BF16)
    r1 = v - hi.astype(F32)
    mid = r1.astype(BF16)
    lo = (r1 - mid.astype(F32)).astype(BF16)
    return hi, mid, lo


def _dot_hi(a, b):
    a_hi = a.astype(BF16)
    a_lo = (a - a_hi.astype(F32)).astype(BF16)
    b_hi = b.astype(BF16)
    b_lo = (b - b_hi.astype(F32)).astype(BF16)
    return (jnp.dot(a_hi, b_hi, preferred_element_type=F32) + jnp.dot(a_hi, b_lo, preferred_element_type=F32)
            + jnp.dot(a_lo, b_hi, preferred_element_type=F32))


def _s5_kernel(u_ref, pw_ref, bb_ref, cc_ref, ca_ref, cb_ref, rowa_ref, ap_ref, rep_ref, y_ref,
               win_ref, mt_ref, wout_ref, buf_ref, hin_ref, *, t, hc, nb, ncl, ncc, pad, nsteps, nstate):
    n = nstate
    th = t * hc
    nl = nb * ncl
    rb = ncc + ncl + ncc
    ntile = th // LANES

    @pl.when(pl.program_id(0) == 0)
    def _():
        buf_ref[...] = jnp.zeros(buf_ref.shape, F32)

    rep = rep_ref[...]
    prep = sum(jnp.dot(part, rep, preferred_element_type=F32) for part in _split3(pw_ref[0]))
    bt = jnp.concatenate([bb_ref[0]] * ntile, axis=1)
    e = []
    for d in range(2):
        pr, pi = prep[2 * d * n:(2 * d + 1) * n], prep[(2 * d + 1) * n:(2 * d + 2) * n]
        br, bi = bt[2 * d * n:(2 * d + 1) * n], bt[(2 * d + 1) * n:(2 * d + 2) * n]
        e.append((pr * br - pi * bi, pr * bi + pi * br))
    win_ref[0:n, :] = e[0][0].astype(BF16)
    win_ref[n:2 * n, :] = e[1][0].astype(BF16)
    win_ref[2 * n:3 * n, :] = e[0][1].astype(BF16)
    win_ref[3 * n:4 * n, :] = e[1][1].astype(BF16)
    cc = cc_ref[0]
    q_f = _dot_hi(cc[0:hc], jnp.concatenate(e[0], axis=0))
    q_b = _dot_hi(cc[hc:2 * hc], jnp.concatenate(e[1], axis=0))
    zeros = jnp.zeros((hc, th), F32)
    line_b = jnp.concatenate([zeros, q_b], axis=1)
    line_f = jnp.concatenate([q_f, zeros], axis=1)
    per_tile = LANES // hc
    rolled_b = [line_b if r == 0 else pltpu.roll(line_b, hc * r, axis=1) for r in range(per_tile)]
    rolled_f = [line_f if r == per_tile - 1 else pltpu.roll(line_f, 2 * th - hc * (per_tile - 1 - r), axis=1)
                for r in range(per_tile)]
    for ti in range(t):
        a, r = divmod(ti, per_tile)
        row_b = rolled_b[r][:, th - LANES * a:2 * th - LANES * a]
        off_f = LANES * (ntile - 1 - a)
        row_f = rolled_f[r][:, off_f:off_f + th]
        mt_ref[ti * hc:(ti + 1) * hc, :] = (row_b + row_f).astype(BF16)
    ca = ca_ref[0]
    cb = cb_ref[0]
    for ti in range(t):
        ra = rowa_ref[0, ti:ti + 1, :]
        rbw = jnp.concatenate([ra[:, 2 * n:4 * n], ra[:, 0:2 * n]], axis=1)
        wout_ref[ti * hc:(ti + 1) * hc, :] = (ca * ra + cb * rbw).astype(BF16)

    u = jnp.concatenate([u_ref[ti] for ti in range(t)], axis=0)
    s = jnp.dot(win_ref[...], u, preferred_element_type=F32).T
    for b in range(nb):
        ctx_rows = s[nl + b * ncc:nl + (b + 1) * ncc]
        buf_ref[b, pad:pad + ncc, :] = ctx_rows
        buf_ref[b, pad + ncc:pad + ncc + ncl, :] = s[b * ncl:(b + 1) * ncl]
        buf_ref[b, pad + ncc + ncl:pad + rb, :] = ctx_rows
    two_n = 2 * n
    is_fwd = lax.broadcasted_iota(jnp.int32, (rb, two_n), 1) < n

    def shifted(b, st, lo):
        return jnp.where(is_fwd, buf_ref[b, pad - st:pad - st + rb, lo:lo + two_n],
                         buf_ref[b, pad + st:pad + st + rb, lo:lo + two_n])

    for k in range(nsteps):
        st = 1 << k
        ar = ap_ref[0, 2 * k:2 * k + 1, :]
        ai = ap_ref[0, 2 * k + 1:2 * k + 2, :]
        for b in range(nb):
            xr = buf_ref[b, pad:pad + rb, 0:two_n]
            xi = buf_ref[b, pad:pad + rb, two_n:2 * two_n]
            pr = shifted(b, st, 0)
            pi = shifted(b, st, two_n)
            buf_ref[b, pad:pad + rb, 0:two_n] = xr + (ar * pr - ai * pi)
            buf_ref[b, pad:pad + rb, two_n:2 * two_n] = xi + (ar * pi + ai * pr)
    for b in range(nb):
        hin_ref[b * ncl:(b + 1) * ncl, 0:two_n] = shifted(b, 1, 0)[ncc:ncc + ncl]
        hin_ref[b * ncl:(b + 1) * ncl, two_n:2 * two_n] = shifted(b, 1, two_n)[ncc:ncc + ncl]

    hin_t = hin_ref[...].T.astype(BF16)
    y = jnp.dot(mt_ref[...], u[:, 0:nl], preferred_element_type=F32)
    y = y + jnp.dot(wout_ref[...], hin_t, preferred_element_type=F32)
    for ti in range(t):
        y_ref[ti] = y[ti * hc:(ti + 1) * hc].astype(y_ref.dtype)


def _s5_params(lam_re, lam_im, log_dt, b_re, b_im, c_re, c_im, t, nsteps):
    _, g, n = lam_re.shape
    h = b_re.shape[-1]
    lam_re, lam_im = lam_re.astype(F32), lam_im.astype(F32)
    dt = jnp.exp(log_dt.astype(F32))[..., None]
    lr, li = lam_re * dt, lam_im * dt

    def power(k):
        mag = jnp.exp(lr[..., None] * k)
        return mag * jnp.cos(li[..., None] * k), mag * jnp.sin(li[..., None] * k)

    a_re, a_im = jnp.exp(lr) * jnp.cos(li), jnp.exp(lr) * jnp.sin(li)
    den = lam_re * lam_re + lam_im * lam_im
    f_re = ((a_re - 1.0) * lam_re + a_im * lam_im) / den
    f_im = (a_im * lam_re - (a_re - 1.0) * lam_im) / den
    bb_re = f_re[..., None] * b_re - f_im[..., None] * b_im
    bb_im = f_re[..., None] * b_im + f_im[..., None] * b_re

    ell = jnp.arange(t, dtype=F32)
    pf_re, pf_im = power((t - 1.0) - ell)
    pb_re, pb_im = power(ell)
    pw = jnp.concatenate([pf_re[0], pf_im[0], pb_re[1], pb_im[1]], axis=1)
    pw = jnp.pad(pw, ((0, 0), (0, 0), (0, LANES - t)))
    bb = jnp.concatenate([bb_re[0], bb_im[0], bb_re[1], bb_im[1]], axis=1)
    bb = jnp.tile(bb, (1, 1, LANES // h))
    cc = jnp.concatenate([jnp.concatenate([c_re[0], -c_im[0]], axis=-1),
                          jnp.concatenate([c_re[1], -c_im[1]], axis=-1)], axis=1)
    ca = jnp.concatenate([c_re[0], c_re[1], -c_re[0], -c_re[1]], axis=-1)
    cb = jnp.concatenate([-c_im[0], -c_im[1], -c_im[0], -c_im[1]], axis=-1)
    tau = jnp.arange(t, dtype=F32)
    rf_re, rf_im = power(tau + 1.0)
    rb_re, rb_im = power(t - tau)
    rowa = jnp.concatenate([rf_re[0], rb_re[1], rf_im[0], rb_im[1]], axis=1)
    rowa = rowa.transpose(0, 2, 1)
    strides = t * (2.0 ** jnp.arange(nsteps, dtype=F32))
    ap_re, ap_im = power(strides)
    ap = jnp.stack([jnp.concatenate([ap_re[0], ap_re[1]], axis=1),
                    jnp.concatenate([ap_im[0], ap_im[1]], axis=1)], axis=-1)
    ap = ap.reshape(g, 2 * n, 2 * nsteps).transpose(0, 2, 1)
    rows = -(-2 * nsteps // SUBLANES) * SUBLANES
    ap = jnp.pad(ap, ((0, 0), (0, rows - 2 * nsteps), (0, 0)))
    lane = jnp.arange(t * h)
    rep = (jnp.arange(LANES)[:, None] == (lane // h)[None, :]).astype(BF16)
    return pw, bb, cc.astype(F32), ca.astype(F32), cb.astype(F32), rowa, ap, rep


def _s5_call(ut, params, t, nb, ncl, ncc, nsteps, n_state, hc):
    pw, bb, cc, ca, cb, rowa, ap, rep = params
    _, sw, width = ut.shape
    nl = nb * ncl
    g = sw // hc
    th = t * hc
    rb = ncc + ncl + ncc
    pad = max(SUBLANES, 1 << (nsteps - 1))
    grp = lambda shp: pl.BlockSpec((1,) + shp, lambda i: (i, 0, 0))
    return pl.pallas_call(
        functools.partial(_s5_kernel, t=t, hc=hc, nb=nb, ncl=ncl, ncc=ncc, pad=pad, nsteps=nsteps, nstate=n_state),
        out_shape=jax.ShapeDtypeStruct((t, sw, nl), BF16),
        grid=(g,),
        in_specs=[
            pl.BlockSpec((t, hc, width), lambda i: (0, i, 0)),
            grp((4 * n_state, LANES)), grp((4 * n_state, LANES)), grp((2 * hc, 2 * n_state)),
            grp((hc, 4 * n_state)), grp((hc, 4 * n_state)), grp((t, 4 * n_state)),
            grp((ap.shape[1], 2 * n_state)),
            _const_spec((LANES, th)),
        ],
        out_specs=pl.BlockSpec((t, hc, nl), lambda i: (0, i, 0)),
        scratch_shapes=[pltpu.VMEM((4 * n_state, th), BF16), pltpu.VMEM((th, th), BF16),
                        pltpu.VMEM((th, 4 * n_state), BF16),
                        pltpu.VMEM((nb, pad + rb + pad, 4 * n_state), F32),
                        pltpu.VMEM((nl, 4 * n_state), F32)],
        compiler_params=_cparams(("arbitrary",)),
        name="s5",
    )(ut, pw, bb, cc, ca, cb, rowa, ap, rep)


def _merge_kernel(y_ref, u_ref, p_ref, z_ref, x_ref, gate_ref, d_ref, gw_ref, gb_ref, ow_ref, fg_ref, o_ref):
    sw = y_ref.shape[-1]
    pw = p_ref.shape[-1]
    y = _gelu_tanh(y_ref[0].astype(F32) + d_ref[...] * u_ref[0].astype(F32)).astype(BF16)
    yy = jnp.dot(y, gw_ref[...], preferred_element_type=F32) + gb_ref[...]
    ssm_out = yy[:, :sw] * _sigmoid(yy[:, sw:])
    z = z_ref[0].astype(F32)
    zs = z * _sigmoid(z)
    br_pool = (p_ref[0].astype(F32) * zs[:, :pw]).astype(BF16)
    br_ssm = (ssm_out * zs[:, pw:]).astype(BF16)
    mix = jnp.dot(br_pool, ow_ref[0:pw, :], preferred_element_type=F32)
    mix = mix + jnp.dot(br_ssm, ow_ref[pw:pw + sw, :], preferred_element_type=F32)
    xo = x_ref[0] + gate_ref[0] * mix
    ms = jnp.mean(xo * xo, axis=-1, keepdims=True)
    o_ref[0] = (xo * lax.rsqrt(ms + EPS) * fg_ref[...]).astype(o_ref.dtype)


def _merge_call(y_ssm, u_ssm, pool_out, z, x, gate, d_skip, glu_w, glu_b, out_w, final_g, tm):
    b, l, d = x.shape
    sw = y_ssm.shape[-1]
    pw = pool_out.shape[-1]
    mixw = z.shape[-1]
    tok = lambda wd: pl.BlockSpec((1, tm, wd), lambda i, j: (i, j, 0))
    return pl.pallas_call(
        _merge_kernel,
        out_shape=jax.ShapeDtypeStruct((b, l, d), x.dtype),
        grid=(b, l // tm),
        in_specs=[
            tok(sw), tok(sw), tok(pw), tok(mixw), tok(d),
            pl.BlockSpec((1, 1, d), lambda i, j: (i, 0, 0)),
            _const_spec((1, sw)), _const_spec((sw, 2 * sw)), _const_spec((1, 2 * sw)), _const_spec((mixw, d)),
            _const_spec((1, d)),
        ],
        out_specs=tok(d),
        compiler_params=_cparams(("arbitrary", "arbitrary")),
        name="merge",
    )(y_ssm, u_ssm, pool_out, z, x, gate, d_skip.reshape(1, sw), glu_w.astype(BF16), glu_b.reshape(1, 2 * sw),
      out_w.astype(BF16), final_g.reshape(1, d))


def kernel(x, c, ctx, c_ctx, ada_w, ada_b, norm_g, in_w, pool_w, pool_scale, s5_lam_re, s5_lam_im, s5_log_dt,
           s5_b_re, s5_b_im, s5_c_re, s5_c_im, s5_d, glu_w, glu_b, out_w, final_g):
    assert ada_w.shape[0] == 1, "single-layer block"
    bsz, seq, d = x.shape
    cl = ctx.shape[1]
    mixw = in_w.shape[2] // 2
    poolw = pool_scale.shape[-1]
    ssmw = s5_d.shape[-1]
    n_grp, n_state = s5_lam_re.shape[2], s5_lam_re.shape[3]
    hc = ssmw // n_grp
    t = CHUNK_T
    assert poolw + ssmw == mixw and seq % t == 0 and cl % t == 0 and 2 * n_state == LANES and LANES % hc == 0

    mod = _ada_call(jnp.concatenate([c, c_ctx[None]], axis=0), ada_w[0], ada_b[0])
    shift, scale, gate = mod[:, :d], mod[:, d:2 * d], mod[:, 2 * d:]
    g1 = norm_g[0].reshape(1, d)
    in_w16 = in_w[0].astype(BF16)

    tm = min(512, seq)
    u_pool, u_ssm, z = _inproj_call(x, scale[:bsz, None], shift[:bsz, None], g1, in_w16,
                                    (poolw, ssmw, mixw), tm, "inproj")
    sc_c = jnp.broadcast_to(scale[bsz][None, None], (bsz, 1, d))
    sh_c = jnp.broadcast_to(shift[bsz][None, None], (bsz, 1, d))
    (uc_ssm,) = _inproj_call(ctx, sc_c, sh_c, g1, in_w16[:, poolw:mixw], (ssmw,), min(256, cl), "inproj_ctx")

    pool_out = _pool_call(u_pool, pool_w[0], pool_scale[0])

    ncl, ncc = seq // t, cl // t
    nsteps = max(1, (ncc + ncl + ncc - 1).bit_length())
    params = _s5_params(s5_lam_re[0], s5_lam_im[0], s5_log_dt[0], s5_b_re[0], s5_b_im[0], s5_c_re[0], s5_c_im[0],
                        t, nsteps)
    ut = _pack_call(u_ssm, uc_ssm, t)
    yt = _s5_call(ut, params, t, bsz, ncl, ncc, nsteps, n_state, hc)
    y_ssm = _unpack_call(yt, bsz, seq)

    return _merge_call(y_ssm, u_ssm, pool_out, z, x, gate[:bsz, None], s5_d[0], glu_w[0], glu_b[0], out_w[0],
                       final_g, tm)
```

```python
import functools
import math

import jax
import jax.numpy as jnp
from jax import lax
from jax.experimental import pallas as pl
from jax.experimental.pallas import tpu as pltpu

GRID_W = 64
POOL_WINDOWS = (2, 4, 8, 16)
EPS = 1e-6
CHUNK_T = 32
LANES = 128
SUBLANES = 8
VMEM_LIMIT = 56 * 1024 * 1024

F32 = jnp.float32
BF16 = jnp.bfloat16


def _cparams(sem):
    return pltpu.CompilerParams(dimension_semantics=sem, vmem_limit_bytes=VMEM_LIMIT)


def _const_spec(shape):
    return pl.BlockSpec(shape, lambda *_: (0,) * len(shape), pipeline_mode=pl.Buffered(1))


def _sigmoid(v):
    return 1.0 / (1.0 + jnp.exp(-v))


def _gelu_tanh(v):
    return 0.5 * v * (1.0 + jnp.tanh(math.sqrt(2.0 / math.pi) * (v + 0.044715 * (v * v * v))))


def _modulated_norm(x, g, scale, shift):
    ms = jnp.mean(x * x, axis=-1, keepdims=True)
    return (x * lax.rsqrt(ms + EPS) * g) * (1.0 + scale) + shift


def _ada_kernel(ct_ref, w_ref, b_ref, o_ref, sb_ref, *, nrows):
    d = w_ref.shape[0]
    tn = w_ref.shape[1]
    ntile = tn // LANES

    @pl.when(pl.program_id(0) == 0)
    def _():
        cv = ct_ref[...]
        s = cv * _sigmoid(cv)
        for r in range(nrows):
            sb_ref[r] = jnp.broadcast_to(s[:, r:r + 1], (d, LANES))

    def body(i, accs):
        k0 = pl.multiple_of(i * SUBLANES, SUBLANES)
        sb = [sb_ref[r, pl.ds(k0, SUBLANES), :] for r in range(nrows)]
        out = []
        for lt in range(ntile):
            w = w_ref[pl.ds(k0, SUBLANES), lt * LANES:(lt + 1) * LANES]
            out.append(tuple(accs[lt][r] + sb[r] * w for r in range(nrows)))
        return tuple(out)

    init = tuple(tuple(jnp.zeros((SUBLANES, LANES), F32) for _ in range(nrows)) for _ in range(ntile))
    accs = lax.fori_loop(0, d // SUBLANES, body, init, unroll=8)
    o_ref[...] = jnp.zeros(o_ref.shape, F32)
    for lt in range(ntile):
        for r in range(nrows):
            o_ref[r:r + 1, lt * LANES:(lt + 1) * LANES] = (jnp.sum(accs[lt][r], axis=0, keepdims=True)
                                                           + b_ref[:, lt * LANES:(lt + 1) * LANES])


def _ada_call(cvecs, ada_w, ada_b):
    nrows, d = cvecs.shape
    n = ada_w.shape[1]
    tn = 768 if n % 768 == 0 else n
    ct = jnp.zeros((d, SUBLANES), F32).at[:, :nrows].set(cvecs.T)
    out = pl.pallas_call(
        functools.partial(_ada_kernel, nrows=nrows),
        out_shape=jax.ShapeDtypeStruct((SUBLANES, n), F32),
        grid=(n // tn,),
        in_specs=[
            pl.BlockSpec((d, SUBLANES), lambda j: (0, 0)),
            pl.BlockSpec((d, tn), lambda j: (0, j)),
            pl.BlockSpec((1, tn), lambda j: (0, j)),
        ],
        out_specs=pl.BlockSpec((SUBLANES, tn), lambda j: (0, j)),
        scratch_shapes=[pltpu.VMEM((nrows, d, LANES), F32)],
        compiler_params=_cparams(("arbitrary",)),
        name="ada",
    )(ct, ada_w, ada_b.reshape(1, n))
    return out[:nrows]


def _inproj_kernel(x_ref, sc_ref, sh_ref, g_ref, w_ref, *o_refs, nchunk):
    h = _modulated_norm(x_ref[0], g_ref[...], sc_ref[0], sh_ref[0]).astype(BF16)
    col = 0
    for o_ref in o_refs:
        width = o_ref.shape[-1]
        for n0 in range(0, width, nchunk):
            acc = jnp.dot(h, w_ref[:, col + n0:col + n0 + nchunk], preferred_element_type=F32)
            o_ref[0, :, n0:n0 + nchunk] = acc.astype(o_ref.dtype)
        col += width


def _inproj_call(x, scale, shift, g, w, widths, tm, name):
    b, l, d = x.shape
    nout = w.shape[1]
    assert sum(widths) == nout and l % tm == 0
    nchunk = min(512, min(widths))
    assert all(wd % nchunk == 0 for wd in widths)
    return pl.pallas_call(
        functools.partial(_inproj_kernel, nchunk=nchunk),
        out_shape=[jax.ShapeDtypeStruct((b, l, wd), BF16) for wd in widths],
        grid=(b, l // tm),
        in_specs=[
            pl.BlockSpec((1, tm, d), lambda i, j: (i, j, 0)),
            pl.BlockSpec((1, 1, d), lambda i, j: (i, 0, 0)),
            pl.BlockSpec((1, 1, d), lambda i, j: (i, 0, 0)),
            _const_spec((1, d)),
            _const_spec((d, nout)),
        ],
        out_specs=[pl.BlockSpec((1, tm, wd), lambda i, j: (i, j, 0)) for wd in widths],
        compiler_params=_cparams(("arbitrary", "arbitrary")),
        name=name,
    )(x, scale, shift, g, w)


def _chunk_perm(t):
    i = jnp.arange(SUBLANES * t)
    src = (i % SUBLANES) * t + i // SUBLANES
    return (src[:, None] == i[None, :]).astype(BF16)


def _pack_kernel(u_ref, uc_ref, perm_ref, ut_ref, scr_ref, *, t, nblk):
    j = pl.program_id(0)
    s = scr_ref.shape[2]
    rows = SUBLANES * t
    ctok = uc_ref.shape[0]

    def permute(src, blk):
        pm = jnp.dot(perm_ref[...], src, preferred_element_type=F32)
        for ti in range(t):
            scr_ref[ti, pl.ds(pl.multiple_of(blk * SUBLANES, SUBLANES), SUBLANES), :] = pm[ti * SUBLANES:(ti + 1) * SUBLANES]

    @pl.when(j < nblk)
    def _():
        def body(blk, carry):
            permute(u_ref[pl.ds(pl.multiple_of(blk * rows, rows), rows), :], blk)
            return carry

        lax.fori_loop(0, LANES // SUBLANES, body, 0, unroll=2)

    @pl.when(j == nblk)
    def _():
        for blk in range(ctok // rows):
            permute(uc_ref[blk * rows:(blk + 1) * rows, :], blk)
        scr_ref[:, ctok // t:LANES, :] = jnp.zeros((t, LANES - ctok // t, s), F32)

    def transpose(ti, carry):
        for k in range(s // LANES):
            ut_ref[ti, LANES * k:LANES * (k + 1), :] = scr_ref[ti, :, LANES * k:LANES * (k + 1)].T.astype(BF16)
        return carry

    lax.fori_loop(0, t, transpose, 0, unroll=2)


def _pack_call(u_ssm, uc_ssm, t):
    b, l, s = u_ssm.shape
    tok = LANES * t
    rows = SUBLANES * t
    ntok = b * l
    ctok = b * uc_ssm.shape[1]
    assert ntok % tok == 0 and ctok <= tok and ctok % rows == 0 and s % LANES == 0
    nblk = ntok // tok
    return pl.pallas_call(
        functools.partial(_pack_kernel, t=t, nblk=nblk),
        out_shape=jax.ShapeDtypeStruct((t, s, ntok // t + LANES), BF16),
        grid=(nblk + 1,),
        in_specs=[pl.BlockSpec((tok, s), lambda j: (jnp.minimum(j, nblk - 1), 0)),
                  _const_spec((ctok, s)), _const_spec((rows, rows))],
        out_specs=pl.BlockSpec((t, s, LANES), lambda j: (0, 0, j)),
        scratch_shapes=[pltpu.VMEM((t, LANES, s), F32)],
        compiler_params=_cparams(("arbitrary",)),
        name="pack",
    )(u_ssm.reshape(ntok, s), uc_ssm.reshape(ctok, s), _chunk_perm(t))


def _unpack_kernel(yt_ref, perm_ref, y_ref, scr_ref, *, t):
    s = scr_ref.shape[2]
    rows = SUBLANES * t

    def transpose(ti, carry):
        for k in range(s // LANES):
            scr_ref[ti, :, LANES * k:LANES * (k + 1)] = yt_ref[ti, LANES * k:LANES * (k + 1), :].astype(F32).T
        return carry

    lax.fori_loop(0, t, transpose, 0, unroll=2)

    def body(blk, carry):
        c0 = pl.multiple_of(blk * SUBLANES, SUBLANES)
        src = jnp.concatenate([scr_ref[ti, pl.ds(c0, SUBLANES), :] for ti in range(t)], axis=0).astype(BF16)
        out = jnp.dot(perm_ref[...], src, preferred_element_type=F32)
        y_ref[pl.ds(pl.multiple_of(blk * rows, rows), rows), :] = out.astype(y_ref.dtype)
        return carry

    lax.fori_loop(0, LANES // SUBLANES, body, 0, unroll=2)


def _unpack_call(yt, b, l):
    t, s, nl = yt.shape
    tok = LANES * t
    rows = SUBLANES * t
    y = pl.pallas_call(
        functools.partial(_unpack_kernel, t=t),
        out_shape=jax.ShapeDtypeStruct((nl * t, s), BF16),
        grid=(nl // LANES,),
        in_specs=[pl.BlockSpec((t, s, LANES), lambda j: (0, 0, j)), _const_spec((rows, rows))],
        out_specs=pl.BlockSpec((tok, s), lambda j: (j, 0)),
        scratch_shapes=[pltpu.VMEM((t, LANES, s), F32)],
        compiler_params=_cparams(("arbitrary",)),
        name="unpack",
    )(yt, _chunk_perm(t).T)
    return y.reshape(b, l, s)


POOL_BLOCK_ROWS = 8


def _pool_kernel(u_ref, *refs, tb, pad):
    nwin = len(POOL_WINDOWS)
    band_refs = refs[:nwin]
    inv_ref, pw_ref, ps_ref, o_ref, pad_ref = refs[nwin:]
    grp = pl.program_id(1)
    l = u_ref.shape[1]
    c = u_ref.shape[2]
    nblk = l // tb
    pad_ref[0:pad, :] = jnp.zeros((pad, c), pad_ref.dtype)
    pad_ref[pad + l:pad + l + pad, :] = jnp.zeros((pad, c), pad_ref.dtype)

    def copy(i, carry):
        r0 = pl.multiple_of(i * tb, tb)
        pad_ref[pl.ds(pl.multiple_of(pad + r0, GRID_W), tb), :] = u_ref[0, pl.ds(r0, tb), :]
        return carry

    lax.fori_loop(0, nblk, copy, 0)

    for gi, w in enumerate(POOL_WINDOWS):
        half = w // 2
        span = band_refs[gi].shape[1]

        @pl.when(grp == gi)
        def _(half=half, span=span, band_ref=band_refs[gi]):
            def blk(i, carry):
                r0 = pl.multiple_of(i * tb, tb)
                win = pad_ref[pl.ds(pl.multiple_of(pad + r0 - half * GRID_W, GRID_W), span), :]
                box = jnp.dot(band_ref[...], win, preferred_element_type=F32)
                variant = jnp.where(i == 0, 0, jnp.where(i == nblk - 1, 2, 1))
                inv = inv_ref[0, variant]
                m = box * jnp.concatenate([inv] * (c // LANES), axis=1)
                dlt = (m - u_ref[0, pl.ds(r0, tb), :].astype(F32)).astype(BF16)
                o = jnp.dot(dlt, pw_ref[0], preferred_element_type=F32) * ps_ref[...]
                o_ref[0, pl.ds(r0, tb), :] = o.astype(o_ref.dtype)
                return carry

            lax.fori_loop(0, nblk, blk, 0)


def _pool_tables(rows, tb):
    br = tb // GRID_W
    i = jnp.arange(tb)
    ri, ci = i // GRID_W, i % GRID_W
    bands, invs = [], []
    for w in POOL_WINDOWS:
        half = w // 2
        j = jnp.arange((br + w) * GRID_W)
        dr = (j // GRID_W)[None, :] - ri[:, None]
        dc = (j % GRID_W)[None, :] - ci[:, None]
        bands.append(((dr >= 0) & (dr < w) & (dc >= -half) & (dc < w - half)).astype(BF16))
        cnt_c = jnp.minimum(ci + w - half, GRID_W) - jnp.maximum(ci - half, 0)
        per_variant = []
        for r_first in (0, br, rows - br):
            r = r_first + ri
            cnt_r = jnp.minimum(r + w - half, rows) - jnp.maximum(r - half, 0)
            per_variant.append(1.0 / (cnt_r * cnt_c).astype(F32))
        invs.append(jnp.stack(per_variant))
    inv = jnp.broadcast_to(jnp.stack(invs)[..., None], (len(POOL_WINDOWS), 3, tb, LANES))
    return bands, inv


def _pool_call(u_pool, pool_w, pool_scale):
    b, l, p = u_pool.shape
    ng = len(POOL_WINDOWS)
    c = p // ng
    rows = l // GRID_W
    tb = POOL_BLOCK_ROWS * GRID_W
    pad = POOL_BLOCK_ROWS * GRID_W
    assert l % tb == 0 and c % LANES == 0 and rows >= 3 * POOL_BLOCK_ROWS and max(POOL_WINDOWS) <= 2 * POOL_BLOCK_ROWS
    bands, inv = _pool_tables(rows, tb)
    return pl.pallas_call(
        functools.partial(_pool_kernel, tb=tb, pad=pad),
        out_shape=jax.ShapeDtypeStruct((b, l, p), BF16),
        grid=(b, ng),
        in_specs=[pl.BlockSpec((1, l, c), lambda i, j: (i, 0, j))]
        + [_const_spec(bd.shape) for bd in bands]
        + [
            pl.BlockSpec((1, 3, tb, LANES), lambda i, j: (j, 0, 0, 0)),
            pl.BlockSpec((1, c, c), lambda i, j: (j, 0, 0)),
            pl.BlockSpec((1, c), lambda i, j: (0, j)),
        ],
        out_specs=pl.BlockSpec((1, l, c), lambda i, j: (i, 0, j)),
        scratch_shapes=[pltpu.VMEM((pad + l + pad, c), BF16)],
        compiler_params=_cparams(("arbitrary", "arbitrary")),
        name="pool",
    )(u_pool, *bands, inv, pool_w.astype(BF16), pool_scale.reshape(1, p))


def _split3(v):
    hi = v.astype(BF16)
    r1 = v - hi.astype(F32)
    mid = r1.astype(BF16)
    lo = (r1 - mid.astype(F32)).astype(BF16)
    return hi, mid, lo


def _dot_hi(a, b):
    a_hi = a.astype(BF16)
    a_lo = (a - a_hi.astype(F32)).astype(BF16)
    b_hi = b.astype(BF16)
    b_lo = (b - b_hi.astype(F32)).astype(BF16)
    return (jnp.dot(a_hi, b_hi, preferred_element_type=F32) + jnp.dot(a_hi, b_lo, preferred_element_type=F32)
            + jnp.dot(a_lo, b_hi, preferred_element_type=F32))


def _s5_kernel(u_ref, pw_ref, bb_ref, cc_ref, ca_ref, cb_ref, rowa_ref, ap_ref, rep_ref, y_ref,
               win_ref, mt_ref, wout_ref, buf_ref, hin_ref, *, t, hc, nb, ncl, ncc, pad, nsteps, nstate):
    n = nstate
    th = t * hc
    nl = nb * ncl
    rb = ncc + ncl
    ntile = th // LANES

    @pl.when(pl.program_id(0) == 0)
    def _():
        buf_ref[...] = jnp.zeros(buf_ref.shape, F32)

    rep = rep_ref[...]
    prep = sum(jnp.dot(part, rep, preferred_element_type=F32) for part in _split3(pw_ref[0]))
    bt = jnp.concatenate([bb_ref[0]] * ntile, axis=1)
    e = []
    for d in range(2):
        pr, pi = prep[2 * d * n:(2 * d + 1) * n], prep[(2 * d + 1) * n:(2 * d + 2) * n]
        br, bi = bt[2 * d * n:(2 * d + 1) * n], bt[(2 * d + 1) * n:(2 * d + 2) * n]
        e.append((pr * br - pi * bi, pr * bi + pi * br))
    win_ref[0:n, :] = e[0][0].astype(BF16)
    win_ref[n:2 * n, :] = e[1][0].astype(BF16)
    win_ref[2 * n:3 * n, :] = e[0][1].astype(BF16)
    win_ref[3 * n:4 * n, :] = e[1][1].astype(BF16)
    cc = cc_ref[0]
    q_f = _dot_hi(cc[0:hc], jnp.concatenate(e[0], axis=0))
    q_b = _dot_hi(cc[hc:2 * hc], jnp.concatenate(e[1], axis=0))
    zeros = jnp.zeros((hc, th), F32)
    line_b = jnp.concatenate([zeros, q_b], axis=1)
    line_f = jnp.concatenate([q_f, zeros], axis=1)
    per_tile = LANES // hc
    rolled_b = [line_b if r == 0 else pltpu.roll(line_b, hc * r, axis=1) for r in range(per_tile)]
    rolled_f = [line_f if r == per_tile - 1 else pltpu.roll(line_f, 2 * th - hc * (per_tile - 1 - r), axis=1)
                for r in range(per_tile)]
    for ti in range(t):
        a, r = divmod(ti, per_tile)
        row_b = rolled_b[r][:, th - LANES * a:2 * th - LANES * a]
        off_f = LANES * (ntile - 1 - a)
        row_f = rolled_f[r][:, off_f:off_f + th]
        mt_ref[ti * hc:(ti + 1) * hc, :] = (row_b + row_f).astype(BF16)
    ca = ca_ref[0]
    cb = cb_ref[0]
    for ti in range(t):
        ra = rowa_ref[0, ti:ti + 1, :]
        rbw = jnp.concatenate([ra[:, 2 * n:4 * n], ra[:, 0:2 * n]], axis=1)
        wout_ref[ti * hc:(ti + 1) * hc, :] = (ca * ra + cb * rbw).astype(BF16)

    u = jnp.concatenate([u_ref[ti] for ti in range(t)], axis=0)
    st_all = jnp.dot(win_ref[...], u, preferred_element_type=F32)
    first_half = lax.broadcasted_iota(jnp.int32, (LANES, LANES), 1) < n
    for q in range(4):
        rows_q = st_all[q * n:(q + 1) * n]
        lat = jnp.concatenate([rows_q[:, b * ncl:(b + 1) * ncl] for b in range(nb)], axis=0).T
        cblk = jnp.concatenate([rows_q[:, nl:nl + LANES]] * nb, axis=0).T
        ctx = jnp.where(first_half, cblk, jnp.concatenate([cblk[ncc:], cblk[:ncc]], axis=0))[0:ncc]
        if q % 2 == 0:
            buf_ref[q, pad:pad + ncc, :] = ctx
            buf_ref[q, pad + ncc:pad + rb, :] = lat
        else:
            buf_ref[q, pad:pad + ncl, :] = lat
            buf_ref[q, pad + ncl:pad + rb, :] = ctx
    for k in range(nsteps):
        st = 1 << k
        for d in range(2):
            sgn = -st if d == 0 else st
            ar = ap_ref[0, 4 * k + 2 * d:4 * k + 2 * d + 1, :]
            ai = ap_ref[0, 4 * k + 2 * d + 1:4 * k + 2 * d + 2, :]
            xr = buf_ref[d, pad:pad + rb, :]
            xi = buf_ref[2 + d, pad:pad + rb, :]
            pr = buf_ref[d, pad + sgn:pad + sgn + rb, :]
            pi = buf_ref[2 + d, pad + sgn:pad + sgn + rb, :]
            buf_ref[d, pad:pad + rb, :] = xr + (ar * pr - ai * pi)
            buf_ref[2 + d, pad:pad + rb, :] = xi + (ar * pi + ai * pr)
    for q in range(4):
        lo = pad + ncc - 1 if q % 2 == 0 else pad + 1
        ent = buf_ref[q, lo:lo + ncl, :].T.astype(BF16)
        for b in range(nb):
            hin_ref[q * n:(q + 1) * n, b * ncl:(b + 1) * ncl] = ent[b * n:(b + 1) * n]

    y = jnp.dot(mt_ref[...], u[:, 0:nl], preferred_element_type=F32)
    y = y + jnp.dot(wout_ref[...], hin_ref[...], preferred_element_type=F32)
    for ti in range(t):
        y_ref[ti] = y[ti * hc:(ti + 1) * hc].astype(y_ref.dtype)


def _s5_params(lam_re, lam_im, log_dt, b_re, b_im, c_re, c_im, t, nsteps):
    _, g, n = lam_re.shape
    h = b_re.shape[-1]
    lam_re, lam_im = lam_re.astype(F32), lam_im.astype(F32)
    dt = jnp.exp(log_dt.astype(F32))[..., None]
    lr, li = lam_re * dt, lam_im * dt

    def power(k):
        mag = jnp.exp(lr[..., None] * k)
        return mag * jnp.cos(li[..., None] * k), mag * jnp.sin(li[..., None] * k)

    a_re, a_im = jnp.exp(lr) * jnp.cos(li), jnp.exp(lr) * jnp.sin(li)
    den = lam_re * lam_re + lam_im * lam_im
    f_re = ((a_re - 1.0) * lam_re + a_im * lam_im) / den
    f_im = (a_im * lam_re - (a_re - 1.0) * lam_im) / den
    bb_re = f_re[..., None] * b_re - f_im[..., None] * b_im
    bb_im = f_re[..., None] * b_im + f_im[..., None] * b_re

    ell = jnp.arange(t, dtype=F32)
    pf_re, pf_im = power((t - 1.0) - ell)
    pb_re, pb_im = power(ell)
    pw = jnp.concatenate([pf_re[0], pf_im[0], pb_re[1], pb_im[1]], axis=1)
    pw = jnp.pad(pw, ((0, 0), (0, 0), (0, LANES - t)))
    bb = jnp.concatenate([bb_re[0], bb_im[0], bb_re[1], bb_im[1]], axis=1)
    bb = jnp.tile(bb, (1, 1, LANES // h))
    cc = jnp.concatenate([jnp.concatenate([c_re[0], -c_im[0]], axis=-1),
                          jnp.concatenate([c_re[1], -c_im[1]], axis=-1)], axis=1)
    ca = jnp.concatenate([c_re[0], c_re[1], -c_re[0], -c_re[1]], axis=-1)
    cb = jnp.concatenate([-c_im[0], -c_im[1], -c_im[0], -c_im[1]], axis=-1)
    tau = jnp.arange(t, dtype=F32)
    rf_re, rf_im = power(tau + 1.0)
    rb_re, rb_im = power(t - tau)
    rowa = jnp.concatenate([rf_re[0], rb_re[1], rf_im[0], rb_im[1]], axis=1)
    rowa = rowa.transpose(0, 2, 1)
    strides = t * (2.0 ** jnp.arange(nsteps, dtype=F32))
    ap_re, ap_im = power(strides)
    ap = jnp.stack([ap_re[0], ap_im[0], ap_re[1], ap_im[1]], axis=-1)
    ap = ap.reshape(g, n, 4 * nsteps).transpose(0, 2, 1)
    ap = jnp.tile(ap, (1, 1, LANES // n))
    rows = -(-4 * nsteps // SUBLANES) * SUBLANES
    ap = jnp.pad(ap, ((0, 0), (0, rows - 4 * nsteps), (0, 0)))
    lane = jnp.arange(t * h)
    rep = (jnp.arange(LANES)[:, None] == (lane // h)[None, :]).astype(BF16)
    return pw, bb, cc.astype(F32), ca.astype(F32), cb.astype(F32), rowa, ap, rep


def _s5_call(ut, params, t, nb, ncl, ncc, nsteps, n_state, hc):
    pw, bb, cc, ca, cb, rowa, ap, rep = params
    _, sw, width = ut.shape
    nl = nb * ncl
    g = sw // hc
    th = t * hc
    rb = ncc + ncl
    pad = max(SUBLANES, 1 << (nsteps - 1))
    assert nb * n_state == LANES and nb == 2
    grp = lambda shp: pl.BlockSpec((1,) + shp, lambda i: (i, 0, 0))
    return pl.pallas_call(
        functools.partial(_s5_kernel, t=t, hc=hc, nb=nb, ncl=ncl, ncc=ncc, pad=pad, nsteps=nsteps, nstate=n_state),
        out_shape=jax.ShapeDtypeStruct((t, sw, nl), BF16),
        grid=(g,),
        in_specs=[
            pl.BlockSpec((t, hc, width), lambda i: (0, i, 0)),
            grp((4 * n_state, LANES)), grp((4 * n_state, LANES)), grp((2 * hc, 2 * n_state)),
            grp((hc, 4 * n_state)), grp((hc, 4 * n_state)), grp((t, 4 * n_state)),
            grp((ap.shape[1], 2 * n_state)),
            _const_spec((LANES, th)),
        ],
        out_specs=pl.BlockSpec((t, hc, nl), lambda i: (0, i, 0)),
        scratch_shapes=[pltpu.VMEM((4 * n_state, th), BF16), pltpu.VMEM((th, th), BF16),
                        pltpu.VMEM((th, 4 * n_state), BF16),
                        pltpu.VMEM((4, pad + rb + pad, LANES), F32),
                        pltpu.VMEM((4 * n_state, nl), BF16)],
        compiler_params=_cparams(("arbitrary",)),
        name="s5",
    )(ut, pw, bb, cc, ca, cb, rowa, ap, rep)


def _merge_kernel(y_ref, u_ref, p_ref, z_ref, x_ref, gate_ref, d_ref, gw_ref, gb_ref, ow_ref, fg_ref, o_ref):
    sw = y_ref.shape[-1]
    pw = p_ref.shape[-1]
    y = _gelu_tanh(y_ref[0].astype(F32) + d_ref[...] * u_ref[0].astype(F32)).astype(BF16)
    yy = jnp.dot(y, gw_ref[...], preferred_element_type=F32) + gb_ref[...]
    ssm_out = yy[:, :sw] * _sigmoid(yy[:, sw:])
    z = z_ref[0].astype(F32)
    zs = z * _sigmoid(z)
    br_pool = (p_ref[0].astype(F32) * zs[:, :pw]).astype(BF16)
    br_ssm = (ssm_out * zs[:, pw:]).astype(BF16)
    mix = jnp.dot(br_pool, ow_ref[0:pw, :], preferred_element_type=F32)
    mix = mix + jnp.dot(br_ssm, ow_ref[pw:pw + sw, :], preferred_element_type=F32)
    xo = x_ref[0] + gate_ref[0] * mix
    ms = jnp.mean(xo * xo, axis=-1, keepdims=True)
    o_ref[0] = (xo * lax.rsqrt(ms + EPS) * fg_ref[...]).astype(o_ref.dtype)


def _merge_call(y_ssm, u_ssm, pool_out, z, x, gate, d_skip, glu_w, glu_b, out_w, final_g, tm):
    b, l, d = x.shape
    sw = y_ssm.shape[-1]
    pw = pool_out.shape[-1]
    mixw = z.shape[-1]
    tok = lambda wd: pl.BlockSpec((1, tm, wd), lambda i, j: (i, j, 0))
    return pl.pallas_call(
        _merge_kernel,
        out_shape=jax.ShapeDtypeStruct((b, l, d), x.dtype),
        grid=(b, l // tm),
        in_specs=[
            tok(sw), tok(sw), tok(pw), tok(mixw), tok(d),
            pl.BlockSpec((1, 1, d), lambda i, j: (i, 0, 0)),
            _const_spec((1, sw)), _const_spec((sw, 2 * sw)), _const_spec((1, 2 * sw)), _const_spec((mixw, d)),
            _const_spec((1, d)),
        ],
        out_specs=tok(d),
        compiler_params=_cparams(("arbitrary", "arbitrary")),
        name="merge",
    )(y_ssm, u_ssm, pool_out, z, x, gate, d_skip.reshape(1, sw), glu_w.astype(BF16), glu_b.reshape(1, 2 * sw),
      out_w.astype(BF16), final_g.reshape(1, d))


def kernel(x, c, ctx, c_ctx, ada_w, ada_b, norm_g, in_w, pool_w, pool_scale, s5_lam_re, s5_lam_im, s5_log_dt,
           s5_b_re, s5_b_im, s5_c_re, s5_c_im, s5_d, glu_w, glu_b, out_w, final_g):
    assert ada_w.shape[0] == 1, "single-layer block"
    bsz, seq, d = x.shape
    cl = ctx.shape[1]
    mixw = in_w.shape[2] // 2
    poolw = pool_scale.shape[-1]
    ssmw = s5_d.shape[-1]
    n_grp, n_state = s5_lam_re.shape[2], s5_lam_re.shape[3]
    hc = ssmw // n_grp
    t = CHUNK_T
    assert poolw + ssmw == mixw and seq % t == 0 and cl % t == 0 and 2 * n_state == LANES and LANES % hc == 0

    mod = _ada_call(jnp.concatenate([c, c_ctx[None]], axis=0), ada_w[0], ada_b[0])
    shift, scale, gate = mod[:, :d], mod[:, d:2 * d], mod[:, 2 * d:]
    g1 = norm_g[0].reshape(1, d)
    in_w16 = in_w[0].astype(BF16)

    tm = min(512, seq)
    u_pool, u_ssm, z = _inproj_call(x, scale[:bsz, None], shift[:bsz, None], g1, in_w16,
                                    (poolw, ssmw, mixw), tm, "inproj")
    sc_c = jnp.broadcast_to(scale[bsz][None, None], (bsz, 1, d))
    sh_c = jnp.broadcast_to(shift[bsz][None, None], (bsz, 1, d))
    (uc_ssm,) = _inproj_call(ctx, sc_c, sh_c, g1, in_w16[:, poolw:mixw], (ssmw,), min(256, cl), "inproj_ctx")

    pool_out = _pool_call(u_pool, pool_w[0], pool_scale[0])

    ncl, ncc = seq // t, cl // t
    nsteps = max(1, (ncc + ncl - 1).bit_length())
    params = _s5_params(s5_lam_re[0], s5_lam_im[0], s5_log_dt[0], s5_b_re[0], s5_b_im[0], s5_c_re[0], s5_c_im[0],
                        t, nsteps)
    ut = _pack_call(u_ssm, uc_ssm, t)
    yt = _s5_call(ut, params, t, bsz, ncl, ncc, nsteps, n_state, hc)
    y_ssm = _unpack_call(yt, bsz, seq)

    return _merge_call(y_ssm, u_ssm, pool_out, z, x, gate[:bsz, None], s5_d[0], glu_w[0], glu_b[0], out_w[0],
                       final_g, tm)
```

```python
import functools
import math

import jax
import jax.numpy as jnp
from jax import lax
from jax.experimental import pallas as pl
from jax.experimental.pallas import tpu as pltpu

GRID_W = 64
POOL_WINDOWS = (2, 4, 8, 16)
EPS = 1e-6
CHUNK_T = 32
S5_GROUPS_PER_STEP = 2
LANES = 128
SUBLANES = 8
VMEM_LIMIT = 56 * 1024 * 1024

F32 = jnp.float32
BF16 = jnp.bfloat16


def _cparams(sem):
    return pltpu.CompilerParams(dimension_semantics=sem, vmem_limit_bytes=VMEM_LIMIT)


def _const_spec(shape):
    return pl.BlockSpec(shape, lambda *_: (0,) * len(shape), pipeline_mode=pl.Buffered(1))


def _sigmoid(v):
    return 1.0 / (1.0 + jnp.exp(-v))


def _gelu_tanh(v):
    return 0.5 * v * (1.0 + jnp.tanh(math.sqrt(2.0 / math.pi) * (v + 0.044715 * (v * v * v))))


def _modulated_norm(x, g, scale, shift):
    ms = jnp.mean(x * x, axis=-1, keepdims=True)
    return (x * lax.rsqrt(ms + EPS) * g) * (1.0 + scale) + shift


def _ada_kernel(ct_ref, w_ref, b_ref, o_ref, sb_ref, *, nrows):
    d = w_ref.shape[0]
    tn = w_ref.shape[1]
    ntile = tn // LANES

    @pl.when(pl.program_id(0) == 0)
    def _():
        cv = ct_ref[...]
        s = cv * _sigmoid(cv)
        for r in range(nrows):
            sb_ref[r] = jnp.broadcast_to(s[:, r:r + 1], (d, LANES))

    def body(i, accs):
        k0 = pl.multiple_of(i * SUBLANES, SUBLANES)
        sb = [sb_ref[r, pl.ds(k0, SUBLANES), :] for r in range(nrows)]
        out = []
        for lt in range(ntile):
            w = w_ref[pl.ds(k0, SUBLANES), lt * LANES:(lt + 1) * LANES]
            out.append(tuple(accs[lt][r] + sb[r] * w for r in range(nrows)))
        return tuple(out)

    init = tuple(tuple(jnp.zeros((SUBLANES, LANES), F32) for _ in range(nrows)) for _ in range(ntile))
    accs = lax.fori_loop(0, d // SUBLANES, body, init, unroll=8)
    o_ref[...] = jnp.zeros(o_ref.shape, F32)
    for lt in range(ntile):
        for r in range(nrows):
            o_ref[r:r + 1, lt * LANES:(lt + 1) * LANES] = (jnp.sum(accs[lt][r], axis=0, keepdims=True)
                                                           + b_ref[:, lt * LANES:(lt + 1) * LANES])


def _ada_call(cvecs, ada_w, ada_b):
    nrows, d = cvecs.shape
    n = ada_w.shape[1]
    tn = 768 if n % 768 == 0 else n
    ct = jnp.zeros((d, SUBLANES), F32).at[:, :nrows].set(cvecs.T)
    out = pl.pallas_call(
        functools.partial(_ada_kernel, nrows=nrows),
        out_shape=jax.ShapeDtypeStruct((SUBLANES, n), F32),
        grid=(n // tn,),
        in_specs=[
            pl.BlockSpec((d, SUBLANES), lambda j: (0, 0)),
            pl.BlockSpec((d, tn), lambda j: (0, j)),
            pl.BlockSpec((1, tn), lambda j: (0, j)),
        ],
        out_specs=pl.BlockSpec((SUBLANES, tn), lambda j: (0, j)),
        scratch_shapes=[pltpu.VMEM((nrows, d, LANES), F32)],
        compiler_params=_cparams(("arbitrary",)),
        name="ada",
    )(ct, ada_w, ada_b.reshape(1, n))
    return out[:nrows]


def _inproj_kernel(x_ref, sc_ref, sh_ref, g_ref, w_ref, *o_refs, nchunk, silu):
    h = _modulated_norm(x_ref[0], g_ref[...], sc_ref[0], sh_ref[0]).astype(BF16)
    col = 0
    for o_ref, gate in zip(o_refs, silu):
        width = o_ref.shape[-1]
        for n0 in range(0, width, nchunk):
            acc = jnp.dot(h, w_ref[:, col + n0:col + n0 + nchunk], preferred_element_type=F32)
            if gate:
                acc = acc * _sigmoid(acc)
            o_ref[0, :, n0:n0 + nchunk] = acc.astype(o_ref.dtype)
        col += width


def _inproj_call(x, scale, shift, g, w, widths, silu, tm, name, wcol=0):
    b, l, d = x.shape
    nout = sum(widths)
    assert w.shape[1] % nout == 0 and l % tm == 0
    nchunk = min(512, min(widths))
    assert all(wd % nchunk == 0 for wd in widths)
    return pl.pallas_call(
        functools.partial(_inproj_kernel, nchunk=nchunk, silu=silu),
        out_shape=[jax.ShapeDtypeStruct((b, l, wd), BF16) for wd in widths],
        grid=(b, l // tm),
        in_specs=[
            pl.BlockSpec((1, tm, d), lambda i, j: (i, j, 0)),
            pl.BlockSpec((1, 1, d), lambda i, j: (i, 0, 0)),
            pl.BlockSpec((1, 1, d), lambda i, j: (i, 0, 0)),
            _const_spec((1, d)),
            pl.BlockSpec((d, nout), lambda i, j: (0, wcol), pipeline_mode=pl.Buffered(1)),
        ],
        out_specs=[pl.BlockSpec((1, tm, wd), lambda i, j: (i, j, 0)) for wd in widths],
        compiler_params=_cparams(("arbitrary", "arbitrary")),
        name=name,
    )(x, scale, shift, g, w)


def _chunk_perm(t):
    i = jnp.arange(SUBLANES * t)
    src = (i % SUBLANES) * t + i // SUBLANES
    return (src[:, None] == i[None, :]).astype(BF16)


def _pack_kernel(u_ref, uc_ref, perm_ref, ut_ref, scr_ref, *, t, nblk):
    j = pl.program_id(0)
    s = scr_ref.shape[2]
    rows = SUBLANES * t
    ctok = uc_ref.shape[0]

    def permute(src, blk):
        pm = jnp.dot(perm_ref[...], src, preferred_element_type=F32)
        for ti in range(t):
            scr_ref[ti, pl.ds(pl.multiple_of(blk * SUBLANES, SUBLANES), SUBLANES), :] = pm[ti * SUBLANES:(ti + 1) * SUBLANES]

    @pl.when(j < nblk)
    def _():
        def body(blk, carry):
            permute(u_ref[pl.ds(pl.multiple_of(blk * rows, rows), rows), :], blk)
            return carry

        lax.fori_loop(0, LANES // SUBLANES, body, 0, unroll=2)

    @pl.when(j == nblk)
    def _():
        for blk in range(ctok // rows):
            permute(uc_ref[blk * rows:(blk + 1) * rows, :], blk)
        scr_ref[:, ctok // t:LANES, :] = jnp.zeros((t, LANES - ctok // t, s), F32)

    def transpose(ti, carry):
        for k in range(s // LANES):
            ut_ref[ti, LANES * k:LANES * (k + 1), :] = scr_ref[ti, :, LANES * k:LANES * (k + 1)].T.astype(BF16)
        return carry

    lax.fori_loop(0, t, transpose, 0, unroll=2)


def _pack_call(u_ssm, uc_ssm, t):
    b, l, s = u_ssm.shape
    tok = LANES * t
    rows = SUBLANES * t
    ntok = b * l
    ctok = b * uc_ssm.shape[1]
    assert ntok % tok == 0 and ctok <= tok and ctok % rows == 0 and s % LANES == 0
    nblk = ntok // tok
    return pl.pallas_call(
        functools.partial(_pack_kernel, t=t, nblk=nblk),
        out_shape=jax.ShapeDtypeStruct((t, s, ntok // t + LANES), BF16),
        grid=(nblk + 1,),
        in_specs=[pl.BlockSpec((tok, s), lambda j: (jnp.minimum(j, nblk - 1), 0)),
                  _const_spec((ctok, s)), _const_spec((rows, rows))],
        out_specs=pl.BlockSpec((t, s, LANES), lambda j: (0, 0, j)),
        scratch_shapes=[pltpu.VMEM((t, LANES, s), F32)],
        compiler_params=_cparams(("arbitrary",)),
        name="pack",
    )(u_ssm.reshape(ntok, s), uc_ssm.reshape(ctok, s), _chunk_perm(t))


def _unpack_kernel(yt_ref, perm_ref, y_ref, scr_ref, *, t):
    s = scr_ref.shape[2]
    rows = SUBLANES * t

    def transpose(ti, carry):
        for k in range(s // LANES):
            scr_ref[ti, :, LANES * k:LANES * (k + 1)] = yt_ref[ti, LANES * k:LANES * (k + 1), :].astype(F32).T
        return carry

    lax.fori_loop(0, t, transpose, 0, unroll=2)

    def body(blk, carry):
        c0 = pl.multiple_of(blk * SUBLANES, SUBLANES)
        src = jnp.concatenate([scr_ref[ti, pl.ds(c0, SUBLANES), :] for ti in range(t)], axis=0).astype(BF16)
        out = jnp.dot(perm_ref[...], src, preferred_element_type=F32)
        y_ref[pl.ds(pl.multiple_of(blk * rows, rows), rows), :] = out.astype(y_ref.dtype)
        return carry

    lax.fori_loop(0, LANES // SUBLANES, body, 0, unroll=2)


def _unpack_call(yt, b, l):
    t, s, nl = yt.shape
    tok = LANES * t
    rows = SUBLANES * t
    y = pl.pallas_call(
        functools.partial(_unpack_kernel, t=t),
        out_shape=jax.ShapeDtypeStruct((nl * t, s), BF16),
        grid=(nl // LANES,),
        in_specs=[pl.BlockSpec((t, s, LANES), lambda j: (0, 0, j)), _const_spec((rows, rows))],
        out_specs=pl.BlockSpec((tok, s), lambda j: (j, 0)),
        scratch_shapes=[pltpu.VMEM((t, LANES, s), F32)],
        compiler_params=_cparams(("arbitrary",)),
        name="unpack",
    )(yt, _chunk_perm(t).T)
    return y.reshape(b, l, s)


POOL_BLOCK_ROWS = 8


def _pool_kernel(u_ref, *refs, tb, pad):
    nwin = len(POOL_WINDOWS)
    band_refs = refs[:nwin]
    inv_ref, pw_ref, ps_ref, o_ref, pad_ref = refs[nwin:]
    grp = pl.program_id(1)
    l = u_ref.shape[1]
    c = u_ref.shape[2]
    nblk = l // tb
    pad_ref[0:pad, :] = jnp.zeros((pad, c), pad_ref.dtype)
    pad_ref[pad + l:pad + l + pad, :] = jnp.zeros((pad, c), pad_ref.dtype)

    def copy(i, carry):
        r0 = pl.multiple_of(i * tb, tb)
        pad_ref[pl.ds(pl.multiple_of(pad + r0, GRID_W), tb), :] = u_ref[0, pl.ds(r0, tb), :]
        return carry

    lax.fori_loop(0, nblk, copy, 0)

    for gi, w in enumerate(POOL_WINDOWS):
        half = w // 2
        span = band_refs[gi].shape[1]

        @pl.when(grp == gi)
        def _(half=half, span=span, band_ref=band_refs[gi]):
            def blk(i, carry):
                r0 = pl.multiple_of(i * tb, tb)
                win = pad_ref[pl.ds(pl.multiple_of(pad + r0 - half * GRID_W, GRID_W), span), :]
                box = jnp.dot(band_ref[...], win, preferred_element_type=F32)
                variant = jnp.where(i == 0, 0, jnp.where(i == nblk - 1, 2, 1))
                inv = inv_ref[0, variant]
                m = box * jnp.concatenate([inv] * (c // LANES), axis=1)
                dlt = (m - u_ref[0, pl.ds(r0, tb), :].astype(F32)).astype(BF16)
                o = jnp.dot(dlt, pw_ref[0], preferred_element_type=F32) * ps_ref[...]
                o_ref[0, pl.ds(r0, tb), :] = o.astype(o_ref.dtype)
                return carry

            lax.fori_loop(0, nblk, blk, 0)


def _pool_tables(rows, tb):
    br = tb // GRID_W
    i = jnp.arange(tb)
    ri, ci = i // GRID_W, i % GRID_W
    bands, invs = [], []
    for w in POOL_WINDOWS:
        half = w // 2
        j = jnp.arange((br + w) * GRID_W)
        dr = (j // GRID_W)[None, :] - ri[:, None]
        dc = (j % GRID_W)[None, :] - ci[:, None]
        bands.append(((dr >= 0) & (dr < w) & (dc >= -half) & (dc < w - half)).astype(BF16))
        cnt_c = jnp.minimum(ci + w - half, GRID_W) - jnp.maximum(ci - half, 0)
        per_variant = []
        for r_first in (0, br, rows - br):
            r = r_first + ri
            cnt_r = jnp.minimum(r + w - half, rows) - jnp.maximum(r - half, 0)
            per_variant.append(1.0 / (cnt_r * cnt_c).astype(F32))
        invs.append(jnp.stack(per_variant))
    inv = jnp.broadcast_to(jnp.stack(invs)[..., None], (len(POOL_WINDOWS), 3, tb, LANES))
    return bands, inv


def _pool_call(u_pool, pool_w, pool_scale):
    b, l, p = u_pool.shape
    ng = len(POOL_WINDOWS)
    c = p // ng
    rows = l // GRID_W
    tb = POOL_BLOCK_ROWS * GRID_W
    pad = (max(POOL_WINDOWS) // 2) * GRID_W
    assert l % tb == 0 and c % LANES == 0 and rows >= 3 * POOL_BLOCK_ROWS and max(POOL_WINDOWS) <= 2 * POOL_BLOCK_ROWS
    bands, inv = _pool_tables(rows, tb)
    return pl.pallas_call(
        functools.partial(_pool_kernel, tb=tb, pad=pad),
        out_shape=jax.ShapeDtypeStruct((b, l, p), BF16),
        grid=(b, ng),
        in_specs=[pl.BlockSpec((1, l, c), lambda i, j: (i, 0, j))]
        + [_const_spec(bd.shape) for bd in bands]
        + [
            pl.BlockSpec((1, 3, tb, LANES), lambda i, j: (j, 0, 0, 0)),
            pl.BlockSpec((1, c, c), lambda i, j: (j, 0, 0)),
            pl.BlockSpec((1, c), lambda i, j: (0, j)),
        ],
        out_specs=pl.BlockSpec((1, l, c), lambda i, j: (i, 0, j)),
        scratch_shapes=[pltpu.VMEM((pad + l + pad, c), BF16)],
        compiler_params=_cparams(("arbitrary", "arbitrary")),
        name="pool",
    )(u_pool, *bands, inv, pool_w.astype(BF16), pool_scale.reshape(1, p))


def _split3(v):
    hi = v.astype(BF16)
    r1 = v - hi.astype(F32)
    mid = r1.astype(BF16)
    lo = (r1 - mid.astype(F32)).astype(BF16)
    return hi, mid, lo


def _dot_hi(a, b):
    a_hi = a.astype(BF16)
    a_lo = (a - a_hi.astype(F32)).astype(BF16)
    b_hi = b.astype(BF16)
    b_lo = (b - b_hi.astype(F32)).astype(BF16)
    return (jnp.dot(a_hi, b_hi, preferred_element_type=F32) + jnp.dot(a_hi, b_lo, preferred_element_type=F32)
            + jnp.dot(a_lo, b_hi, preferred_element_type=F32))


def _s5_kernel(u_ref, pw_ref, bb_ref, cc_ref, ca_ref, cb_ref, rowa_ref, ap_ref, rep_ref, y_ref,
               win_ref, mt_ref, wout_ref, buf_ref, hin_ref, *, gps, hc, **static):
    @pl.when(pl.program_id(0) == 0)
    def _():
        buf_ref[...] = jnp.zeros(buf_ref.shape, F32)

    for gg in range(gps):
        rows = slice(gg * hc, (gg + 1) * hc)
        _s5_group(u_ref.at[:, rows, :], pw_ref.at[gg], bb_ref.at[gg], cc_ref.at[gg], ca_ref.at[gg], cb_ref.at[gg],
                  rowa_ref.at[gg], ap_ref.at[gg], rep_ref, y_ref.at[:, rows, :], win_ref.at[gg], mt_ref.at[gg],
                  wout_ref.at[gg], buf_ref.at[gg], hin_ref.at[gg], hc=hc, **static)


def _s5_group(u_ref, pw_ref, bb_ref, cc_ref, ca_ref, cb_ref, rowa_ref, ap_ref, rep_ref, y_ref,
              win_ref, mt_ref, wout_ref, buf_ref, hin_ref, *, t, hc, nb, ncl, ncc, pad, nsteps, nstate):
    n = nstate
    th = t * hc
    nl = nb * ncl
    rb = ncc + ncl
    ntile = th // LANES

    rep = rep_ref[...]
    prep = sum(jnp.dot(part, rep, preferred_element_type=F32) for part in _split3(pw_ref[...]))
    bt = jnp.concatenate([bb_ref[...]] * ntile, axis=1)
    e = []
    for d in range(2):
        pr, pi = prep[2 * d * n:(2 * d + 1) * n], prep[(2 * d + 1) * n:(2 * d + 2) * n]
        br, bi = bt[2 * d * n:(2 * d + 1) * n], bt[(2 * d + 1) * n:(2 * d + 2) * n]
        e.append((pr * br - pi * bi, pr * bi + pi * br))
    win_ref[0:n, :] = e[0][0].astype(BF16)
    win_ref[n:2 * n, :] = e[1][0].astype(BF16)
    win_ref[2 * n:3 * n, :] = e[0][1].astype(BF16)
    win_ref[3 * n:4 * n, :] = e[1][1].astype(BF16)
    cc = cc_ref[...]
    q_f = _dot_hi(cc[0:hc], jnp.concatenate(e[0], axis=0))
    q_b = _dot_hi(cc[hc:2 * hc], jnp.concatenate(e[1], axis=0))
    zeros = jnp.zeros((hc, th), F32)
    line_b = jnp.concatenate([zeros, q_b], axis=1)
    line_f = jnp.concatenate([q_f, zeros], axis=1)
    per_tile = LANES // hc
    rolled_b = [line_b if r == 0 else pltpu.roll(line_b, hc * r, axis=1) for r in range(per_tile)]
    rolled_f = [line_f if r == per_tile - 1 else pltpu.roll(line_f, 2 * th - hc * (per_tile - 1 - r), axis=1)
                for r in range(per_tile)]
    for ti in range(t):
        a, r = divmod(ti, per_tile)
        row_b = rolled_b[r][:, th - LANES * a:2 * th - LANES * a]
        off_f = LANES * (ntile - 1 - a)
        row_f = rolled_f[r][:, off_f:off_f + th]
        mt_ref[ti * hc:(ti + 1) * hc, :] = (row_b + row_f).astype(BF16)
    ca = ca_ref[...]
    cb = cb_ref[...]
    for ti in range(t):
        ra = rowa_ref[ti:ti + 1, :]
        rbw = jnp.concatenate([ra[:, 2 * n:4 * n], ra[:, 0:2 * n]], axis=1)
        wout_ref[ti * hc:(ti + 1) * hc, :] = (ca * ra + cb * rbw).astype(BF16)

    u = jnp.concatenate([u_ref[ti] for ti in range(t)], axis=0)
    st_all = jnp.dot(win_ref[...], u, preferred_element_type=F32)
    first_half = lax.broadcasted_iota(jnp.int32, (LANES, LANES), 1) < n
    for q in range(4):
        rows_q = st_all[q * n:(q + 1) * n]
        lat = jnp.concatenate([rows_q[:, b * ncl:(b + 1) * ncl] for b in range(nb)], axis=0).T
        cblk = jnp.concatenate([rows_q[:, nl:nl + LANES]] * nb, axis=0).T
        ctx = jnp.where(first_half, cblk, jnp.concatenate([cblk[ncc:], cblk[:ncc]], axis=0))[0:ncc]
        if q % 2 == 0:
            buf_ref[q, pad:pad + ncc, :] = ctx
            buf_ref[q, pad + ncc:pad + rb, :] = lat
        else:
            buf_ref[q, pad:pad + ncl, :] = lat
            buf_ref[q, pad + ncl:pad + rb, :] = ctx
    for k in range(nsteps):
        st = 1 << k
        for d in range(2):
            sgn = -st if d == 0 else st
            ar = ap_ref[4 * k + 2 * d:4 * k + 2 * d + 1, :]
            ai = ap_ref[4 * k + 2 * d + 1:4 * k + 2 * d + 2, :]
            xr = buf_ref[d, pad:pad + rb, :]
            xi = buf_ref[2 + d, pad:pad + rb, :]
            pr = buf_ref[d, pad + sgn:pad + sgn + rb, :]
            pi = buf_ref[2 + d, pad + sgn:pad + sgn + rb, :]
            buf_ref[d, pad:pad + rb, :] = xr + (ar * pr - ai * pi)
            buf_ref[2 + d, pad:pad + rb, :] = xi + (ar * pi + ai * pr)
    for q in range(4):
        lo = pad + ncc - 1 if q % 2 == 0 else pad + 1
        ent = buf_ref[q, lo:lo + ncl, :].T.astype(BF16)
        for b in range(nb):
            hin_ref[q * n:(q + 1) * n, b * ncl:(b + 1) * ncl] = ent[b * n:(b + 1) * n]

    y = jnp.dot(mt_ref[...], u[:, 0:nl], preferred_element_type=F32)
    y = y + jnp.dot(wout_ref[...], hin_ref[...], preferred_element_type=F32)
    for ti in range(t):
        y_ref[ti] = y[ti * hc:(ti + 1) * hc].astype(y_ref.dtype)


def _s5_params(lam_re, lam_im, log_dt, b_re, b_im, c_re, c_im, t, nsteps):
    _, g, n = lam_re.shape
    h = b_re.shape[-1]
    lam_re, lam_im = lam_re.astype(F32), lam_im.astype(F32)
    dt = jnp.exp(log_dt.astype(F32))[..., None]
    lr, li = lam_re * dt, lam_im * dt

    def power(k):
        mag = jnp.exp(lr[..., None] * k)
        return mag * jnp.cos(li[..., None] * k), mag * jnp.sin(li[..., None] * k)

    a_re, a_im = jnp.exp(lr) * jnp.cos(li), jnp.exp(lr) * jnp.sin(li)
    den = lam_re * lam_re + lam_im * lam_im
    f_re = ((a_re - 1.0) * lam_re + a_im * lam_im) / den
    f_im = (a_im * lam_re - (a_re - 1.0) * lam_im) / den
    bb_re = f_re[..., None] * b_re - f_im[..., None] * b_im
    bb_im = f_re[..., None] * b_im + f_im[..., None] * b_re

    ell = jnp.arange(t, dtype=F32)
    pf_re, pf_im = power((t - 1.0) - ell)
    pb_re, pb_im = power(ell)
    pw = jnp.concatenate([pf_re[0], pf_im[0], pb_re[1], pb_im[1]], axis=1)
    bb = jnp.concatenate([bb_re[0], bb_im[0], bb_re[1], bb_im[1]], axis=1)
    bb = jnp.tile(bb, (1, 1, LANES // h))
    cc = jnp.concatenate([jnp.concatenate([c_re[0], -c_im[0]], axis=-1),
                          jnp.concatenate([c_re[1], -c_im[1]], axis=-1)], axis=1)
    ca = jnp.concatenate([c_re[0], c_re[1], -c_re[0], -c_re[1]], axis=-1)
    cb = jnp.concatenate([-c_im[0], -c_im[1], -c_im[0], -c_im[1]], axis=-1)
    tau = jnp.arange(t, dtype=F32)
    rf_re, rf_im = power(tau + 1.0)
    rb_re, rb_im = power(t - tau)
    rowa = jnp.concatenate([rf_re[0], rb_re[1], rf_im[0], rb_im[1]], axis=1)
    rowa = rowa.transpose(0, 2, 1)
    strides = t * (2.0 ** jnp.arange(nsteps, dtype=F32))
    ap_re, ap_im = power(strides)
    ap = jnp.stack([ap_re[0], ap_im[0], ap_re[1], ap_im[1]], axis=-1)
    ap = ap.reshape(g, n, 4 * nsteps).transpose(0, 2, 1)
    ap = jnp.tile(ap, (1, 1, LANES // n))
    rows = -(-4 * nsteps // SUBLANES) * SUBLANES
    ap = jnp.pad(ap, ((0, 0), (0, rows - 4 * nsteps), (0, 0)))
    lane = jnp.arange(t * h)
    rep = (jnp.arange(t)[:, None] == (lane // h)[None, :]).astype(BF16)
    return pw, bb, cc.astype(F32), ca.astype(F32), cb.astype(F32), rowa, ap, rep


def _s5_call(ut, params, t, nb, ncl, ncc, nsteps, n_state, hc):
    pw, bb, cc, ca, cb, rowa, ap, rep = params
    _, sw, width = ut.shape
    nl = nb * ncl
    g = sw // hc
    th = t * hc
    rb = ncc + ncl
    pad = max(SUBLANES, 1 << (nsteps - 1))
    assert nb * n_state == LANES and nb == 2
    gps = S5_GROUPS_PER_STEP
    assert g % gps == 0
    grp = lambda shp: pl.BlockSpec((gps,) + shp, lambda i: (i, 0, 0))
    return pl.pallas_call(
        functools.partial(_s5_kernel, gps=gps, t=t, hc=hc, nb=nb, ncl=ncl, ncc=ncc, pad=pad, nsteps=nsteps,
                          nstate=n_state),
        out_shape=jax.ShapeDtypeStruct((t, sw, nl), BF16),
        grid=(g // gps,),
        in_specs=[
            pl.BlockSpec((t, gps * hc, width), lambda i: (0, i, 0)),
            grp((4 * n_state, t)), grp((4 * n_state, LANES)), grp((2 * hc, 2 * n_state)),
            grp((hc, 4 * n_state)), grp((hc, 4 * n_state)), grp((t, 4 * n_state)),
            grp((ap.shape[1], 2 * n_state)),
            _const_spec((t, th)),
        ],
        out_specs=pl.BlockSpec((t, gps * hc, nl), lambda i: (0, i, 0)),
        scratch_shapes=[pltpu.VMEM((gps, 4 * n_state, th), BF16), pltpu.VMEM((gps, th, th), BF16),
                        pltpu.VMEM((gps, th, 4 * n_state), BF16),
                        pltpu.VMEM((gps, 4, pad + rb + pad, LANES), F32),
                        pltpu.VMEM((gps, 4 * n_state, nl), BF16)],
        compiler_params=_cparams(("arbitrary",)),
        name="s5",
    )(ut, pw, bb, cc, ca, cb, rowa, ap, rep)


def _merge_kernel(y_ref, u_ref, p_ref, zs_ref, x_ref, gate_ref, d_ref, gw_ref, gb_ref, ow_ref, fg_ref, o_ref):
    sw = y_ref.shape[-1]
    pw = p_ref.shape[-1]
    y = _gelu_tanh(y_ref[0].astype(F32) + d_ref[...] * u_ref[0].astype(F32)).astype(BF16)
    yy = jnp.dot(y, gw_ref[...], preferred_element_type=F32) + gb_ref[...]
    ssm_out = yy[:, :sw] * _sigmoid(yy[:, sw:])
    br_pool = p_ref[0] * zs_ref[0, :, 0:pw]
    br_ssm = (ssm_out * zs_ref[0, :, pw:pw + sw].astype(F32)).astype(BF16)
    mix = jnp.dot(br_pool, ow_ref[0:pw, :], preferred_element_type=F32)
    mix = mix + jnp.dot(br_ssm, ow_ref[pw:pw + sw, :], preferred_element_type=F32)
    xo = x_ref[0] + gate_ref[0] * mix
    ms = jnp.mean(xo * xo, axis=-1, keepdims=True)
    o_ref[0] = (xo * lax.rsqrt(ms + EPS) * fg_ref[...]).astype(o_ref.dtype)


def _merge_call(y_ssm, u_ssm, pool_out, zs, x, gate, d_skip, glu_w, glu_b, out_w, final_g, tm):
    b, l, d = x.shape
    sw = y_ssm.shape[-1]
    pw = pool_out.shape[-1]
    mixw = zs.shape[-1]
    tok = lambda wd: pl.BlockSpec((1, tm, wd), lambda i, j: (i, j, 0))
    return pl.pallas_call(
        _merge_kernel,
        out_shape=jax.ShapeDtypeStruct((b, l, d), x.dtype),
        grid=(b, l // tm),
        in_specs=[
            tok(sw), tok(sw), tok(pw), tok(mixw), tok(d),
            pl.BlockSpec((1, 1, d), lambda i, j: (i, 0, 0)),
            _const_spec((1, sw)), _const_spec((sw, 2 * sw)), _const_spec((1, 2 * sw)), _const_spec((mixw, d)),
            _const_spec((1, d)),
        ],
        out_specs=tok(d),
        compiler_params=_cparams(("arbitrary", "arbitrary")),
        name="merge",
    )(y_ssm, u_ssm, pool_out, zs, x, gate, d_skip.reshape(1, sw), glu_w.astype(BF16), glu_b.reshape(1, 2 * sw),
      out_w.astype(BF16), final_g.reshape(1, d))


def kernel(x, c, ctx, c_ctx, ada_w, ada_b, norm_g, in_w, pool_w, pool_scale, s5_lam_re, s5_lam_im, s5_log_dt,
           s5_b_re, s5_b_im, s5_c_re, s5_c_im, s5_d, glu_w, glu_b, out_w, final_g):
    assert ada_w.shape[0] == 1, "single-layer block"
    bsz, seq, d = x.shape
    cl = ctx.shape[1]
    mixw = in_w.shape[2] // 2
    poolw = pool_scale.shape[-1]
    ssmw = s5_d.shape[-1]
    n_grp, n_state = s5_lam_re.shape[2], s5_lam_re.shape[3]
    hc = ssmw // n_grp
    t = CHUNK_T
    assert poolw + ssmw == mixw and seq % t == 0 and cl % t == 0 and 2 * n_state == LANES and LANES % hc == 0

    mod = _ada_call(jnp.concatenate([c, c_ctx[None]], axis=0), ada_w[0], ada_b[0])
    shift, scale, gate = mod[:, :d], mod[:, d:2 * d], mod[:, 2 * d:]
    g1 = norm_g[0].reshape(1, d)
    in_w16 = in_w[0].astype(BF16)

    tm = min(512, seq)
    u_pool, u_ssm, zs = _inproj_call(x, scale[:bsz, None], shift[:bsz, None], g1, in_w16,
                                     (poolw, ssmw, mixw), (False, False, True), tm, "inproj")
    sc_c = jnp.broadcast_to(scale[bsz][None, None], (bsz, 1, d))
    sh_c = jnp.broadcast_to(shift[bsz][None, None], (bsz, 1, d))
    assert poolw % ssmw == 0
    (uc_ssm,) = _inproj_call(ctx, sc_c, sh_c, g1, in_w16, (ssmw,), (False,), min(256, cl), "inproj_ctx",
                             wcol=poolw // ssmw)

    pool_out = _pool_call(u_pool, pool_w[0], pool_scale[0])

    ncl, ncc = seq // t, cl // t
    nsteps = max(1, (ncc + ncl - 1).bit_length())
    params = _s5_params(s5_lam_re[0], s5_lam_im[0], s5_log_dt[0], s5_b_re[0], s5_b_im[0], s5_c_re[0], s5_c_im[0],
                        t, nsteps)
    ut = _pack_call(u_ssm, uc_ssm, t)
    yt = _s5_call(ut, params, t, bsz, ncl, ncc, nsteps, n_state, hc)
    y_ssm = _unpack_call(yt, bsz, seq)

    return _merge_call(y_ssm, u_ssm, pool_out, zs, x, gate[:bsz, None], s5_d[0], glu_w[0], glu_b[0], out_w[0],
                       final_g, tm)
```

```python
import functools
import math

import jax
import jax.numpy as jnp
import numpy as np
from jax import lax
from jax.experimental import pallas as pl
from jax.experimental.pallas import tpu as pltpu

GRID_W = 64
POOL_WINDOWS = (2, 4, 8, 16)
EPS = 1e-6
CHUNK_T = 32
S5_GROUPS_PER_STEP = 2
LANES = 128
SUBLANES = 8
VMEM_LIMIT = 56 * 1024 * 1024

F32 = jnp.float32
BF16 = jnp.bfloat16
TN_DIMS = (((0,), (0,)), ((), ()))


def _cparams(sem):
    return pltpu.CompilerParams(dimension_semantics=sem, vmem_limit_bytes=VMEM_LIMIT)


def _const_spec(shape):
    return pl.BlockSpec(shape, lambda *_: (0,) * len(shape), pipeline_mode=pl.Buffered(1))


def _sigmoid(v):
    return 1.0 / (1.0 + jnp.exp(-v))


def _gelu_tanh(v):
    return 0.5 * v * (1.0 + jnp.tanh(math.sqrt(2.0 / math.pi) * (v + 0.044715 * (v * v * v))))


def _modulated_norm(x, g, scale, shift):
    ms = jnp.mean(x * x, axis=-1, keepdims=True)
    return (x * lax.rsqrt(ms + EPS) * g) * (1.0 + scale) + shift


def _ada_kernel(ct_ref, w_ref, b_ref, o_ref, sb_ref, *, nrows):
    d = w_ref.shape[0]
    tn = w_ref.shape[1]
    ntile = tn // LANES

    @pl.when(pl.program_id(0) == 0)
    def _():
        cv = ct_ref[...]
        s = cv * _sigmoid(cv)
        for r in range(nrows):
            sb_ref[r] = jnp.broadcast_to(s[:, r:r + 1], (d, LANES))

    def body(i, accs):
        k0 = pl.multiple_of(i * SUBLANES, SUBLANES)
        sb = [sb_ref[r, pl.ds(k0, SUBLANES), :] for r in range(nrows)]
        out = []
        for lt in range(ntile):
            w = w_ref[pl.ds(k0, SUBLANES), lt * LANES:(lt + 1) * LANES]
            out.append(tuple(accs[lt][r] + sb[r] * w for r in range(nrows)))
        return tuple(out)

    init = tuple(tuple(jnp.zeros((SUBLANES, LANES), F32) for _ in range(nrows)) for _ in range(ntile))
    accs = lax.fori_loop(0, d // SUBLANES, body, init, unroll=8)
    o_ref[...] = jnp.zeros(o_ref.shape, F32)
    for lt in range(ntile):
        for r in range(nrows):
            o_ref[r:r + 1, lt * LANES:(lt + 1) * LANES] = (jnp.sum(accs[lt][r], axis=0, keepdims=True)
                                                           + b_ref[:, lt * LANES:(lt + 1) * LANES])


def _ada_call(cvecs, ada_w, ada_b):
    nrows, d = cvecs.shape
    n = ada_w.shape[1]
    tn = 768 if n % 768 == 0 else n
    ct = jnp.zeros((d, SUBLANES), F32).at[:, :nrows].set(cvecs.T)
    out = pl.pallas_call(
        functools.partial(_ada_kernel, nrows=nrows),
        out_shape=jax.ShapeDtypeStruct((SUBLANES, n), F32),
        grid=(n // tn,),
        in_specs=[
            pl.BlockSpec((d, SUBLANES), lambda j: (0, 0)),
            pl.BlockSpec((d, tn), lambda j: (0, j)),
            pl.BlockSpec((1, tn), lambda j: (0, j)),
        ],
        out_specs=pl.BlockSpec((SUBLANES, tn), lambda j: (0, j)),
        scratch_shapes=[pltpu.VMEM((nrows, d, LANES), F32)],
        compiler_params=_cparams(("arbitrary",)),
        name="ada",
    )(ct, ada_w, ada_b.reshape(1, n))
    return out[:nrows]


def _inproj_kernel(x_ref, sc_ref, sh_ref, g_ref, w_ref, *o_refs, nchunk, silu):
    h = _modulated_norm(x_ref[0], g_ref[...], sc_ref[0], sh_ref[0]).astype(BF16)
    col = 0
    for o_ref, gate in zip(o_refs, silu):
        width = o_ref.shape[-1]
        for n0 in range(0, width, nchunk):
            acc = jnp.dot(h, w_ref[:, col + n0:col + n0 + nchunk], preferred_element_type=F32)
            if gate:
                acc = acc * _sigmoid(acc)
            o_ref[0, :, n0:n0 + nchunk] = acc.astype(o_ref.dtype)
        col += width


def _inproj_call(x, scale, shift, g, w, widths, silu, tm, name, wcol=0):
    b, l, d = x.shape
    nout = sum(widths)
    assert w.shape[1] % nout == 0 and l % tm == 0
    nchunk = min(512, min(widths))
    assert all(wd % nchunk == 0 for wd in widths)
    return pl.pallas_call(
        functools.partial(_inproj_kernel, nchunk=nchunk, silu=silu),
        out_shape=[jax.ShapeDtypeStruct((b, l, wd), BF16) for wd in widths],
        grid=(b, l // tm),
        in_specs=[
            pl.BlockSpec((1, tm, d), lambda i, j: (i, j, 0)),
            pl.BlockSpec((1, 1, d), lambda i, j: (i, 0, 0)),
            pl.BlockSpec((1, 1, d), lambda i, j: (i, 0, 0)),
            _const_spec((1, d)),
            pl.BlockSpec((d, nout), lambda i, j: (0, wcol), pipeline_mode=pl.Buffered(1)),
        ],
        out_specs=[pl.BlockSpec((1, tm, wd), lambda i, j: (i, j, 0)) for wd in widths],
        compiler_params=_cparams(("arbitrary", "arbitrary")),
        name=name,
    )(x, scale, shift, g, w)


def _chunk_perm(t):
    i = np.arange(SUBLANES * t)
    src = (i % SUBLANES) * t + i // SUBLANES
    return jnp.asarray(src[:, None] == i[None, :], BF16)


def _pack_kernel(u_ref, uc_ref, perm_ref, ut_ref, scr_ref, *, t, nblk):
    j = pl.program_id(0)
    s = scr_ref.shape[2]
    rows = SUBLANES * t
    ctok = uc_ref.shape[0]

    def permute(src, blk):
        pm = jnp.dot(perm_ref[...], src, preferred_element_type=F32)
        for ti in range(t):
            scr_ref[ti, pl.ds(pl.multiple_of(blk * SUBLANES, SUBLANES), SUBLANES), :] = pm[ti * SUBLANES:(ti + 1) * SUBLANES]

    @pl.when(j < nblk)
    def _():
        def body(blk, carry):
            permute(u_ref[pl.ds(pl.multiple_of(blk * rows, rows), rows), :], blk)
            return carry

        lax.fori_loop(0, LANES // SUBLANES, body, 0, unroll=2)

    @pl.when(j == nblk)
    def _():
        for blk in range(ctok // rows):
            permute(uc_ref[blk * rows:(blk + 1) * rows, :], blk)
        scr_ref[:, ctok // t:LANES, :] = jnp.zeros((t, LANES - ctok // t, s), F32)

    def transpose(ti, carry):
        for k in range(s // LANES):
            ut_ref[ti, LANES * k:LANES * (k + 1), :] = scr_ref[ti, :, LANES * k:LANES * (k + 1)].T.astype(BF16)
        return carry

    lax.fori_loop(0, t, transpose, 0, unroll=2)


def _pack_call(u_ssm, uc_ssm, t):
    b, l, s = u_ssm.shape
    tok = LANES * t
    rows = SUBLANES * t
    ntok = b * l
    ctok = b * uc_ssm.shape[1]
    assert ntok % tok == 0 and ctok <= tok and ctok % rows == 0 and s % LANES == 0
    nblk = ntok // tok
    return pl.pallas_call(
        functools.partial(_pack_kernel, t=t, nblk=nblk),
        out_shape=jax.ShapeDtypeStruct((t, s, ntok // t + LANES), BF16),
        grid=(nblk + 1,),
        in_specs=[pl.BlockSpec((tok, s), lambda j: (jnp.minimum(j, nblk - 1), 0)),
                  _const_spec((ctok, s)), _const_spec((rows, rows))],
        out_specs=pl.BlockSpec((t, s, LANES), lambda j: (0, 0, j)),
        scratch_shapes=[pltpu.VMEM((t, LANES, s), F32)],
        compiler_params=_cparams(("arbitrary",)),
        name="pack",
    )(u_ssm.reshape(ntok, s), uc_ssm.reshape(ctok, s), _chunk_perm(t))


def _unpack_kernel(yt_ref, perm_ref, y_ref, scr_ref, *, t):
    s = scr_ref.shape[2]
    rows = SUBLANES * t

    def transpose(ti, carry):
        for k in range(s // LANES):
            scr_ref[ti, :, LANES * k:LANES * (k + 1)] = yt_ref[ti, LANES * k:LANES * (k + 1), :].astype(F32).T
        return carry

    lax.fori_loop(0, t, transpose, 0, unroll=2)

    def body(blk, carry):
        c0 = pl.multiple_of(blk * SUBLANES, SUBLANES)
        src = jnp.concatenate([scr_ref[ti, pl.ds(c0, SUBLANES), :] for ti in range(t)], axis=0).astype(BF16)
        out = jnp.dot(perm_ref[...], src, preferred_element_type=F32)
        y_ref[pl.ds(pl.multiple_of(blk * rows, rows), rows), :] = out.astype(y_ref.dtype)
        return carry

    lax.fori_loop(0, LANES // SUBLANES, body, 0, unroll=2)


def _unpack_call(yt, b, l):
    t, s, nl = yt.shape
    tok = LANES * t
    rows = SUBLANES * t
    y = pl.pallas_call(
        functools.partial(_unpack_kernel, t=t),
        out_shape=jax.ShapeDtypeStruct((nl * t, s), BF16),
        grid=(nl // LANES,),
        in_specs=[pl.BlockSpec((t, s, LANES), lambda j: (0, 0, j)), _const_spec((rows, rows))],
        out_specs=pl.BlockSpec((tok, s), lambda j: (j, 0)),
        scratch_shapes=[pltpu.VMEM((t, LANES, s), F32)],
        compiler_params=_cparams(("arbitrary",)),
        name="unpack",
    )(yt, _chunk_perm(t).T)
    return y.reshape(b, l, s)


POOL_BLOCK_ROWS = 8


def _pool_kernel(u_ref, *refs, tb, pad):
    nwin = len(POOL_WINDOWS)
    band_refs = refs[:nwin]
    inv_ref, pw_ref, ps_ref, o_ref, pad_ref = refs[nwin:]
    grp = pl.program_id(1)
    l = u_ref.shape[1]
    c = u_ref.shape[2]
    nblk = l // tb
    pad_ref[0:pad, :] = jnp.zeros((pad, c), pad_ref.dtype)
    pad_ref[pad + l:pad + l + pad, :] = jnp.zeros((pad, c), pad_ref.dtype)

    def copy(i, carry):
        r0 = pl.multiple_of(i * tb, tb)
        pad_ref[pl.ds(pl.multiple_of(pad + r0, GRID_W), tb), :] = u_ref[0, pl.ds(r0, tb), :]
        return carry

    lax.fori_loop(0, nblk, copy, 0)

    for gi, w in enumerate(POOL_WINDOWS):
        half = w // 2
        span = band_refs[gi].shape[1]

        @pl.when(grp == gi)
        def _(half=half, span=span, band_ref=band_refs[gi]):
            def blk(i, carry):
                r0 = pl.multiple_of(i * tb, tb)
                win = pad_ref[pl.ds(pl.multiple_of(pad + r0 - half * GRID_W, GRID_W), span), :]
                box = jnp.dot(band_ref[...], win, preferred_element_type=F32)
                variant = jnp.where(i == 0, 0, jnp.where(i == nblk - 1, 2, 1))
                inv = inv_ref[0, variant]
                m = box * jnp.concatenate([inv] * (c // LANES), axis=1)
                dlt = (m - u_ref[0, pl.ds(r0, tb), :].astype(F32)).astype(BF16)
                o = jnp.dot(dlt, pw_ref[0], preferred_element_type=F32) * ps_ref[...]
                o_ref[0, pl.ds(r0, tb), :] = o.astype(o_ref.dtype)
                return carry

            lax.fori_loop(0, nblk, blk, 0)


def _pool_tables(rows, tb):
    br = tb // GRID_W
    i = np.arange(tb)
    ri, ci = i // GRID_W, i % GRID_W
    bands, invs = [], []
    for w in POOL_WINDOWS:
        half = w // 2
        j = np.arange((br + w) * GRID_W)
        dr = (j // GRID_W)[None, :] - ri[:, None]
        dc = (j % GRID_W)[None, :] - ci[:, None]
        bands.append(jnp.asarray((dr >= 0) & (dr < w) & (dc >= -half) & (dc < w - half), BF16))
        cnt_c = np.minimum(ci + w - half, GRID_W) - np.maximum(ci - half, 0)
        per_variant = []
        for r_first in (0, br, rows - br):
            r = r_first + ri
            cnt_r = np.minimum(r + w - half, rows) - np.maximum(r - half, 0)
            per_variant.append(np.float32(1.0) / (cnt_r * cnt_c).astype(np.float32))
        invs.append(np.stack(per_variant))
    inv = np.broadcast_to(np.stack(invs)[..., None], (len(POOL_WINDOWS), 3, tb, LANES))
    return bands, jnp.asarray(inv, F32)


def _pool_call(u_pool, pool_w, pool_scale):
    b, l, p = u_pool.shape
    ng = len(POOL_WINDOWS)
    c = p // ng
    rows = l // GRID_W
    tb = POOL_BLOCK_ROWS * GRID_W
    pad = (max(POOL_WINDOWS) // 2) * GRID_W
    assert l % tb == 0 and c % LANES == 0 and rows >= 3 * POOL_BLOCK_ROWS and max(POOL_WINDOWS) <= 2 * POOL_BLOCK_ROWS
    bands, inv = _pool_tables(rows, tb)
    return pl.pallas_call(
        functools.partial(_pool_kernel, tb=tb, pad=pad),
        out_shape=jax.ShapeDtypeStruct((b, l, p), BF16),
        grid=(b, ng),
        in_specs=[pl.BlockSpec((1, l, c), lambda i, j: (i, 0, j))]
        + [_const_spec(bd.shape) for bd in bands]
        + [
            pl.BlockSpec((1, 3, tb, LANES), lambda i, j: (j, 0, 0, 0)),
            pl.BlockSpec((1, c, c), lambda i, j: (j, 0, 0)),
            pl.BlockSpec((1, c), lambda i, j: (0, j)),
        ],
        out_specs=pl.BlockSpec((1, l, c), lambda i, j: (i, 0, j)),
        scratch_shapes=[pltpu.VMEM((pad + l + pad, c), BF16)],
        compiler_params=_cparams(("arbitrary", "arbitrary")),
        name="pool",
    )(u_pool, *bands, inv, pool_w.astype(BF16), pool_scale.reshape(1, p))


def _split3(v):
    hi = v.astype(BF16)
    r1 = v - hi.astype(F32)
    mid = r1.astype(BF16)
    lo = (r1 - mid.astype(F32)).astype(BF16)
    return hi, mid, lo


def _dot_hi(a, b):
    a_hi = a.astype(BF16)
    a_lo = (a - a_hi.astype(F32)).astype(BF16)
    b_hi = b.astype(BF16)
    b_lo = (b - b_hi.astype(F32)).astype(BF16)
    return (jnp.dot(a_hi, b_hi, preferred_element_type=F32) + jnp.dot(a_hi, b_lo, preferred_element_type=F32)
            + jnp.dot(a_lo, b_hi, preferred_element_type=F32))


def _s5_kernel(u_ref, pw_ref, bb_ref, cc_ref, ca_ref, cb_ref, rowa_ref, ap_ref, rep_ref, y_ref,
               win_ref, mt_ref, wout_ref, buf_ref, hin_ref, *, gps, hc, **static):
    @pl.when(pl.program_id(0) == 0)
    def _():
        buf_ref[...] = jnp.zeros(buf_ref.shape, F32)

    for gg in range(gps):
        rows = slice(gg * hc, (gg + 1) * hc)
        _s5_group(u_ref.at[:, rows, :], pw_ref.at[gg], bb_ref.at[gg], cc_ref.at[gg], ca_ref.at[gg], cb_ref.at[gg],
                  rowa_ref.at[gg], ap_ref.at[gg], rep_ref, y_ref.at[:, rows, :], win_ref.at[gg], mt_ref.at[gg],
                  wout_ref.at[gg], buf_ref.at[gg], hin_ref.at[gg], hc=hc, **static)


def _s5_group(u_ref, pw_ref, bb_ref, cc_ref, ca_ref, cb_ref, rowa_ref, ap_ref, rep_ref, y_ref,
              win_ref, mt_ref, wout_ref, buf_ref, hin_ref, *, t, hc, nb, ncl, ncc, pad, nsteps, nstate):
    n = nstate
    th = t * hc
    nl = nb * ncl
    rb = ncc + ncl
    ntile = th // LANES

    rep = rep_ref[...]
    prep = sum(lax.dot_general(part, rep, TN_DIMS, preferred_element_type=F32) for part in _split3(pw_ref[...]))
    bt = jnp.concatenate([bb_ref[...]] * ntile, axis=1)
    e = []
    for d in range(2):
        pr, pi = prep[2 * d * n:(2 * d + 1) * n], prep[(2 * d + 1) * n:(2 * d + 2) * n]
        br, bi = bt[2 * d * n:(2 * d + 1) * n], bt[(2 * d + 1) * n:(2 * d + 2) * n]
        e.append((pr * br - pi * bi, pr * bi + pi * br))
    win_ref[0:n, :] = e[0][0].astype(BF16)
    win_ref[n:2 * n, :] = e[1][0].astype(BF16)
    win_ref[2 * n:3 * n, :] = e[0][1].astype(BF16)
    win_ref[3 * n:4 * n, :] = e[1][1].astype(BF16)
    cc = cc_ref[...]
    q_f = _dot_hi(cc[0:hc], jnp.concatenate(e[0], axis=0))
    q_b = _dot_hi(cc[hc:2 * hc], jnp.concatenate(e[1], axis=0))
    zeros = jnp.zeros((hc, th), F32)
    line_b = jnp.concatenate([zeros, q_b], axis=1)
    line_f = jnp.concatenate([q_f, zeros], axis=1)
    per_tile = LANES // hc
    rolled_b = [line_b if r == 0 else pltpu.roll(line_b, hc * r, axis=1) for r in range(per_tile)]
    rolled_f = [line_f if r == per_tile - 1 else pltpu.roll(line_f, 2 * th - hc * (per_tile - 1 - r), axis=1)
                for r in range(per_tile)]
    for ti in range(t):
        a, r = divmod(ti, per_tile)
        row_b = rolled_b[r][:, th - LANES * a:2 * th - LANES * a]
        off_f = LANES * (ntile - 1 - a)
        row_f = rolled_f[r][:, off_f:off_f + th]
        mt_ref[ti * hc:(ti + 1) * hc, :] = (row_b + row_f).astype(BF16)
    ca = ca_ref[...]
    cb = cb_ref[...]
    for ti in range(t):
        ra = rowa_ref[ti:ti + 1, :]
        rbw = jnp.concatenate([ra[:, 2 * n:4 * n], ra[:, 0:2 * n]], axis=1)
        wout_ref[ti * hc:(ti + 1) * hc, :] = (ca * ra + cb * rbw).astype(BF16)

    u = jnp.concatenate([u_ref[ti] for ti in range(t)], axis=0)
    st_all = jnp.dot(win_ref[...], u, preferred_element_type=F32)
    first_half = lax.broadcasted_iota(jnp.int32, (LANES, LANES), 1) < n
    for q in range(4):
        rows_q = st_all[q * n:(q + 1) * n]
        lat = jnp.concatenate([rows_q[:, b * ncl:(b + 1) * ncl] for b in range(nb)], axis=0).T
        cblk = jnp.concatenate([rows_q[:, nl:nl + LANES]] * nb, axis=0).T
        ctx = jnp.where(first_half, cblk, jnp.concatenate([cblk[ncc:], cblk[:ncc]], axis=0))[0:ncc]
        if q % 2 == 0:
            buf_ref[q, pad:pad + ncc, :] = ctx
            buf_ref[q, pad + ncc:pad + rb, :] = lat
        else:
            buf_ref[q, pad:pad + ncl, :] = lat
            buf_ref[q, pad + ncl:pad + rb, :] = ctx
    for k in range(nsteps):
        st = 1 << k
        for d in range(2):
            sgn = -st if d == 0 else st
            ar = ap_ref[4 * k + 2 * d:4 * k + 2 * d + 1, :]
            ai = ap_ref[4 * k + 2 * d + 1:4 * k + 2 * d + 2, :]
            xr = buf_ref[d, pad:pad + rb, :]
            xi = buf_ref[2 + d, pad:pad + rb, :]
            pr = buf_ref[d, pad + sgn:pad + sgn + rb, :]
            pi = buf_ref[2 + d, pad + sgn:pad + sgn + rb, :]
            buf_ref[d, pad:pad + rb, :] = xr + (ar * pr - ai * pi)
            buf_ref[2 + d, pad:pad + rb, :] = xi + (ar * pi + ai * pr)
    for q in range(4):
        lo = pad + ncc - 1 if q % 2 == 0 else pad + 1
        ent = buf_ref[q, lo:lo + ncl, :].T.astype(BF16)
        for b in range(nb):
            hin_ref[q * n:(q + 1) * n, b * ncl:(b + 1) * ncl] = ent[b * n:(b + 1) * n]

    y = jnp.dot(mt_ref[...], u[:, 0:nl], preferred_element_type=F32)
    y = y + jnp.dot(wout_ref[...], hin_ref[...], preferred_element_type=F32)
    for ti in range(t):
        y_ref[ti] = y[ti * hc:(ti + 1) * hc].astype(y_ref.dtype)


def _s5_params(lam_re, lam_im, log_dt, b_re, b_im, c_re, c_im, t, nsteps):
    _, g, n = lam_re.shape
    h = b_re.shape[-1]
    lam_re, lam_im = lam_re.astype(F32), lam_im.astype(F32)
    dt = jnp.exp(log_dt.astype(F32))[..., None]
    lr, li = lam_re * dt, lam_im * dt

    def power(d, k):
        kk = k[None, :, None]
        mag = jnp.exp(lr[d][:, None, :] * kk)
        return mag * jnp.cos(li[d][:, None, :] * kk), mag * jnp.sin(li[d][:, None, :] * kk)

    a_re, a_im = jnp.exp(lr) * jnp.cos(li), jnp.exp(lr) * jnp.sin(li)
    den = lam_re * lam_re + lam_im * lam_im
    f_re = ((a_re - 1.0) * lam_re + a_im * lam_im) / den
    f_im = (a_im * lam_re - (a_re - 1.0) * lam_im) / den
    bb_re = f_re[..., None] * b_re - f_im[..., None] * b_im
    bb_im = f_re[..., None] * b_im + f_im[..., None] * b_re
    bb = jnp.concatenate([bb_re[0], bb_im[0], bb_re[1], bb_im[1]], axis=1)
    til = jnp.asarray(np.arange(h)[:, None] == (np.arange(LANES) % h)[None, :], F32)
    bb = jnp.einsum('grh,hl->grl', bb, til, precision=lax.Precision.HIGHEST)

    ell = jnp.arange(t, dtype=F32)
    pf_re, pf_im = power(0, (t - 1.0) - ell)
    pb_re, pb_im = power(1, ell)
    pw = jnp.concatenate([pf_re, pf_im, pb_re, pb_im], axis=-1)
    cc = jnp.concatenate([jnp.concatenate([c_re[0], -c_im[0]], axis=-1),
                          jnp.concatenate([c_re[1], -c_im[1]], axis=-1)], axis=1)
    ca = jnp.concatenate([c_re[0], c_re[1], -c_re[0], -c_re[1]], axis=-1)
    cb = jnp.concatenate([-c_im[0], -c_im[1], -c_im[0], -c_im[1]], axis=-1)
    rf_re, rf_im = power(0, ell + 1.0)
    rb_re, rb_im = power(1, t - ell)
    rowa = jnp.concatenate([rf_re, rb_re, rf_im, rb_im], axis=-1)
    strides = t * (2.0 ** jnp.arange(nsteps, dtype=F32))
    apf_re, apf_im = power(0, strides)
    apb_re, apb_im = power(1, strides)
    ap = jnp.stack([apf_re, apf_im, apb_re, apb_im], axis=2).reshape(g, 4 * nsteps, n)
    ap = jnp.concatenate([ap] * (LANES // n), axis=-1)
    rows = -(-4 * nsteps // SUBLANES) * SUBLANES
    ap = jnp.pad(ap, ((0, 0), (0, rows - 4 * nsteps), (0, 0)))
    rep = jnp.asarray(np.arange(t)[:, None] == (np.arange(t * h) // h)[None, :], BF16)
    return pw, bb, cc.astype(F32), ca.astype(F32), cb.astype(F32), rowa, ap, rep


def _s5_call(ut, params, t, nb, ncl, ncc, nsteps, n_state, hc):
    pw, bb, cc, ca, cb, rowa, ap, rep = params
    _, sw, width = ut.shape
    nl = nb * ncl
    g = sw // hc
    th = t * hc
    rb = ncc + ncl
    pad = max(SUBLANES, 1 << (nsteps - 1))
    assert nb * n_state == LANES and nb == 2
    gps = S5_GROUPS_PER_STEP
    assert g % gps == 0
    grp = lambda shp: pl.BlockSpec((gps,) + shp, lambda i: (i, 0, 0))
    return pl.pallas_call(
        functools.partial(_s5_kernel, gps=gps, t=t, hc=hc, nb=nb, ncl=ncl, ncc=ncc, pad=pad, nsteps=nsteps,
                          nstate=n_state),
        out_shape=jax.ShapeDtypeStruct((t, sw, nl), BF16),
        grid=(g // gps,),
        in_specs=[
            pl.BlockSpec((t, gps * hc, width), lambda i: (0, i, 0)),
            grp((t, 4 * n_state)), grp((4 * n_state, LANES)), grp((2 * hc, 2 * n_state)),
            grp((hc, 4 * n_state)), grp((hc, 4 * n_state)), grp((t, 4 * n_state)),
            grp((ap.shape[1], 2 * n_state)),
            _const_spec((t, th)),
        ],
        out_specs=pl.BlockSpec((t, gps * hc, nl), lambda i: (0, i, 0)),
        scratch_shapes=[pltpu.VMEM((gps, 4 * n_state, th), BF16), pltpu.VMEM((gps, th, th), BF16),
                        pltpu.VMEM((gps, th, 4 * n_state), BF16),
                        pltpu.VMEM((gps, 4, pad + rb + pad, LANES), F32),
                        pltpu.VMEM((gps, 4 * n_state, nl), BF16)],
        compiler_params=_cparams(("arbitrary",)),
        name="s5",
    )(ut, pw, bb, cc, ca, cb, rowa, ap, rep)


def _merge_kernel(y_ref, u_ref, p_ref, zs_ref, x_ref, gate_ref, d_ref, gw_ref, gb_ref, ow_ref, fg_ref, o_ref):
    sw = y_ref.shape[-1]
    pw = p_ref.shape[-1]
    y = _gelu_tanh(y_ref[0].astype(F32) + d_ref[...] * u_ref[0].astype(F32)).astype(BF16)
    yy = jnp.dot(y, gw_ref[...], preferred_element_type=F32) + gb_ref[...]
    ssm_out = yy[:, :sw] * _sigmoid(yy[:, sw:])
    br_pool = p_ref[0] * zs_ref[0, :, 0:pw]
    br_ssm = (ssm_out * zs_ref[0, :, pw:pw + sw].astype(F32)).astype(BF16)
    mix = jnp.dot(br_pool, ow_ref[0:pw, :], preferred_element_type=F32)
    mix = mix + jnp.dot(br_ssm, ow_ref[pw:pw + sw, :], preferred_element_type=F32)
    xo = x_ref[0] + gate_ref[0] * mix
    ms = jnp.mean(xo * xo, axis=-1, keepdims=True)
    o_ref[0] = (xo * lax.rsqrt(ms + EPS) * fg_ref[...]).astype(o_ref.dtype)


def _merge_call(y_ssm, u_ssm, pool_out, zs, x, gate, d_skip, glu_w, glu_b, out_w, final_g, tm):
    b, l, d = x.shape
    sw = y_ssm.shape[-1]
    pw = pool_out.shape[-1]
    mixw = zs.shape[-1]
    tok = lambda wd: pl.BlockSpec((1, tm, wd), lambda i, j: (i, j, 0))
    return pl.pallas_call(
        _merge_kernel,
        out_shape=jax.ShapeDtypeStruct((b, l, d), x.dtype),
        grid=(b, l // tm),
        in_specs=[
            tok(sw), tok(sw), tok(pw), tok(mixw), tok(d),
            pl.BlockSpec((1, 1, d), lambda i, j: (i, 0, 0)),
            _const_spec((1, sw)), _const_spec((sw, 2 * sw)), _const_spec((1, 2 * sw)), _const_spec((mixw, d)),
            _const_spec((1, d)),
        ],
        out_specs=tok(d),
        compiler_params=_cparams(("arbitrary", "arbitrary")),
        name="merge",
    )(y_ssm, u_ssm, pool_out, zs, x, gate, d_skip.reshape(1, sw), glu_w.astype(BF16), glu_b.reshape(1, 2 * sw),
      out_w.astype(BF16), final_g.reshape(1, d))


def kernel(x, c, ctx, c_ctx, ada_w, ada_b, norm_g, in_w, pool_w, pool_scale, s5_lam_re, s5_lam_im, s5_log_dt,
           s5_b_re, s5_b_im, s5_c_re, s5_c_im, s5_d, glu_w, glu_b, out_w, final_g):
    assert ada_w.shape[0] == 1, "single-layer block"
    bsz, seq, d = x.shape
    cl = ctx.shape[1]
    mixw = in_w.shape[2] // 2
    poolw = pool_scale.shape[-1]
    ssmw = s5_d.shape[-1]
    n_grp, n_state = s5_lam_re.shape[2], s5_lam_re.shape[3]
    hc = ssmw // n_grp
    t = CHUNK_T
    assert poolw + ssmw == mixw and seq % t == 0 and cl % t == 0 and 2 * n_state == LANES and LANES % hc == 0

    mod = _ada_call(jnp.concatenate([c, c_ctx[None]], axis=0), ada_w[0], ada_b[0])
    shift, scale, gate = mod[:, :d], mod[:, d:2 * d], mod[:, 2 * d:]
    g1 = norm_g[0].reshape(1, d)
    in_w16 = in_w[0].astype(BF16)

    tm = min(512, seq)
    u_pool, u_ssm, zs = _inproj_call(x, scale[:bsz, None], shift[:bsz, None], g1, in_w16,
                                     (poolw, ssmw, mixw), (False, False, True), min(1024, seq), "inproj")
    sc_c = jnp.broadcast_to(scale[bsz][None, None], (bsz, 1, d))
    sh_c = jnp.broadcast_to(shift[bsz][None, None], (bsz, 1, d))
    assert poolw % ssmw == 0
    (uc_ssm,) = _inproj_call(ctx, sc_c, sh_c, g1, in_w16, (ssmw,), (False,), min(256, cl), "inproj_ctx",
                             wcol=poolw // ssmw)

    pool_out = _pool_call(u_pool, pool_w[0], pool_scale[0])

    ncl, ncc = seq // t, cl // t
    nsteps = max(1, (ncc + ncl - 1).bit_length())
    params = _s5_params(s5_lam_re[0], s5_lam_im[0], s5_log_dt[0], s5_b_re[0], s5_b_im[0], s5_c_re[0], s5_c_im[0],
                        t, nsteps)
    ut = _pack_call(u_ssm, uc_ssm, t)
    yt = _s5_call(ut, params, t, bsz, ncl, ncc, nsteps, n_state, hc)
    y_ssm = _unpack_call(yt, bsz, seq)

    return _merge_call(y_ssm, u_ssm, pool_out, zs, x, gate[:bsz, None], s5_d[0], glu_w[0], glu_b[0], out_w[0],
                       final_g, tm)
```

```python
import functools
import math

import jax
import jax.numpy as jnp
import numpy as np
from jax import lax
from jax.experimental import pallas as pl
from jax.experimental.pallas import tpu as pltpu

GRID_W = 64
POOL_WINDOWS = (2, 4, 8, 16)
EPS = 1e-6
CHUNK_T = 32
S5_GROUPS_PER_STEP = 2
LANES = 128
SUBLANES = 8
VMEM_LIMIT = 56 * 1024 * 1024

F32 = jnp.float32
BF16 = jnp.bfloat16
TN_DIMS = (((0,), (0,)), ((), ()))
NT_DIMS = (((1,), (1,)), ((), ()))


def _cparams(sem):
    return pltpu.CompilerParams(dimension_semantics=sem, vmem_limit_bytes=VMEM_LIMIT)


def _const_spec(shape):
    return pl.BlockSpec(shape, lambda *_: (0,) * len(shape), pipeline_mode=pl.Buffered(1))


def _sigmoid(v):
    return 1.0 / (1.0 + jnp.exp(-v))


def _gelu_tanh(v):
    return 0.5 * v * (1.0 + jnp.tanh(math.sqrt(2.0 / math.pi) * (v + 0.044715 * (v * v * v))))


def _modulated_norm(x, g, scale, shift):
    ms = jnp.mean(x * x, axis=-1, keepdims=True)
    return (x * lax.rsqrt(ms + EPS) * g) * (1.0 + scale) + shift


def _ada_kernel(ct_ref, w_ref, b_ref, o_ref, sb_ref, *, nrows):
    d = w_ref.shape[0]
    tn = w_ref.shape[1]
    ntile = tn // LANES

    @pl.when(pl.program_id(0) == 0)
    def _():
        cv = ct_ref[...]
        s = cv * _sigmoid(cv)
        for r in range(nrows):
            sb_ref[r] = jnp.broadcast_to(s[:, r:r + 1], (d, LANES))

    def body(i, accs):
        k0 = pl.multiple_of(i * SUBLANES, SUBLANES)
        sb = [sb_ref[r, pl.ds(k0, SUBLANES), :] for r in range(nrows)]
        out = []
        for lt in range(ntile):
            w = w_ref[pl.ds(k0, SUBLANES), lt * LANES:(lt + 1) * LANES]
            out.append(tuple(accs[lt][r] + sb[r] * w for r in range(nrows)))
        return tuple(out)

    init = tuple(tuple(jnp.zeros((SUBLANES, LANES), F32) for _ in range(nrows)) for _ in range(ntile))
    accs = lax.fori_loop(0, d // SUBLANES, body, init, unroll=8)
    o_ref[...] = jnp.zeros(o_ref.shape, F32)
    for lt in range(ntile):
        for r in range(nrows):
            o_ref[r:r + 1, lt * LANES:(lt + 1) * LANES] = (jnp.sum(accs[lt][r], axis=0, keepdims=True)
                                                           + b_ref[:, lt * LANES:(lt + 1) * LANES])


def _ada_call(cvecs, ada_w, ada_b):
    nrows, d = cvecs.shape
    n = ada_w.shape[1]
    tn = 768 if n % 768 == 0 else n
    ct = jnp.zeros((d, SUBLANES), F32).at[:, :nrows].set(cvecs.T)
    out = pl.pallas_call(
        functools.partial(_ada_kernel, nrows=nrows),
        out_shape=jax.ShapeDtypeStruct((SUBLANES, n), F32),
        grid=(n // tn,),
        in_specs=[
            pl.BlockSpec((d, SUBLANES), lambda j: (0, 0)),
            pl.BlockSpec((d, tn), lambda j: (0, j)),
            pl.BlockSpec((1, tn), lambda j: (0, j)),
        ],
        out_specs=pl.BlockSpec((SUBLANES, tn), lambda j: (0, j)),
        scratch_shapes=[pltpu.VMEM((nrows, d, LANES), F32)],
        compiler_params=_cparams(("arbitrary",)),
        name="ada",
    )(ct, ada_w, ada_b.reshape(1, n))
    return out[:nrows]


def _inproj_kernel(x_ref, sc_ref, sh_ref, g_ref, w_ref, *o_refs, nchunk, silu):
    h = _modulated_norm(x_ref[0], g_ref[...], sc_ref[0], sh_ref[0]).astype(BF16)
    col = 0
    for o_ref, gate in zip(o_refs, silu):
        width = o_ref.shape[-1]
        for n0 in range(0, width, nchunk):
            acc = jnp.dot(h, w_ref[:, col + n0:col + n0 + nchunk], preferred_element_type=F32)
            if gate:
                acc = acc * _sigmoid(acc)
            o_ref[0, :, n0:n0 + nchunk] = acc.astype(o_ref.dtype)
        col += width


def _inproj_call(x, scale, shift, g, w, widths, silu, tm, name, wcol=0):
    b, l, d = x.shape
    nout = sum(widths)
    assert w.shape[1] % nout == 0 and l % tm == 0
    nchunk = min(512, min(widths))
    assert all(wd % nchunk == 0 for wd in widths)
    return pl.pallas_call(
        functools.partial(_inproj_kernel, nchunk=nchunk, silu=silu),
        out_shape=[jax.ShapeDtypeStruct((b, l, wd), BF16) for wd in widths],
        grid=(b, l // tm),
        in_specs=[
            pl.BlockSpec((1, tm, d), lambda i, j: (i, j, 0)),
            pl.BlockSpec((1, 1, d), lambda i, j: (i, 0, 0)),
            pl.BlockSpec((1, 1, d), lambda i, j: (i, 0, 0)),
            _const_spec((1, d)),
            pl.BlockSpec((d, nout), lambda i, j: (0, wcol), pipeline_mode=pl.Buffered(1)),
        ],
        out_specs=[pl.BlockSpec((1, tm, wd), lambda i, j: (i, j, 0)) for wd in widths],
        compiler_params=_cparams(("arbitrary", "arbitrary")),
        name=name,
    )(x, scale, shift, g, w)


def _chunk_perm(t):
    i = np.arange(SUBLANES * t)
    src = (i % SUBLANES) * t + i // SUBLANES
    return jnp.asarray(src[:, None] == i[None, :], BF16)


def _pack_kernel(u_ref, uc_ref, perm_ref, ut_ref, scr_ref, *, t, nblk):
    j = pl.program_id(0)
    s = scr_ref.shape[2]
    rows = SUBLANES * t
    ctok = uc_ref.shape[0]

    def permute(src, blk):
        pm = jnp.dot(perm_ref[...], src, preferred_element_type=F32)
        for ti in range(t):
            scr_ref[ti, pl.ds(pl.multiple_of(blk * SUBLANES, SUBLANES), SUBLANES), :] = pm[ti * SUBLANES:(ti + 1) * SUBLANES]

    @pl.when(j < nblk)
    def _():
        def body(blk, carry):
            permute(u_ref[pl.ds(pl.multiple_of(blk * rows, rows), rows), :], blk)
            return carry

        lax.fori_loop(0, LANES // SUBLANES, body, 0, unroll=2)

    @pl.when(j == nblk)
    def _():
        for blk in range(ctok // rows):
            permute(uc_ref[blk * rows:(blk + 1) * rows, :], blk)
        scr_ref[:, ctok // t:LANES, :] = jnp.zeros((t, LANES - ctok // t, s), F32)

    def transpose(ti, carry):
        for k in range(s // LANES):
            ut_ref[ti, LANES * k:LANES * (k + 1), :] = scr_ref[ti, :, LANES * k:LANES * (k + 1)].T.astype(BF16)
        return carry

    lax.fori_loop(0, t, transpose, 0, unroll=2)


def _pack_call(u_ssm, uc_ssm, t):
    b, l, s = u_ssm.shape
    tok = LANES * t
    rows = SUBLANES * t
    ntok = b * l
    ctok = b * uc_ssm.shape[1]
    assert ntok % tok == 0 and ctok <= tok and ctok % rows == 0 and s % LANES == 0
    nblk = ntok // tok
    return pl.pallas_call(
        functools.partial(_pack_kernel, t=t, nblk=nblk),
        out_shape=jax.ShapeDtypeStruct((t, s, ntok // t + LANES), BF16),
        grid=(nblk + 1,),
        in_specs=[pl.BlockSpec((tok, s), lambda j: (jnp.minimum(j, nblk - 1), 0)),
                  _const_spec((ctok, s)), _const_spec((rows, rows))],
        out_specs=pl.BlockSpec((t, s, LANES), lambda j: (0, 0, j)),
        scratch_shapes=[pltpu.VMEM((t, LANES, s), F32)],
        compiler_params=_cparams(("arbitrary",)),
        name="pack",
    )(u_ssm.reshape(ntok, s), uc_ssm.reshape(ctok, s), _chunk_perm(t))


def _unpack_kernel(yt_ref, perm_ref, y_ref, scr_ref, *, t):
    s = scr_ref.shape[2]
    rows = SUBLANES * t

    def transpose(ti, carry):
        for k in range(s // LANES):
            scr_ref[ti, :, LANES * k:LANES * (k + 1)] = yt_ref[ti, LANES * k:LANES * (k + 1), :].astype(F32).T
        return carry

    lax.fori_loop(0, t, transpose, 0, unroll=2)

    def body(blk, carry):
        c0 = pl.multiple_of(blk * SUBLANES, SUBLANES)
        src = jnp.concatenate([scr_ref[ti, pl.ds(c0, SUBLANES), :] for ti in range(t)], axis=0).astype(BF16)
        out = jnp.dot(perm_ref[...], src, preferred_element_type=F32)
        y_ref[pl.ds(pl.multiple_of(blk * rows, rows), rows), :] = out.astype(y_ref.dtype)
        return carry

    lax.fori_loop(0, LANES // SUBLANES, body, 0, unroll=2)


def _unpack_call(yt, b, l):
    t, s, nl = yt.shape
    tok = LANES * t
    rows = SUBLANES * t
    y = pl.pallas_call(
        functools.partial(_unpack_kernel, t=t),
        out_shape=jax.ShapeDtypeStruct((nl * t, s), BF16),
        grid=(nl // LANES,),
        in_specs=[pl.BlockSpec((t, s, LANES), lambda j: (0, 0, j)), _const_spec((rows, rows))],
        out_specs=pl.BlockSpec((tok, s), lambda j: (j, 0)),
        scratch_shapes=[pltpu.VMEM((t, LANES, s), F32)],
        compiler_params=_cparams(("arbitrary",)),
        name="unpack",
    )(yt, _chunk_perm(t).T)
    return y.reshape(b, l, s)


POOL_BLOCK_ROWS = 8


def _pool_kernel(u_ref, *refs, tb, pad):
    nwin = len(POOL_WINDOWS)
    band_refs = refs[:nwin]
    inv_ref, pw_ref, ps_ref, o_ref, pad_ref = refs[nwin:]
    grp = pl.program_id(1)
    l = u_ref.shape[1]
    c = u_ref.shape[2]
    nblk = l // tb
    pad_ref[0:pad, :] = jnp.zeros((pad, c), pad_ref.dtype)
    pad_ref[pad + l:pad + l + pad, :] = jnp.zeros((pad, c), pad_ref.dtype)

    def copy(i, carry):
        r0 = pl.multiple_of(i * tb, tb)
        pad_ref[pl.ds(pl.multiple_of(pad + r0, GRID_W), tb), :] = u_ref[0, pl.ds(r0, tb), :]
        return carry

    lax.fori_loop(0, nblk, copy, 0)

    for gi, w in enumerate(POOL_WINDOWS):
        half = w // 2
        span = band_refs[gi].shape[1]

        @pl.when(grp == gi)
        def _(half=half, span=span, band_ref=band_refs[gi]):
            def blk(i, carry):
                r0 = pl.multiple_of(i * tb, tb)
                win = pad_ref[pl.ds(pl.multiple_of(pad + r0 - half * GRID_W, GRID_W), span), :]
                box = jnp.dot(band_ref[...], win, preferred_element_type=F32)
                variant = jnp.where(i == 0, 0, jnp.where(i == nblk - 1, 2, 1))
                inv = inv_ref[0, variant]
                m = box * jnp.concatenate([inv] * (c // LANES), axis=1)
                dlt = (m - u_ref[0, pl.ds(r0, tb), :].astype(F32)).astype(BF16)
                o = jnp.dot(dlt, pw_ref[0], preferred_element_type=F32) * ps_ref[...]
                o_ref[0, pl.ds(r0, tb), :] = o.astype(o_ref.dtype)
                return carry

            lax.fori_loop(0, nblk, blk, 0)


def _pool_tables(rows, tb):
    br = tb // GRID_W
    i = np.arange(tb)
    ri, ci = i // GRID_W, i % GRID_W
    bands, invs = [], []
    for w in POOL_WINDOWS:
        half = w // 2
        j = np.arange((br + w) * GRID_W)
        dr = (j // GRID_W)[None, :] - ri[:, None]
        dc = (j % GRID_W)[None, :] - ci[:, None]
        bands.append(jnp.asarray((dr >= 0) & (dr < w) & (dc >= -half) & (dc < w - half), BF16))
        cnt_c = np.minimum(ci + w - half, GRID_W) - np.maximum(ci - half, 0)
        per_variant = []
        for r_first in (0, br, rows - br):
            r = r_first + ri
            cnt_r = np.minimum(r + w - half, rows) - np.maximum(r - half, 0)
            per_variant.append(np.float32(1.0) / (cnt_r * cnt_c).astype(np.float32))
        invs.append(np.stack(per_variant))
    inv = np.broadcast_to(np.stack(invs)[..., None], (len(POOL_WINDOWS), 3, tb, LANES))
    return bands, jnp.asarray(inv, F32)


def _pool_call(u_pool, pool_w, pool_scale):
    b, l, p = u_pool.shape
    ng = len(POOL_WINDOWS)
    c = p // ng
    rows = l // GRID_W
    tb = POOL_BLOCK_ROWS * GRID_W
    pad = (max(POOL_WINDOWS) // 2) * GRID_W
    assert l % tb == 0 and c % LANES == 0 and rows >= 3 * POOL_BLOCK_ROWS and max(POOL_WINDOWS) <= 2 * POOL_BLOCK_ROWS
    bands, inv = _pool_tables(rows, tb)
    return pl.pallas_call(
        functools.partial(_pool_kernel, tb=tb, pad=pad),
        out_shape=jax.ShapeDtypeStruct((b, l, p), BF16),
        grid=(b, ng),
        in_specs=[pl.BlockSpec((1, l, c), lambda i, j: (i, 0, j))]
        + [_const_spec(bd.shape) for bd in bands]
        + [
            pl.BlockSpec((1, 3, tb, LANES), lambda i, j: (j, 0, 0, 0)),
            pl.BlockSpec((1, c, c), lambda i, j: (j, 0, 0)),
            pl.BlockSpec((1, c), lambda i, j: (0, j)),
        ],
        out_specs=pl.BlockSpec((1, l, c), lambda i, j: (i, 0, j)),
        scratch_shapes=[pltpu.VMEM((pad + l + pad, c), BF16)],
        compiler_params=_cparams(("arbitrary", "arbitrary")),
        name="pool",
    )(u_pool, *bands, inv, pool_w.astype(BF16), pool_scale.reshape(1, p))


def _dot_hi_nt(a, b):
    a_hi = a.astype(BF16)
    a_lo = (a - a_hi.astype(F32)).astype(BF16)
    b_hi = b.astype(BF16)
    b_lo = (b - b_hi.astype(F32)).astype(BF16)
    dot = lambda p, q: lax.dot_general(p, q, NT_DIMS, preferred_element_type=F32)
    return dot(a_hi, b_hi) + dot(a_hi, b_lo) + dot(a_lo, b_hi)


def _s5_kernel(u_ref, ea_ref, eb_ref, ba_ref, bm_ref, cc_ref, ca_ref, cb_ref, oa_ref, ob_ref, ap_ref, a8_ref, y_ref,
               et_ref, win_ref, mt_ref, wout_ref, buf_ref, cbuf_ref, hin_ref, *, gps, hc, **static):
    @pl.when(pl.program_id(0) == 0)
    def _():
        buf_ref[...] = jnp.zeros(buf_ref.shape, F32)
        cbuf_ref[...] = jnp.zeros(cbuf_ref.shape, F32)

    for gg in range(gps):
        rows = slice(gg * hc, (gg + 1) * hc)
        _s5_group(u_ref.at[:, rows, :], ea_ref.at[gg], eb_ref.at[gg], ba_ref.at[gg], bm_ref.at[gg], cc_ref.at[gg],
                  ca_ref.at[gg], cb_ref.at[gg], oa_ref.at[gg], ob_ref.at[gg], ap_ref.at[gg], a8_ref.at[gg],
                  y_ref.at[:, rows, :], et_ref.at[gg], win_ref.at[gg], mt_ref.at[gg], wout_ref.at[gg], buf_ref.at[gg],
                  cbuf_ref.at[gg], hin_ref.at[gg],
                  hc=hc, **static)


def _s5_group(u_ref, ea_ref, eb_ref, ba_ref, bm_ref, cc_ref, ca_ref, cb_ref, oa_ref, ob_ref, ap_ref, a8_ref, y_ref,
              et_ref, win_ref, mt_ref, wout_ref, buf_ref, cbuf_ref, hin_ref, *, t, hc, nb, ncl, ncc, pad, nsteps, nstate):
    n = nstate
    th = t * hc
    nl = nb * ncl
    rb = ncc + ncl
    ntile = th // LANES

    ba = ba_ref[...]
    bm = bm_ref[...]
    ca = ca_ref[...]
    cb = cb_ref[...]
    for ti in range(t):
        rows = slice(ti * hc, (ti + 1) * hc)
        e_rows = ba * ea_ref[ti:ti + 1, :] + bm * eb_ref[ti:ti + 1, :]
        et_ref[rows, :] = e_rows
        win_ref[rows, :] = e_rows.astype(BF16)
        wout_ref[rows, :] = (ca * oa_ref[ti:ti + 1, :] + cb * ob_ref[ti:ti + 1, :]).astype(BF16)
    cc = cc_ref[...]
    q_f = _dot_hi_nt(cc[0:hc], et_ref[:, 0:2 * n])
    q_b = _dot_hi_nt(cc[hc:2 * hc], et_ref[:, 2 * n:4 * n])
    zeros = jnp.zeros((hc, th), F32)
    line_b = jnp.concatenate([zeros, q_b], axis=1)
    line_f = jnp.concatenate([q_f, zeros], axis=1)
    per_tile = LANES // hc
    rolled_b = [line_b if r == 0 else pltpu.roll(line_b, hc * r, axis=1) for r in range(per_tile)]
    rolled_f = [line_f if r == per_tile - 1 else pltpu.roll(line_f, 2 * th - hc * (per_tile - 1 - r), axis=1)
                for r in range(per_tile)]
    for ti in range(t):
        a, r = divmod(ti, per_tile)
        row_b = rolled_b[r][:, th - LANES * a:2 * th - LANES * a]
        off_f = LANES * (ntile - 1 - a)
        row_f = rolled_f[r][:, off_f:off_f + th]
        mt_ref[ti * hc:(ti + 1) * hc, :] = (row_b + row_f).astype(BF16)

    u = jnp.concatenate([u_ref[ti] for ti in range(t)], axis=0)
    st_all = lax.dot_general(win_ref[...], u, TN_DIMS, preferred_element_type=F32)
    first_half = lax.broadcasted_iota(jnp.int32, (LANES, LANES), 1) < n
    for q in range(4):
        rows_q = st_all[q * n:(q + 1) * n]
        lat = jnp.concatenate([rows_q[:, b * ncl:(b + 1) * ncl] for b in range(nb)], axis=0).T
        cblk = jnp.concatenate([rows_q[:, nl:nl + LANES]] * nb, axis=0).T
        ctx = jnp.where(first_half, cblk, jnp.concatenate([cblk[ncc:], cblk[:ncc]], axis=0))[0:ncc]
        if q < 2:
            buf_ref[q, pad:pad + ncc, :] = ctx
            buf_ref[q, pad + ncc:pad + rb, :] = lat
        else:
            buf_ref[q, pad:pad + ncl, :] = lat
            buf_ref[q, pad + ncl:pad + rb, :] = ctx
    def step(ref, d, k, sgn, lo, rows, keep=None):
        ar = ap_ref[4 * k + 2 * d:4 * k + 2 * d + 1, :]
        ai = ap_ref[4 * k + 2 * d + 1:4 * k + 2 * d + 2, :]
        pr = ref[2 * d, lo + sgn:lo + sgn + rows, :]
        pi = ref[2 * d + 1, lo + sgn:lo + sgn + rows, :]
        if keep is not None:
            pr = jnp.where(keep, pr, 0.0)
            pi = jnp.where(keep, pi, 0.0)
        xr = ref[2 * d, lo:lo + rows, :]
        xi = ref[2 * d + 1, lo:lo + rows, :]
        ref[2 * d, lo:lo + rows, :] = xr + (ar * pr - ai * pi)
        ref[2 * d + 1, lo:lo + rows, :] = xi + (ar * pi + ai * pr)

    rbp = buf_ref.shape[1] - 2 * pad
    nv = rbp // SUBLANES
    cpad = (cbuf_ref.shape[1] - nv) // 2
    in_block = lax.broadcasted_iota(jnp.int32, (rbp, LANES), 0) % SUBLANES
    nlocal = SUBLANES.bit_length() - 1
    for k in range(nlocal):
        st = 1 << k
        step(buf_ref, 0, k, -st, pad, rbp, in_block >= st)
        step(buf_ref, 1, k, st, pad, rbp, in_block < SUBLANES - st)
    for d in range(2):
        edge = SUBLANES - 1 if d == 0 else 0
        for q in (2 * d, 2 * d + 1):
            cbuf_ref[q, cpad:cpad + nv, :] = buf_ref[q, pl.ds(pad + edge, nv, stride=SUBLANES), :]
    for m in range(max(0, (nv - 1).bit_length())):
        st = 1 << m
        step(cbuf_ref, 0, nlocal + m, -st, cpad, nv)
        step(cbuf_ref, 1, nlocal + m, st, cpad, nv)
    for d in range(2):
        a8r = a8_ref[2 * SUBLANES * d:2 * SUBLANES * d + SUBLANES, :]
        a8i = a8_ref[2 * SUBLANES * d + SUBLANES:2 * SUBLANES * (d + 1), :]
        for j in range(nv):
            src = j - 1 if d == 0 else j + 1
            if 0 <= src < nv:
                cr = jnp.broadcast_to(cbuf_ref[2 * d, cpad + src:cpad + src + 1, :], (SUBLANES, LANES))
                ci = jnp.broadcast_to(cbuf_ref[2 * d + 1, cpad + src:cpad + src + 1, :], (SUBLANES, LANES))
                rows = slice(pad + j * SUBLANES, pad + (j + 1) * SUBLANES)
                buf_ref[2 * d, rows, :] = buf_ref[2 * d, rows, :] + (a8r * cr - a8i * ci)
                buf_ref[2 * d + 1, rows, :] = buf_ref[2 * d + 1, rows, :] + (a8r * ci + a8i * cr)
    for q in range(4):
        lo = pad + ncc - 1 if q < 2 else pad + 1
        ent = buf_ref[q, lo:lo + ncl, :].T.astype(BF16)
        for b in range(nb):
            hin_ref[q * n:(q + 1) * n, b * ncl:(b + 1) * ncl] = ent[b * n:(b + 1) * n]

    y = jnp.dot(mt_ref[...], u[:, 0:nl], preferred_element_type=F32)
    y = y + jnp.dot(wout_ref[...], hin_ref[...], preferred_element_type=F32)
    for ti in range(t):
        y_ref[ti] = y[ti * hc:(ti + 1) * hc].astype(y_ref.dtype)


def _s5_params(lam_re, lam_im, log_dt, b_re, b_im, c_re, c_im, t, nsteps):
    _, g, n = lam_re.shape
    h = b_re.shape[-1]
    lam_re, lam_im = lam_re.astype(F32), lam_im.astype(F32)
    dt = jnp.exp(log_dt.astype(F32))[..., None]
    lr, li = lam_re * dt, lam_im * dt

    def power(d, k):
        kk = k[None, :, None]
        mag = jnp.exp(lr[d][:, None, :] * kk)
        return mag * jnp.cos(li[d][:, None, :] * kk), mag * jnp.sin(li[d][:, None, :] * kk)

    a_re, a_im = jnp.exp(lr) * jnp.cos(li), jnp.exp(lr) * jnp.sin(li)
    den = lam_re * lam_re + lam_im * lam_im
    f_re = ((a_re - 1.0) * lam_re + a_im * lam_im) / den
    f_im = (a_im * lam_re - (a_re - 1.0) * lam_im) / den
    bt_re = (f_re[..., None] * b_re - f_im[..., None] * b_im).transpose(0, 1, 3, 2)
    bt_im = (f_re[..., None] * b_im + f_im[..., None] * b_re).transpose(0, 1, 3, 2)
    lanes4 = lambda f0, f1, b0, b1: jnp.concatenate([f0, f1, b0, b1], axis=-1)

    ell = jnp.arange(t, dtype=F32)
    pf_re, pf_im = power(0, (t - 1.0) - ell)
    pb_re, pb_im = power(1, ell)
    ea = lanes4(pf_re, pf_re, pb_re, pb_re)
    eb = lanes4(pf_im, pf_im, pb_im, pb_im)
    ba = lanes4(bt_re[0], bt_im[0], bt_re[1], bt_im[1])
    bm = lanes4(-bt_im[0], bt_re[0], -bt_im[1], bt_re[1])
    cc = jnp.concatenate([jnp.concatenate([c_re[0], -c_im[0]], axis=-1),
                          jnp.concatenate([c_re[1], -c_im[1]], axis=-1)], axis=1)
    rf_re, rf_im = power(0, ell + 1.0)
    rb_re, rb_im = power(1, t - ell)
    oa = lanes4(rf_re, rf_im, rb_re, rb_im)
    ob = lanes4(rf_im, rf_re, rb_im, rb_re)
    ca = lanes4(c_re[0], -c_re[0], c_re[1], -c_re[1])
    cb = lanes4(-c_im[0], -c_im[0], -c_im[1], -c_im[1])
    strides = t * (2.0 ** jnp.arange(nsteps, dtype=F32))
    apf_re, apf_im = power(0, strides)
    apb_re, apb_im = power(1, strides)
    ap = jnp.stack([apf_re, apf_im, apb_re, apb_im], axis=2).reshape(g, 4 * nsteps, n)
    ap = jnp.concatenate([ap] * (LANES // n), axis=-1)
    rows = -(-4 * nsteps // SUBLANES) * SUBLANES
    ap = jnp.pad(ap, ((0, 0), (0, rows - 4 * nsteps), (0, 0)))
    i8 = jnp.arange(SUBLANES, dtype=F32)
    f8_re, f8_im = power(0, t * (i8 + 1.0))
    b8_re, b8_im = power(1, t * (SUBLANES - i8))
    a8 = jnp.concatenate([f8_re, f8_im, b8_re, b8_im], axis=1)
    a8 = jnp.concatenate([a8] * (LANES // n), axis=-1)
    return tuple(v.astype(F32) for v in (ea, eb, ba, bm, cc, ca, cb, oa, ob, ap, a8))


def _s5_call(ut, params, t, nb, ncl, ncc, nsteps, n_state, hc):
    ap = params[-2]
    _, sw, width = ut.shape
    nl = nb * ncl
    g = sw // hc
    th = t * hc
    rbp = -(-(ncc + ncl) // SUBLANES) * SUBLANES
    nv = rbp // SUBLANES
    pad = SUBLANES
    cpad = max(SUBLANES, 1 << max(0, (nv - 1).bit_length() - 1))
    assert nb * n_state == LANES and nb == 2 and 4 * nsteps <= ap.shape[1]
    gps = S5_GROUPS_PER_STEP
    assert g % gps == 0
    grp = lambda shp: pl.BlockSpec((gps,) + shp, lambda i: (i, 0, 0))
    return pl.pallas_call(
        functools.partial(_s5_kernel, gps=gps, t=t, hc=hc, nb=nb, ncl=ncl, ncc=ncc, pad=pad, nsteps=nsteps,
                          nstate=n_state),
        out_shape=jax.ShapeDtypeStruct((t, sw, nl), BF16),
        grid=(g // gps,),
        in_specs=[
            pl.BlockSpec((t, gps * hc, width), lambda i: (0, i, 0)),
            grp((t, 4 * n_state)), grp((t, 4 * n_state)), grp((hc, 4 * n_state)), grp((hc, 4 * n_state)),
            grp((2 * hc, 2 * n_state)), grp((hc, 4 * n_state)), grp((hc, 4 * n_state)),
            grp((t, 4 * n_state)), grp((t, 4 * n_state)), grp((ap.shape[1], LANES)), grp((4 * SUBLANES, LANES)),
        ],
        out_specs=pl.BlockSpec((t, gps * hc, nl), lambda i: (0, i, 0)),
        scratch_shapes=[pltpu.VMEM((gps, th, 4 * n_state), F32), pltpu.VMEM((gps, th, 4 * n_state), BF16),
                        pltpu.VMEM((gps, th, th), BF16), pltpu.VMEM((gps, th, 4 * n_state), BF16),
                        pltpu.VMEM((gps, 4, pad + rbp + pad, LANES), F32),
                        pltpu.VMEM((gps, 4, cpad + nv + cpad, LANES), F32),
                        pltpu.VMEM((gps, 4 * n_state, nl), BF16)],
        compiler_params=_cparams(("arbitrary",)),
        name="s5",
    )(ut, *params)


def _merge_kernel(y_ref, u_ref, p_ref, zs_ref, x_ref, gate_ref, d_ref, gw_ref, gb_ref, ow_ref, fg_ref, o_ref):
    sw = y_ref.shape[-1]
    pw = p_ref.shape[-1]
    y = _gelu_tanh(y_ref[0].astype(F32) + d_ref[...] * u_ref[0].astype(F32)).astype(BF16)
    yy = jnp.dot(y, gw_ref[...], preferred_element_type=F32) + gb_ref[...]
    ssm_out = yy[:, :sw] * _sigmoid(yy[:, sw:])
    br_pool = p_ref[0] * zs_ref[0, :, 0:pw]
    br_ssm = (ssm_out * zs_ref[0, :, pw:pw + sw].astype(F32)).astype(BF16)
    mix = jnp.dot(br_pool, ow_ref[0:pw, :], preferred_element_type=F32)
    mix = mix + jnp.dot(br_ssm, ow_ref[pw:pw + sw, :], preferred_element_type=F32)
    xo = x_ref[0] + gate_ref[0] * mix
    ms = jnp.mean(xo * xo, axis=-1, keepdims=True)
    o_ref[0] = (xo * lax.rsqrt(ms + EPS) * fg_ref[...]).astype(o_ref.dtype)


def _merge_call(y_ssm, u_ssm, pool_out, zs, x, gate, d_skip, glu_w, glu_b, out_w, final_g, tm):
    b, l, d = x.shape
    sw = y_ssm.shape[-1]
    pw = pool_out.shape[-1]
    mixw = zs.shape[-1]
    tok = lambda wd: pl.BlockSpec((1, tm, wd), lambda i, j: (i, j, 0))
    return pl.pallas_call(
        _merge_kernel,
        out_shape=jax.ShapeDtypeStruct((b, l, d), x.dtype),
        grid=(b, l // tm),
        in_specs=[
            tok(sw), tok(sw), tok(pw), tok(mixw), tok(d),
            pl.BlockSpec((1, 1, d), lambda i, j: (i, 0, 0)),
            _const_spec((1, sw)), _const_spec((sw, 2 * sw)), _const_spec((1, 2 * sw)), _const_spec((mixw, d)),
            _const_spec((1, d)),
        ],
        out_specs=tok(d),
        compiler_params=_cparams(("arbitrary", "arbitrary")),
        name="merge",
    )(y_ssm, u_ssm, pool_out, zs, x, gate, d_skip.reshape(1, sw), glu_w.astype(BF16), glu_b.reshape(1, 2 * sw),
      out_w.astype(BF16), final_g.reshape(1, d))


def kernel(x, c, ctx, c_ctx, ada_w, ada_b, norm_g, in_w, pool_w, pool_scale, s5_lam_re, s5_lam_im, s5_log_dt,
           s5_b_re, s5_b_im, s5_c_re, s5_c_im, s5_d, glu_w, glu_b, out_w, final_g):
    assert ada_w.shape[0] == 1, "single-layer block"
    bsz, seq, d = x.shape
    cl = ctx.shape[1]
    mixw = in_w.shape[2] // 2
    poolw = pool_scale.shape[-1]
    ssmw = s5_d.shape[-1]
    n_grp, n_state = s5_lam_re.shape[2], s5_lam_re.shape[3]
    hc = ssmw // n_grp
    t = CHUNK_T
    assert poolw + ssmw == mixw and seq % t == 0 and cl % t == 0 and 2 * n_state == LANES and LANES % hc == 0

    mod = _ada_call(jnp.concatenate([c, c_ctx[None]], axis=0), ada_w[0], ada_b[0])
    shift, scale, gate = mod[:, :d], mod[:, d:2 * d], mod[:, 2 * d:]
    g1 = norm_g[0].reshape(1, d)
    in_w16 = in_w[0].astype(BF16)

    tm = min(512, seq)
    u_pool, u_ssm, zs = _inproj_call(x, scale[:bsz, None], shift[:bsz, None], g1, in_w16,
                                     (poolw, ssmw, mixw), (False, False, True), min(1024, seq), "inproj")
    sc_c = jnp.broadcast_to(scale[bsz][None, None], (bsz, 1, d))
    sh_c = jnp.broadcast_to(shift[bsz][None, None], (bsz, 1, d))
    assert poolw % ssmw == 0
    (uc_ssm,) = _inproj_call(ctx, sc_c, sh_c, g1, in_w16, (ssmw,), (False,), min(256, cl), "inproj_ctx",
                             wcol=poolw // ssmw)

    pool_out = _pool_call(u_pool, pool_w[0], pool_scale[0])

    ncl, ncc = seq // t, cl // t
    nsteps = max(1, (ncc + ncl - 1).bit_length())
    params = _s5_params(s5_lam_re[0], s5_lam_im[0], s5_log_dt[0], s5_b_re[0], s5_b_im[0], s5_c_re[0], s5_c_im[0],
                        t, nsteps)
    ut = _pack_call(u_ssm, uc_ssm, t)
    yt = _s5_call(ut, params, t, bsz, ncl, ncc, nsteps, n_state, hc)
    y_ssm = _unpack_call(yt, bsz, seq)

    return _merge_call(y_ssm, u_ssm, pool_out, zs, x, gate[:bsz, None], s5_d[0], glu_w[0], glu_b[0], out_w[0],
                       final_g, tm)
```

```python
import functools
import math

import jax
import jax.numpy as jnp
import numpy as np
from jax import lax
from jax.experimental import pallas as pl
from jax.experimental.pallas import tpu as pltpu

GRID_W = 64
POOL_WINDOWS = (2, 4, 8, 16)
EPS = 1e-6
CHUNK_T = 32
S5_GROUPS_PER_STEP = 2
LANES = 128
SUBLANES = 8
VMEM_LIMIT = 56 * 1024 * 1024

F32 = jnp.float32
BF16 = jnp.bfloat16
TN_DIMS = (((0,), (0,)), ((), ()))
NT_DIMS = (((1,), (1,)), ((), ()))


def _cparams(sem):
    return pltpu.CompilerParams(dimension_semantics=sem, vmem_limit_bytes=VMEM_LIMIT)


def _const_spec(shape):
    return pl.BlockSpec(shape, lambda *_: (0,) * len(shape), pipeline_mode=pl.Buffered(1))


def _sigmoid(v):
    return 1.0 / (1.0 + jnp.exp(-v))


def _gelu_tanh(v):
    return 0.5 * v * (1.0 + jnp.tanh(math.sqrt(2.0 / math.pi) * (v + 0.044715 * (v * v * v))))


def _modulated_norm(x, g, scale, shift):
    ms = jnp.mean(x * x, axis=-1, keepdims=True)
    return (x * lax.rsqrt(ms + EPS) * g) * (1.0 + scale) + shift


def _ada_kernel(ct_ref, w_ref, b_ref, o_ref, sb_ref, *, nrows):
    d = w_ref.shape[0]
    tn = w_ref.shape[1]
    ntile = tn // LANES

    @pl.when(pl.program_id(0) == 0)
    def _():
        cv = ct_ref[...]
        s = cv * _sigmoid(cv)
        for r in range(nrows):
            sb_ref[r] = jnp.broadcast_to(s[:, r:r + 1], (d, LANES))

    def body(i, accs):
        k0 = pl.multiple_of(i * SUBLANES, SUBLANES)
        sb = [sb_ref[r, pl.ds(k0, SUBLANES), :] for r in range(nrows)]
        out = []
        for lt in range(ntile):
            w = w_ref[pl.ds(k0, SUBLANES), lt * LANES:(lt + 1) * LANES]
            out.append(tuple(accs[lt][r] + sb[r] * w for r in range(nrows)))
        return tuple(out)

    init = tuple(tuple(jnp.zeros((SUBLANES, LANES), F32) for _ in range(nrows)) for _ in range(ntile))
    accs = lax.fori_loop(0, d // SUBLANES, body, init, unroll=8)
    o_ref[...] = jnp.zeros(o_ref.shape, F32)
    for lt in range(ntile):
        for r in range(nrows):
            o_ref[r:r + 1, lt * LANES:(lt + 1) * LANES] = (jnp.sum(accs[lt][r], axis=0, keepdims=True)
                                                           + b_ref[:, lt * LANES:(lt + 1) * LANES])


def _ada_call(cvecs, ada_w, ada_b):
    nrows, d = cvecs.shape
    n = ada_w.shape[1]
    tn = 768 if n % 768 == 0 else n
    ct = jnp.zeros((d, SUBLANES), F32).at[:, :nrows].set(cvecs.T)
    out = pl.pallas_call(
        functools.partial(_ada_kernel, nrows=nrows),
        out_shape=jax.ShapeDtypeStruct((SUBLANES, n), F32),
        grid=(n // tn,),
        in_specs=[
            pl.BlockSpec((d, SUBLANES), lambda j: (0, 0)),
            pl.BlockSpec((d, tn), lambda j: (0, j)),
            pl.BlockSpec((1, tn), lambda j: (0, j)),
        ],
        out_specs=pl.BlockSpec((SUBLANES, tn), lambda j: (0, j)),
        scratch_shapes=[pltpu.VMEM((nrows, d, LANES), F32)],
        compiler_params=_cparams(("arbitrary",)),
        name="ada",
    )(ct, ada_w, ada_b.reshape(1, n))
    return out[:nrows]


def _inproj_kernel(x_ref, sc_ref, sh_ref, g_ref, w_ref, *o_refs, nchunk, silu):
    h = _modulated_norm(x_ref[0], g_ref[...], sc_ref[0], sh_ref[0]).astype(BF16)
    col = 0
    for o_ref, gate in zip(o_refs, silu):
        width = o_ref.shape[-1]
        for n0 in range(0, width, nchunk):
            acc = jnp.dot(h, w_ref[:, col + n0:col + n0 + nchunk], preferred_element_type=F32)
            if gate:
                acc = acc * _sigmoid(acc)
            o_ref[0, :, n0:n0 + nchunk] = acc.astype(o_ref.dtype)
        col += width


def _inproj_call(x, scale, shift, g, w, widths, silu, tm, name, wcol=0):
    b, l, d = x.shape
    nout = sum(widths)
    assert w.shape[1] % nout == 0 and l % tm == 0
    nchunk = min(512, min(widths))
    assert all(wd % nchunk == 0 for wd in widths)
    return pl.pallas_call(
        functools.partial(_inproj_kernel, nchunk=nchunk, silu=silu),
        out_shape=[jax.ShapeDtypeStruct((b, l, wd), BF16) for wd in widths],
        grid=(b, l // tm),
        in_specs=[
            pl.BlockSpec((1, tm, d), lambda i, j: (i, j, 0)),
            pl.BlockSpec((1, 1, d), lambda i, j: (i, 0, 0)),
            pl.BlockSpec((1, 1, d), lambda i, j: (i, 0, 0)),
            _const_spec((1, d)),
            pl.BlockSpec((d, nout), lambda i, j: (0, wcol), pipeline_mode=pl.Buffered(1)),
        ],
        out_specs=[pl.BlockSpec((1, tm, wd), lambda i, j: (i, j, 0)) for wd in widths],
        compiler_params=_cparams(("arbitrary", "arbitrary")),
        name=name,
    )(x, scale, shift, g, w)


def _chunk_perm(t):
    i = np.arange(SUBLANES * t)
    src = (i % SUBLANES) * t + i // SUBLANES
    return jnp.asarray(src[:, None] == i[None, :], BF16)


def _pack_kernel(u_ref, uc_ref, perm_ref, ut_ref, scr_ref, *, t, nblk):
    j = pl.program_id(0)
    s = scr_ref.shape[2]
    rows = SUBLANES * t
    ctok = uc_ref.shape[0]

    def permute(src, blk):
        pm = jnp.dot(perm_ref[...], src, preferred_element_type=F32)
        for ti in range(t):
            scr_ref[ti, pl.ds(pl.multiple_of(blk * SUBLANES, SUBLANES), SUBLANES), :] = pm[ti * SUBLANES:(ti + 1) * SUBLANES]

    @pl.when(j < nblk)
    def _():
        def body(blk, carry):
            permute(u_ref[pl.ds(pl.multiple_of(blk * rows, rows), rows), :], blk)
            return carry

        lax.fori_loop(0, LANES // SUBLANES, body, 0, unroll=2)

    @pl.when(j == nblk)
    def _():
        for blk in range(ctok // rows):
            permute(uc_ref[blk * rows:(blk + 1) * rows, :], blk)
        scr_ref[:, ctok // t:LANES, :] = jnp.zeros((t, LANES - ctok // t, s), F32)

    def transpose(ti, carry):
        for k in range(s // LANES):
            ut_ref[ti, LANES * k:LANES * (k + 1), :] = scr_ref[ti, :, LANES * k:LANES * (k + 1)].T.astype(BF16)
        return carry

    lax.fori_loop(0, t, transpose, 0, unroll=2)


def _pack_call(u_ssm, uc_ssm, t):
    b, l, s = u_ssm.shape
    tok = LANES * t
    rows = SUBLANES * t
    ntok = b * l
    ctok = b * uc_ssm.shape[1]
    assert ntok % tok == 0 and ctok <= tok and ctok % rows == 0 and s % LANES == 0
    nblk = ntok // tok
    return pl.pallas_call(
        functools.partial(_pack_kernel, t=t, nblk=nblk),
        out_shape=jax.ShapeDtypeStruct((t, s, ntok // t + LANES), BF16),
        grid=(nblk + 1,),
        in_specs=[pl.BlockSpec((tok, s), lambda j: (jnp.minimum(j, nblk - 1), 0)),
                  _const_spec((ctok, s)), _const_spec((rows, rows))],
        out_specs=pl.BlockSpec((t, s, LANES), lambda j: (0, 0, j)),
        scratch_shapes=[pltpu.VMEM((t, LANES, s), F32)],
        compiler_params=_cparams(("arbitrary",)),
        name="pack",
    )(u_ssm.reshape(ntok, s), uc_ssm.reshape(ctok, s), _chunk_perm(t))


def _unpack_kernel(yt_ref, perm_ref, y_ref, scr_ref, *, t):
    s = scr_ref.shape[2]
    rows = SUBLANES * t

    def transpose(ti, carry):
        for k in range(s // LANES):
            scr_ref[ti, :, LANES * k:LANES * (k + 1)] = yt_ref[ti, LANES * k:LANES * (k + 1), :].astype(F32).T
        return carry

    lax.fori_loop(0, t, transpose, 0, unroll=2)

    def body(blk, carry):
        c0 = pl.multiple_of(blk * SUBLANES, SUBLANES)
        src = jnp.concatenate([scr_ref[ti, pl.ds(c0, SUBLANES), :] for ti in range(t)], axis=0).astype(BF16)
        out = jnp.dot(perm_ref[...], src, preferred_element_type=F32)
        y_ref[pl.ds(pl.multiple_of(blk * rows, rows), rows), :] = out.astype(y_ref.dtype)
        return carry

    lax.fori_loop(0, LANES // SUBLANES, body, 0, unroll=2)


def _unpack_call(yt, b, l):
    t, s, nl = yt.shape
    tok = LANES * t
    rows = SUBLANES * t
    y = pl.pallas_call(
        functools.partial(_unpack_kernel, t=t),
        out_shape=jax.ShapeDtypeStruct((nl * t, s), BF16),
        grid=(nl // LANES,),
        in_specs=[pl.BlockSpec((t, s, LANES), lambda j: (0, 0, j)), _const_spec((rows, rows))],
        out_specs=pl.BlockSpec((tok, s), lambda j: (j, 0)),
        scratch_shapes=[pltpu.VMEM((t, LANES, s), F32)],
        compiler_params=_cparams(("arbitrary",)),
        name="unpack",
    )(yt, _chunk_perm(t).T)
    return y.reshape(b, l, s)


POOL_BLOCK_ROWS = 8
POOL_BLOCKS_PER_STEP = 4


def _pool_kernel(u_ref, *refs, tb, pad):
    nwin = len(POOL_WINDOWS)
    band_refs = refs[:nwin]
    inv_ref, pw_ref, ps_ref, o_ref, pad_ref = refs[nwin:]
    grp = pl.program_id(1)
    l = u_ref.shape[1]
    c = u_ref.shape[2]
    nblk = l // tb
    pad_ref[0:pad, :] = jnp.zeros((pad, c), pad_ref.dtype)
    pad_ref[pad + l:pad + l + pad, :] = jnp.zeros((pad, c), pad_ref.dtype)

    def copy(i, carry):
        r0 = pl.multiple_of(i * tb, tb)
        pad_ref[pl.ds(pl.multiple_of(pad + r0, GRID_W), tb), :] = u_ref[0, pl.ds(r0, tb), :]
        return carry

    lax.fori_loop(0, nblk, copy, 0)

    for gi, w in enumerate(POOL_WINDOWS):
        half = w // 2
        span = band_refs[gi].shape[1]

        @pl.when(grp == gi)
        def _(half=half, span=span, band_ref=band_refs[gi]):
            def blk(i, carry):
                first = POOL_BLOCKS_PER_STEP * i
                r0 = pl.multiple_of(first * tb, POOL_BLOCKS_PER_STEP * tb)
                wins = [pad_ref[pl.ds(pl.multiple_of(pad + r0 + k * tb - half * GRID_W, GRID_W), span), :]
                        for k in range(POOL_BLOCKS_PER_STEP)]
                box = jnp.dot(band_ref[...], jnp.concatenate(wins, axis=1), preferred_element_type=F32)
                dlts = []
                for k in range(POOL_BLOCKS_PER_STEP):
                    j = first + k
                    variant = jnp.where(j == 0, 0, jnp.where(j == nblk - 1, 2, 1))
                    inv = jnp.concatenate([inv_ref[0, variant]] * (c // LANES), axis=1)
                    u_blk = u_ref[0, pl.ds(r0 + k * tb, tb), :].astype(F32)
                    dlts.append((box[:, k * c:(k + 1) * c] * inv - u_blk).astype(BF16))
                o = jnp.dot(jnp.concatenate(dlts, axis=0), pw_ref[0], preferred_element_type=F32) * ps_ref[...]
                o_ref[0, pl.ds(r0, POOL_BLOCKS_PER_STEP * tb), :] = o.astype(o_ref.dtype)
                return carry

            lax.fori_loop(0, nblk // POOL_BLOCKS_PER_STEP, blk, 0)


def _pool_tables(rows, tb):
    br = tb // GRID_W
    i = np.arange(tb)
    ri, ci = i // GRID_W, i % GRID_W
    bands, invs = [], []
    for w in POOL_WINDOWS:
        half = w // 2
        j = np.arange((br + w) * GRID_W)
        dr = (j // GRID_W)[None, :] - ri[:, None]
        dc = (j % GRID_W)[None, :] - ci[:, None]
        bands.append(jnp.asarray((dr >= 0) & (dr < w) & (dc >= -half) & (dc < w - half), BF16))
        cnt_c = np.minimum(ci + w - half, GRID_W) - np.maximum(ci - half, 0)
        per_variant = []
        for r_first in (0, br, rows - br):
            r = r_first + ri
            cnt_r = np.minimum(r + w - half, rows) - np.maximum(r - half, 0)
            per_variant.append(np.float32(1.0) / (cnt_r * cnt_c).astype(np.float32))
        invs.append(np.stack(per_variant))
    inv = np.broadcast_to(np.stack(invs)[..., None], (len(POOL_WINDOWS), 3, tb, LANES))
    return bands, jnp.asarray(inv, F32)


def _pool_call(u_pool, pool_w, pool_scale):
    b, l, p = u_pool.shape
    ng = len(POOL_WINDOWS)
    c = p // ng
    rows = l // GRID_W
    tb = POOL_BLOCK_ROWS * GRID_W
    pad = (max(POOL_WINDOWS) // 2) * GRID_W
    assert l % (tb * POOL_BLOCKS_PER_STEP) == 0 and c % LANES == 0
    assert rows >= 3 * POOL_BLOCK_ROWS and max(POOL_WINDOWS) <= 2 * POOL_BLOCK_ROWS
    bands, inv = _pool_tables(rows, tb)
    return pl.pallas_call(
        functools.partial(_pool_kernel, tb=tb, pad=pad),
        out_shape=jax.ShapeDtypeStruct((b, l, p), BF16),
        grid=(b, ng),
        in_specs=[pl.BlockSpec((1, l, c), lambda i, j: (i, 0, j))]
        + [_const_spec(bd.shape) for bd in bands]
        + [
            pl.BlockSpec((1, 3, tb, LANES), lambda i, j: (j, 0, 0, 0)),
            pl.BlockSpec((1, c, c), lambda i, j: (j, 0, 0)),
            pl.BlockSpec((1, c), lambda i, j: (0, j)),
        ],
        out_specs=pl.BlockSpec((1, l, c), lambda i, j: (i, 0, j)),
        scratch_shapes=[pltpu.VMEM((pad + l + pad, c), BF16)],
        compiler_params=_cparams(("arbitrary", "arbitrary")),
        name="pool",
    )(u_pool, *bands, inv, pool_w.astype(BF16), pool_scale.reshape(1, p))


def _dot_hi_nt(a, b):
    a_hi = a.astype(BF16)
    a_lo = (a - a_hi.astype(F32)).astype(BF16)
    b_hi = b.astype(BF16)
    b_lo = (b - b_hi.astype(F32)).astype(BF16)
    dot = lambda p, q: lax.dot_general(p, q, NT_DIMS, preferred_element_type=F32)
    return dot(a_hi, b_hi) + dot(a_hi, b_lo) + dot(a_lo, b_hi)


def _s5_kernel(u_ref, ea_ref, eb_ref, ba_ref, bm_ref, cc_ref, ca_ref, cb_ref, oa_ref, ob_ref, ap_ref, a8_ref, y_ref,
               et_ref, win_ref, mt_ref, wout_ref, buf_ref, cbuf_ref, hin_ref, *, gps, hc, **static):
    @pl.when(pl.program_id(0) == 0)
    def _():
        buf_ref[...] = jnp.zeros(buf_ref.shape, F32)
        cbuf_ref[...] = jnp.zeros(cbuf_ref.shape, F32)

    for gg in range(gps):
        rows = slice(gg * hc, (gg + 1) * hc)
        _s5_group(u_ref.at[:, rows, :], ea_ref.at[gg], eb_ref.at[gg], ba_ref.at[gg], bm_ref.at[gg], cc_ref.at[gg],
                  ca_ref.at[gg], cb_ref.at[gg], oa_ref.at[gg], ob_ref.at[gg], ap_ref.at[gg], a8_ref.at[gg],
                  y_ref.at[:, rows, :], et_ref.at[gg], win_ref.at[gg], mt_ref.at[gg], wout_ref.at[gg], buf_ref.at[gg],
                  cbuf_ref.at[gg], hin_ref.at[gg],
                  hc=hc, **static)


def _s5_group(u_ref, ea_ref, eb_ref, ba_ref, bm_ref, cc_ref, ca_ref, cb_ref, oa_ref, ob_ref, ap_ref, a8_ref, y_ref,
              et_ref, win_ref, mt_ref, wout_ref, buf_ref, cbuf_ref, hin_ref, *, t, hc, nb, ncl, ncc, pad, nsteps, nstate):
    n = nstate
    th = t * hc
    nl = nb * ncl
    rb = ncc + ncl
    ntile = th // LANES

    ba = ba_ref[...]
    bm = bm_ref[...]
    ca = ca_ref[...]
    cb = cb_ref[...]
    for ti in range(t):
        rows = slice(ti * hc, (ti + 1) * hc)
        e_rows = ba * ea_ref[ti:ti + 1, :] + bm * eb_ref[ti:ti + 1, :]
        et_ref[rows, :] = e_rows
        win_ref[rows, :] = e_rows.astype(BF16)
        wout_ref[rows, :] = (ca * oa_ref[ti:ti + 1, :] + cb * ob_ref[ti:ti + 1, :]).astype(BF16)
    cc = cc_ref[...]
    q_f = _dot_hi_nt(cc[0:hc], et_ref[:, 0:2 * n])
    q_b = _dot_hi_nt(cc[hc:2 * hc], et_ref[:, 2 * n:4 * n])
    zeros = jnp.zeros((hc, th), F32)
    line_b = jnp.concatenate([zeros, q_b], axis=1)
    line_f = jnp.concatenate([q_f, zeros], axis=1)
    per_tile = LANES // hc
    rolled_b = [line_b if r == 0 else pltpu.roll(line_b, hc * r, axis=1) for r in range(per_tile)]
    rolled_f = [line_f if r == per_tile - 1 else pltpu.roll(line_f, 2 * th - hc * (per_tile - 1 - r), axis=1)
                for r in range(per_tile)]
    for ti in range(t):
        a, r = divmod(ti, per_tile)
        row_b = rolled_b[r][:, th - LANES * a:2 * th - LANES * a]
        off_f = LANES * (ntile - 1 - a)
        row_f = rolled_f[r][:, off_f:off_f + th]
        mt_ref[ti * hc:(ti + 1) * hc, :] = (row_b + row_f).astype(BF16)

    u = jnp.concatenate([u_ref[ti] for ti in range(t)], axis=0)
    st_all = lax.dot_general(win_ref[...], u, TN_DIMS, preferred_element_type=F32)
    first_half = lax.broadcasted_iota(jnp.int32, (LANES, LANES), 1) < n
    for q in range(4):
        rows_q = st_all[q * n:(q + 1) * n]
        lat = jnp.concatenate([rows_q[:, b * ncl:(b + 1) * ncl] for b in range(nb)], axis=0).T
        cblk = jnp.concatenate([rows_q[:, nl:nl + LANES]] * nb, axis=0).T
        ctx = jnp.where(first_half, cblk, jnp.concatenate([cblk[ncc:], cblk[:ncc]], axis=0))[0:ncc]
        if q < 2:
            buf_ref[q, pad:pad + ncc, :] = ctx
            buf_ref[q, pad + ncc:pad + rb, :] = lat
        else:
            buf_ref[q, pad:pad + ncl, :] = lat
            buf_ref[q, pad + ncl:pad + rb, :] = ctx
    def step(ref, d, k, sgn, lo, rows, keep=None):
        ar = ap_ref[4 * k + 2 * d:4 * k + 2 * d + 1, :]
        ai = ap_ref[4 * k + 2 * d + 1:4 * k + 2 * d + 2, :]
        pr = ref[2 * d, lo + sgn:lo + sgn + rows, :]
        pi = ref[2 * d + 1, lo + sgn:lo + sgn + rows, :]
        if keep is not None:
            pr = jnp.where(keep, pr, 0.0)
            pi = jnp.where(keep, pi, 0.0)
        xr = ref[2 * d, lo:lo + rows, :]
        xi = ref[2 * d + 1, lo:lo + rows, :]
        ref[2 * d, lo:lo + rows, :] = xr + (ar * pr - ai * pi)
        ref[2 * d + 1, lo:lo + rows, :] = xi + (ar * pi + ai * pr)

    rbp = buf_ref.shape[1] - 2 * pad
    nv = rbp // SUBLANES
    cpad = (cbuf_ref.shape[1] - nv) // 2
    in_block = lax.broadcasted_iota(jnp.int32, (rbp, LANES), 0) % SUBLANES
    nlocal = SUBLANES.bit_length() - 1
    for k in range(nlocal):
        st = 1 << k
        step(buf_ref, 0, k, -st, pad, rbp, in_block >= st)
        step(buf_ref, 1, k, st, pad, rbp, in_block < SUBLANES - st)
    for d in range(2):
        edge = SUBLANES - 1 if d == 0 else 0
        for q in (2 * d, 2 * d + 1):
            cbuf_ref[q, cpad:cpad + nv, :] = buf_ref[q, pl.ds(pad + edge, nv, stride=SUBLANES), :]
    for m in range(max(0, (nv - 1).bit_length())):
        st = 1 << m
        step(cbuf_ref, 0, nlocal + m, -st, cpad, nv)
        step(cbuf_ref, 1, nlocal + m, st, cpad, nv)
    for d in range(2):
        a8r = a8_ref[2 * SUBLANES * d:2 * SUBLANES * d + SUBLANES, :]
        a8i = a8_ref[2 * SUBLANES * d + SUBLANES:2 * SUBLANES * (d + 1), :]
        for j in range(nv):
            src = j - 1 if d == 0 else j + 1
            if 0 <= src < nv:
                cr = jnp.broadcast_to(cbuf_ref[2 * d, cpad + src:cpad + src + 1, :], (SUBLANES, LANES))
                ci = jnp.broadcast_to(cbuf_ref[2 * d + 1, cpad + src:cpad + src + 1, :], (SUBLANES, LANES))
                rows = slice(pad + j * SUBLANES, pad + (j + 1) * SUBLANES)
                buf_ref[2 * d, rows, :] = buf_ref[2 * d, rows, :] + (a8r * cr - a8i * ci)
                buf_ref[2 * d + 1, rows, :] = buf_ref[2 * d + 1, rows, :] + (a8r * ci + a8i * cr)
    for q in range(4):
        lo = pad + ncc - 1 if q < 2 else pad + 1
        ent = buf_ref[q, lo:lo + ncl, :].T.astype(BF16)
        for b in range(nb):
            hin_ref[q * n:(q + 1) * n, b * ncl:(b + 1) * ncl] = ent[b * n:(b + 1) * n]

    y = jnp.dot(mt_ref[...], u[:, 0:nl], preferred_element_type=F32)
    y = y + jnp.dot(wout_ref[...], hin_ref[...], preferred_element_type=F32)
    for ti in range(t):
        y_ref[ti] = y[ti * hc:(ti + 1) * hc].astype(y_ref.dtype)


def _s5_params(lam_re, lam_im, log_dt, b_re, b_im, c_re, c_im, t, nsteps):
    _, g, n = lam_re.shape
    h = b_re.shape[-1]
    lam_re, lam_im = lam_re.astype(F32), lam_im.astype(F32)
    dt = jnp.exp(log_dt.astype(F32))[..., None]
    lr, li = lam_re * dt, lam_im * dt

    def power(d, k):
        kk = k[None, :, None]
        mag = jnp.exp(lr[d][:, None, :] * kk)
        return mag * jnp.cos(li[d][:, None, :] * kk), mag * jnp.sin(li[d][:, None, :] * kk)

    a_re, a_im = jnp.exp(lr) * jnp.cos(li), jnp.exp(lr) * jnp.sin(li)
    den = lam_re * lam_re + lam_im * lam_im
    f_re = ((a_re - 1.0) * lam_re + a_im * lam_im) / den
    f_im = (a_im * lam_re - (a_re - 1.0) * lam_im) / den
    bt_re = (f_re[..., None] * b_re - f_im[..., None] * b_im).transpose(0, 1, 3, 2)
    bt_im = (f_re[..., None] * b_im + f_im[..., None] * b_re).transpose(0, 1, 3, 2)
    lanes4 = lambda f0, f1, b0, b1: jnp.concatenate([f0, f1, b0, b1], axis=-1)

    strides = t * (2.0 ** jnp.arange(nsteps, dtype=F32))
    exps = jnp.concatenate([jnp.arange(t + 1, dtype=F32), strides, t * jnp.arange(1, SUBLANES + 1, dtype=F32)])
    (allf_re, allf_im), (allb_re, allb_im) = power(0, exps), power(1, exps)
    pf_re, pf_im = allf_re[:, t - 1::-1], allf_im[:, t - 1::-1]
    pb_re, pb_im = allb_re[:, 0:t], allb_im[:, 0:t]
    ea = lanes4(pf_re, pf_re, pb_re, pb_re)
    eb = lanes4(pf_im, pf_im, pb_im, pb_im)
    ba = lanes4(bt_re[0], bt_im[0], bt_re[1], bt_im[1])
    bm = lanes4(-bt_im[0], bt_re[0], -bt_im[1], bt_re[1])
    cc = jnp.concatenate([jnp.concatenate([c_re[0], -c_im[0]], axis=-1),
                          jnp.concatenate([c_re[1], -c_im[1]], axis=-1)], axis=1)
    rf_re, rf_im = allf_re[:, 1:t + 1], allf_im[:, 1:t + 1]
    rb_re, rb_im = allb_re[:, t:0:-1], allb_im[:, t:0:-1]
    oa = lanes4(rf_re, rf_im, rb_re, rb_im)
    ob = lanes4(rf_im, rf_re, rb_im, rb_re)
    ca = lanes4(c_re[0], -c_re[0], c_re[1], -c_re[1])
    cb = lanes4(-c_im[0], -c_im[0], -c_im[1], -c_im[1])
    st0, st1 = t + 1, t + 1 + nsteps
    apf_re, apf_im, apb_re, apb_im = (v[:, st0:st1] for v in (allf_re, allf_im, allb_re, allb_im))
    ap = jnp.stack([apf_re, apf_im, apb_re, apb_im], axis=2).reshape(g, 4 * nsteps, n)
    ap = jnp.concatenate([ap] * (LANES // n), axis=-1)
    rows = -(-4 * nsteps // SUBLANES) * SUBLANES
    ap = jnp.pad(ap, ((0, 0), (0, rows - 4 * nsteps), (0, 0)))
    f8_re, f8_im = allf_re[:, st1:], allf_im[:, st1:]
    b8_re, b8_im = allb_re[:, :st1 - 1:-1], allb_im[:, :st1 - 1:-1]
    a8 = jnp.concatenate([f8_re, f8_im, b8_re, b8_im], axis=1)
    a8 = jnp.concatenate([a8] * (LANES // n), axis=-1)
    return tuple(v.astype(F32) for v in (ea, eb, ba, bm, cc, ca, cb, oa, ob, ap, a8))


def _s5_call(ut, params, t, nb, ncl, ncc, nsteps, n_state, hc):
    ap = params[-2]
    _, sw, width = ut.shape
    nl = nb * ncl
    g = sw // hc
    th = t * hc
    rbp = -(-(ncc + ncl) // SUBLANES) * SUBLANES
    nv = rbp // SUBLANES
    pad = SUBLANES
    cpad = max(SUBLANES, 1 << max(0, (nv - 1).bit_length() - 1))
    assert nb * n_state == LANES and nb == 2 and 4 * nsteps <= ap.shape[1]
    gps = S5_GROUPS_PER_STEP
    assert g % gps == 0
    grp = lambda shp: pl.BlockSpec((gps,) + shp, lambda i: (i, 0, 0))
    return pl.pallas_call(
        functools.partial(_s5_kernel, gps=gps, t=t, hc=hc, nb=nb, ncl=ncl, ncc=ncc, pad=pad, nsteps=nsteps,
                          nstate=n_state),
        out_shape=jax.ShapeDtypeStruct((t, sw, nl), BF16),
        grid=(g // gps,),
        in_specs=[
            pl.BlockSpec((t, gps * hc, width), lambda i: (0, i, 0)),
            grp((t, 4 * n_state)), grp((t, 4 * n_state)), grp((hc, 4 * n_state)), grp((hc, 4 * n_state)),
            grp((2 * hc, 2 * n_state)), grp((hc, 4 * n_state)), grp((hc, 4 * n_state)),
            grp((t, 4 * n_state)), grp((t, 4 * n_state)), grp((ap.shape[1], LANES)), grp((4 * SUBLANES, LANES)),
        ],
        out_specs=pl.BlockSpec((t, gps * hc, nl), lambda i: (0, i, 0)),
        scratch_shapes=[pltpu.VMEM((gps, th, 4 * n_state), F32), pltpu.VMEM((gps, th, 4 * n_state), BF16),
                        pltpu.VMEM((gps, th, th), BF16), pltpu.VMEM((gps, th, 4 * n_state), BF16),
                        pltpu.VMEM((gps, 4, pad + rbp + pad, LANES), F32),
                        pltpu.VMEM((gps, 4, cpad + nv + cpad, LANES), F32),
                        pltpu.VMEM((gps, 4 * n_state, nl), BF16)],
        compiler_params=_cparams(("arbitrary",)),
        name="s5",
    )(ut, *params)


def _merge_kernel(y_ref, u_ref, p_ref, zs_ref, x_ref, gate_ref, d_ref, gw_ref, gb_ref, ow_ref, fg_ref, o_ref):
    sw = y_ref.shape[-1]
    pw = p_ref.shape[-1]
    y = _gelu_tanh(y_ref[0].astype(F32) + d_ref[...] * u_ref[0].astype(F32)).astype(BF16)
    yy = jnp.dot(y, gw_ref[...], preferred_element_type=F32) + gb_ref[...]
    ssm_out = yy[:, :sw] * _sigmoid(yy[:, sw:])
    br_pool = p_ref[0] * zs_ref[0, :, 0:pw]
    br_ssm = (ssm_out * zs_ref[0, :, pw:pw + sw].astype(F32)).astype(BF16)
    mix = jnp.dot(br_pool, ow_ref[0:pw, :], preferred_element_type=F32)
    mix = mix + jnp.dot(br_ssm, ow_ref[pw:pw + sw, :], preferred_element_type=F32)
    xo = x_ref[0] + gate_ref[0] * mix
    ms = jnp.mean(xo * xo, axis=-1, keepdims=True)
    o_ref[0] = (xo * lax.rsqrt(ms + EPS) * fg_ref[...]).astype(o_ref.dtype)


def _merge_call(y_ssm, u_ssm, pool_out, zs, x, gate, d_skip, glu_w, glu_b, out_w, final_g, tm):
    b, l, d = x.shape
    sw = y_ssm.shape[-1]
    pw = pool_out.shape[-1]
    mixw = zs.shape[-1]
    tok = lambda wd: pl.BlockSpec((1, tm, wd), lambda i, j: (i, j, 0))
    return pl.pallas_call(
        _merge_kernel,
        out_shape=jax.ShapeDtypeStruct((b, l, d), x.dtype),
        grid=(b, l // tm),
        in_specs=[
            tok(sw), tok(sw), tok(pw), tok(mixw), tok(d),
            pl.BlockSpec((1, 1, d), lambda i, j: (i, 0, 0)),
            _const_spec((1, sw)), _const_spec((sw, 2 * sw)), _const_spec((1, 2 * sw)), _const_spec((mixw, d)),
            _const_spec((1, d)),
        ],
        out_specs=tok(d),
        compiler_params=_cparams(("arbitrary", "arbitrary")),
        name="merge",
    )(y_ssm, u_ssm, pool_out, zs, x, gate, d_skip.reshape(1, sw), glu_w.astype(BF16), glu_b.reshape(1, 2 * sw),
      out_w.astype(BF16), final_g.reshape(1, d))


def kernel(x, c, ctx, c_ctx, ada_w, ada_b, norm_g, in_w, pool_w, pool_scale, s5_lam_re, s5_lam_im, s5_log_dt,
           s5_b_re, s5_b_im, s5_c_re, s5_c_im, s5_d, glu_w, glu_b, out_w, final_g):
    assert ada_w.shape[0] == 1, "single-layer block"
    bsz, seq, d = x.shape
    cl = ctx.shape[1]
    mixw = in_w.shape[2] // 2
    poolw = pool_scale.shape[-1]
    ssmw = s5_d.shape[-1]
    n_grp, n_state = s5_lam_re.shape[2], s5_lam_re.shape[3]
    hc = ssmw // n_grp
    t = CHUNK_T
    assert poolw + ssmw == mixw and seq % t == 0 and cl % t == 0 and 2 * n_state == LANES and LANES % hc == 0

    mod = _ada_call(jnp.concatenate([c, c_ctx[None]], axis=0), ada_w[0], ada_b[0])
    shift, scale, gate = mod[:, :d], mod[:, d:2 * d], mod[:, 2 * d:]
    g1 = norm_g[0].reshape(1, d)
    in_w16 = in_w[0].astype(BF16)

    tm = min(512, seq)
    u_pool, u_ssm, zs = _inproj_call(x, scale[:bsz, None], shift[:bsz, None], g1, in_w16,
                                     (poolw, ssmw, mixw), (False, False, True), min(1024, seq), "inproj")
    sc_c = jnp.broadcast_to(scale[bsz][None, None], (bsz, 1, d))
    sh_c = jnp.broadcast_to(shift[bsz][None, None], (bsz, 1, d))
    assert poolw % ssmw == 0
    (uc_ssm,) = _inproj_call(ctx, sc_c, sh_c, g1, in_w16, (ssmw,), (False,), min(256, cl), "inproj_ctx",
                             wcol=poolw // ssmw)

    pool_out = _pool_call(u_pool, pool_w[0], pool_scale[0])

    ncl, ncc = seq // t, cl // t
    nsteps = max(1, (ncc + ncl - 1).bit_length())
    params = _s5_params(s5_lam_re[0], s5_lam_im[0], s5_log_dt[0], s5_b_re[0], s5_b_im[0], s5_c_re[0], s5_c_im[0],
                        t, nsteps)
    ut = _pack_call(u_ssm, uc_ssm, t)
    yt = _s5_call(ut, params, t, bsz, ncl, ncc, nsteps, n_state, hc)
    y_ssm = _unpack_call(yt, bsz, seq)

    return _merge_call(y_ssm, u_ssm, pool_out, zs, x, gate[:bsz, None], s5_d[0], glu_w[0], glu_b[0], out_w[0],
                       final_g, tm)
```

```python
import functools
import math

import jax
import jax.numpy as jnp
import numpy as np
from jax import lax
from jax.experimental import pallas as pl
from jax.experimental.pallas import tpu as pltpu

GRID_W = 64
POOL_WINDOWS = (2, 4, 8, 16)
EPS = 1e-6
CHUNK_T = 32
S5_GROUPS_PER_STEP = 4
LANES = 128
SUBLANES = 8
VMEM_LIMIT = 56 * 1024 * 1024

F32 = jnp.float32
BF16 = jnp.bfloat16
TN_DIMS = (((0,), (0,)), ((), ()))
NT_DIMS = (((1,), (1,)), ((), ()))


def _cparams(sem):
    return pltpu.CompilerParams(dimension_semantics=sem, vmem_limit_bytes=VMEM_LIMIT)


def _const_spec(shape):
    return pl.BlockSpec(shape, lambda *_: (0,) * len(shape), pipeline_mode=pl.Buffered(1))


def _sigmoid(v):
    return 1.0 / (1.0 + jnp.exp(-v))


def _gelu_tanh(v):
    return 0.5 * v * (1.0 + jnp.tanh(math.sqrt(2.0 / math.pi) * (v + 0.044715 * (v * v * v))))


def _modulated_norm(x, g, scale, shift):
    ms = jnp.mean(x * x, axis=-1, keepdims=True)
    return (x * lax.rsqrt(ms + EPS) * g) * (1.0 + scale) + shift


def _ada_kernel(ct_ref, w_ref, b_ref, o_ref, sb_ref, *, nrows):
    d = w_ref.shape[0]
    tn = w_ref.shape[1]
    ntile = tn // LANES

    @pl.when(pl.program_id(0) == 0)
    def _():
        cv = ct_ref[...]
        s = cv * _sigmoid(cv)
        for r in range(nrows):
            sb_ref[r] = jnp.broadcast_to(s[:, r:r + 1], (d, LANES))

    def body(i, accs):
        k0 = pl.multiple_of(i * SUBLANES, SUBLANES)
        sb = [sb_ref[r, pl.ds(k0, SUBLANES), :] for r in range(nrows)]
        out = []
        for lt in range(ntile):
            w = w_ref[pl.ds(k0, SUBLANES), lt * LANES:(lt + 1) * LANES]
            out.append(tuple(accs[lt][r] + sb[r] * w for r in range(nrows)))
        return tuple(out)

    init = tuple(tuple(jnp.zeros((SUBLANES, LANES), F32) for _ in range(nrows)) for _ in range(ntile))
    accs = lax.fori_loop(0, d // SUBLANES, body, init, unroll=8)
    o_ref[...] = jnp.zeros(o_ref.shape, F32)
    for lt in range(ntile):
        for r in range(nrows):
            o_ref[r:r + 1, lt * LANES:(lt + 1) * LANES] = (jnp.sum(accs[lt][r], axis=0, keepdims=True)
                                                           + b_ref[:, lt * LANES:(lt + 1) * LANES])


def _ada_call(cvecs, ada_w, ada_b):
    nrows, d = cvecs.shape
    n = ada_w.shape[1]
    tn = 1536 if n % 1536 == 0 else n
    ct = jnp.zeros((d, SUBLANES), F32).at[:, :nrows].set(cvecs.T)
    out = pl.pallas_call(
        functools.partial(_ada_kernel, nrows=nrows),
        out_shape=jax.ShapeDtypeStruct((SUBLANES, n), F32),
        grid=(n // tn,),
        in_specs=[
            pl.BlockSpec((d, SUBLANES), lambda j: (0, 0)),
            pl.BlockSpec((d, tn), lambda j: (0, j)),
            pl.BlockSpec((1, tn), lambda j: (0, j)),
        ],
        out_specs=pl.BlockSpec((SUBLANES, tn), lambda j: (0, j)),
        scratch_shapes=[pltpu.VMEM((nrows, d, LANES), F32)],
        compiler_params=_cparams(("arbitrary",)),
        name="ada",
    )(ct, ada_w, ada_b.reshape(1, n))
    return out[:nrows]


def _inproj_kernel(x_ref, sc_ref, sh_ref, g_ref, w_ref, *o_refs, nchunk, silu):
    h = _modulated_norm(x_ref[0], g_ref[...], sc_ref[0], sh_ref[0]).astype(BF16)
    col = 0
    for o_ref, gate in zip(o_refs, silu):
        width = o_ref.shape[-1]
        for n0 in range(0, width, nchunk):
            acc = jnp.dot(h, w_ref[:, col + n0:col + n0 + nchunk], preferred_element_type=F32)
            if gate:
                acc = acc * _sigmoid(acc)
            o_ref[0, :, n0:n0 + nchunk] = acc.astype(o_ref.dtype)
        col += width


def _inproj_call(x, scale, shift, g, w, widths, silu, tm, name, wcol=0):
    b, l, d = x.shape
    nout = sum(widths)
    assert w.shape[1] % nout == 0 and l % tm == 0
    nchunk = min(1024, min(widths))
    assert all(wd % nchunk == 0 for wd in widths)
    return pl.pallas_call(
        functools.partial(_inproj_kernel, nchunk=nchunk, silu=silu),
        out_shape=[jax.ShapeDtypeStruct((b, l, wd), BF16) for wd in widths],
        grid=(b, l // tm),
        in_specs=[
            pl.BlockSpec((1, tm, d), lambda i, j: (i, j, 0)),
            pl.BlockSpec((1, 1, d), lambda i, j: (i, 0, 0)),
            pl.BlockSpec((1, 1, d), lambda i, j: (i, 0, 0)),
            _const_spec((1, d)),
            pl.BlockSpec((d, nout), lambda i, j: (0, wcol), pipeline_mode=pl.Buffered(1)),
        ],
        out_specs=[pl.BlockSpec((1, tm, wd), lambda i, j: (i, j, 0)) for wd in widths],
        compiler_params=_cparams(("arbitrary", "arbitrary")),
        name=name,
    )(x, scale, shift, g, w)


def _chunk_perm(t):
    i = np.arange(SUBLANES * t)
    src = (i % SUBLANES) * t + i // SUBLANES
    return jnp.asarray(src[:, None] == i[None, :], BF16)


def _pack_kernel(u_ref, uc_ref, perm_ref, ut_ref, scr_ref, *, t, nblk):
    j = pl.program_id(0)
    s = scr_ref.shape[2]
    rows = SUBLANES * t
    ctok = uc_ref.shape[0]

    def permute(src, blk):
        pm = jnp.dot(perm_ref[...], src, preferred_element_type=F32)
        for ti in range(t):
            scr_ref[ti, pl.ds(pl.multiple_of(blk * SUBLANES, SUBLANES), SUBLANES), :] = pm[ti * SUBLANES:(ti + 1) * SUBLANES]

    @pl.when(j < nblk)
    def _():
        def body(blk, carry):
            permute(u_ref[pl.ds(pl.multiple_of(blk * rows, rows), rows), :], blk)
            return carry

        lax.fori_loop(0, LANES // SUBLANES, body, 0, unroll=2)

    @pl.when(j == nblk)
    def _():
        for blk in range(ctok // rows):
            permute(uc_ref[blk * rows:(blk + 1) * rows, :], blk)
        scr_ref[:, ctok // t:LANES, :] = jnp.zeros((t, LANES - ctok // t, s), F32)

    def transpose(ti, carry):
        for k in range(s // LANES):
            ut_ref[ti, LANES * k:LANES * (k + 1), :] = scr_ref[ti, :, LANES * k:LANES * (k + 1)].T.astype(BF16)
        return carry

    lax.fori_loop(0, t, transpose, 0, unroll=2)


def _pack_call(u_ssm, uc_ssm, t):
    b, l, s = u_ssm.shape
    tok = LANES * t
    rows = SUBLANES * t
    ntok = b * l
    ctok = b * uc_ssm.shape[1]
    assert ntok % tok == 0 and ctok <= tok and ctok % rows == 0 and s % LANES == 0
    nblk = ntok // tok
    return pl.pallas_call(
        functools.partial(_pack_kernel, t=t, nblk=nblk),
        out_shape=jax.ShapeDtypeStruct((t, s, ntok // t + LANES), BF16),
        grid=(nblk + 1,),
        in_specs=[pl.BlockSpec((tok, s), lambda j: (jnp.minimum(j, nblk - 1), 0)),
                  _const_spec((ctok, s)), _const_spec((rows, rows))],
        out_specs=pl.BlockSpec((t, s, LANES), lambda j: (0, 0, j)),
        scratch_shapes=[pltpu.VMEM((t, LANES, s), F32)],
        compiler_params=_cparams(("arbitrary",)),
        name="pack",
    )(u_ssm.reshape(ntok, s), uc_ssm.reshape(ctok, s), _chunk_perm(t))


def _unpack_kernel(yt_ref, perm_ref, y_ref, scr_ref, *, t):
    s = scr_ref.shape[2]
    rows = SUBLANES * t

    def transpose(ti, carry):
        for k in range(s // LANES):
            scr_ref[ti, :, LANES * k:LANES * (k + 1)] = yt_ref[ti, LANES * k:LANES * (k + 1), :].astype(F32).T
        return carry

    lax.fori_loop(0, t, transpose, 0, unroll=2)

    def body(blk, carry):
        c0 = pl.multiple_of(blk * SUBLANES, SUBLANES)
        src = jnp.concatenate([scr_ref[ti, pl.ds(c0, SUBLANES), :] for ti in range(t)], axis=0).astype(BF16)
        out = jnp.dot(perm_ref[...], src, preferred_element_type=F32)
        y_ref[pl.ds(pl.multiple_of(blk * rows, rows), rows), :] = out.astype(y_ref.dtype)
        return carry

    lax.fori_loop(0, LANES // SUBLANES, body, 0, unroll=2)


def _unpack_call(yt, b, l):
    t, s, nl = yt.shape
    tok = LANES * t
    rows = SUBLANES * t
    y = pl.pallas_call(
        functools.partial(_unpack_kernel, t=t),
        out_shape=jax.ShapeDtypeStruct((nl * t, s), BF16),
        grid=(nl // LANES,),
        in_specs=[pl.BlockSpec((t, s, LANES), lambda j: (0, 0, j)), _const_spec((rows, rows))],
        out_specs=pl.BlockSpec((tok, s), lambda j: (j, 0)),
        scratch_shapes=[pltpu.VMEM((t, LANES, s), F32)],
        compiler_params=_cparams(("arbitrary",)),
        name="unpack",
    )(yt, _chunk_perm(t).T)
    return y.reshape(b, l, s)


POOL_BLOCK_ROWS = 8
POOL_BLOCKS_PER_STEP = 4


def _pool_kernel(u_ref, *refs, tb, pad):
    nwin = len(POOL_WINDOWS)
    band_refs = refs[:nwin]
    inv_ref, pw_ref, ps_ref, o_ref, pad_ref = refs[nwin:]
    grp = pl.program_id(1)
    l = u_ref.shape[1]
    c = u_ref.shape[2]
    nblk = l // tb
    pad_ref[0:pad, :] = jnp.zeros((pad, c), pad_ref.dtype)
    pad_ref[pad + l:pad + l + pad, :] = jnp.zeros((pad, c), pad_ref.dtype)

    def copy(i, carry):
        r0 = pl.multiple_of(i * tb, tb)
        pad_ref[pl.ds(pl.multiple_of(pad + r0, GRID_W), tb), :] = u_ref[0, pl.ds(r0, tb), :]
        return carry

    lax.fori_loop(0, nblk, copy, 0)

    for gi, w in enumerate(POOL_WINDOWS):
        half = w // 2
        span = band_refs[gi].shape[1]

        @pl.when(grp == gi)
        def _(half=half, span=span, band_ref=band_refs[gi]):
            def blk(i, carry):
                first = POOL_BLOCKS_PER_STEP * i
                r0 = pl.multiple_of(first * tb, POOL_BLOCKS_PER_STEP * tb)
                wins = [pad_ref[pl.ds(pl.multiple_of(pad + r0 + k * tb - half * GRID_W, GRID_W), span), :]
                        for k in range(POOL_BLOCKS_PER_STEP)]
                box = jnp.dot(band_ref[...], jnp.concatenate(wins, axis=1), preferred_element_type=F32)
                dlts = []
                for k in range(POOL_BLOCKS_PER_STEP):
                    j = first + k
                    variant = jnp.where(j == 0, 0, jnp.where(j == nblk - 1, 2, 1))
                    inv = jnp.concatenate([inv_ref[0, variant]] * (c // LANES), axis=1)
                    u_blk = u_ref[0, pl.ds(r0 + k * tb, tb), :].astype(F32)
                    dlts.append((box[:, k * c:(k + 1) * c] * inv - u_blk).astype(BF16))
                o = jnp.dot(jnp.concatenate(dlts, axis=0), pw_ref[0], preferred_element_type=F32) * ps_ref[...]
                o_ref[0, pl.ds(r0, POOL_BLOCKS_PER_STEP * tb), :] = o.astype(o_ref.dtype)
                return carry

            lax.fori_loop(0, nblk // POOL_BLOCKS_PER_STEP, blk, 0)


def _pool_tables(rows, tb):
    br = tb // GRID_W
    i = np.arange(tb)
    ri, ci = i // GRID_W, i % GRID_W
    bands, invs = [], []
    for w in POOL_WINDOWS:
        half = w // 2
        j = np.arange((br + w) * GRID_W)
        dr = (j // GRID_W)[None, :] - ri[:, None]
        dc = (j % GRID_W)[None, :] - ci[:, None]
        bands.append(jnp.asarray((dr >= 0) & (dr < w) & (dc >= -half) & (dc < w - half), BF16))
        cnt_c = np.minimum(ci + w - half, GRID_W) - np.maximum(ci - half, 0)
        per_variant = []
        for r_first in (0, br, rows - br):
            r = r_first + ri
            cnt_r = np.minimum(r + w - half, rows) - np.maximum(r - half, 0)
            per_variant.append(np.float32(1.0) / (cnt_r * cnt_c).astype(np.float32))
        invs.append(np.stack(per_variant))
    inv = np.broadcast_to(np.stack(invs)[..., None], (len(POOL_WINDOWS), 3, tb, LANES))
    return bands, jnp.asarray(inv, F32)


def _pool_call(u_pool, pool_w, pool_scale):
    b, l, p = u_pool.shape
    ng = len(POOL_WINDOWS)
    c = p // ng
    rows = l // GRID_W
    tb = POOL_BLOCK_ROWS * GRID_W
    pad = (max(POOL_WINDOWS) // 2) * GRID_W
    assert l % (tb * POOL_BLOCKS_PER_STEP) == 0 and c % LANES == 0
    assert rows >= 3 * POOL_BLOCK_ROWS and max(POOL_WINDOWS) <= 2 * POOL_BLOCK_ROWS
    bands, inv = _pool_tables(rows, tb)
    return pl.pallas_call(
        functools.partial(_pool_kernel, tb=tb, pad=pad),
        out_shape=jax.ShapeDtypeStruct((b, l, p), BF16),
        grid=(b, ng),
        in_specs=[pl.BlockSpec((1, l, c), lambda i, j: (i, 0, j))]
        + [_const_spec(bd.shape) for bd in bands]
        + [
            pl.BlockSpec((1, 3, tb, LANES), lambda i, j: (j, 0, 0, 0)),
            pl.BlockSpec((1, c, c), lambda i, j: (j, 0, 0)),
            pl.BlockSpec((1, c), lambda i, j: (0, j)),
        ],
        out_specs=pl.BlockSpec((1, l, c), lambda i, j: (i, 0, j)),
        scratch_shapes=[pltpu.VMEM((pad + l + pad, c), BF16)],
        compiler_params=_cparams(("arbitrary", "arbitrary")),
        name="pool",
    )(u_pool, *bands, inv, pool_w.astype(BF16), pool_scale.reshape(1, p))


def _dot_hi_nt(a, b):
    a_hi = a.astype(BF16)
    a_lo = (a - a_hi.astype(F32)).astype(BF16)
    b_hi = b.astype(BF16)
    b_lo = (b - b_hi.astype(F32)).astype(BF16)
    dot = lambda p, q: lax.dot_general(p, q, NT_DIMS, preferred_element_type=F32)
    return dot(a_hi, b_hi) + dot(a_hi, b_lo) + dot(a_lo, b_hi)


def _s5_kernel(u_ref, ea_ref, eb_ref, ba_ref, bm_ref, cc_ref, ca_ref, cb_ref, oa_ref, ob_ref, ap_ref, a8_ref, y_ref,
               et_ref, win_ref, mt_ref, wout_ref, buf_ref, cbuf_ref, hin_ref, *, gps, hc, **static):
    @pl.when(pl.program_id(0) == 0)
    def _():
        buf_ref[...] = jnp.zeros(buf_ref.shape, F32)
        cbuf_ref[...] = jnp.zeros(cbuf_ref.shape, F32)

    for gg in range(gps):
        rows = slice(gg * hc, (gg + 1) * hc)
        _s5_group(u_ref.at[:, rows, :], ea_ref.at[gg], eb_ref.at[gg], ba_ref.at[gg], bm_ref.at[gg], cc_ref.at[gg],
                  ca_ref.at[gg], cb_ref.at[gg], oa_ref.at[gg], ob_ref.at[gg], ap_ref.at[gg], a8_ref.at[gg],
                  y_ref.at[:, rows, :], et_ref.at[gg], win_ref.at[gg], mt_ref.at[gg], wout_ref.at[gg], buf_ref.at[gg],
                  cbuf_ref.at[gg], hin_ref.at[gg],
                  hc=hc, **static)


def _s5_group(u_ref, ea_ref, eb_ref, ba_ref, bm_ref, cc_ref, ca_ref, cb_ref, oa_ref, ob_ref, ap_ref, a8_ref, y_ref,
              et_ref, win_ref, mt_ref, wout_ref, buf_ref, cbuf_ref, hin_ref, *, t, hc, nb, ncl, ncc, pad, nsteps, nstate):
    n = nstate
    th = t * hc
    nl = nb * ncl
    rb = ncc + ncl
    ntile = th // LANES

    ba = ba_ref[...]
    bm = bm_ref[...]
    ca = ca_ref[...]
    cb = cb_ref[...]
    for ti in range(t):
        rows = slice(ti * hc, (ti + 1) * hc)
        e_rows = ba * ea_ref[ti:ti + 1, :] + bm * eb_ref[ti:ti + 1, :]
        et_ref[rows, :] = e_rows
        win_ref[rows, :] = e_rows.astype(BF16)
        wout_ref[rows, :] = (ca * oa_ref[ti:ti + 1, :] + cb * ob_ref[ti:ti + 1, :]).astype(BF16)
    cc = cc_ref[...]
    q_f = _dot_hi_nt(cc[0:hc], et_ref[:, 0:2 * n])
    q_b = _dot_hi_nt(cc[hc:2 * hc], et_ref[:, 2 * n:4 * n])
    zeros = jnp.zeros((hc, th), F32)
    line_b = jnp.concatenate([zeros, q_b], axis=1)
    line_f = jnp.concatenate([q_f, zeros], axis=1)
    per_tile = LANES // hc
    rolled_b = [line_b if r == 0 else pltpu.roll(line_b, hc * r, axis=1) for r in range(per_tile)]
    rolled_f = [line_f if r == per_tile - 1 else pltpu.roll(line_f, 2 * th - hc * (per_tile - 1 - r), axis=1)
                for r in range(per_tile)]
    for ti in range(t):
        a, r = divmod(ti, per_tile)
        row_b = rolled_b[r][:, th - LANES * a:2 * th - LANES * a]
        off_f = LANES * (ntile - 1 - a)
        row_f = rolled_f[r][:, off_f:off_f + th]
        mt_ref[ti * hc:(ti + 1) * hc, :] = (row_b + row_f).astype(BF16)

    u = jnp.concatenate([u_ref[ti] for ti in range(t)], axis=0)
    st_all = lax.dot_general(win_ref[...], u, TN_DIMS, preferred_element_type=F32)
    first_half = lax.broadcasted_iota(jnp.int32, (LANES, LANES), 1) < n
    for q in range(4):
        rows_q = st_all[q * n:(q + 1) * n]
        lat = jnp.concatenate([rows_q[:, b * ncl:(b + 1) * ncl] for b in range(nb)], axis=0).T
        cblk = jnp.concatenate([rows_q[:, nl:nl + LANES]] * nb, axis=0).T
        ctx = jnp.where(first_half, cblk, jnp.concatenate([cblk[ncc:], cblk[:ncc]], axis=0))[0:ncc]
        if q < 2:
            buf_ref[q, pad:pad + ncc, :] = ctx
            buf_ref[q, pad + ncc:pad + rb, :] = lat
        else:
            buf_ref[q, pad:pad + ncl, :] = lat
            buf_ref[q, pad + ncl:pad + rb, :] = ctx
    def step(ref, d, k, sgn, lo, rows, keep=None):
        ar = ap_ref[4 * k + 2 * d:4 * k + 2 * d + 1, :]
        ai = ap_ref[4 * k + 2 * d + 1:4 * k + 2 * d + 2, :]
        pr = ref[2 * d, lo + sgn:lo + sgn + rows, :]
        pi = ref[2 * d + 1, lo + sgn:lo + sgn + rows, :]
        if keep is not None:
            pr = jnp.where(keep, pr, 0.0)
            pi = jnp.where(keep, pi, 0.0)
        xr = ref[2 * d, lo:lo + rows, :]
        xi = ref[2 * d + 1, lo:lo + rows, :]
        ref[2 * d, lo:lo + rows, :] = xr + (ar * pr - ai * pi)
        ref[2 * d + 1, lo:lo + rows, :] = xi + (ar * pi + ai * pr)

    rbp = buf_ref.shape[1] - 2 * pad
    nv = rbp // SUBLANES
    cpad = (cbuf_ref.shape[1] - nv) // 2
    in_block = lax.broadcasted_iota(jnp.int32, (rbp, LANES), 0) % SUBLANES
    nlocal = SUBLANES.bit_length() - 1
    for k in range(nlocal):
        st = 1 << k
        step(buf_ref, 0, k, -st, pad, rbp, in_block >= st)
        step(buf_ref, 1, k, st, pad, rbp, in_block < SUBLANES - st)
    for d in range(2):
        edge = SUBLANES - 1 if d == 0 else 0
        for q in (2 * d, 2 * d + 1):
            cbuf_ref[q, cpad:cpad + nv, :] = buf_ref[q, pl.ds(pad + edge, nv, stride=SUBLANES), :]
    for m in range(max(0, (nv - 1).bit_length())):
        st = 1 << m
        step(cbuf_ref, 0, nlocal + m, -st, cpad, nv)
        step(cbuf_ref, 1, nlocal + m, st, cpad, nv)
    for d in range(2):
        a8r = a8_ref[2 * SUBLANES * d:2 * SUBLANES * d + SUBLANES, :]
        a8i = a8_ref[2 * SUBLANES * d + SUBLANES:2 * SUBLANES * (d + 1), :]
        for j in range(nv):
            src = j - 1 if d == 0 else j + 1
            if 0 <= src < nv:
                cr = jnp.broadcast_to(cbuf_ref[2 * d, cpad + src:cpad + src + 1, :], (SUBLANES, LANES))
                ci = jnp.broadcast_to(cbuf_ref[2 * d + 1, cpad + src:cpad + src + 1, :], (SUBLANES, LANES))
                rows = slice(pad + j * SUBLANES, pad + (j + 1) * SUBLANES)
                buf_ref[2 * d, rows, :] = buf_ref[2 * d, rows, :] + (a8r * cr - a8i * ci)
                buf_ref[2 * d + 1, rows, :] = buf_ref[2 * d + 1, rows, :] + (a8r * ci + a8i * cr)
    for q in range(4):
        lo = pad + ncc - 1 if q < 2 else pad + 1
        ent = buf_ref[q, lo:lo + ncl, :].T.astype(BF16)
        for b in range(nb):
            hin_ref[q * n:(q + 1) * n, b * ncl:(b + 1) * ncl] = ent[b * n:(b + 1) * n]

    y = jnp.dot(mt_ref[...], u[:, 0:nl], preferred_element_type=F32)
    y = y + jnp.dot(wout_ref[...], hin_ref[...], preferred_element_type=F32)
    for ti in range(t):
        y_ref[ti] = y[ti * hc:(ti + 1) * hc].astype(y_ref.dtype)


def _s5_params(lam_re, lam_im, log_dt, b_re, b_im, c_re, c_im, t, nsteps):
    _, g, n = lam_re.shape
    h = b_re.shape[-1]
    lam_re, lam_im = lam_re.astype(F32), lam_im.astype(F32)
    dt = jnp.exp(log_dt.astype(F32))[..., None]
    lr, li = lam_re * dt, lam_im * dt

    def power(d, k):
        kk = k[None, :, None]
        mag = jnp.exp(lr[d][:, None, :] * kk)
        return mag * jnp.cos(li[d][:, None, :] * kk), mag * jnp.sin(li[d][:, None, :] * kk)

    a_re, a_im = jnp.exp(lr) * jnp.cos(li), jnp.exp(lr) * jnp.sin(li)
    den = lam_re * lam_re + lam_im * lam_im
    f_re = ((a_re - 1.0) * lam_re + a_im * lam_im) / den
    f_im = (a_im * lam_re - (a_re - 1.0) * lam_im) / den
    bt_re = (f_re[..., None] * b_re - f_im[..., None] * b_im).transpose(0, 1, 3, 2)
    bt_im = (f_re[..., None] * b_im + f_im[..., None] * b_re).transpose(0, 1, 3, 2)
    lanes4 = lambda f0, f1, b0, b1: jnp.concatenate([f0, f1, b0, b1], axis=-1)

    strides = t * (2.0 ** jnp.arange(nsteps, dtype=F32))
    exps = jnp.concatenate([jnp.arange(t + 1, dtype=F32), strides, t * jnp.arange(1, SUBLANES + 1, dtype=F32)])
    (allf_re, allf_im), (allb_re, allb_im) = power(0, exps), power(1, exps)
    pf_re, pf_im = allf_re[:, t - 1::-1], allf_im[:, t - 1::-1]
    pb_re, pb_im = allb_re[:, 0:t], allb_im[:, 0:t]
    ea = lanes4(pf_re, pf_re, pb_re, pb_re)
    eb = lanes4(pf_im, pf_im, pb_im, pb_im)
    ba = lanes4(bt_re[0], bt_im[0], bt_re[1], bt_im[1])
    bm = lanes4(-bt_im[0], bt_re[0], -bt_im[1], bt_re[1])
    cc = jnp.concatenate([jnp.concatenate([c_re[0], -c_im[0]], axis=-1),
                          jnp.concatenate([c_re[1], -c_im[1]], axis=-1)], axis=1)
    rf_re, rf_im = allf_re[:, 1:t + 1], allf_im[:, 1:t + 1]
    rb_re, rb_im = allb_re[:, t:0:-1], allb_im[:, t:0:-1]
    oa = lanes4(rf_re, rf_im, rb_re, rb_im)
    ob = lanes4(rf_im, rf_re, rb_im, rb_re)
    ca = lanes4(c_re[0], -c_re[0], c_re[1], -c_re[1])
    cb = lanes4(-c_im[0], -c_im[0], -c_im[1], -c_im[1])
    st0, st1 = t + 1, t + 1 + nsteps
    apf_re, apf_im, apb_re, apb_im = (v[:, st0:st1] for v in (allf_re, allf_im, allb_re, allb_im))
    ap = jnp.stack([apf_re, apf_im, apb_re, apb_im], axis=2).reshape(g, 4 * nsteps, n)
    ap = jnp.concatenate([ap] * (LANES // n), axis=-1)
    rows = -(-4 * nsteps // SUBLANES) * SUBLANES
    ap = jnp.pad(ap, ((0, 0), (0, rows - 4 * nsteps), (0, 0)))
    f8_re, f8_im = allf_re[:, st1:], allf_im[:, st1:]
    b8_re, b8_im = allb_re[:, :st1 - 1:-1], allb_im[:, :st1 - 1:-1]
    a8 = jnp.concatenate([f8_re, f8_im, b8_re, b8_im], axis=1)
    a8 = jnp.concatenate([a8] * (LANES // n), axis=-1)
    return tuple(v.astype(F32) for v in (ea, eb, ba, bm, cc, ca, cb, oa, ob, ap, a8))


def _s5_call(ut, params, t, nb, ncl, ncc, nsteps, n_state, hc):
    ap = params[-2]
    _, sw, width = ut.shape
    nl = nb * ncl
    g = sw // hc
    th = t * hc
    rbp = -(-(ncc + ncl) // SUBLANES) * SUBLANES
    nv = rbp // SUBLANES
    pad = SUBLANES
    cpad = max(SUBLANES, 1 << max(0, (nv - 1).bit_length() - 1))
    assert nb * n_state == LANES and nb == 2 and 4 * nsteps <= ap.shape[1]
    gps = S5_GROUPS_PER_STEP
    assert g % gps == 0
    grp = lambda shp: pl.BlockSpec((gps,) + shp, lambda i: (i, 0, 0))
    return pl.pallas_call(
        functools.partial(_s5_kernel, gps=gps, t=t, hc=hc, nb=nb, ncl=ncl, ncc=ncc, pad=pad, nsteps=nsteps,
                          nstate=n_state),
        out_shape=jax.ShapeDtypeStruct((t, sw, nl), BF16),
        grid=(g // gps,),
        in_specs=[
            pl.BlockSpec((t, gps * hc, width), lambda i: (0, i, 0)),
            grp((t, 4 * n_state)), grp((t, 4 * n_state)), grp((hc, 4 * n_state)), grp((hc, 4 * n_state)),
            grp((2 * hc, 2 * n_state)), grp((hc, 4 * n_state)), grp((hc, 4 * n_state)),
            grp((t, 4 * n_state)), grp((t, 4 * n_state)), grp((ap.shape[1], LANES)), grp((4 * SUBLANES, LANES)),
        ],
        out_specs=pl.BlockSpec((t, gps * hc, nl), lambda i: (0, i, 0)),
        scratch_shapes=[pltpu.VMEM((gps, th, 4 * n_state), F32), pltpu.VMEM((gps, th, 4 * n_state), BF16),
                        pltpu.VMEM((gps, th, th), BF16), pltpu.VMEM((gps, th, 4 * n_state), BF16),
                        pltpu.VMEM((gps, 4, pad + rbp + pad, LANES), F32),
                        pltpu.VMEM((gps, 4, cpad + nv + cpad, LANES), F32),
                        pltpu.VMEM((gps, 4 * n_state, nl), BF16)],
        compiler_params=_cparams(("arbitrary",)),
        name="s5",
    )(ut, *params)


def _merge_kernel(y_ref, u_ref, p_ref, zs_ref, x_ref, gate_ref, d_ref, gw_ref, gb_ref, ow_ref, fg_ref, o_ref):
    sw = y_ref.shape[-1]
    pw = p_ref.shape[-1]
    y = _gelu_tanh(y_ref[0].astype(F32) + d_ref[...] * u_ref[0].astype(F32)).astype(BF16)
    yy = jnp.dot(y, gw_ref[...], preferred_element_type=F32) + gb_ref[...]
    ssm_out = yy[:, :sw] * _sigmoid(yy[:, sw:])
    br_pool = p_ref[0] * zs_ref[0, :, 0:pw]
    br_ssm = (ssm_out * zs_ref[0, :, pw:pw + sw].astype(F32)).astype(BF16)
    mix = jnp.dot(br_pool, ow_ref[0:pw, :], preferred_element_type=F32)
    mix = mix + jnp.dot(br_ssm, ow_ref[pw:pw + sw, :], preferred_element_type=F32)
    xo = x_ref[0] + gate_ref[0] * mix
    ms = jnp.mean(xo * xo, axis=-1, keepdims=True)
    o_ref[0] = (xo * lax.rsqrt(ms + EPS) * fg_ref[...]).astype(o_ref.dtype)


def _merge_call(y_ssm, u_ssm, pool_out, zs, x, gate, d_skip, glu_w, glu_b, out_w, final_g, tm):
    b, l, d = x.shape
    sw = y_ssm.shape[-1]
    pw = pool_out.shape[-1]
    mixw = zs.shape[-1]
    tok = lambda wd: pl.BlockSpec((1, tm, wd), lambda i, j: (i, j, 0))
    return pl.pallas_call(
        _merge_kernel,
        out_shape=jax.ShapeDtypeStruct((b, l, d), x.dtype),
        grid=(b, l // tm),
        in_specs=[
            tok(sw), tok(sw), tok(pw), tok(mixw), tok(d),
            pl.BlockSpec((1, 1, d), lambda i, j: (i, 0, 0)),
            _const_spec((1, sw)), _const_spec((sw, 2 * sw)), _const_spec((1, 2 * sw)), _const_spec((mixw, d)),
            _const_spec((1, d)),
        ],
        out_specs=tok(d),
        compiler_params=_cparams(("arbitrary", "arbitrary")),
        name="merge",
    )(y_ssm, u_ssm, pool_out, zs, x, gate, d_skip.reshape(1, sw), glu_w.astype(BF16), glu_b.reshape(1, 2 * sw),
      out_w.astype(BF16), final_g.reshape(1, d))


def kernel(x, c, ctx, c_ctx, ada_w, ada_b, norm_g, in_w, pool_w, pool_scale, s5_lam_re, s5_lam_im, s5_log_dt,
           s5_b_re, s5_b_im, s5_c_re, s5_c_im, s5_d, glu_w, glu_b, out_w, final_g):
    assert ada_w.shape[0] == 1, "single-layer block"
    bsz, seq, d = x.shape
    cl = ctx.shape[1]
    mixw = in_w.shape[2] // 2
    poolw = pool_scale.shape[-1]
    ssmw = s5_d.shape[-1]
    n_grp, n_state = s5_lam_re.shape[2], s5_lam_re.shape[3]
    hc = ssmw // n_grp
    t = CHUNK_T
    assert poolw + ssmw == mixw and seq % t == 0 and cl % t == 0 and 2 * n_state == LANES and LANES % hc == 0

    mod = _ada_call(jnp.concatenate([c, c_ctx[None]], axis=0), ada_w[0], ada_b[0])
    shift, scale, gate = mod[:, :d], mod[:, d:2 * d], mod[:, 2 * d:]
    g1 = norm_g[0].reshape(1, d)
    in_w16 = in_w[0].astype(BF16)

    tm = min(512, seq)
    u_pool, u_ssm, zs = _inproj_call(x, scale[:bsz, None], shift[:bsz, None], g1, in_w16,
                                     (poolw, ssmw, mixw), (False, False, True), min(1024, seq), "inproj")
    sc_c = jnp.broadcast_to(scale[bsz][None, None], (bsz, 1, d))
    sh_c = jnp.broadcast_to(shift[bsz][None, None], (bsz, 1, d))
    assert poolw % ssmw == 0
    (uc_ssm,) = _inproj_call(ctx, sc_c, sh_c, g1, in_w16, (ssmw,), (False,), min(256, cl), "inproj_ctx",
                             wcol=poolw // ssmw)

    pool_out = _pool_call(u_pool, pool_w[0], pool_scale[0])

    ncl, ncc = seq // t, cl // t
    nsteps = max(1, (ncc + ncl - 1).bit_length())
    params = _s5_params(s5_lam_re[0], s5_lam_im[0], s5_log_dt[0], s5_b_re[0], s5_b_im[0], s5_c_re[0], s5_c_im[0],
                        t, nsteps)
    ut = _pack_call(u_ssm, uc_ssm, t)
    yt = _s5_call(ut, params, t, bsz, ncl, ncc, nsteps, n_state, hc)
    y_ssm = _unpack_call(yt, bsz, seq)

    return _merge_call(y_ssm, u_ssm, pool_out, zs, x, gate[:bsz, None], s5_d[0], glu_w[0], glu_b[0], out_w[0],
                       final_g, tm)
```

```python
import functools
import math

import jax
import jax.numpy as jnp
import numpy as np
from jax import lax
from jax.experimental import pallas as pl
from jax.experimental.pallas import tpu as pltpu

GRID_W = 64
POOL_WINDOWS = (2, 4, 8, 16)
EPS = 1e-6
CHUNK_T = 32
S5_GROUPS_PER_STEP = 4
LANES = 128
SUBLANES = 8
VMEM_LIMIT = 56 * 1024 * 1024

F32 = jnp.float32
BF16 = jnp.bfloat16
TN_DIMS = (((0,), (0,)), ((), ()))
NT_DIMS = (((1,), (1,)), ((), ()))


def _cparams(sem):
    return pltpu.CompilerParams(dimension_semantics=sem, vmem_limit_bytes=VMEM_LIMIT)


def _const_spec(shape):
    return pl.BlockSpec(shape, lambda *_: (0,) * len(shape), pipeline_mode=pl.Buffered(1))


def _sigmoid(v):
    return 1.0 / (1.0 + jnp.exp(-v))


def _gelu_tanh(v):
    return 0.5 * v * (1.0 + jnp.tanh(math.sqrt(2.0 / math.pi) * (v + 0.044715 * (v * v * v))))


def _modulated_norm(x, g, scale, shift):
    ms = jnp.mean(x * x, axis=-1, keepdims=True)
    return (x * lax.rsqrt(ms + EPS) * g) * (1.0 + scale) + shift


def _ada_kernel(ct_ref, w_ref, b_ref, o_ref, sb_ref, *, nrows):
    d = w_ref.shape[0]
    tn = w_ref.shape[1]
    ntile = tn // LANES

    @pl.when(pl.program_id(0) == 0)
    def _():
        cv = ct_ref[...]
        s = cv * _sigmoid(cv)
        for r in range(nrows):
            sb_ref[r] = jnp.broadcast_to(s[:, r:r + 1], (d, LANES))

    def body(i, accs):
        k0 = pl.multiple_of(i * SUBLANES, SUBLANES)
        sb = [sb_ref[r, pl.ds(k0, SUBLANES), :] for r in range(nrows)]
        out = []
        for lt in range(ntile):
            w = w_ref[pl.ds(k0, SUBLANES), lt * LANES:(lt + 1) * LANES]
            out.append(tuple(accs[lt][r] + sb[r] * w for r in range(nrows)))
        return tuple(out)

    init = tuple(tuple(jnp.zeros((SUBLANES, LANES), F32) for _ in range(nrows)) for _ in range(ntile))
    accs = lax.fori_loop(0, d // SUBLANES, body, init, unroll=8)
    o_ref[...] = jnp.zeros(o_ref.shape, F32)
    for lt in range(ntile):
        for r in range(nrows):
            o_ref[r:r + 1, lt * LANES:(lt + 1) * LANES] = (jnp.sum(accs[lt][r], axis=0, keepdims=True)
                                                           + b_ref[:, lt * LANES:(lt + 1) * LANES])


def _ada_call(cvecs, ada_w, ada_b):
    nrows, d = cvecs.shape
    n = ada_w.shape[1]
    tn = 1536 if n % 1536 == 0 else n
    ct = jnp.zeros((d, SUBLANES), F32).at[:, :nrows].set(cvecs.T)
    out = pl.pallas_call(
        functools.partial(_ada_kernel, nrows=nrows),
        out_shape=jax.ShapeDtypeStruct((SUBLANES, n), F32),
        grid=(n // tn,),
        in_specs=[
            pl.BlockSpec((d, SUBLANES), lambda j: (0, 0)),
            pl.BlockSpec((d, tn), lambda j: (0, j)),
            pl.BlockSpec((1, tn), lambda j: (0, j)),
        ],
        out_specs=pl.BlockSpec((SUBLANES, tn), lambda j: (0, j)),
        scratch_shapes=[pltpu.VMEM((nrows, d, LANES), F32)],
        compiler_params=_cparams(("arbitrary",)),
        name="ada",
    )(ct, ada_w, ada_b.reshape(1, n))
    return out[:nrows]


def _inproj_kernel(x_ref, sc_ref, sh_ref, g_ref, w_ref, *o_refs, nchunk, silu):
    h = _modulated_norm(x_ref[0], g_ref[...], sc_ref[0], sh_ref[0]).astype(BF16)
    col = 0
    for o_ref, gate in zip(o_refs, silu):
        width = o_ref.shape[-1]
        for n0 in range(0, width, nchunk):
            acc = jnp.dot(h, w_ref[:, col + n0:col + n0 + nchunk], preferred_element_type=F32)
            if gate:
                acc = acc * _sigmoid(acc)
            o_ref[0, :, n0:n0 + nchunk] = acc.astype(o_ref.dtype)
        col += width


def _inproj_call(x, scale, shift, g, w, widths, silu, tm, name, wcol=0):
    b, l, d = x.shape
    nout = sum(widths)
    assert w.shape[1] % nout == 0 and l % tm == 0
    nchunk = min(1024, min(widths))
    assert all(wd % nchunk == 0 for wd in widths)
    return pl.pallas_call(
        functools.partial(_inproj_kernel, nchunk=nchunk, silu=silu),
        out_shape=[jax.ShapeDtypeStruct((b, l, wd), BF16) for wd in widths],
        grid=(b, l // tm),
        in_specs=[
            pl.BlockSpec((1, tm, d), lambda i, j: (i, j, 0)),
            pl.BlockSpec((1, 1, d), lambda i, j: (i, 0, 0)),
            pl.BlockSpec((1, 1, d), lambda i, j: (i, 0, 0)),
            _const_spec((1, d)),
            pl.BlockSpec((d, nout), lambda i, j: (0, wcol), pipeline_mode=pl.Buffered(1)),
        ],
        out_specs=[pl.BlockSpec((1, tm, wd), lambda i, j: (i, j, 0)) for wd in widths],
        compiler_params=_cparams(("arbitrary", "arbitrary")),
        name=name,
    )(x, scale, shift, g, w)


def _chunk_perm(t):
    i = np.arange(SUBLANES * t)
    src = (i % SUBLANES) * t + i // SUBLANES
    return jnp.asarray(src[:, None] == i[None, :], BF16)


def _pack_kernel(u_ref, uc_ref, perm_ref, ut_ref, scr_ref, *, t, nblk):
    j = pl.program_id(0)
    s = scr_ref.shape[2]
    rows = SUBLANES * t
    ctok = uc_ref.shape[0]

    def permute(src, blk):
        pm = jnp.dot(perm_ref[...], src, preferred_element_type=F32)
        for ti in range(t):
            scr_ref[ti, pl.ds(pl.multiple_of(blk * SUBLANES, SUBLANES), SUBLANES), :] = pm[ti * SUBLANES:(ti + 1) * SUBLANES]

    @pl.when(j < nblk)
    def _():
        def body(blk, carry):
            permute(u_ref[pl.ds(pl.multiple_of(blk * rows, rows), rows), :], blk)
            return carry

        lax.fori_loop(0, LANES // SUBLANES, body, 0, unroll=2)

    @pl.when(j == nblk)
    def _():
        for blk in range(ctok // rows):
            permute(uc_ref[blk * rows:(blk + 1) * rows, :], blk)
        scr_ref[:, ctok // t:LANES, :] = jnp.zeros((t, LANES - ctok // t, s), F32)

    def transpose(ti, carry):
        for k in range(s // LANES):
            ut_ref[ti, LANES * k:LANES * (k + 1), :] = scr_ref[ti, :, LANES * k:LANES * (k + 1)].T.astype(BF16)
        return carry

    lax.fori_loop(0, t, transpose, 0, unroll=2)


def _pack_call(u_ssm, uc_ssm, t):
    b, l, s = u_ssm.shape
    tok = LANES * t
    rows = SUBLANES * t
    ntok = b * l
    ctok = b * uc_ssm.shape[1]
    assert ntok % tok == 0 and ctok <= tok and ctok % rows == 0 and s % LANES == 0
    nblk = ntok // tok
    return pl.pallas_call(
        functools.partial(_pack_kernel, t=t, nblk=nblk),
        out_shape=jax.ShapeDtypeStruct((t, s, ntok // t + LANES), BF16),
        grid=(nblk + 1,),
        in_specs=[pl.BlockSpec((tok, s), lambda j: (jnp.minimum(j, nblk - 1), 0)),
                  _const_spec((ctok, s)), _const_spec((rows, rows))],
        out_specs=pl.BlockSpec((t, s, LANES), lambda j: (0, 0, j)),
        scratch_shapes=[pltpu.VMEM((t, LANES, s), F32)],
        compiler_params=_cparams(("arbitrary",)),
        name="pack",
    )(u_ssm.reshape(ntok, s), uc_ssm.reshape(ctok, s), _chunk_perm(t))


def _unpack_kernel(yt_ref, perm_ref, y_ref, scr_ref, *, t):
    s = scr_ref.shape[2]
    rows = SUBLANES * t

    def transpose(ti, carry):
        for k in range(s // LANES):
            scr_ref[ti, :, LANES * k:LANES * (k + 1)] = yt_ref[ti, LANES * k:LANES * (k + 1), :].astype(F32).T
        return carry

    lax.fori_loop(0, t, transpose, 0, unroll=2)

    def body(blk, carry):
        c0 = pl.multiple_of(blk * SUBLANES, SUBLANES)
        src = jnp.concatenate([scr_ref[ti, pl.ds(c0, SUBLANES), :] for ti in range(t)], axis=0).astype(BF16)
        out = jnp.dot(perm_ref[...], src, preferred_element_type=F32)
        y_ref[pl.ds(pl.multiple_of(blk * rows, rows), rows), :] = out.astype(y_ref.dtype)
        return carry

    lax.fori_loop(0, LANES // SUBLANES, body, 0, unroll=2)


def _unpack_call(yt, b, l):
    t, s, nl = yt.shape
    tok = LANES * t
    rows = SUBLANES * t
    y = pl.pallas_call(
        functools.partial(_unpack_kernel, t=t),
        out_shape=jax.ShapeDtypeStruct((nl * t, s), BF16),
        grid=(nl // LANES,),
        in_specs=[pl.BlockSpec((t, s, LANES), lambda j: (0, 0, j)), _const_spec((rows, rows))],
        out_specs=pl.BlockSpec((tok, s), lambda j: (j, 0)),
        scratch_shapes=[pltpu.VMEM((t, LANES, s), F32)],
        compiler_params=_cparams(("arbitrary",)),
        name="unpack",
    )(yt, _chunk_perm(t).T)
    return y.reshape(b, l, s)


POOL_BLOCK_ROWS = 8
POOL_BLOCKS_PER_STEP = 4


def _pool_kernel(u_ref, *refs, tb, pad):
    nwin = len(POOL_WINDOWS)
    band_refs = refs[:nwin]
    inv_ref, pw_ref, ps_ref, o_ref, pad_ref = refs[nwin:]
    grp = pl.program_id(1)
    l = u_ref.shape[1]
    c = u_ref.shape[2]
    nblk = l // tb
    pad_ref[0:pad, :] = jnp.zeros((pad, c), pad_ref.dtype)
    pad_ref[pad + l:pad + l + pad, :] = jnp.zeros((pad, c), pad_ref.dtype)

    def copy(i, carry):
        r0 = pl.multiple_of(i * tb, tb)
        pad_ref[pl.ds(pl.multiple_of(pad + r0, GRID_W), tb), :] = u_ref[0, pl.ds(r0, tb), :]
        return carry

    lax.fori_loop(0, nblk, copy, 0)

    for gi, w in enumerate(POOL_WINDOWS):
        half = w // 2
        span = band_refs[gi].shape[1]

        @pl.when(grp == gi)
        def _(half=half, span=span, band_ref=band_refs[gi]):
            def blk(i, carry):
                first = POOL_BLOCKS_PER_STEP * i
                r0 = pl.multiple_of(first * tb, POOL_BLOCKS_PER_STEP * tb)
                wins = [pad_ref[pl.ds(pl.multiple_of(pad + r0 + k * tb - half * GRID_W, GRID_W), span), :]
                        for k in range(POOL_BLOCKS_PER_STEP)]
                box = jnp.dot(band_ref[...], jnp.concatenate(wins, axis=1), preferred_element_type=F32)
                dlts = []
                for k in range(POOL_BLOCKS_PER_STEP):
                    j = first + k
                    variant = jnp.where(j == 0, 0, jnp.where(j == nblk - 1, 2, 1))
                    inv = jnp.concatenate([inv_ref[0, variant]] * (c // LANES), axis=1)
                    u_blk = u_ref[0, pl.ds(r0 + k * tb, tb), :].astype(F32)
                    dlts.append((box[:, k * c:(k + 1) * c] * inv - u_blk).astype(BF16))
                o = jnp.dot(jnp.concatenate(dlts, axis=0), pw_ref[0], preferred_element_type=F32) * ps_ref[...]
                o_ref[0, pl.ds(r0, POOL_BLOCKS_PER_STEP * tb), :] = o.astype(o_ref.dtype)
                return carry

            lax.fori_loop(0, nblk // POOL_BLOCKS_PER_STEP, blk, 0)


def _pool_tables(rows, tb):
    br = tb // GRID_W
    i = np.arange(tb)
    ri, ci = i // GRID_W, i % GRID_W
    bands, invs = [], []
    for w in POOL_WINDOWS:
        half = w // 2
        j = np.arange((br + w) * GRID_W)
        dr = (j // GRID_W)[None, :] - ri[:, None]
        dc = (j % GRID_W)[None, :] - ci[:, None]
        bands.append(jnp.asarray((dr >= 0) & (dr < w) & (dc >= -half) & (dc < w - half), BF16))
        cnt_c = np.minimum(ci + w - half, GRID_W) - np.maximum(ci - half, 0)
        per_variant = []
        for r_first in (0, br, rows - br):
            r = r_first + ri
            cnt_r = np.minimum(r + w - half, rows) - np.maximum(r - half, 0)
            per_variant.append(np.float32(1.0) / (cnt_r * cnt_c).astype(np.float32))
        invs.append(np.stack(per_variant))
    inv = np.broadcast_to(np.stack(invs)[..., None], (len(POOL_WINDOWS), 3, tb, LANES))
    return bands, jnp.asarray(inv, F32)


def _pool_call(u_pool, pool_w, pool_scale):
    b, l, p = u_pool.shape
    ng = len(POOL_WINDOWS)
    c = p // ng
    rows = l // GRID_W
    tb = POOL_BLOCK_ROWS * GRID_W
    pad = (max(POOL_WINDOWS) // 2) * GRID_W
    assert l % (tb * POOL_BLOCKS_PER_STEP) == 0 and c % LANES == 0
    assert rows >= 3 * POOL_BLOCK_ROWS and max(POOL_WINDOWS) <= 2 * POOL_BLOCK_ROWS
    bands, inv = _pool_tables(rows, tb)
    return pl.pallas_call(
        functools.partial(_pool_kernel, tb=tb, pad=pad),
        out_shape=jax.ShapeDtypeStruct((b, l, p), BF16),
        grid=(b, ng),
        in_specs=[pl.BlockSpec((1, l, c), lambda i, j: (i, 0, j))]
        + [_const_spec(bd.shape) for bd in bands]
        + [
            pl.BlockSpec((1, 3, tb, LANES), lambda i, j: (j, 0, 0, 0)),
            pl.BlockSpec((1, c, c), lambda i, j: (j, 0, 0)),
            pl.BlockSpec((1, c), lambda i, j: (0, j)),
        ],
        out_specs=pl.BlockSpec((1, l, c), lambda i, j: (i, 0, j)),
        scratch_shapes=[pltpu.VMEM((pad + l + pad, c), BF16)],
        compiler_params=_cparams(("arbitrary", "arbitrary")),
        name="pool",
    )(u_pool, *bands, inv, pool_w.astype(BF16), pool_scale.reshape(1, p))


def _dot_hi_nt(a, b):
    a_hi = a.astype(BF16)
    a_lo = (a - a_hi.astype(F32)).astype(BF16)
    b_hi = b.astype(BF16)
    b_lo = (b - b_hi.astype(F32)).astype(BF16)
    dot = lambda p, q: lax.dot_general(p, q, NT_DIMS, preferred_element_type=F32)
    return dot(a_hi, b_hi) + dot(a_hi, b_lo) + dot(a_lo, b_hi)


def _s5_kernel(u_ref, ea_ref, eb_ref, ba_ref, bm_ref, cc_ref, ca_ref, cb_ref, oa_ref, ob_ref, ap_ref, a8_ref, y_ref,
               et_ref, win_ref, mt_ref, wout_ref, buf_ref, cbuf_ref, hin_ref, *, gps, hc, **static):
    @pl.when(pl.program_id(0) == 0)
    def _():
        buf_ref[...] = jnp.zeros(buf_ref.shape, F32)
        cbuf_ref[...] = jnp.zeros(cbuf_ref.shape, F32)

    for gg in range(gps):
        rows = slice(gg * hc, (gg + 1) * hc)
        _s5_group(u_ref.at[:, rows, :], ea_ref.at[gg], eb_ref.at[gg], ba_ref.at[gg], bm_ref.at[gg], cc_ref.at[gg],
                  ca_ref.at[gg], cb_ref.at[gg], oa_ref.at[gg], ob_ref.at[gg], ap_ref.at[gg], a8_ref.at[gg],
                  y_ref.at[:, rows, :], et_ref.at[gg], win_ref.at[gg], mt_ref.at[gg], wout_ref.at[gg], buf_ref.at[gg],
                  cbuf_ref.at[gg], hin_ref.at[gg],
                  hc=hc, **static)


def _s5_group(u_ref, ea_ref, eb_ref, ba_ref, bm_ref, cc_ref, ca_ref, cb_ref, oa_ref, ob_ref, ap_ref, a8_ref, y_ref,
              et_ref, win_ref, mt_ref, wout_ref, buf_ref, cbuf_ref, hin_ref, *, t, hc, nb, ncl, ncc, pad, nsteps, nstate):
    n = nstate
    th = t * hc
    nl = nb * ncl
    rb = ncc + ncl
    ntile = th // LANES

    ba = ba_ref[...]
    bm = bm_ref[...]
    ca = ca_ref[...]
    cb = cb_ref[...]
    for ti in range(t):
        rows = slice(ti * hc, (ti + 1) * hc)
        e_rows = ba * ea_ref[ti:ti + 1, :] + bm * eb_ref[ti:ti + 1, :]
        et_ref[rows, :] = e_rows
        win_ref[rows, :] = e_rows.astype(BF16)
        wout_ref[rows, :] = (ca * oa_ref[ti:ti + 1, :] + cb * ob_ref[ti:ti + 1, :]).astype(BF16)
    cc = cc_ref[...]
    q_f = _dot_hi_nt(cc[0:hc], et_ref[:, 0:2 * n])
    q_b = _dot_hi_nt(cc[hc:2 * hc], et_ref[:, 2 * n:4 * n])
    zeros = jnp.zeros((hc, th), F32)
    line_b = jnp.concatenate([zeros, q_b], axis=1)
    line_f = jnp.concatenate([q_f, zeros], axis=1)
    per_tile = LANES // hc
    rolled_b = [line_b if r == 0 else pltpu.roll(line_b, hc * r, axis=1) for r in range(per_tile)]
    rolled_f = [line_f if r == per_tile - 1 else pltpu.roll(line_f, 2 * th - hc * (per_tile - 1 - r), axis=1)
                for r in range(per_tile)]
    for ti in range(t):
        a, r = divmod(ti, per_tile)
        row_b = rolled_b[r][:, th - LANES * a:2 * th - LANES * a]
        off_f = LANES * (ntile - 1 - a)
        row_f = rolled_f[r][:, off_f:off_f + th]
        mt_ref[ti * hc:(ti + 1) * hc, :] = (row_b + row_f).astype(BF16)

    u = jnp.concatenate([u_ref[ti] for ti in range(t)], axis=0)
    st_all = lax.dot_general(win_ref[...], u, TN_DIMS, preferred_element_type=F32)
    first_half = lax.broadcasted_iota(jnp.int32, (LANES, LANES), 1) < n
    for q in range(4):
        rows_q = st_all[q * n:(q + 1) * n]
        lat = jnp.concatenate([rows_q[:, b * ncl:(b + 1) * ncl] for b in range(nb)], axis=0).T
        cblk = jnp.concatenate([rows_q[:, nl:nl + LANES]] * nb, axis=0).T
        ctx = jnp.where(first_half, cblk, jnp.concatenate([cblk[ncc:], cblk[:ncc]], axis=0))[0:ncc]
        if q < 2:
            buf_ref[q, pad:pad + ncc, :] = ctx
            buf_ref[q, pad + ncc:pad + rb, :] = lat
        else:
            buf_ref[q, pad:pad + ncl, :] = lat
            buf_ref[q, pad + ncl:pad + rb, :] = ctx
    def step(ref, d, k, sgn, lo, rows, keep=None):
        ar = ap_ref[4 * k + 2 * d:4 * k + 2 * d + 1, :]
        ai = ap_ref[4 * k + 2 * d + 1:4 * k + 2 * d + 2, :]
        pr = ref[2 * d, lo + sgn:lo + sgn + rows, :]
        pi = ref[2 * d + 1, lo + sgn:lo + sgn + rows, :]
        if keep is not None:
            pr = jnp.where(keep, pr, 0.0)
            pi = jnp.where(keep, pi, 0.0)
        xr = ref[2 * d, lo:lo + rows, :]
        xi = ref[2 * d + 1, lo:lo + rows, :]
        ref[2 * d, lo:lo + rows, :] = xr + (ar * pr - ai * pi)
        ref[2 * d + 1, lo:lo + rows, :] = xi + (ar * pi + ai * pr)

    rbp = buf_ref.shape[1] - 2 * pad
    nv = rbp // SUBLANES
    cpad = (cbuf_ref.shape[1] - nv) // 2
    in_block = lax.broadcasted_iota(jnp.int32, (rbp, LANES), 0) % SUBLANES
    nlocal = SUBLANES.bit_length() - 1
    for k in range(nlocal):
        st = 1 << k
        step(buf_ref, 0, k, -st, pad, rbp, in_block >= st)
        step(buf_ref, 1, k, st, pad, rbp, in_block < SUBLANES - st)
    for d in range(2):
        edge = SUBLANES - 1 if d == 0 else 0
        for q in (2 * d, 2 * d + 1):
            cbuf_ref[q, cpad:cpad + nv, :] = buf_ref[q, pl.ds(pad + edge, nv, stride=SUBLANES), :]
    for m in range(max(0, (nv - 1).bit_length())):
        st = 1 << m
        step(cbuf_ref, 0, nlocal + m, -st, cpad, nv)
        step(cbuf_ref, 1, nlocal + m, st, cpad, nv)
    for d in range(2):
        a8r = a8_ref[2 * SUBLANES * d:2 * SUBLANES * d + SUBLANES, :]
        a8i = a8_ref[2 * SUBLANES * d + SUBLANES:2 * SUBLANES * (d + 1), :]
        for j in range(nv):
            src = j - 1 if d == 0 else j + 1
            if 0 <= src < nv:
                cr = jnp.broadcast_to(cbuf_ref[2 * d, cpad + src:cpad + src + 1, :], (SUBLANES, LANES))
                ci = jnp.broadcast_to(cbuf_ref[2 * d + 1, cpad + src:cpad + src + 1, :], (SUBLANES, LANES))
                rows = slice(pad + j * SUBLANES, pad + (j + 1) * SUBLANES)
                buf_ref[2 * d, rows, :] = buf_ref[2 * d, rows, :] + (a8r * cr - a8i * ci)
                buf_ref[2 * d + 1, rows, :] = buf_ref[2 * d + 1, rows, :] + (a8r * ci + a8i * cr)
    for q in range(4):
        lo = pad + ncc - 1 if q < 2 else pad + 1
        ent = buf_ref[q, lo:lo + ncl, :].T.astype(BF16)
        for b in range(nb):
            hin_ref[q * n:(q + 1) * n, b * ncl:(b + 1) * ncl] = ent[b * n:(b + 1) * n]

    y = jnp.dot(mt_ref[...], u[:, 0:nl], preferred_element_type=F32)
    y = y + jnp.dot(wout_ref[...], hin_ref[...], preferred_element_type=F32)
    for ti in range(t):
        y_ref[ti] = y[ti * hc:(ti + 1) * hc].astype(y_ref.dtype)


def _s5_params(lam_re, lam_im, log_dt, b_re, b_im, c_re, c_im, t, nsteps):
    _, g, n = lam_re.shape
    h = b_re.shape[-1]
    lam_re, lam_im = lam_re.astype(F32), lam_im.astype(F32)
    dt = jnp.exp(log_dt.astype(F32))[..., None]
    lr, li = lam_re * dt, lam_im * dt

    def power(d, k):
        kk = k[None, :, None]
        mag = jnp.exp(lr[d][:, None, :] * kk)
        return mag * jnp.cos(li[d][:, None, :] * kk), mag * jnp.sin(li[d][:, None, :] * kk)

    a_re, a_im = jnp.exp(lr) * jnp.cos(li), jnp.exp(lr) * jnp.sin(li)
    den = lam_re * lam_re + lam_im * lam_im
    f_re = ((a_re - 1.0) * lam_re + a_im * lam_im) / den
    f_im = (a_im * lam_re - (a_re - 1.0) * lam_im) / den
    bt_re = (f_re[..., None] * b_re - f_im[..., None] * b_im).transpose(0, 1, 3, 2)
    bt_im = (f_re[..., None] * b_im + f_im[..., None] * b_re).transpose(0, 1, 3, 2)
    lanes4 = lambda f0, f1, b0, b1: jnp.concatenate([f0, f1, b0, b1], axis=-1)

    strides = t * (2.0 ** jnp.arange(nsteps, dtype=F32))
    exps = jnp.concatenate([jnp.arange(t + 1, dtype=F32), strides, t * jnp.arange(1, SUBLANES + 1, dtype=F32)])
    (allf_re, allf_im), (allb_re, allb_im) = power(0, exps), power(1, exps)
    pf_re, pf_im = allf_re[:, t - 1::-1], allf_im[:, t - 1::-1]
    pb_re, pb_im = allb_re[:, 0:t], allb_im[:, 0:t]
    ea = lanes4(pf_re, pf_re, pb_re, pb_re)
    eb = lanes4(pf_im, pf_im, pb_im, pb_im)
    ba = lanes4(bt_re[0], bt_im[0], bt_re[1], bt_im[1])
    bm = lanes4(-bt_im[0], bt_re[0], -bt_im[1], bt_re[1])
    cc = jnp.concatenate([jnp.concatenate([c_re[0], -c_im[0]], axis=-1),
                          jnp.concatenate([c_re[1], -c_im[1]], axis=-1)], axis=1)
    rf_re, rf_im = allf_re[:, 1:t + 1], allf_im[:, 1:t + 1]
    rb_re, rb_im = allb_re[:, t:0:-1], allb_im[:, t:0:-1]
    oa = lanes4(rf_re, rf_im, rb_re, rb_im)
    ob = lanes4(rf_im, rf_re, rb_im, rb_re)
    ca = lanes4(c_re[0], -c_re[0], c_re[1], -c_re[1])
    cb = lanes4(-c_im[0], -c_im[0], -c_im[1], -c_im[1])
    st0, st1 = t + 1, t + 1 + nsteps
    apf_re, apf_im, apb_re, apb_im = (v[:, st0:st1] for v in (allf_re, allf_im, allb_re, allb_im))
    ap = jnp.stack([apf_re, apf_im, apb_re, apb_im], axis=2).reshape(g, 4 * nsteps, n)
    ap = jnp.concatenate([ap] * (LANES // n), axis=-1)
    rows = -(-4 * nsteps // SUBLANES) * SUBLANES
    ap = jnp.pad(ap, ((0, 0), (0, rows - 4 * nsteps), (0, 0)))
    f8_re, f8_im = allf_re[:, st1:], allf_im[:, st1:]
    b8_re, b8_im = allb_re[:, :st1 - 1:-1], allb_im[:, :st1 - 1:-1]
    a8 = jnp.concatenate([f8_re, f8_im, b8_re, b8_im], axis=1)
    a8 = jnp.concatenate([a8] * (LANES // n), axis=-1)
    return tuple(v.astype(F32) for v in (ea, eb, ba, bm, cc, ca, cb, oa, ob, ap, a8))


def _s5_call(ut, params, t, nb, ncl, ncc, nsteps, n_state, hc):
    ap = params[-2]
    _, sw, width = ut.shape
    nl = nb * ncl
    g = sw // hc
    th = t * hc
    rbp = -(-(ncc + ncl) // SUBLANES) * SUBLANES
    nv = rbp // SUBLANES
    pad = SUBLANES
    cpad = max(SUBLANES, 1 << max(0, (nv - 1).bit_length() - 1))
    assert nb * n_state == LANES and nb == 2 and 4 * nsteps <= ap.shape[1]
    gps = S5_GROUPS_PER_STEP
    assert g % gps == 0
    grp = lambda shp: pl.BlockSpec((gps,) + shp, lambda i: (i, 0, 0))
    return pl.pallas_call(
        functools.partial(_s5_kernel, gps=gps, t=t, hc=hc, nb=nb, ncl=ncl, ncc=ncc, pad=pad, nsteps=nsteps,
                          nstate=n_state),
        out_shape=jax.ShapeDtypeStruct((t, sw, nl), BF16),
        grid=(g // gps,),
        in_specs=[
            pl.BlockSpec((t, gps * hc, width), lambda i: (0, i, 0)),
            grp((t, 4 * n_state)), grp((t, 4 * n_state)), grp((hc, 4 * n_state)), grp((hc, 4 * n_state)),
            grp((2 * hc, 2 * n_state)), grp((hc, 4 * n_state)), grp((hc, 4 * n_state)),
            grp((t, 4 * n_state)), grp((t, 4 * n_state)), grp((ap.shape[1], LANES)), grp((4 * SUBLANES, LANES)),
        ],
        out_specs=pl.BlockSpec((t, gps * hc, nl), lambda i: (0, i, 0)),
        scratch_shapes=[pltpu.VMEM((gps, th, 4 * n_state), F32), pltpu.VMEM((gps, th, 4 * n_state), BF16),
                        pltpu.VMEM((gps, th, th), BF16), pltpu.VMEM((gps, th, 4 * n_state), BF16),
                        pltpu.VMEM((gps, 4, pad + rbp + pad, LANES), F32),
                        pltpu.VMEM((gps, 4, cpad + nv + cpad, LANES), F32),
                        pltpu.VMEM((gps, 4 * n_state, nl), BF16)],
        compiler_params=_cparams(("arbitrary",)),
        name="s5",
    )(ut, *params)


def _merge_kernel(y_ref, u_ref, p_ref, zs_ref, x_ref, gate_ref, d_ref, gw_ref, gb_ref, ow_ref, fg_ref, o_ref):
    sw = y_ref.shape[-1]
    pw = p_ref.shape[-1]
    y = _gelu_tanh(y_ref[0].astype(F32) + d_ref[...] * u_ref[0].astype(F32)).astype(BF16)
    yy = jnp.dot(y, gw_ref[...], preferred_element_type=F32) + gb_ref[...]
    ssm_out = yy[:, :sw] * _sigmoid(yy[:, sw:])
    br_pool = p_ref[0] * zs_ref[0, :, 0:pw]
    br_ssm = (ssm_out * zs_ref[0, :, pw:pw + sw].astype(F32)).astype(BF16)
    mix = jnp.dot(jnp.concatenate([br_pool, br_ssm], axis=1), ow_ref[...], preferred_element_type=F32)
    xo = x_ref[0] + gate_ref[0] * mix
    ms = jnp.mean(xo * xo, axis=-1, keepdims=True)
    o_ref[0] = (xo * lax.rsqrt(ms + EPS) * fg_ref[...]).astype(o_ref.dtype)


def _merge_call(y_ssm, u_ssm, pool_out, zs, x, gate, d_skip, glu_w, glu_b, out_w, final_g, tm):
    b, l, d = x.shape
    sw = y_ssm.shape[-1]
    pw = pool_out.shape[-1]
    mixw = zs.shape[-1]
    tok = lambda wd: pl.BlockSpec((1, tm, wd), lambda i, j: (i, j, 0))
    return pl.pallas_call(
        _merge_kernel,
        out_shape=jax.ShapeDtypeStruct((b, l, d), x.dtype),
        grid=(b, l // tm),
        in_specs=[
            tok(sw), tok(sw), tok(pw), tok(mixw), tok(d),
            pl.BlockSpec((1, 1, d), lambda i, j: (i, 0, 0)),
            _const_spec((1, sw)), _const_spec((sw, 2 * sw)), _const_spec((1, 2 * sw)), _const_spec((mixw, d)),
            _const_spec((1, d)),
        ],
        out_specs=tok(d),
        compiler_params=_cparams(("arbitrary", "arbitrary")),
        name="merge",
    )(y_ssm, u_ssm, pool_out, zs, x, gate, d_skip.reshape(1, sw), glu_w.astype(BF16), glu_b.reshape(1, 2 * sw),
      out_w.astype(BF16), final_g.reshape(1, d))


def kernel(x, c, ctx, c_ctx, ada_w, ada_b, norm_g, in_w, pool_w, pool_scale, s5_lam_re, s5_lam_im, s5_log_dt,
           s5_b_re, s5_b_im, s5_c_re, s5_c_im, s5_d, glu_w, glu_b, out_w, final_g):
    assert ada_w.shape[0] == 1, "single-layer block"
    bsz, seq, d = x.shape
    cl = ctx.shape[1]
    mixw = in_w.shape[2] // 2
    poolw = pool_scale.shape[-1]
    ssmw = s5_d.shape[-1]
    n_grp, n_state = s5_lam_re.shape[2], s5_lam_re.shape[3]
    hc = ssmw // n_grp
    t = CHUNK_T
    assert poolw + ssmw == mixw and seq % t == 0 and cl % t == 0 and 2 * n_state == LANES and LANES % hc == 0

    mod = _ada_call(jnp.concatenate([c, c_ctx[None]], axis=0), ada_w[0], ada_b[0])
    shift, scale, gate = mod[:, :d], mod[:, d:2 * d], mod[:, 2 * d:]
    g1 = norm_g[0].reshape(1, d)
    in_w16 = in_w[0].astype(BF16)

    tm = min(512, seq)
    u_pool, u_ssm, zs = _inproj_call(x, scale[:bsz, None], shift[:bsz, None], g1, in_w16,
                                     (poolw, ssmw, mixw), (False, False, True), min(1024, seq), "inproj")
    sc_c = jnp.broadcast_to(scale[bsz][None, None], (bsz, 1, d))
    sh_c = jnp.broadcast_to(shift[bsz][None, None], (bsz, 1, d))
    assert poolw % ssmw == 0
    (uc_ssm,) = _inproj_call(ctx, sc_c, sh_c, g1, in_w16, (ssmw,), (False,), min(256, cl), "inproj_ctx",
                             wcol=poolw // ssmw)

    pool_out = _pool_call(u_pool, pool_w[0], pool_scale[0])

    ncl, ncc = seq // t, cl // t
    nsteps = max(1, (ncc + ncl - 1).bit_length())
    params = _s5_params(s5_lam_re[0], s5_lam_im[0], s5_log_dt[0], s5_b_re[0], s5_b_im[0], s5_c_re[0], s5_c_im[0],
                        t, nsteps)
    ut = _pack_call(u_ssm, uc_ssm, t)
    yt = _s5_call(ut, params, t, bsz, ncl, ncc, nsteps, n_state, hc)
    y_ssm = _unpack_call(yt, bsz, seq)

    return _merge_call(y_ssm, u_ssm, pool_out, zs, x, gate[:bsz, None], s5_d[0], glu_w[0], glu_b[0], out_w[0],
                       final_g, tm)
```

```python
import functools
import math

import jax
import jax.numpy as jnp
import numpy as np
from jax import lax
from jax.experimental import pallas as pl
from jax.experimental.pallas import tpu as pltpu

GRID_W = 64
POOL_WINDOWS = (2, 4, 8, 16)
EPS = 1e-6
CHUNK_T = 32
S5_GROUPS_PER_STEP = 4
LANES = 128
SUBLANES = 8
VMEM_LIMIT = 56 * 1024 * 1024

F32 = jnp.float32
BF16 = jnp.bfloat16
TN_DIMS = (((0,), (0,)), ((), ()))
NT_DIMS = (((1,), (1,)), ((), ()))


def _cparams(sem):
    return pltpu.CompilerParams(dimension_semantics=sem, vmem_limit_bytes=VMEM_LIMIT)


def _const_spec(shape):
    return pl.BlockSpec(shape, lambda *_: (0,) * len(shape), pipeline_mode=pl.Buffered(1))


def _sigmoid(v):
    return 1.0 / (1.0 + jnp.exp(-v))


def _gelu_tanh(v):
    return 0.5 * v * (1.0 + jnp.tanh(math.sqrt(2.0 / math.pi) * (v + 0.044715 * (v * v * v))))


def _modulated_norm(x, g, scale, shift):
    ms = jnp.mean(x * x, axis=-1, keepdims=True)
    return (x * lax.rsqrt(ms + EPS) * g) * (1.0 + scale) + shift


def _ada_kernel(ct_ref, w_ref, b_ref, o_ref, sb_ref, *, nrows):
    d = w_ref.shape[0]
    tn = w_ref.shape[1]
    ntile = tn // LANES

    @pl.when(pl.program_id(0) == 0)
    def _():
        cv = ct_ref[...]
        s = cv * _sigmoid(cv)
        for r in range(nrows):
            sb_ref[r] = jnp.broadcast_to(s[:, r:r + 1], (d, LANES))

    def body(i, accs):
        k0 = pl.multiple_of(i * SUBLANES, SUBLANES)
        sb = [sb_ref[r, pl.ds(k0, SUBLANES), :] for r in range(nrows)]
        out = []
        for lt in range(ntile):
            w = w_ref[pl.ds(k0, SUBLANES), lt * LANES:(lt + 1) * LANES]
            out.append(tuple(accs[lt][r] + sb[r] * w for r in range(nrows)))
        return tuple(out)

    init = tuple(tuple(jnp.zeros((SUBLANES, LANES), F32) for _ in range(nrows)) for _ in range(ntile))
    accs = lax.fori_loop(0, d // SUBLANES, body, init, unroll=8)
    o_ref[...] = jnp.zeros(o_ref.shape, F32)
    for lt in range(ntile):
        for r in range(nrows):
            o_ref[r:r + 1, lt * LANES:(lt + 1) * LANES] = (jnp.sum(accs[lt][r], axis=0, keepdims=True)
                                                           + b_ref[:, lt * LANES:(lt + 1) * LANES])


def _ada_call(cvecs, ada_w, ada_b):
    nrows, d = cvecs.shape
    n = ada_w.shape[1]
    tn = 1536 if n % 1536 == 0 else n
    ct = jnp.zeros((d, SUBLANES), F32).at[:, :nrows].set(cvecs.T)
    out = pl.pallas_call(
        functools.partial(_ada_kernel, nrows=nrows),
        out_shape=jax.ShapeDtypeStruct((SUBLANES, n), F32),
        grid=(n // tn,),
        in_specs=[
            pl.BlockSpec((d, SUBLANES), lambda j: (0, 0)),
            pl.BlockSpec((d, tn), lambda j: (0, j)),
            pl.BlockSpec((1, tn), lambda j: (0, j)),
        ],
        out_specs=pl.BlockSpec((SUBLANES, tn), lambda j: (0, j)),
        scratch_shapes=[pltpu.VMEM((nrows, d, LANES), F32)],
        compiler_params=_cparams(("arbitrary",)),
        name="ada",
    )(ct, ada_w, ada_b.reshape(1, n))
    return out[:nrows]


def _inproj_kernel(x_ref, sc_ref, sh_ref, g_ref, w_ref, *o_refs, nchunk, silu):
    h = _modulated_norm(x_ref[0], g_ref[...], sc_ref[0], sh_ref[0]).astype(BF16)
    col = 0
    for o_ref, gate in zip(o_refs, silu):
        width = o_ref.shape[-1]
        for n0 in range(0, width, nchunk):
            acc = jnp.dot(h, w_ref[:, col + n0:col + n0 + nchunk], preferred_element_type=F32)
            if gate:
                acc = acc * _sigmoid(acc)
            o_ref[0, :, n0:n0 + nchunk] = acc.astype(o_ref.dtype)
        col += width


def _inproj_call(x, scale, shift, g, w, widths, silu, tm, name, wcol=0):
    b, l, d = x.shape
    nout = sum(widths)
    assert w.shape[1] % nout == 0 and l % tm == 0
    nchunk = min(1024, min(widths))
    assert all(wd % nchunk == 0 for wd in widths)
    return pl.pallas_call(
        functools.partial(_inproj_kernel, nchunk=nchunk, silu=silu),
        out_shape=[jax.ShapeDtypeStruct((b, l, wd), BF16) for wd in widths],
        grid=(b, l // tm),
        in_specs=[
            pl.BlockSpec((1, tm, d), lambda i, j: (i, j, 0)),
            pl.BlockSpec((1, 1, d), lambda i, j: (i, 0, 0)),
            pl.BlockSpec((1, 1, d), lambda i, j: (i, 0, 0)),
            _const_spec((1, d)),
            pl.BlockSpec((d, nout), lambda i, j: (0, wcol), pipeline_mode=pl.Buffered(1)),
        ],
        out_specs=[pl.BlockSpec((1, tm, wd), lambda i, j: (i, j, 0)) for wd in widths],
        compiler_params=_cparams(("arbitrary", "arbitrary")),
        name=name,
    )(x, scale, shift, g, w)


def _chunk_perm(t):
    i = np.arange(SUBLANES * t)
    src = (i % SUBLANES) * t + i // SUBLANES
    return jnp.asarray(src[:, None] == i[None, :], BF16)


def _pack_kernel(u_ref, uc_ref, perm_ref, ut_ref, scr_ref, *, t, nblk):
    j = pl.program_id(0)
    s = scr_ref.shape[2]
    rows = SUBLANES * t
    ctok = uc_ref.shape[0]

    def permute(src, blk):
        pm = jnp.dot(perm_ref[...], src, preferred_element_type=F32)
        for ti in range(t):
            scr_ref[ti, pl.ds(pl.multiple_of(blk * SUBLANES, SUBLANES), SUBLANES), :] = pm[ti * SUBLANES:(ti + 1) * SUBLANES]

    @pl.when(j < nblk)
    def _():
        def body(blk, carry):
            permute(u_ref[pl.ds(pl.multiple_of(blk * rows, rows), rows), :], blk)
            return carry

        lax.fori_loop(0, LANES // SUBLANES, body, 0, unroll=2)

    @pl.when(j == nblk)
    def _():
        for blk in range(ctok // rows):
            permute(uc_ref[blk * rows:(blk + 1) * rows, :], blk)
        scr_ref[:, ctok // t:LANES, :] = jnp.zeros((t, LANES - ctok // t, s), F32)

    def transpose(ti, carry):
        for k in range(s // LANES):
            ut_ref[ti, LANES * k:LANES * (k + 1), :] = scr_ref[ti, :, LANES * k:LANES * (k + 1)].T.astype(BF16)
        return carry

    lax.fori_loop(0, t, transpose, 0, unroll=2)


def _pack_call(u_ssm, uc_ssm, t):
    b, l, s = u_ssm.shape
    tok = LANES * t
    rows = SUBLANES * t
    ntok = b * l
    ctok = b * uc_ssm.shape[1]
    assert ntok % tok == 0 and ctok <= tok and ctok % rows == 0 and s % LANES == 0
    nblk = ntok // tok
    return pl.pallas_call(
        functools.partial(_pack_kernel, t=t, nblk=nblk),
        out_shape=jax.ShapeDtypeStruct((t, s, ntok // t + LANES), BF16),
        grid=(nblk + 1,),
        in_specs=[pl.BlockSpec((tok, s), lambda j: (jnp.minimum(j, nblk - 1), 0)),
                  _const_spec((ctok, s)), _const_spec((rows, rows))],
        out_specs=pl.BlockSpec((t, s, LANES), lambda j: (0, 0, j)),
        scratch_shapes=[pltpu.VMEM((t, LANES, s), F32)],
        compiler_params=_cparams(("arbitrary",)),
        name="pack",
    )(u_ssm.reshape(ntok, s), uc_ssm.reshape(ctok, s), _chunk_perm(t))


def _unpack_kernel(yt_ref, perm_ref, y_ref, scr_ref, *, t):
    s = scr_ref.shape[2]
    rows = SUBLANES * t

    def transpose(ti, carry):
        for k in range(s // LANES):
            scr_ref[ti, :, LANES * k:LANES * (k + 1)] = yt_ref[ti, LANES * k:LANES * (k + 1), :].astype(F32).T
        return carry

    lax.fori_loop(0, t, transpose, 0, unroll=2)

    def body(blk, carry):
        c0 = pl.multiple_of(blk * SUBLANES, SUBLANES)
        src = jnp.concatenate([scr_ref[ti, pl.ds(c0, SUBLANES), :] for ti in range(t)], axis=0).astype(BF16)
        out = jnp.dot(perm_ref[...], src, preferred_element_type=F32)
        y_ref[pl.ds(pl.multiple_of(blk * rows, rows), rows), :] = out.astype(y_ref.dtype)
        return carry

    lax.fori_loop(0, LANES // SUBLANES, body, 0, unroll=2)


def _unpack_call(yt, b, l):
    t, s, nl = yt.shape
    tok = LANES * t
    rows = SUBLANES * t
    y = pl.pallas_call(
        functools.partial(_unpack_kernel, t=t),
        out_shape=jax.ShapeDtypeStruct((nl * t, s), BF16),
        grid=(nl // LANES,),
        in_specs=[pl.BlockSpec((t, s, LANES), lambda j: (0, 0, j)), _const_spec((rows, rows))],
        out_specs=pl.BlockSpec((tok, s), lambda j: (j, 0)),
        scratch_shapes=[pltpu.VMEM((t, LANES, s), F32)],
        compiler_params=_cparams(("arbitrary",)),
        name="unpack",
    )(yt, _chunk_perm(t).T)
    return y.reshape(b, l, s)


POOL_BLOCK_ROWS = 8
POOL_BLOCKS_PER_STEP = 4


def _pool_kernel(u_ref, *refs, tb, pad):
    nwin = len(POOL_WINDOWS)
    band_refs = refs[:nwin]
    inv_ref, pw_ref, ps_ref, o_ref, pad_ref = refs[nwin:]
    grp = pl.program_id(1)
    l = u_ref.shape[1]
    c = u_ref.shape[2]
    nblk = l // tb
    pad_ref[0:pad, :] = jnp.zeros((pad, c), pad_ref.dtype)
    pad_ref[pad + l:pad + l + pad, :] = jnp.zeros((pad, c), pad_ref.dtype)

    def copy(i, carry):
        r0 = pl.multiple_of(i * tb, tb)
        pad_ref[pl.ds(pl.multiple_of(pad + r0, GRID_W), tb), :] = u_ref[0, pl.ds(r0, tb), :]
        return carry

    lax.fori_loop(0, nblk, copy, 0)

    for gi, w in enumerate(POOL_WINDOWS):
        half = w // 2
        span = band_refs[gi].shape[1]

        @pl.when(grp == gi)
        def _(half=half, span=span, band_ref=band_refs[gi]):
            def blk(i, carry):
                first = POOL_BLOCKS_PER_STEP * i
                r0 = pl.multiple_of(first * tb, POOL_BLOCKS_PER_STEP * tb)
                wins = [pad_ref[pl.ds(pl.multiple_of(pad + r0 + k * tb - half * GRID_W, GRID_W), span), :]
                        for k in range(POOL_BLOCKS_PER_STEP)]
                box = jnp.dot(band_ref[...], jnp.concatenate(wins, axis=1), preferred_element_type=F32)
                dlts = []
                for k in range(POOL_BLOCKS_PER_STEP):
                    j = first + k
                    variant = jnp.where(j == 0, 0, jnp.where(j == nblk - 1, 2, 1))
                    inv = jnp.concatenate([inv_ref[0, variant]] * (c // LANES), axis=1)
                    u_blk = u_ref[0, pl.ds(r0 + k * tb, tb), :].astype(F32)
                    dlts.append((box[:, k * c:(k + 1) * c] * inv - u_blk).astype(BF16))
                o = jnp.dot(jnp.concatenate(dlts, axis=0), pw_ref[0], preferred_element_type=F32) * ps_ref[...]
                o_ref[0, pl.ds(r0, POOL_BLOCKS_PER_STEP * tb), :] = o.astype(o_ref.dtype)
                return carry

            lax.fori_loop(0, nblk // POOL_BLOCKS_PER_STEP, blk, 0)


def _pool_tables(rows, tb):
    br = tb // GRID_W
    i = np.arange(tb)
    ri, ci = i // GRID_W, i % GRID_W
    bands, invs = [], []
    for w in POOL_WINDOWS:
        half = w // 2
        j = np.arange((br + w) * GRID_W)
        dr = (j // GRID_W)[None, :] - ri[:, None]
        dc = (j % GRID_W)[None, :] - ci[:, None]
        bands.append(jnp.asarray((dr >= 0) & (dr < w) & (dc >= -half) & (dc < w - half), BF16))
        cnt_c = np.minimum(ci + w - half, GRID_W) - np.maximum(ci - half, 0)
        per_variant = []
        for r_first in (0, br, rows - br):
            r = r_first + ri
            cnt_r = np.minimum(r + w - half, rows) - np.maximum(r - half, 0)
            per_variant.append(np.float32(1.0) / (cnt_r * cnt_c).astype(np.float32))
        invs.append(np.stack(per_variant))
    inv = np.broadcast_to(np.stack(invs)[..., None], (len(POOL_WINDOWS), 3, tb, LANES))
    return bands, jnp.asarray(inv, F32)


def _pool_call(u_pool, pool_w, pool_scale):
    b, l, p = u_pool.shape
    ng = len(POOL_WINDOWS)
    c = p // ng
    rows = l // GRID_W
    tb = POOL_BLOCK_ROWS * GRID_W
    pad = (max(POOL_WINDOWS) // 2) * GRID_W
    assert l % (tb * POOL_BLOCKS_PER_STEP) == 0 and c % LANES == 0
    assert rows >= 3 * POOL_BLOCK_ROWS and max(POOL_WINDOWS) <= 2 * POOL_BLOCK_ROWS
    bands, inv = _pool_tables(rows, tb)
    return pl.pallas_call(
        functools.partial(_pool_kernel, tb=tb, pad=pad),
        out_shape=jax.ShapeDtypeStruct((b, l, p), BF16),
        grid=(b, ng),
        in_specs=[pl.BlockSpec((1, l, c), lambda i, j: (i, 0, j))]
        + [_const_spec(bd.shape) for bd in bands]
        + [
            pl.BlockSpec((1, 3, tb, LANES), lambda i, j: (j, 0, 0, 0)),
            pl.BlockSpec((1, c, c), lambda i, j: (j, 0, 0)),
            pl.BlockSpec((1, c), lambda i, j: (0, j)),
        ],
        out_specs=pl.BlockSpec((1, l, c), lambda i, j: (i, 0, j)),
        scratch_shapes=[pltpu.VMEM((pad + l + pad, c), BF16)],
        compiler_params=_cparams(("arbitrary", "arbitrary")),
        name="pool",
    )(u_pool, *bands, inv, pool_w.astype(BF16), pool_scale.reshape(1, p))


def _dot_hi_nt(a, b):
    a_hi = a.astype(BF16)
    a_lo = (a - a_hi.astype(F32)).astype(BF16)
    b_hi = b.astype(BF16)
    b_lo = (b - b_hi.astype(F32)).astype(BF16)
    dot = lambda p, q: lax.dot_general(p, q, NT_DIMS, preferred_element_type=F32)
    return dot(a_hi, b_hi) + dot(a_hi, b_lo) + dot(a_lo, b_hi)


def _s5_kernel(u_ref, ea_ref, eb_ref, ba_ref, bm_ref, cc_ref, ca_ref, cb_ref, oa_ref, ob_ref, ap_ref, a8_ref, y_ref,
               et_ref, win_ref, mt_ref, wout_ref, buf_ref, cbuf_ref, hin_ref, *, gps, hc, **static):
    @pl.when(pl.program_id(0) == 0)
    def _():
        buf_ref[...] = jnp.zeros(buf_ref.shape, F32)
        cbuf_ref[...] = jnp.zeros(cbuf_ref.shape, F32)

    for gg in range(gps):
        rows = slice(gg * hc, (gg + 1) * hc)
        _s5_group(u_ref.at[:, rows, :], ea_ref.at[gg], eb_ref.at[gg], ba_ref.at[gg], bm_ref.at[gg], cc_ref.at[gg],
                  ca_ref.at[gg], cb_ref.at[gg], oa_ref.at[gg], ob_ref.at[gg], ap_ref.at[gg], a8_ref.at[gg],
                  y_ref.at[:, rows, :], et_ref.at[gg], win_ref.at[gg], mt_ref.at[gg], wout_ref.at[gg], buf_ref.at[gg],
                  cbuf_ref.at[gg], hin_ref.at[gg],
                  hc=hc, **static)


def _s5_group(u_ref, ea_ref, eb_ref, ba_ref, bm_ref, cc_ref, ca_ref, cb_ref, oa_ref, ob_ref, ap_ref, a8_ref, y_ref,
              et_ref, win_ref, mt_ref, wout_ref, buf_ref, cbuf_ref, hin_ref, *, t, hc, nb, ncl, ncc, pad, nsteps, nstate):
    n = nstate
    th = t * hc
    nl = nb * ncl
    rb = ncc + ncl
    ntile = th // LANES

    ba = ba_ref[...]
    bm = bm_ref[...]
    ca = ca_ref[...]
    cb = cb_ref[...]
    for ti in range(t):
        rows = slice(ti * hc, (ti + 1) * hc)
        e_rows = ba * ea_ref[ti:ti + 1, :] + bm * eb_ref[ti:ti + 1, :]
        et_ref[rows, :] = e_rows
        win_ref[rows, :] = e_rows.astype(BF16)
        wout_ref[rows, :] = (ca * oa_ref[ti:ti + 1, :] + cb * ob_ref[ti:ti + 1, :]).astype(BF16)
    cc = cc_ref[...]
    q_f = _dot_hi_nt(cc[0:hc], et_ref[:, 0:2 * n])
    q_b = _dot_hi_nt(cc[hc:2 * hc], et_ref[:, 2 * n:4 * n])
    zeros = jnp.zeros((hc, th), F32)
    line_b = jnp.concatenate([zeros, q_b], axis=1)
    line_f = jnp.concatenate([q_f, zeros], axis=1)
    per_tile = LANES // hc
    rolled_b = [line_b if r == 0 else pltpu.roll(line_b, hc * r, axis=1) for r in range(per_tile)]
    rolled_f = [line_f if r == per_tile - 1 else pltpu.roll(line_f, 2 * th - hc * (per_tile - 1 - r), axis=1)
                for r in range(per_tile)]
    for ti in range(t):
        a, r = divmod(ti, per_tile)
        row_b = rolled_b[r][:, th - LANES * a:2 * th - LANES * a]
        off_f = LANES * (ntile - 1 - a)
        row_f = rolled_f[r][:, off_f:off_f + th]
        mt_ref[ti * hc:(ti + 1) * hc, :] = (row_b + row_f).astype(BF16)

    u = jnp.concatenate([u_ref[ti] for ti in range(t)], axis=0)
    st_all = lax.dot_general(win_ref[...], u, TN_DIMS, preferred_element_type=F32)
    first_half = lax.broadcasted_iota(jnp.int32, (LANES, LANES), 1) < n
    for q in range(4):
        rows_q = st_all[q * n:(q + 1) * n]
        lat = jnp.concatenate([rows_q[:, b * ncl:(b + 1) * ncl] for b in range(nb)], axis=0).T
        cblk = jnp.concatenate([rows_q[:, nl:nl + LANES]] * nb, axis=0).T
        ctx = jnp.where(first_half, cblk, jnp.concatenate([cblk[ncc:], cblk[:ncc]], axis=0))[0:ncc]
        if q < 2:
            buf_ref[q, pad:pad + ncc, :] = ctx
            buf_ref[q, pad + ncc:pad + rb, :] = lat
        else:
            buf_ref[q, pad:pad + ncl, :] = lat
            buf_ref[q, pad + ncl:pad + rb, :] = ctx
    def step(ref, d, k, sgn, lo, rows, keep=None):
        ar = ap_ref[4 * k + 2 * d:4 * k + 2 * d + 1, :]
        ai = ap_ref[4 * k + 2 * d + 1:4 * k + 2 * d + 2, :]
        pr = ref[2 * d, lo + sgn:lo + sgn + rows, :]
        pi = ref[2 * d + 1, lo + sgn:lo + sgn + rows, :]
        if keep is not None:
            pr = jnp.where(keep, pr, 0.0)
            pi = jnp.where(keep, pi, 0.0)
        xr = ref[2 * d, lo:lo + rows, :]
        xi = ref[2 * d + 1, lo:lo + rows, :]
        ref[2 * d, lo:lo + rows, :] = xr + (ar * pr - ai * pi)
        ref[2 * d + 1, lo:lo + rows, :] = xi + (ar * pi + ai * pr)

    rbp = buf_ref.shape[1] - 2 * pad
    nv = rbp // SUBLANES
    cpad = (cbuf_ref.shape[1] - nv) // 2
    in_block = lax.broadcasted_iota(jnp.int32, (rbp, LANES), 0) % SUBLANES
    nlocal = SUBLANES.bit_length() - 1
    nchunk = 3 if nv % 3 == 0 else 1
    crow = rbp // nchunk
    blk = in_block[0:crow]
    for d in range(2):
        for ch in range(nchunk):
            rows = slice(pad + ch * crow, pad + (ch + 1) * crow)
            xr = buf_ref[2 * d, rows, :]
            xi = buf_ref[2 * d + 1, rows, :]
            for k in range(nlocal):
                st = 1 << k
                ar = ap_ref[4 * k + 2 * d:4 * k + 2 * d + 1, :]
                ai = ap_ref[4 * k + 2 * d + 1:4 * k + 2 * d + 2, :]
                keep = blk >= st if d == 0 else blk < SUBLANES - st
                shift = st if d == 0 else SUBLANES - st
                rot = lambda v: pltpu.roll(v.reshape(crow // SUBLANES, SUBLANES, LANES), shift, axis=1).reshape(crow, LANES)
                pr = jnp.where(keep, rot(xr), 0.0)
                pi = jnp.where(keep, rot(xi), 0.0)
                xr, xi = xr + (ar * pr - ai * pi), xi + (ar * pi + ai * pr)
            buf_ref[2 * d, rows, :] = xr
            buf_ref[2 * d + 1, rows, :] = xi
    for d in range(2):
        edge = SUBLANES - 1 if d == 0 else 0
        for q in (2 * d, 2 * d + 1):
            cbuf_ref[q, cpad:cpad + nv, :] = buf_ref[q, pl.ds(pad + edge, nv, stride=SUBLANES), :]
    for m in range(max(0, (nv - 1).bit_length())):
        st = 1 << m
        step(cbuf_ref, 0, nlocal + m, -st, cpad, nv)
        step(cbuf_ref, 1, nlocal + m, st, cpad, nv)
    for d in range(2):
        a8r = a8_ref[2 * SUBLANES * d:2 * SUBLANES * d + SUBLANES, :]
        a8i = a8_ref[2 * SUBLANES * d + SUBLANES:2 * SUBLANES * (d + 1), :]
        for j in range(nv):
            src = j - 1 if d == 0 else j + 1
            if 0 <= src < nv:
                cr = jnp.broadcast_to(cbuf_ref[2 * d, cpad + src:cpad + src + 1, :], (SUBLANES, LANES))
                ci = jnp.broadcast_to(cbuf_ref[2 * d + 1, cpad + src:cpad + src + 1, :], (SUBLANES, LANES))
                rows = slice(pad + j * SUBLANES, pad + (j + 1) * SUBLANES)
                buf_ref[2 * d, rows, :] = buf_ref[2 * d, rows, :] + (a8r * cr - a8i * ci)
                buf_ref[2 * d + 1, rows, :] = buf_ref[2 * d + 1, rows, :] + (a8r * ci + a8i * cr)
    for q in range(4):
        lo = pad + ncc - 1 if q < 2 else pad + 1
        ent = buf_ref[q, lo:lo + ncl, :].T.astype(BF16)
        for b in range(nb):
            hin_ref[q * n:(q + 1) * n, b * ncl:(b + 1) * ncl] = ent[b * n:(b + 1) * n]

    y = jnp.dot(mt_ref[...], u[:, 0:nl], preferred_element_type=F32)
    y = y + jnp.dot(wout_ref[...], hin_ref[...], preferred_element_type=F32)
    for ti in range(t):
        y_ref[ti] = y[ti * hc:(ti + 1) * hc].astype(y_ref.dtype)


def _s5_params(lam_re, lam_im, log_dt, b_re, b_im, c_re, c_im, t, nsteps):
    _, g, n = lam_re.shape
    h = b_re.shape[-1]
    lam_re, lam_im = lam_re.astype(F32), lam_im.astype(F32)
    dt = jnp.exp(log_dt.astype(F32))[..., None]
    lr, li = lam_re * dt, lam_im * dt

    def power(d, k):
        kk = k[None, :, None]
        mag = jnp.exp(lr[d][:, None, :] * kk)
        return mag * jnp.cos(li[d][:, None, :] * kk), mag * jnp.sin(li[d][:, None, :] * kk)

    a_re, a_im = jnp.exp(lr) * jnp.cos(li), jnp.exp(lr) * jnp.sin(li)
    den = lam_re * lam_re + lam_im * lam_im
    f_re = ((a_re - 1.0) * lam_re + a_im * lam_im) / den
    f_im = (a_im * lam_re - (a_re - 1.0) * lam_im) / den
    bt_re = (f_re[..., None] * b_re - f_im[..., None] * b_im).transpose(0, 1, 3, 2)
    bt_im = (f_re[..., None] * b_im + f_im[..., None] * b_re).transpose(0, 1, 3, 2)
    lanes4 = lambda f0, f1, b0, b1: jnp.concatenate([f0, f1, b0, b1], axis=-1)

    strides = t * (2.0 ** jnp.arange(nsteps, dtype=F32))
    exps = jnp.concatenate([jnp.arange(t + 1, dtype=F32), strides, t * jnp.arange(1, SUBLANES + 1, dtype=F32)])
    (allf_re, allf_im), (allb_re, allb_im) = power(0, exps), power(1, exps)
    pf_re, pf_im = allf_re[:, t - 1::-1], allf_im[:, t - 1::-1]
    pb_re, pb_im = allb_re[:, 0:t], allb_im[:, 0:t]
    ea = lanes4(pf_re, pf_re, pb_re, pb_re)
    eb = lanes4(pf_im, pf_im, pb_im, pb_im)
    ba = lanes4(bt_re[0], bt_im[0], bt_re[1], bt_im[1])
    bm = lanes4(-bt_im[0], bt_re[0], -bt_im[1], bt_re[1])
    cc = jnp.concatenate([jnp.concatenate([c_re[0], -c_im[0]], axis=-1),
                          jnp.concatenate([c_re[1], -c_im[1]], axis=-1)], axis=1)
    rf_re, rf_im = allf_re[:, 1:t + 1], allf_im[:, 1:t + 1]
    rb_re, rb_im = allb_re[:, t:0:-1], allb_im[:, t:0:-1]
    oa = lanes4(rf_re, rf_im, rb_re, rb_im)
    ob = lanes4(rf_im, rf_re, rb_im, rb_re)
    ca = lanes4(c_re[0], -c_re[0], c_re[1], -c_re[1])
    cb = lanes4(-c_im[0], -c_im[0], -c_im[1], -c_im[1])
    st0, st1 = t + 1, t + 1 + nsteps
    apf_re, apf_im, apb_re, apb_im = (v[:, st0:st1] for v in (allf_re, allf_im, allb_re, allb_im))
    ap = jnp.stack([apf_re, apf_im, apb_re, apb_im], axis=2).reshape(g, 4 * nsteps, n)
    ap = jnp.concatenate([ap] * (LANES // n), axis=-1)
    rows = -(-4 * nsteps // SUBLANES) * SUBLANES
    ap = jnp.pad(ap, ((0, 0), (0, rows - 4 * nsteps), (0, 0)))
    f8_re, f8_im = allf_re[:, st1:], allf_im[:, st1:]
    b8_re, b8_im = allb_re[:, :st1 - 1:-1], allb_im[:, :st1 - 1:-1]
    a8 = jnp.concatenate([f8_re, f8_im, b8_re, b8_im], axis=1)
    a8 = jnp.concatenate([a8] * (LANES // n), axis=-1)
    return tuple(v.astype(F32) for v in (ea, eb, ba, bm, cc, ca, cb, oa, ob, ap, a8))


def _s5_call(ut, params, t, nb, ncl, ncc, nsteps, n_state, hc):
    ap = params[-2]
    _, sw, width = ut.shape
    nl = nb * ncl
    g = sw // hc
    th = t * hc
    rbp = -(-(ncc + ncl) // SUBLANES) * SUBLANES
    nv = rbp // SUBLANES
    pad = SUBLANES
    cpad = max(SUBLANES, 1 << max(0, (nv - 1).bit_length() - 1))
    assert nb * n_state == LANES and nb == 2 and 4 * nsteps <= ap.shape[1]
    gps = S5_GROUPS_PER_STEP
    assert g % gps == 0
    grp = lambda shp: pl.BlockSpec((gps,) + shp, lambda i: (i, 0, 0))
    return pl.pallas_call(
        functools.partial(_s5_kernel, gps=gps, t=t, hc=hc, nb=nb, ncl=ncl, ncc=ncc, pad=pad, nsteps=nsteps,
                          nstate=n_state),
        out_shape=jax.ShapeDtypeStruct((t, sw, nl), BF16),
        grid=(g // gps,),
        in_specs=[
            pl.BlockSpec((t, gps * hc, width), lambda i: (0, i, 0)),
            grp((t, 4 * n_state)), grp((t, 4 * n_state)), grp((hc, 4 * n_state)), grp((hc, 4 * n_state)),
            grp((2 * hc, 2 * n_state)), grp((hc, 4 * n_state)), grp((hc, 4 * n_state)),
            grp((t, 4 * n_state)), grp((t, 4 * n_state)), grp((ap.shape[1], LANES)), grp((4 * SUBLANES, LANES)),
        ],
        out_specs=pl.BlockSpec((t, gps * hc, nl), lambda i: (0, i, 0)),
        scratch_shapes=[pltpu.VMEM((gps, th, 4 * n_state), F32), pltpu.VMEM((gps, th, 4 * n_state), BF16),
                        pltpu.VMEM((gps, th, th), BF16), pltpu.VMEM((gps, th, 4 * n_state), BF16),
                        pltpu.VMEM((gps, 4, pad + rbp + pad, LANES), F32),
                        pltpu.VMEM((gps, 4, cpad + nv + cpad, LANES), F32),
                        pltpu.VMEM((gps, 4 * n_state, nl), BF16)],
        compiler_params=_cparams(("arbitrary",)),
        name="s5",
    )(ut, *params)


def _merge_kernel(y_ref, u_ref, p_ref, zs_ref, x_ref, gate_ref, d_ref, gw_ref, gb_ref, ow_ref, fg_ref, o_ref):
    sw = y_ref.shape[-1]
    pw = p_ref.shape[-1]
    y = _gelu_tanh(y_ref[0].astype(F32) + d_ref[...] * u_ref[0].astype(F32)).astype(BF16)
    yy = jnp.dot(y, gw_ref[...], preferred_element_type=F32) + gb_ref[...]
    ssm_out = yy[:, :sw] * _sigmoid(yy[:, sw:])
    br_pool = p_ref[0] * zs_ref[0, :, 0:pw]
    br_ssm = (ssm_out * zs_ref[0, :, pw:pw + sw].astype(F32)).astype(BF16)
    mix = jnp.dot(jnp.concatenate([br_pool, br_ssm], axis=1), ow_ref[...], preferred_element_type=F32)
    xo = x_ref[0] + gate_ref[0] * mix
    ms = jnp.mean(xo * xo, axis=-1, keepdims=True)
    o_ref[0] = (xo * lax.rsqrt(ms + EPS) * fg_ref[...]).astype(o_ref.dtype)


def _merge_call(y_ssm, u_ssm, pool_out, zs, x, gate, d_skip, glu_w, glu_b, out_w, final_g, tm):
    b, l, d = x.shape
    sw = y_ssm.shape[-1]
    pw = pool_out.shape[-1]
    mixw = zs.shape[-1]
    tok = lambda wd: pl.BlockSpec((1, tm, wd), lambda i, j: (i, j, 0))
    return pl.pallas_call(
        _merge_kernel,
        out_shape=jax.ShapeDtypeStruct((b, l, d), x.dtype),
        grid=(b, l // tm),
        in_specs=[
            tok(sw), tok(sw), tok(pw), tok(mixw), tok(d),
            pl.BlockSpec((1, 1, d), lambda i, j: (i, 0, 0)),
            _const_spec((1, sw)), _const_spec((sw, 2 * sw)), _const_spec((1, 2 * sw)), _const_spec((mixw, d)),
            _const_spec((1, d)),
        ],
        out_specs=tok(d),
        compiler_params=_cparams(("arbitrary", "arbitrary")),
        name="merge",
    )(y_ssm, u_ssm, pool_out, zs, x, gate, d_skip.reshape(1, sw), glu_w.astype(BF16), glu_b.reshape(1, 2 * sw),
      out_w.astype(BF16), final_g.reshape(1, d))


def kernel(x, c, ctx, c_ctx, ada_w, ada_b, norm_g, in_w, pool_w, pool_scale, s5_lam_re, s5_lam_im, s5_log_dt,
           s5_b_re, s5_b_im, s5_c_re, s5_c_im, s5_d, glu_w, glu_b, out_w, final_g):
    assert ada_w.shape[0] == 1, "single-layer block"
    bsz, seq, d = x.shape
    cl = ctx.shape[1]
    mixw = in_w.shape[2] // 2
    poolw = pool_scale.shape[-1]
    ssmw = s5_d.shape[-1]
    n_grp, n_state = s5_lam_re.shape[2], s5_lam_re.shape[3]
    hc = ssmw // n_grp
    t = CHUNK_T
    assert poolw + ssmw == mixw and seq % t == 0 and cl % t == 0 and 2 * n_state == LANES and LANES % hc == 0

    mod = _ada_call(jnp.concatenate([c, c_ctx[None]], axis=0), ada_w[0], ada_b[0])
    shift, scale, gate = mod[:, :d], mod[:, d:2 * d], mod[:, 2 * d:]
    g1 = norm_g[0].reshape(1, d)
    in_w16 = in_w[0].astype(BF16)

    tm = min(512, seq)
    u_pool, u_ssm, zs = _inproj_call(x, scale[:bsz, None], shift[:bsz, None], g1, in_w16,
                                     (poolw, ssmw, mixw), (False, False, True), min(1024, seq), "inproj")
    sc_c = jnp.broadcast_to(scale[bsz][None, None], (bsz, 1, d))
    sh_c = jnp.broadcast_to(shift[bsz][None, None], (bsz, 1, d))
    assert poolw % ssmw == 0
    (uc_ssm,) = _inproj_call(ctx, sc_c, sh_c, g1, in_w16, (ssmw,), (False,), min(256, cl), "inproj_ctx",
                             wcol=poolw // ssmw)

    pool_out = _pool_call(u_pool, pool_w[0], pool_scale[0])

    ncl, ncc = seq // t, cl // t
    nsteps = max(1, (ncc + ncl - 1).bit_length())
    params = _s5_params(s5_lam_re[0], s5_lam_im[0], s5_log_dt[0], s5_b_re[0], s5_b_im[0], s5_c_re[0], s5_c_im[0],
                        t, nsteps)
    ut = _pack_call(u_ssm, uc_ssm, t)
    yt = _s5_call(ut, params, t, bsz, ncl, ncc, nsteps, n_state, hc)
    y_ssm = _unpack_call(yt, bsz, seq)

    return _merge_call(y_ssm, u_ssm, pool_out, zs, x, gate[:bsz, None], s5_d[0], glu_w[0], glu_b[0], out_w[0],
                       final_g, tm)
```

```python
import functools
import math

import jax
import jax.numpy as jnp
import numpy as np
from jax import lax
from jax.experimental import pallas as pl
from jax.experimental.pallas import tpu as pltpu

GRID_W = 64
POOL_WINDOWS = (2, 4, 8, 16)
EPS = 1e-6
CHUNK_T = 32
S5_GROUPS_PER_STEP = 4
LANES = 128
SUBLANES = 8
VMEM_LIMIT = 56 * 1024 * 1024

F32 = jnp.float32
BF16 = jnp.bfloat16
TN_DIMS = (((0,), (0,)), ((), ()))
NT_DIMS = (((1,), (1,)), ((), ()))


def _cparams(sem):
    return pltpu.CompilerParams(dimension_semantics=sem, vmem_limit_bytes=VMEM_LIMIT)


def _const_spec(shape):
    return pl.BlockSpec(shape, lambda *_: (0,) * len(shape), pipeline_mode=pl.Buffered(1))


def _sigmoid(v):
    return 1.0 / (1.0 + jnp.exp(-v))


def _gelu_tanh(v):
    return 0.5 * v * (1.0 + jnp.tanh(math.sqrt(2.0 / math.pi) * (v + 0.044715 * (v * v * v))))


def _modulated_norm(x, g, scale, shift):
    ms = jnp.mean(x * x, axis=-1, keepdims=True)
    return (x * lax.rsqrt(ms + EPS) * g) * (1.0 + scale) + shift


ADA_LANE_TILES = 6


def _ada_kernel(ct_ref, w_ref, b_ref, o_ref, sb_ref, acc_ref, *, nrows):
    tk, n = w_ref.shape
    step = pl.program_id(0)

    @pl.when(step == 0)
    def _():
        acc_ref[...] = jnp.zeros(acc_ref.shape, F32)

    cv = ct_ref[...]
    s = cv * _sigmoid(cv)
    for r in range(nrows):
        sb_ref[r] = jnp.broadcast_to(s[:, r:r + 1], (tk, LANES))

    group = ADA_LANE_TILES * LANES
    for g0 in range(0, n, group):
        def body(i, accs, g0=g0):
            k0 = pl.multiple_of(i * SUBLANES, SUBLANES)
            sb = [sb_ref[r, pl.ds(k0, SUBLANES), :] for r in range(nrows)]
            out = []
            for lt in range(ADA_LANE_TILES):
                w = w_ref[pl.ds(k0, SUBLANES), g0 + lt * LANES:g0 + (lt + 1) * LANES]
                out.append(tuple(accs[lt][r] + sb[r] * w for r in range(nrows)))
            return tuple(out)

        init = tuple(tuple(jnp.zeros((SUBLANES, LANES), F32) for _ in range(nrows)) for _ in range(ADA_LANE_TILES))
        accs = lax.fori_loop(0, tk // SUBLANES, body, init, unroll=8)
        for lt in range(ADA_LANE_TILES):
            lanes = slice(g0 + lt * LANES, g0 + (lt + 1) * LANES)
            for r in range(nrows):
                acc_ref[r, :, lanes] = acc_ref[r, :, lanes] + accs[lt][r]

    @pl.when(step == pl.num_programs(0) - 1)
    def _():
        o_ref[...] = jnp.zeros(o_ref.shape, F32)
        for r in range(nrows):
            o_ref[r:r + 1, :] = jnp.sum(acc_ref[r], axis=0, keepdims=True) + b_ref[...]


def _ada_call(cvecs, ada_w, ada_b):
    nrows, d = cvecs.shape
    n = ada_w.shape[1]
    tk = 256 if d % 256 == 0 else d
    assert n % (ADA_LANE_TILES * LANES) == 0
    ct = jnp.zeros((d, SUBLANES), F32).at[:, :nrows].set(cvecs.T)
    out = pl.pallas_call(
        functools.partial(_ada_kernel, nrows=nrows),
        out_shape=jax.ShapeDtypeStruct((SUBLANES, n), F32),
        grid=(d // tk,),
        in_specs=[
            pl.BlockSpec((tk, SUBLANES), lambda j: (j, 0)),
            pl.BlockSpec((tk, n), lambda j: (j, 0)),
            _const_spec((1, n)),
        ],
        out_specs=pl.BlockSpec((SUBLANES, n), lambda j: (0, 0)),
        scratch_shapes=[pltpu.VMEM((nrows, tk, LANES), F32), pltpu.VMEM((nrows, SUBLANES, n), F32)],
        compiler_params=_cparams(("arbitrary",)),
        name="ada",
    )(ct, ada_w, ada_b.reshape(1, n))
    return out[:nrows]


def _inproj_kernel(x_ref, sc_ref, sh_ref, g_ref, w_ref, *o_refs, nchunk, silu):
    h = _modulated_norm(x_ref[0], g_ref[...], sc_ref[0], sh_ref[0]).astype(BF16)
    col = 0
    for o_ref, gate in zip(o_refs, silu):
        width = o_ref.shape[-1]
        for n0 in range(0, width, nchunk):
            acc = jnp.dot(h, w_ref[:, col + n0:col + n0 + nchunk], preferred_element_type=F32)
            if gate:
                acc = acc * _sigmoid(acc)
            o_ref[0, :, n0:n0 + nchunk] = acc.astype(o_ref.dtype)
        col += width


def _inproj_call(x, scale, shift, g, w, widths, silu, tm, name, wcol=0):
    b, l, d = x.shape
    nout = sum(widths)
    assert w.shape[1] % nout == 0 and l % tm == 0
    nchunk = min(1024, min(widths))
    assert all(wd % nchunk == 0 for wd in widths)
    return pl.pallas_call(
        functools.partial(_inproj_kernel, nchunk=nchunk, silu=silu),
        out_shape=[jax.ShapeDtypeStruct((b, l, wd), BF16) for wd in widths],
        grid=(b, l // tm),
        in_specs=[
            pl.BlockSpec((1, tm, d), lambda i, j: (i, j, 0)),
            pl.BlockSpec((1, 1, d), lambda i, j: (i, 0, 0)),
            pl.BlockSpec((1, 1, d), lambda i, j: (i, 0, 0)),
            _const_spec((1, d)),
            pl.BlockSpec((d, nout), lambda i, j: (0, wcol), pipeline_mode=pl.Buffered(1)),
        ],
        out_specs=[pl.BlockSpec((1, tm, wd), lambda i, j: (i, j, 0)) for wd in widths],
        compiler_params=_cparams(("arbitrary", "arbitrary")),
        name=name,
    )(x, scale, shift, g, w)


def _chunk_perm(t):
    i = np.arange(SUBLANES * t)
    src = (i % SUBLANES) * t + i // SUBLANES
    return jnp.asarray(src[:, None] == i[None, :], BF16)


def _pack_kernel(u_ref, uc_ref, perm_ref, ut_ref, scr_ref, *, t, nblk):
    j = pl.program_id(0)
    s = scr_ref.shape[2]
    rows = SUBLANES * t
    ctok = uc_ref.shape[0]

    def permute(src, blk):
        pm = jnp.dot(perm_ref[...], src, preferred_element_type=F32)
        for ti in range(t):
            scr_ref[ti, pl.ds(pl.multiple_of(blk * SUBLANES, SUBLANES), SUBLANES), :] = pm[ti * SUBLANES:(ti + 1) * SUBLANES]

    @pl.when(j < nblk)
    def _():
        def body(blk, carry):
            permute(u_ref[pl.ds(pl.multiple_of(blk * rows, rows), rows), :], blk)
            return carry

        lax.fori_loop(0, LANES // SUBLANES, body, 0, unroll=2)

    @pl.when(j == nblk)
    def _():
        for blk in range(ctok // rows):
            permute(uc_ref[blk * rows:(blk + 1) * rows, :], blk)
        scr_ref[:, ctok // t:LANES, :] = jnp.zeros((t, LANES - ctok // t, s), F32)

    def transpose(ti, carry):
        for k in range(s // LANES):
            ut_ref[ti, LANES * k:LANES * (k + 1), :] = scr_ref[ti, :, LANES * k:LANES * (k + 1)].T.astype(BF16)
        return carry

    lax.fori_loop(0, t, transpose, 0, unroll=2)


def _pack_call(u_ssm, uc_ssm, t):
    b, l, s = u_ssm.shape
    tok = LANES * t
    rows = SUBLANES * t
    ntok = b * l
    ctok = b * uc_ssm.shape[1]
    assert ntok % tok == 0 and ctok <= tok and ctok % rows == 0 and s % LANES == 0
    nblk = ntok // tok
    return pl.pallas_call(
        functools.partial(_pack_kernel, t=t, nblk=nblk),
        out_shape=jax.ShapeDtypeStruct((t, s, ntok // t + LANES), BF16),
        grid=(nblk + 1,),
        in_specs=[pl.BlockSpec((tok, s), lambda j: (jnp.minimum(j, nblk - 1), 0)),
                  _const_spec((ctok, s)), _const_spec((rows, rows))],
        out_specs=pl.BlockSpec((t, s, LANES), lambda j: (0, 0, j)),
        scratch_shapes=[pltpu.VMEM((t, LANES, s), F32)],
        compiler_params=_cparams(("arbitrary",)),
        name="pack",
    )(u_ssm.reshape(ntok, s), uc_ssm.reshape(ctok, s), _chunk_perm(t))


def _unpack_kernel(yt_ref, perm_ref, y_ref, scr_ref, *, t):
    s = scr_ref.shape[2]
    rows = SUBLANES * t

    def transpose(ti, carry):
        for k in range(s // LANES):
            scr_ref[ti, :, LANES * k:LANES * (k + 1)] = yt_ref[ti, LANES * k:LANES * (k + 1), :].astype(F32).T
        return carry

    lax.fori_loop(0, t, transpose, 0, unroll=2)

    def body(blk, carry):
        c0 = pl.multiple_of(blk * SUBLANES, SUBLANES)
        src = jnp.concatenate([scr_ref[ti, pl.ds(c0, SUBLANES), :] for ti in range(t)], axis=0).astype(BF16)
        out = jnp.dot(perm_ref[...], src, preferred_element_type=F32)
        y_ref[pl.ds(pl.multiple_of(blk * rows, rows), rows), :] = out.astype(y_ref.dtype)
        return carry

    lax.fori_loop(0, LANES // SUBLANES, body, 0, unroll=2)


def _unpack_call(yt, b, l):
    t, s, nl = yt.shape
    tok = LANES * t
    rows = SUBLANES * t
    y = pl.pallas_call(
        functools.partial(_unpack_kernel, t=t),
        out_shape=jax.ShapeDtypeStruct((nl * t, s), BF16),
        grid=(nl // LANES,),
        in_specs=[pl.BlockSpec((t, s, LANES), lambda j: (0, 0, j)), _const_spec((rows, rows))],
        out_specs=pl.BlockSpec((tok, s), lambda j: (j, 0)),
        scratch_shapes=[pltpu.VMEM((t, LANES, s), F32)],
        compiler_params=_cparams(("arbitrary",)),
        name="unpack",
    )(yt, _chunk_perm(t).T)
    return y.reshape(b, l, s)


POOL_BLOCK_ROWS = 8
POOL_BLOCKS_PER_STEP = 4


def _pool_kernel(u_ref, *refs, tb, pad):
    nwin = len(POOL_WINDOWS)
    band_refs = refs[:nwin]
    inv_ref, pw_ref, ps_ref, o_ref, pad_ref = refs[nwin:]
    grp = pl.program_id(1)
    l = u_ref.shape[1]
    c = u_ref.shape[2]
    nblk = l // tb
    pad_ref[0:pad, :] = jnp.zeros((pad, c), pad_ref.dtype)
    pad_ref[pad + l:pad + l + pad, :] = jnp.zeros((pad, c), pad_ref.dtype)

    def copy(i, carry):
        r0 = pl.multiple_of(i * tb, tb)
        pad_ref[pl.ds(pl.multiple_of(pad + r0, GRID_W), tb), :] = u_ref[0, pl.ds(r0, tb), :]
        return carry

    lax.fori_loop(0, nblk, copy, 0)

    for gi, w in enumerate(POOL_WINDOWS):
        half = w // 2
        span = band_refs[gi].shape[1]

        @pl.when(grp == gi)
        def _(half=half, span=span, band_ref=band_refs[gi]):
            def blk(i, carry):
                first = POOL_BLOCKS_PER_STEP * i
                r0 = pl.multiple_of(first * tb, POOL_BLOCKS_PER_STEP * tb)
                wins = [pad_ref[pl.ds(pl.multiple_of(pad + r0 + k * tb - half * GRID_W, GRID_W), span), :]
                        for k in range(POOL_BLOCKS_PER_STEP)]
                box = jnp.dot(band_ref[...], jnp.concatenate(wins, axis=1), preferred_element_type=F32)
                dlts = []
                for k in range(POOL_BLOCKS_PER_STEP):
                    j = first + k
                    variant = jnp.where(j == 0, 0, jnp.where(j == nblk - 1, 2, 1))
                    inv = jnp.concatenate([inv_ref[0, variant]] * (c // LANES), axis=1)
                    u_blk = u_ref[0, pl.ds(r0 + k * tb, tb), :].astype(F32)
                    dlts.append((box[:, k * c:(k + 1) * c] * inv - u_blk).astype(BF16))
                o = jnp.dot(jnp.concatenate(dlts, axis=0), pw_ref[0], preferred_element_type=F32) * ps_ref[...]
                o_ref[0, pl.ds(r0, POOL_BLOCKS_PER_STEP * tb), :] = o.astype(o_ref.dtype)
                return carry

            lax.fori_loop(0, nblk // POOL_BLOCKS_PER_STEP, blk, 0)


def _pool_tables(rows, tb):
    br = tb // GRID_W
    i = np.arange(tb)
    ri, ci = i // GRID_W, i % GRID_W
    bands, invs = [], []
    for w in POOL_WINDOWS:
        half = w // 2
        j = np.arange((br + w) * GRID_W)
        dr = (j // GRID_W)[None, :] - ri[:, None]
        dc = (j % GRID_W)[None, :] - ci[:, None]
        bands.append(jnp.asarray((dr >= 0) & (dr < w) & (dc >= -half) & (dc < w - half), BF16))
        cnt_c = np.minimum(ci + w - half, GRID_W) - np.maximum(ci - half, 0)
        per_variant = []
        for r_first in (0, br, rows - br):
            r = r_first + ri
            cnt_r = np.minimum(r + w - half, rows) - np.maximum(r - half, 0)
            per_variant.append(np.float32(1.0) / (cnt_r * cnt_c).astype(np.float32))
        invs.append(np.stack(per_variant))
    inv = np.broadcast_to(np.stack(invs)[..., None], (len(POOL_WINDOWS), 3, tb, LANES))
    return bands, jnp.asarray(inv, F32)


def _pool_call(u_pool, pool_w, pool_scale):
    b, l, p = u_pool.shape
    ng = len(POOL_WINDOWS)
    c = p // ng
    rows = l // GRID_W
    tb = POOL_BLOCK_ROWS * GRID_W
    pad = (max(POOL_WINDOWS) // 2) * GRID_W
    assert l % (tb * POOL_BLOCKS_PER_STEP) == 0 and c % LANES == 0
    assert rows >= 3 * POOL_BLOCK_ROWS and max(POOL_WINDOWS) <= 2 * POOL_BLOCK_ROWS
    bands, inv = _pool_tables(rows, tb)
    return pl.pallas_call(
        functools.partial(_pool_kernel, tb=tb, pad=pad),
        out_shape=jax.ShapeDtypeStruct((b, l, p), BF16),
        grid=(b, ng),
        in_specs=[pl.BlockSpec((1, l, c), lambda i, j: (i, 0, j))]
        + [_const_spec(bd.shape) for bd in bands]
        + [
            pl.BlockSpec((1, 3, tb, LANES), lambda i, j: (j, 0, 0, 0)),
            pl.BlockSpec((1, c, c), lambda i, j: (j, 0, 0)),
            pl.BlockSpec((1, c), lambda i, j: (0, j)),
        ],
        out_specs=pl.BlockSpec((1, l, c), lambda i, j: (i, 0, j)),
        scratch_shapes=[pltpu.VMEM((pad + l + pad, c), BF16)],
        compiler_params=_cparams(("arbitrary", "arbitrary")),
        name="pool",
    )(u_pool, *bands, inv, pool_w.astype(BF16), pool_scale.reshape(1, p))


def _dot_hi_nt(a, b):
    a_hi = a.astype(BF16)
    a_lo = (a - a_hi.astype(F32)).astype(BF16)
    b_hi = b.astype(BF16)
    b_lo = (b - b_hi.astype(F32)).astype(BF16)
    dot = lambda p, q: lax.dot_general(p, q, NT_DIMS, preferred_element_type=F32)
    return dot(a_hi, b_hi) + dot(a_hi, b_lo) + dot(a_lo, b_hi)


def _s5_kernel(u_ref, ea_ref, eb_ref, ba_ref, bm_ref, cc_ref, ca_ref, cb_ref, oa_ref, ob_ref, ap_ref, a8_ref, y_ref,
               et_ref, win_ref, mt_ref, wout_ref, buf_ref, cbuf_ref, hin_ref, *, gps, hc, **static):
    @pl.when(pl.program_id(0) == 0)
    def _():
        buf_ref[...] = jnp.zeros(buf_ref.shape, F32)
        cbuf_ref[...] = jnp.zeros(cbuf_ref.shape, F32)

    for gg in range(gps):
        rows = slice(gg * hc, (gg + 1) * hc)
        _s5_group(u_ref.at[:, rows, :], ea_ref.at[gg], eb_ref.at[gg], ba_ref.at[gg], bm_ref.at[gg], cc_ref.at[gg],
                  ca_ref.at[gg], cb_ref.at[gg], oa_ref.at[gg], ob_ref.at[gg], ap_ref.at[gg], a8_ref.at[gg],
                  y_ref.at[:, rows, :], et_ref.at[gg], win_ref.at[gg], mt_ref.at[gg], wout_ref.at[gg], buf_ref.at[gg],
                  cbuf_ref.at[gg], hin_ref.at[gg],
                  hc=hc, **static)


def _s5_group(u_ref, ea_ref, eb_ref, ba_ref, bm_ref, cc_ref, ca_ref, cb_ref, oa_ref, ob_ref, ap_ref, a8_ref, y_ref,
              et_ref, win_ref, mt_ref, wout_ref, buf_ref, cbuf_ref, hin_ref, *, t, hc, nb, ncl, ncc, pad, nsteps, nstate):
    n = nstate
    th = t * hc
    nl = nb * ncl
    rb = ncc + ncl
    ntile = th // LANES

    ba = ba_ref[...]
    bm = bm_ref[...]
    ca = ca_ref[...]
    cb = cb_ref[...]
    for ti in range(t):
        rows = slice(ti * hc, (ti + 1) * hc)
        e_rows = ba * ea_ref[ti:ti + 1, :] + bm * eb_ref[ti:ti + 1, :]
        et_ref[rows, :] = e_rows
        win_ref[rows, :] = e_rows.astype(BF16)
        wout_ref[rows, :] = (ca * oa_ref[ti:ti + 1, :] + cb * ob_ref[ti:ti + 1, :]).astype(BF16)
    cc = cc_ref[...]
    q_f = _dot_hi_nt(cc[0:hc], et_ref[:, 0:2 * n])
    q_b = _dot_hi_nt(cc[hc:2 * hc], et_ref[:, 2 * n:4 * n])
    zeros = jnp.zeros((hc, th), F32)
    line_b = jnp.concatenate([zeros, q_b], axis=1)
    line_f = jnp.concatenate([q_f, zeros], axis=1)
    per_tile = LANES // hc
    rolled_b = [line_b if r == 0 else pltpu.roll(line_b, hc * r, axis=1) for r in range(per_tile)]
    rolled_f = [line_f if r == per_tile - 1 else pltpu.roll(line_f, 2 * th - hc * (per_tile - 1 - r), axis=1)
                for r in range(per_tile)]
    for ti in range(t):
        a, r = divmod(ti, per_tile)
        row_b = rolled_b[r][:, th - LANES * a:2 * th - LANES * a]
        off_f = LANES * (ntile - 1 - a)
        row_f = rolled_f[r][:, off_f:off_f + th]
        mt_ref[ti * hc:(ti + 1) * hc, :] = (row_b + row_f).astype(BF16)

    u = jnp.concatenate([u_ref[ti] for ti in range(t)], axis=0)
    st_all = lax.dot_general(win_ref[...], u, TN_DIMS, preferred_element_type=F32)
    first_half = lax.broadcasted_iota(jnp.int32, (LANES, LANES), 1) < n
    for q in range(4):
        rows_q = st_all[q * n:(q + 1) * n]
        lat = jnp.concatenate([rows_q[:, b * ncl:(b + 1) * ncl] for b in range(nb)], axis=0).T
        cblk = jnp.concatenate([rows_q[:, nl:nl + LANES]] * nb, axis=0).T
        ctx = jnp.where(first_half, cblk, jnp.concatenate([cblk[ncc:], cblk[:ncc]], axis=0))[0:ncc]
        if q < 2:
            buf_ref[q, pad:pad + ncc, :] = ctx
            buf_ref[q, pad + ncc:pad + rb, :] = lat
        else:
            buf_ref[q, pad:pad + ncl, :] = lat
            buf_ref[q, pad + ncl:pad + rb, :] = ctx
    def step(ref, d, k, sgn, lo, rows, keep=None):
        ar = ap_ref[4 * k + 2 * d:4 * k + 2 * d + 1, :]
        ai = ap_ref[4 * k + 2 * d + 1:4 * k + 2 * d + 2, :]
        pr = ref[2 * d, lo + sgn:lo + sgn + rows, :]
        pi = ref[2 * d + 1, lo + sgn:lo + sgn + rows, :]
        if keep is not None:
            pr = jnp.where(keep, pr, 0.0)
            pi = jnp.where(keep, pi, 0.0)
        xr = ref[2 * d, lo:lo + rows, :]
        xi = ref[2 * d + 1, lo:lo + rows, :]
        ref[2 * d, lo:lo + rows, :] = xr + (ar * pr - ai * pi)
        ref[2 * d + 1, lo:lo + rows, :] = xi + (ar * pi + ai * pr)

    rbp = buf_ref.shape[1] - 2 * pad
    nv = rbp // SUBLANES
    cpad = (cbuf_ref.shape[1] - nv) // 2
    in_block = lax.broadcasted_iota(jnp.int32, (rbp, LANES), 0) % SUBLANES
    nlocal = SUBLANES.bit_length() - 1
    nchunk = 3 if nv % 3 == 0 else 1
    crow = rbp // nchunk
    blk = in_block[0:crow]
    for d in range(2):
        for ch in range(nchunk):
            rows = slice(pad + ch * crow, pad + (ch + 1) * crow)
            xr = buf_ref[2 * d, rows, :]
            xi = buf_ref[2 * d + 1, rows, :]
            for k in range(nlocal):
                st = 1 << k
                ar = ap_ref[4 * k + 2 * d:4 * k + 2 * d + 1, :]
                ai = ap_ref[4 * k + 2 * d + 1:4 * k + 2 * d + 2, :]
                keep = blk >= st if d == 0 else blk < SUBLANES - st
                shift = st if d == 0 else SUBLANES - st
                rot = lambda v: pltpu.roll(v.reshape(crow // SUBLANES, SUBLANES, LANES), shift, axis=1).reshape(crow, LANES)
                pr = jnp.where(keep, rot(xr), 0.0)
                pi = jnp.where(keep, rot(xi), 0.0)
                xr, xi = xr + (ar * pr - ai * pi), xi + (ar * pi + ai * pr)
            buf_ref[2 * d, rows, :] = xr
            buf_ref[2 * d + 1, rows, :] = xi
    for d in range(2):
        edge = SUBLANES - 1 if d == 0 else 0
        for q in (2 * d, 2 * d + 1):
            cbuf_ref[q, cpad:cpad + nv, :] = buf_ref[q, pl.ds(pad + edge, nv, stride=SUBLANES), :]
    for m in range(max(0, (nv - 1).bit_length())):
        st = 1 << m
        step(cbuf_ref, 0, nlocal + m, -st, cpad, nv)
        step(cbuf_ref, 1, nlocal + m, st, cpad, nv)
    for d in range(2):
        a8r = a8_ref[2 * SUBLANES * d:2 * SUBLANES * d + SUBLANES, :]
        a8i = a8_ref[2 * SUBLANES * d + SUBLANES:2 * SUBLANES * (d + 1), :]
        for j in range(nv):
            src = j - 1 if d == 0 else j + 1
            if 0 <= src < nv:
                cr = jnp.broadcast_to(cbuf_ref[2 * d, cpad + src:cpad + src + 1, :], (SUBLANES, LANES))
                ci = jnp.broadcast_to(cbuf_ref[2 * d + 1, cpad + src:cpad + src + 1, :], (SUBLANES, LANES))
                rows = slice(pad + j * SUBLANES, pad + (j + 1) * SUBLANES)
                buf_ref[2 * d, rows, :] = buf_ref[2 * d, rows, :] + (a8r * cr - a8i * ci)
                buf_ref[2 * d + 1, rows, :] = buf_ref[2 * d + 1, rows, :] + (a8r * ci + a8i * cr)
    for q in range(4):
        lo = pad + ncc - 1 if q < 2 else pad + 1
        ent = buf_ref[q, lo:lo + ncl, :].T.astype(BF16)
        for b in range(nb):
            hin_ref[q * n:(q + 1) * n, b * ncl:(b + 1) * ncl] = ent[b * n:(b + 1) * n]

    y = jnp.dot(mt_ref[...], u[:, 0:nl], preferred_element_type=F32)
    y = y + jnp.dot(wout_ref[...], hin_ref[...], preferred_element_type=F32)
    for ti in range(t):
        y_ref[ti] = y[ti * hc:(ti + 1) * hc].astype(y_ref.dtype)


def _s5_params(lam_re, lam_im, log_dt, b_re, b_im, c_re, c_im, t, nsteps):
    _, g, n = lam_re.shape
    h = b_re.shape[-1]
    lam_re, lam_im = lam_re.astype(F32), lam_im.astype(F32)
    dt = jnp.exp(log_dt.astype(F32))[..., None]
    lr, li = lam_re * dt, lam_im * dt

    def power(d, k):
        kk = k[None, :, None]
        mag = jnp.exp(lr[d][:, None, :] * kk)
        return mag * jnp.cos(li[d][:, None, :] * kk), mag * jnp.sin(li[d][:, None, :] * kk)

    a_re, a_im = jnp.exp(lr) * jnp.cos(li), jnp.exp(lr) * jnp.sin(li)
    den = lam_re * lam_re + lam_im * lam_im
    f_re = ((a_re - 1.0) * lam_re + a_im * lam_im) / den
    f_im = (a_im * lam_re - (a_re - 1.0) * lam_im) / den
    bt_re = (f_re[..., None] * b_re - f_im[..., None] * b_im).transpose(0, 1, 3, 2)
    bt_im = (f_re[..., None] * b_im + f_im[..., None] * b_re).transpose(0, 1, 3, 2)
    lanes4 = lambda f0, f1, b0, b1: jnp.concatenate([f0, f1, b0, b1], axis=-1)

    strides = t * (2.0 ** jnp.arange(nsteps, dtype=F32))
    exps = jnp.concatenate([jnp.arange(t + 1, dtype=F32), strides, t * jnp.arange(1, SUBLANES + 1, dtype=F32)])
    (allf_re, allf_im), (allb_re, allb_im) = power(0, exps), power(1, exps)
    pf_re, pf_im = allf_re[:, t - 1::-1], allf_im[:, t - 1::-1]
    pb_re, pb_im = allb_re[:, 0:t], allb_im[:, 0:t]
    ea = lanes4(pf_re, pf_re, pb_re, pb_re)
    eb = lanes4(pf_im, pf_im, pb_im, pb_im)
    ba = lanes4(bt_re[0], bt_im[0], bt_re[1], bt_im[1])
    bm = lanes4(-bt_im[0], bt_re[0], -bt_im[1], bt_re[1])
    cc = jnp.concatenate([jnp.concatenate([c_re[0], -c_im[0]], axis=-1),
                          jnp.concatenate([c_re[1], -c_im[1]], axis=-1)], axis=1)
    rf_re, rf_im = allf_re[:, 1:t + 1], allf_im[:, 1:t + 1]
    rb_re, rb_im = allb_re[:, t:0:-1], allb_im[:, t:0:-1]
    oa = lanes4(rf_re, rf_im, rb_re, rb_im)
    ob = lanes4(rf_im, rf_re, rb_im, rb_re)
    ca = lanes4(c_re[0], -c_re[0], c_re[1], -c_re[1])
    cb = lanes4(-c_im[0], -c_im[0], -c_im[1], -c_im[1])
    st0, st1 = t + 1, t + 1 + nsteps
    apf_re, apf_im, apb_re, apb_im = (v[:, st0:st1] for v in (allf_re, allf_im, allb_re, allb_im))
    ap = jnp.stack([apf_re, apf_im, apb_re, apb_im], axis=2).reshape(g, 4 * nsteps, n)
    ap = jnp.concatenate([ap] * (LANES // n), axis=-1)
    rows = -(-4 * nsteps // SUBLANES) * SUBLANES
    ap = jnp.pad(ap, ((0, 0), (0, rows - 4 * nsteps), (0, 0)))
    f8_re, f8_im = allf_re[:, st1:], allf_im[:, st1:]
    b8_re, b8_im = allb_re[:, :st1 - 1:-1], allb_im[:, :st1 - 1:-1]
    a8 = jnp.concatenate([f8_re, f8_im, b8_re, b8_im], axis=1)
    a8 = jnp.concatenate([a8] * (LANES // n), axis=-1)
    return tuple(v.astype(F32) for v in (ea, eb, ba, bm, cc, ca, cb, oa, ob, ap, a8))


def _s5_call(ut, params, t, nb, ncl, ncc, nsteps, n_state, hc):
    ap = params[-2]
    _, sw, width = ut.shape
    nl = nb * ncl
    g = sw // hc
    th = t * hc
    rbp = -(-(ncc + ncl) // SUBLANES) * SUBLANES
    nv = rbp // SUBLANES
    pad = SUBLANES
    cpad = max(SUBLANES, 1 << max(0, (nv - 1).bit_length() - 1))
    assert nb * n_state == LANES and nb == 2 and 4 * nsteps <= ap.shape[1]
    gps = S5_GROUPS_PER_STEP
    assert g % gps == 0
    grp = lambda shp: pl.BlockSpec((gps,) + shp, lambda i: (i, 0, 0))
    return pl.pallas_call(
        functools.partial(_s5_kernel, gps=gps, t=t, hc=hc, nb=nb, ncl=ncl, ncc=ncc, pad=pad, nsteps=nsteps,
                          nstate=n_state),
        out_shape=jax.ShapeDtypeStruct((t, sw, nl), BF16),
        grid=(g // gps,),
        in_specs=[
            pl.BlockSpec((t, gps * hc, width), lambda i: (0, i, 0)),
            grp((t, 4 * n_state)), grp((t, 4 * n_state)), grp((hc, 4 * n_state)), grp((hc, 4 * n_state)),
            grp((2 * hc, 2 * n_state)), grp((hc, 4 * n_state)), grp((hc, 4 * n_state)),
            grp((t, 4 * n_state)), grp((t, 4 * n_state)), grp((ap.shape[1], LANES)), grp((4 * SUBLANES, LANES)),
        ],
        out_specs=pl.BlockSpec((t, gps * hc, nl), lambda i: (0, i, 0)),
        scratch_shapes=[pltpu.VMEM((gps, th, 4 * n_state), F32), pltpu.VMEM((gps, th, 4 * n_state), BF16),
                        pltpu.VMEM((gps, th, th), BF16), pltpu.VMEM((gps, th, 4 * n_state), BF16),
                        pltpu.VMEM((gps, 4, pad + rbp + pad, LANES), F32),
                        pltpu.VMEM((gps, 4, cpad + nv + cpad, LANES), F32),
                        pltpu.VMEM((gps, 4 * n_state, nl), BF16)],
        compiler_params=_cparams(("arbitrary",)),
        name="s5",
    )(ut, *params)


def _merge_kernel(y_ref, u_ref, p_ref, zs_ref, x_ref, gate_ref, d_ref, gw_ref, gb_ref, ow_ref, fg_ref, o_ref):
    sw = y_ref.shape[-1]
    pw = p_ref.shape[-1]
    y = _gelu_tanh(y_ref[0].astype(F32) + d_ref[...] * u_ref[0].astype(F32)).astype(BF16)
    yy = jnp.dot(y, gw_ref[...], preferred_element_type=F32) + gb_ref[...]
    ssm_out = yy[:, :sw] * _sigmoid(yy[:, sw:])
    br_pool = p_ref[0] * zs_ref[0, :, 0:pw]
    br_ssm = (ssm_out * zs_ref[0, :, pw:pw + sw].astype(F32)).astype(BF16)
    mix = jnp.dot(jnp.concatenate([br_pool, br_ssm], axis=1), ow_ref[...], preferred_element_type=F32)
    xo = x_ref[0] + gate_ref[0] * mix
    ms = jnp.mean(xo * xo, axis=-1, keepdims=True)
    o_ref[0] = (xo * lax.rsqrt(ms + EPS) * fg_ref[...]).astype(o_ref.dtype)


def _merge_call(y_ssm, u_ssm, pool_out, zs, x, gate, d_skip, glu_w, glu_b, out_w, final_g, tm):
    b, l, d = x.shape
    sw = y_ssm.shape[-1]
    pw = pool_out.shape[-1]
    mixw = zs.shape[-1]
    tok = lambda wd: pl.BlockSpec((1, tm, wd), lambda i, j: (i, j, 0))
    return pl.pallas_call(
        _merge_kernel,
        out_shape=jax.ShapeDtypeStruct((b, l, d), x.dtype),
        grid=(b, l // tm),
        in_specs=[
            tok(sw), tok(sw), tok(pw), tok(mixw), tok(d),
            pl.BlockSpec((1, 1, d), lambda i, j: (i, 0, 0)),
            _const_spec((1, sw)), _const_spec((sw, 2 * sw)), _const_spec((1, 2 * sw)), _const_spec((mixw, d)),
            _const_spec((1, d)),
        ],
        out_specs=tok(d),
        compiler_params=_cparams(("arbitrary", "arbitrary")),
        name="merge",
    )(y_ssm, u_ssm, pool_out, zs, x, gate, d_skip.reshape(1, sw), glu_w.astype(BF16), glu_b.reshape(1, 2 * sw),
      out_w.astype(BF16), final_g.reshape(1, d))


def kernel(x, c, ctx, c_ctx, ada_w, ada_b, norm_g, in_w, pool_w, pool_scale, s5_lam_re, s5_lam_im, s5_log_dt,
           s5_b_re, s5_b_im, s5_c_re, s5_c_im, s5_d, glu_w, glu_b, out_w, final_g):
    assert ada_w.shape[0] == 1, "single-layer block"
    bsz, seq, d = x.shape
    cl = ctx.shape[1]
    mixw = in_w.shape[2] // 2
    poolw = pool_scale.shape[-1]
    ssmw = s5_d.shape[-1]
    n_grp, n_state = s5_lam_re.shape[2], s5_lam_re.shape[3]
    hc = ssmw // n_grp
    t = CHUNK_T
    assert poolw + ssmw == mixw and seq % t == 0 and cl % t == 0 and 2 * n_state == LANES and LANES % hc == 0

    mod = _ada_call(jnp.concatenate([c, c_ctx[None]], axis=0), ada_w[0], ada_b[0])
    shift, scale, gate = mod[:, :d], mod[:, d:2 * d], mod[:, 2 * d:]
    g1 = norm_g[0].reshape(1, d)
    in_w16 = in_w[0].astype(BF16)

    tm = min(512, seq)
    u_pool, u_ssm, zs = _inproj_call(x, scale[:bsz, None], shift[:bsz, None], g1, in_w16,
                                     (poolw, ssmw, mixw), (False, False, True), min(1024, seq), "inproj")
    sc_c = jnp.broadcast_to(scale[bsz][None, None], (bsz, 1, d))
    sh_c = jnp.broadcast_to(shift[bsz][None, None], (bsz, 1, d))
    assert poolw % ssmw == 0
    (uc_ssm,) = _inproj_call(ctx, sc_c, sh_c, g1, in_w16, (ssmw,), (False,), min(256, cl), "inproj_ctx",
                             wcol=poolw // ssmw)

    pool_out = _pool_call(u_pool, pool_w[0], pool_scale[0])

    ncl, ncc = seq // t, cl // t
    nsteps = max(1, (ncc + ncl - 1).bit_length())
    params = _s5_params(s5_lam_re[0], s5_lam_im[0], s5_log_dt[0], s5_b_re[0], s5_b_im[0], s5_c_re[0], s5_c_im[0],
                        t, nsteps)
    ut = _pack_call(u_ssm, uc_ssm, t)
    yt = _s5_call(ut, params, t, bsz, ncl, ncc, nsteps, n_state, hc)
    y_ssm = _unpack_call(yt, bsz, seq)

    return _merge_call(y_ssm, u_ssm, pool_out, zs, x, gate[:bsz, None], s5_d[0], glu_w[0], glu_b[0], out_w[0],
                       final_g, tm)
```

```python
import functools
import math

import jax
import jax.numpy as jnp
import numpy as np
from jax import lax
from jax.experimental import pallas as pl
from jax.experimental.pallas import tpu as pltpu

GRID_W = 64
POOL_WINDOWS = (2, 4, 8, 16)
EPS = 1e-6
CHUNK_T = 32
S5_GROUPS_PER_STEP = 4
LANES = 128
SUBLANES = 8
VMEM_LIMIT = 56 * 1024 * 1024

F32 = jnp.float32
BF16 = jnp.bfloat16
TN_DIMS = (((0,), (0,)), ((), ()))
NT_DIMS = (((1,), (1,)), ((), ()))


def _cparams(sem):
    return pltpu.CompilerParams(dimension_semantics=sem, vmem_limit_bytes=VMEM_LIMIT)


def _const_spec(shape):
    return pl.BlockSpec(shape, lambda *_: (0,) * len(shape), pipeline_mode=pl.Buffered(1))


def _sigmoid(v):
    return 1.0 / (1.0 + jnp.exp(-v))


def _gelu_tanh(v):
    return 0.5 * v * (1.0 + jnp.tanh(math.sqrt(2.0 / math.pi) * (v + 0.044715 * (v * v * v))))


def _modulated_norm(x, g, scale, shift):
    ms = jnp.mean(x * x, axis=-1, keepdims=True)
    return (x * lax.rsqrt(ms + EPS) * g) * (1.0 + scale) + shift


ADA_LANE_TILES = 6


def _ada_kernel(ct_ref, w_ref, b_ref, o_ref, sb_ref, acc_ref, *, nrows):
    tk, n = w_ref.shape
    step = pl.program_id(0)

    @pl.when(step == 0)
    def _():
        acc_ref[...] = jnp.zeros(acc_ref.shape, F32)

    cv = ct_ref[...]
    s = cv * _sigmoid(cv)
    for r in range(nrows):
        sb_ref[r] = jnp.broadcast_to(s[:, r:r + 1], (tk, LANES))

    group = ADA_LANE_TILES * LANES
    for g0 in range(0, n, group):
        def body(i, accs, g0=g0):
            k0 = pl.multiple_of(i * SUBLANES, SUBLANES)
            sb = [sb_ref[r, pl.ds(k0, SUBLANES), :] for r in range(nrows)]
            out = []
            for lt in range(ADA_LANE_TILES):
                w = w_ref[pl.ds(k0, SUBLANES), g0 + lt * LANES:g0 + (lt + 1) * LANES]
                out.append(tuple(accs[lt][r] + sb[r] * w for r in range(nrows)))
            return tuple(out)

        init = tuple(tuple(jnp.zeros((SUBLANES, LANES), F32) for _ in range(nrows)) for _ in range(ADA_LANE_TILES))
        accs = lax.fori_loop(0, tk // SUBLANES, body, init, unroll=8)
        for lt in range(ADA_LANE_TILES):
            lanes = slice(g0 + lt * LANES, g0 + (lt + 1) * LANES)
            for r in range(nrows):
                acc_ref[r, :, lanes] = acc_ref[r, :, lanes] + accs[lt][r]

    @pl.when(step == pl.num_programs(0) - 1)
    def _():
        o_ref[...] = jnp.zeros(o_ref.shape, F32)
        for r in range(nrows):
            o_ref[r:r + 1, :] = jnp.sum(acc_ref[r], axis=0, keepdims=True) + b_ref[...]


def _ada_call(cvecs, ada_w, ada_b):
    nrows, d = cvecs.shape
    n = ada_w.shape[1]
    tk = 256 if d % 256 == 0 else d
    assert n % (ADA_LANE_TILES * LANES) == 0
    ct = jnp.zeros((d, SUBLANES), F32).at[:, :nrows].set(cvecs.T)
    out = pl.pallas_call(
        functools.partial(_ada_kernel, nrows=nrows),
        out_shape=jax.ShapeDtypeStruct((SUBLANES, n), F32),
        grid=(d // tk,),
        in_specs=[
            pl.BlockSpec((tk, SUBLANES), lambda j: (j, 0)),
            pl.BlockSpec((tk, n), lambda j: (j, 0)),
            _const_spec((1, n)),
        ],
        out_specs=pl.BlockSpec((SUBLANES, n), lambda j: (0, 0)),
        scratch_shapes=[pltpu.VMEM((nrows, tk, LANES), F32), pltpu.VMEM((nrows, SUBLANES, n), F32)],
        compiler_params=_cparams(("arbitrary",)),
        name="ada",
    )(ct, ada_w, ada_b.reshape(1, n))
    return out[:nrows]


def _inproj_kernel(x_ref, sc_ref, sh_ref, g_ref, w_ref, *o_refs, nchunk, silu):
    h = _modulated_norm(x_ref[0], g_ref[...], sc_ref[0], sh_ref[0]).astype(BF16)
    col = 0
    for o_ref, gate in zip(o_refs, silu):
        grouped = len(o_ref.shape) == 4
        cw = o_ref.shape[-1]
        width = cw * o_ref.shape[1] if grouped else cw
        for n0 in range(0, width, nchunk):
            acc = jnp.dot(h, w_ref[:, col + n0:col + n0 + nchunk], preferred_element_type=F32)
            if gate:
                acc = acc * _sigmoid(acc)
            if grouped:
                for c0 in range(0, nchunk, cw):
                    o_ref[0, (n0 + c0) // cw] = acc[:, c0:c0 + cw].astype(o_ref.dtype)
            else:
                o_ref[0, :, n0:n0 + nchunk] = acc.astype(o_ref.dtype)
        col += width


def _inproj_call(x, scale, shift, g, w, widths, silu, tm, name, wcol=0, groups=None):
    b, l, d = x.shape
    nout = sum(widths)
    assert w.shape[1] % nout == 0 and l % tm == 0
    nchunk = min(1024, min(widths))
    assert all(wd % nchunk == 0 for wd in widths)
    groups = groups or (1,) * len(widths)
    shape = lambda wd, ng, rows: (b, rows, wd) if ng == 1 else (b, ng, rows, wd // ng)
    spec = lambda wd, ng: (pl.BlockSpec((1, tm, wd), lambda i, j: (i, j, 0)) if ng == 1 else
                           pl.BlockSpec((1, ng, tm, wd // ng), lambda i, j: (i, 0, j, 0)))
    return pl.pallas_call(
        functools.partial(_inproj_kernel, nchunk=nchunk, silu=silu),
        out_shape=[jax.ShapeDtypeStruct(shape(wd, ng, l), BF16) for wd, ng in zip(widths, groups)],
        grid=(b, l // tm),
        in_specs=[
            pl.BlockSpec((1, tm, d), lambda i, j: (i, j, 0)),
            pl.BlockSpec((1, 1, d), lambda i, j: (i, 0, 0)),
            pl.BlockSpec((1, 1, d), lambda i, j: (i, 0, 0)),
            _const_spec((1, d)),
            pl.BlockSpec((d, nout), lambda i, j: (0, wcol), pipeline_mode=pl.Buffered(1)),
        ],
        out_specs=[spec(wd, ng) for wd, ng in zip(widths, groups)],
        compiler_params=_cparams(("arbitrary", "arbitrary")),
        name=name,
    )(x, scale, shift, g, w)


def _chunk_perm(t):
    i = np.arange(SUBLANES * t)
    src = (i % SUBLANES) * t + i // SUBLANES
    return jnp.asarray(src[:, None] == i[None, :], BF16)


def _pack_kernel(u_ref, uc_ref, perm_ref, ut_ref, scr_ref, *, t, nblk):
    j = pl.program_id(0)
    s = scr_ref.shape[2]
    rows = SUBLANES * t
    ctok = uc_ref.shape[0]

    def permute(src, blk):
        pm = jnp.dot(perm_ref[...], src, preferred_element_type=F32)
        for ti in range(t):
            scr_ref[ti, pl.ds(pl.multiple_of(blk * SUBLANES, SUBLANES), SUBLANES), :] = pm[ti * SUBLANES:(ti + 1) * SUBLANES]

    @pl.when(j < nblk)
    def _():
        def body(blk, carry):
            permute(u_ref[pl.ds(pl.multiple_of(blk * rows, rows), rows), :], blk)
            return carry

        lax.fori_loop(0, LANES // SUBLANES, body, 0, unroll=2)

    @pl.when(j == nblk)
    def _():
        for blk in range(ctok // rows):
            permute(uc_ref[blk * rows:(blk + 1) * rows, :], blk)
        scr_ref[:, ctok // t:LANES, :] = jnp.zeros((t, LANES - ctok // t, s), F32)

    def transpose(ti, carry):
        for k in range(s // LANES):
            ut_ref[0, ti, LANES * k:LANES * (k + 1), :] = scr_ref[ti, :, LANES * k:LANES * (k + 1)].T.astype(BF16)
        return carry

    lax.fori_loop(0, t, transpose, 0, unroll=2)


def _pack_call(u_ssm, uc_ssm, t):
    b, l, s = u_ssm.shape
    tok = LANES * t
    rows = SUBLANES * t
    ntok = b * l
    ctok = b * uc_ssm.shape[1]
    assert ntok % tok == 0 and ctok <= tok and ctok % rows == 0 and s % LANES == 0
    nblk = ntok // tok
    return pl.pallas_call(
        functools.partial(_pack_kernel, t=t, nblk=nblk),
        out_shape=jax.ShapeDtypeStruct((nblk + 1, t, s, LANES), BF16),
        grid=(nblk + 1,),
        in_specs=[pl.BlockSpec((tok, s), lambda j: (jnp.minimum(j, nblk - 1), 0)),
                  _const_spec((ctok, s)), _const_spec((rows, rows))],
        out_specs=pl.BlockSpec((1, t, s, LANES), lambda j: (j, 0, 0, 0)),
        scratch_shapes=[pltpu.VMEM((t, LANES, s), F32)],
        compiler_params=_cparams(("arbitrary",)),
        name="pack",
    )(u_ssm.reshape(ntok, s), uc_ssm.reshape(ctok, s), _chunk_perm(t))


def _unpack_kernel(yt_ref, perm_ref, y_ref, scr_ref, *, t):
    s = scr_ref.shape[2]
    rows = SUBLANES * t

    def transpose(ti, carry):
        for k in range(s // LANES):
            scr_ref[ti, :, LANES * k:LANES * (k + 1)] = yt_ref[0, ti, LANES * k:LANES * (k + 1), :].astype(F32).T
        return carry

    lax.fori_loop(0, t, transpose, 0, unroll=2)

    def body(blk, carry):
        c0 = pl.multiple_of(blk * SUBLANES, SUBLANES)
        src = jnp.concatenate([scr_ref[ti, pl.ds(c0, SUBLANES), :] for ti in range(t)], axis=0).astype(BF16)
        out = jnp.dot(perm_ref[...], src, preferred_element_type=F32)
        y_ref[pl.ds(pl.multiple_of(blk * rows, rows), rows), :] = out.astype(y_ref.dtype)
        return carry

    lax.fori_loop(0, LANES // SUBLANES, body, 0, unroll=2)


def _unpack_call(yt, b, l):
    ngrp, t, s, _ = yt.shape
    nl = ngrp * LANES
    tok = LANES * t
    rows = SUBLANES * t
    y = pl.pallas_call(
        functools.partial(_unpack_kernel, t=t),
        out_shape=jax.ShapeDtypeStruct((nl * t, s), BF16),
        grid=(ngrp,),
        in_specs=[pl.BlockSpec((1, t, s, LANES), lambda j: (j, 0, 0, 0)), _const_spec((rows, rows))],
        out_specs=pl.BlockSpec((tok, s), lambda j: (j, 0)),
        scratch_shapes=[pltpu.VMEM((t, LANES, s), F32)],
        compiler_params=_cparams(("arbitrary",)),
        name="unpack",
    )(yt, _chunk_perm(t).T)
    return y.reshape(b, l, s)


POOL_BLOCK_ROWS = 8
POOL_BLOCKS_PER_STEP = 4


def _pool_kernel(u_ref, *refs, tb, pad):
    nwin = len(POOL_WINDOWS)
    band_refs = refs[:nwin]
    inv_ref, pw_ref, ps_ref, o_ref, pad_ref = refs[nwin:]
    grp = pl.program_id(1)
    l = u_ref.shape[2]
    c = u_ref.shape[3]
    nblk = l // tb
    pad_ref[0:pad, :] = jnp.zeros((pad, c), pad_ref.dtype)
    pad_ref[pad + l:pad + l + pad, :] = jnp.zeros((pad, c), pad_ref.dtype)

    def copy(i, carry):
        r0 = pl.multiple_of(i * tb, tb)
        pad_ref[pl.ds(pl.multiple_of(pad + r0, GRID_W), tb), :] = u_ref[0, 0, pl.ds(r0, tb), :]
        return carry

    lax.fori_loop(0, nblk, copy, 0)

    for gi, w in enumerate(POOL_WINDOWS):
        half = w // 2
        span = band_refs[gi].shape[1]

        @pl.when(grp == gi)
        def _(half=half, span=span, band_ref=band_refs[gi]):
            def blk(i, carry):
                first = POOL_BLOCKS_PER_STEP * i
                r0 = pl.multiple_of(first * tb, POOL_BLOCKS_PER_STEP * tb)
                wins = [pad_ref[pl.ds(pl.multiple_of(pad + r0 + k * tb - half * GRID_W, GRID_W), span), :]
                        for k in range(POOL_BLOCKS_PER_STEP)]
                box = jnp.dot(band_ref[...], jnp.concatenate(wins, axis=1), preferred_element_type=F32)
                dlts = []
                for k in range(POOL_BLOCKS_PER_STEP):
                    j = first + k
                    variant = jnp.where(j == 0, 0, jnp.where(j == nblk - 1, 2, 1))
                    inv = jnp.concatenate([inv_ref[0, variant]] * (c // LANES), axis=1)
                    u_blk = u_ref[0, 0, pl.ds(r0 + k * tb, tb), :].astype(F32)
                    dlts.append((box[:, k * c:(k + 1) * c] * inv - u_blk).astype(BF16))
                o = jnp.dot(jnp.concatenate(dlts, axis=0), pw_ref[0], preferred_element_type=F32) * ps_ref[...]
                o_ref[0, 0, pl.ds(r0, POOL_BLOCKS_PER_STEP * tb), :] = o.astype(o_ref.dtype)
                return carry

            lax.fori_loop(0, nblk // POOL_BLOCKS_PER_STEP, blk, 0)


def _pool_tables(rows, tb):
    br = tb // GRID_W
    i = np.arange(tb)
    ri, ci = i // GRID_W, i % GRID_W
    bands, invs = [], []
    for w in POOL_WINDOWS:
        half = w // 2
        j = np.arange((br + w) * GRID_W)
        dr = (j // GRID_W)[None, :] - ri[:, None]
        dc = (j % GRID_W)[None, :] - ci[:, None]
        bands.append(jnp.asarray((dr >= 0) & (dr < w) & (dc >= -half) & (dc < w - half), BF16))
        cnt_c = np.minimum(ci + w - half, GRID_W) - np.maximum(ci - half, 0)
        per_variant = []
        for r_first in (0, br, rows - br):
            r = r_first + ri
            cnt_r = np.minimum(r + w - half, rows) - np.maximum(r - half, 0)
            per_variant.append(np.float32(1.0) / (cnt_r * cnt_c).astype(np.float32))
        invs.append(np.stack(per_variant))
    inv = np.broadcast_to(np.stack(invs)[..., None], (len(POOL_WINDOWS), 3, tb, LANES))
    return bands, jnp.asarray(inv, F32)


def _pool_call(u_pool, pool_w, pool_scale):
    b, ng, l, c = u_pool.shape
    assert ng == len(POOL_WINDOWS)
    p = ng * c
    rows = l // GRID_W
    tb = POOL_BLOCK_ROWS * GRID_W
    pad = (max(POOL_WINDOWS) // 2) * GRID_W
    assert l % (tb * POOL_BLOCKS_PER_STEP) == 0 and c % LANES == 0
    assert rows >= 3 * POOL_BLOCK_ROWS and max(POOL_WINDOWS) <= 2 * POOL_BLOCK_ROWS
    bands, inv = _pool_tables(rows, tb)
    return pl.pallas_call(
        functools.partial(_pool_kernel, tb=tb, pad=pad),
        out_shape=jax.ShapeDtypeStruct((b, ng, l, c), BF16),
        grid=(b, ng),
        in_specs=[pl.BlockSpec((1, 1, l, c), lambda i, j: (i, j, 0, 0))]
        + [_const_spec(bd.shape) for bd in bands]
        + [
            pl.BlockSpec((1, 3, tb, LANES), lambda i, j: (j, 0, 0, 0)),
            pl.BlockSpec((1, c, c), lambda i, j: (j, 0, 0)),
            pl.BlockSpec((1, c), lambda i, j: (0, j)),
        ],
        out_specs=pl.BlockSpec((1, 1, l, c), lambda i, j: (i, j, 0, 0)),
        scratch_shapes=[pltpu.VMEM((pad + l + pad, c), BF16)],
        compiler_params=_cparams(("arbitrary", "arbitrary")),
        name="pool",
    )(u_pool, *bands, inv, pool_w.astype(BF16), pool_scale.reshape(1, p))


def _dot_hi_nt(a, b):
    a_hi = a.astype(BF16)
    a_lo = (a - a_hi.astype(F32)).astype(BF16)
    b_hi = b.astype(BF16)
    b_lo = (b - b_hi.astype(F32)).astype(BF16)
    dot = lambda p, q: lax.dot_general(p, q, NT_DIMS, preferred_element_type=F32)
    return dot(a_hi, b_hi) + dot(a_hi, b_lo) + dot(a_lo, b_hi)


def _s5_kernel(u_ref, ea_ref, eb_ref, ba_ref, bm_ref, cc_ref, ca_ref, cb_ref, oa_ref, ob_ref, ap_ref, a8_ref, y_ref,
               et_ref, win_ref, mt_ref, wout_ref, buf_ref, cbuf_ref, hin_ref, *, gps, hc, **static):
    @pl.when(pl.program_id(0) == 0)
    def _():
        buf_ref[...] = jnp.zeros(buf_ref.shape, F32)
        cbuf_ref[...] = jnp.zeros(cbuf_ref.shape, F32)

    for gg in range(gps):
        rows = slice(gg * hc, (gg + 1) * hc)
        _s5_group(u_ref.at[:, :, rows, :], ea_ref.at[gg], eb_ref.at[gg], ba_ref.at[gg], bm_ref.at[gg], cc_ref.at[gg],
                  ca_ref.at[gg], cb_ref.at[gg], oa_ref.at[gg], ob_ref.at[gg], ap_ref.at[gg], a8_ref.at[gg],
                  y_ref.at[:, :, rows, :], et_ref.at[gg], win_ref.at[gg], mt_ref.at[gg], wout_ref.at[gg], buf_ref.at[gg],
                  cbuf_ref.at[gg], hin_ref.at[gg],
                  hc=hc, **static)


def _s5_group(u_ref, ea_ref, eb_ref, ba_ref, bm_ref, cc_ref, ca_ref, cb_ref, oa_ref, ob_ref, ap_ref, a8_ref, y_ref,
              et_ref, win_ref, mt_ref, wout_ref, buf_ref, cbuf_ref, hin_ref, *, t, hc, nb, ncl, ncc, pad, nsteps, nstate):
    n = nstate
    th = t * hc
    nl = nb * ncl
    rb = ncc + ncl
    ntile = th // LANES

    ba = ba_ref[...]
    bm = bm_ref[...]
    ca = ca_ref[...]
    cb = cb_ref[...]
    for ti in range(t):
        rows = slice(ti * hc, (ti + 1) * hc)
        e_rows = ba * ea_ref[ti:ti + 1, :] + bm * eb_ref[ti:ti + 1, :]
        et_ref[rows, :] = e_rows
        win_ref[rows, :] = e_rows.astype(BF16)
        wout_ref[rows, :] = (ca * oa_ref[ti:ti + 1, :] + cb * ob_ref[ti:ti + 1, :]).astype(BF16)
    cc = cc_ref[...]
    q_f = _dot_hi_nt(cc[0:hc], et_ref[:, 0:2 * n])
    q_b = _dot_hi_nt(cc[hc:2 * hc], et_ref[:, 2 * n:4 * n])
    zeros = jnp.zeros((hc, th), F32)
    line_b = jnp.concatenate([zeros, q_b], axis=1)
    line_f = jnp.concatenate([q_f, zeros], axis=1)
    per_tile = LANES // hc
    rolled_b = [line_b if r == 0 else pltpu.roll(line_b, hc * r, axis=1) for r in range(per_tile)]
    rolled_f = [line_f if r == per_tile - 1 else pltpu.roll(line_f, 2 * th - hc * (per_tile - 1 - r), axis=1)
                for r in range(per_tile)]
    for ti in range(t):
        a, r = divmod(ti, per_tile)
        row_b = rolled_b[r][:, th - LANES * a:2 * th - LANES * a]
        off_f = LANES * (ntile - 1 - a)
        row_f = rolled_f[r][:, off_f:off_f + th]
        mt_ref[ti * hc:(ti + 1) * hc, :] = (row_b + row_f).astype(BF16)

    u = jnp.concatenate([jnp.concatenate([u_ref[cg, ti] for cg in range(u_ref.shape[0])], axis=1)
                         for ti in range(t)], axis=0)
    st_all = lax.dot_general(win_ref[...], u, TN_DIMS, preferred_element_type=F32)
    first_half = lax.broadcasted_iota(jnp.int32, (LANES, LANES), 1) < n
    for q in range(4):
        rows_q = st_all[q * n:(q + 1) * n]
        lat = jnp.concatenate([rows_q[:, b * ncl:(b + 1) * ncl] for b in range(nb)], axis=0).T
        cblk = jnp.concatenate([rows_q[:, nl:nl + LANES]] * nb, axis=0).T
        ctx = jnp.where(first_half, cblk, jnp.concatenate([cblk[ncc:], cblk[:ncc]], axis=0))[0:ncc]
        if q < 2:
            buf_ref[q, pad:pad + ncc, :] = ctx
            buf_ref[q, pad + ncc:pad + rb, :] = lat
        else:
            buf_ref[q, pad:pad + ncl, :] = lat
            buf_ref[q, pad + ncl:pad + rb, :] = ctx
    def step(ref, d, k, sgn, lo, rows, keep=None):
        ar = ap_ref[4 * k + 2 * d:4 * k + 2 * d + 1, :]
        ai = ap_ref[4 * k + 2 * d + 1:4 * k + 2 * d + 2, :]
        pr = ref[2 * d, lo + sgn:lo + sgn + rows, :]
        pi = ref[2 * d + 1, lo + sgn:lo + sgn + rows, :]
        if keep is not None:
            pr = jnp.where(keep, pr, 0.0)
            pi = jnp.where(keep, pi, 0.0)
        xr = ref[2 * d, lo:lo + rows, :]
        xi = ref[2 * d + 1, lo:lo + rows, :]
        ref[2 * d, lo:lo + rows, :] = xr + (ar * pr - ai * pi)
        ref[2 * d + 1, lo:lo + rows, :] = xi + (ar * pi + ai * pr)

    rbp = buf_ref.shape[1] - 2 * pad
    nv = rbp // SUBLANES
    cpad = (cbuf_ref.shape[1] - nv) // 2
    in_block = lax.broadcasted_iota(jnp.int32, (rbp, LANES), 0) % SUBLANES
    nlocal = SUBLANES.bit_length() - 1
    nchunk = 3 if nv % 3 == 0 else 1
    crow = rbp // nchunk
    blk = in_block[0:crow]
    for d in range(2):
        for ch in range(nchunk):
            rows = slice(pad + ch * crow, pad + (ch + 1) * crow)
            xr = buf_ref[2 * d, rows, :]
            xi = buf_ref[2 * d + 1, rows, :]
            for k in range(nlocal):
                st = 1 << k
                ar = ap_ref[4 * k + 2 * d:4 * k + 2 * d + 1, :]
                ai = ap_ref[4 * k + 2 * d + 1:4 * k + 2 * d + 2, :]
                keep = blk >= st if d == 0 else blk < SUBLANES - st
                shift = st if d == 0 else SUBLANES - st
                rot = lambda v: pltpu.roll(v.reshape(crow // SUBLANES, SUBLANES, LANES), shift, axis=1).reshape(crow, LANES)
                pr = jnp.where(keep, rot(xr), 0.0)
                pi = jnp.where(keep, rot(xi), 0.0)
                xr, xi = xr + (ar * pr - ai * pi), xi + (ar * pi + ai * pr)
            buf_ref[2 * d, rows, :] = xr
            buf_ref[2 * d + 1, rows, :] = xi
    for d in range(2):
        edge = SUBLANES - 1 if d == 0 else 0
        for q in (2 * d, 2 * d + 1):
            cbuf_ref[q, cpad:cpad + nv, :] = buf_ref[q, pl.ds(pad + edge, nv, stride=SUBLANES), :]
    for m in range(max(0, (nv - 1).bit_length())):
        st = 1 << m
        step(cbuf_ref, 0, nlocal + m, -st, cpad, nv)
        step(cbuf_ref, 1, nlocal + m, st, cpad, nv)
    for d in range(2):
        a8r = a8_ref[2 * SUBLANES * d:2 * SUBLANES * d + SUBLANES, :]
        a8i = a8_ref[2 * SUBLANES * d + SUBLANES:2 * SUBLANES * (d + 1), :]
        for j in range(nv):
            src = j - 1 if d == 0 else j + 1
            if 0 <= src < nv:
                cr = jnp.broadcast_to(cbuf_ref[2 * d, cpad + src:cpad + src + 1, :], (SUBLANES, LANES))
                ci = jnp.broadcast_to(cbuf_ref[2 * d + 1, cpad + src:cpad + src + 1, :], (SUBLANES, LANES))
                rows = slice(pad + j * SUBLANES, pad + (j + 1) * SUBLANES)
                buf_ref[2 * d, rows, :] = buf_ref[2 * d, rows, :] + (a8r * cr - a8i * ci)
                buf_ref[2 * d + 1, rows, :] = buf_ref[2 * d + 1, rows, :] + (a8r * ci + a8i * cr)
    for q in range(4):
        lo = pad + ncc - 1 if q < 2 else pad + 1
        ent = buf_ref[q, lo:lo + ncl, :].T.astype(BF16)
        for b in range(nb):
            hin_ref[q * n:(q + 1) * n, b * ncl:(b + 1) * ncl] = ent[b * n:(b + 1) * n]

    y = jnp.dot(mt_ref[...], u[:, 0:nl], preferred_element_type=F32)
    y = y + jnp.dot(wout_ref[...], hin_ref[...], preferred_element_type=F32)
    for ti in range(t):
        for cg in range(y_ref.shape[0]):
            y_ref[cg, ti] = y[ti * hc:(ti + 1) * hc, cg * LANES:(cg + 1) * LANES].astype(y_ref.dtype)


def _s5_params(lam_re, lam_im, log_dt, b_re, b_im, c_re, c_im, t, nsteps):
    _, g, n = lam_re.shape
    h = b_re.shape[-1]
    lam_re, lam_im = lam_re.astype(F32), lam_im.astype(F32)
    dt = jnp.exp(log_dt.astype(F32))[..., None]
    lr, li = lam_re * dt, lam_im * dt

    def power(d, k):
        kk = k[None, :, None]
        mag = jnp.exp(lr[d][:, None, :] * kk)
        return mag * jnp.cos(li[d][:, None, :] * kk), mag * jnp.sin(li[d][:, None, :] * kk)

    a_re, a_im = jnp.exp(lr) * jnp.cos(li), jnp.exp(lr) * jnp.sin(li)
    den = lam_re * lam_re + lam_im * lam_im
    f_re = ((a_re - 1.0) * lam_re + a_im * lam_im) / den
    f_im = (a_im * lam_re - (a_re - 1.0) * lam_im) / den
    bt_re = (f_re[..., None] * b_re - f_im[..., None] * b_im).transpose(0, 1, 3, 2)
    bt_im = (f_re[..., None] * b_im + f_im[..., None] * b_re).transpose(0, 1, 3, 2)
    lanes4 = lambda f0, f1, b0, b1: jnp.concatenate([f0, f1, b0, b1], axis=-1)

    strides = t * (2.0 ** jnp.arange(nsteps, dtype=F32))
    exps = jnp.concatenate([jnp.arange(t + 1, dtype=F32), strides, t * jnp.arange(1, SUBLANES + 1, dtype=F32)])
    (allf_re, allf_im), (allb_re, allb_im) = power(0, exps), power(1, exps)
    pf_re, pf_im = allf_re[:, t - 1::-1], allf_im[:, t - 1::-1]
    pb_re, pb_im = allb_re[:, 0:t], allb_im[:, 0:t]
    ea = lanes4(pf_re, pf_re, pb_re, pb_re)
    eb = lanes4(pf_im, pf_im, pb_im, pb_im)
    ba = lanes4(bt_re[0], bt_im[0], bt_re[1], bt_im[1])
    bm = lanes4(-bt_im[0], bt_re[0], -bt_im[1], bt_re[1])
    cc = jnp.concatenate([jnp.concatenate([c_re[0], -c_im[0]], axis=-1),
                          jnp.concatenate([c_re[1], -c_im[1]], axis=-1)], axis=1)
    rf_re, rf_im = allf_re[:, 1:t + 1], allf_im[:, 1:t + 1]
    rb_re, rb_im = allb_re[:, t:0:-1], allb_im[:, t:0:-1]
    oa = lanes4(rf_re, rf_im, rb_re, rb_im)
    ob = lanes4(rf_im, rf_re, rb_im, rb_re)
    ca = lanes4(c_re[0], -c_re[0], c_re[1], -c_re[1])
    cb = lanes4(-c_im[0], -c_im[0], -c_im[1], -c_im[1])
    st0, st1 = t + 1, t + 1 + nsteps
    apf_re, apf_im, apb_re, apb_im = (v[:, st0:st1] for v in (allf_re, allf_im, allb_re, allb_im))
    ap = jnp.stack([apf_re, apf_im, apb_re, apb_im], axis=2).reshape(g, 4 * nsteps, n)
    ap = jnp.concatenate([ap] * (LANES // n), axis=-1)
    rows = -(-4 * nsteps // SUBLANES) * SUBLANES
    ap = jnp.pad(ap, ((0, 0), (0, rows - 4 * nsteps), (0, 0)))
    f8_re, f8_im = allf_re[:, st1:], allf_im[:, st1:]
    b8_re, b8_im = allb_re[:, :st1 - 1:-1], allb_im[:, :st1 - 1:-1]
    a8 = jnp.concatenate([f8_re, f8_im, b8_re, b8_im], axis=1)
    a8 = jnp.concatenate([a8] * (LANES // n), axis=-1)
    return tuple(v.astype(F32) for v in (ea, eb, ba, bm, cc, ca, cb, oa, ob, ap, a8))


def _s5_call(ut, params, t, nb, ncl, ncc, nsteps, n_state, hc):
    ap = params[-2]
    ngrp_in, _, sw, _ = ut.shape
    nl = nb * ncl
    g = sw // hc
    th = t * hc
    rbp = -(-(ncc + ncl) // SUBLANES) * SUBLANES
    nv = rbp // SUBLANES
    pad = SUBLANES
    cpad = max(SUBLANES, 1 << max(0, (nv - 1).bit_length() - 1))
    assert nb * n_state == LANES and nb == 2 and 4 * nsteps <= ap.shape[1]
    gps = S5_GROUPS_PER_STEP
    assert g % gps == 0
    grp = lambda shp: pl.BlockSpec((gps,) + shp, lambda i: (i, 0, 0))
    return pl.pallas_call(
        functools.partial(_s5_kernel, gps=gps, t=t, hc=hc, nb=nb, ncl=ncl, ncc=ncc, pad=pad, nsteps=nsteps,
                          nstate=n_state),
        out_shape=jax.ShapeDtypeStruct((nl // LANES, t, sw, LANES), BF16),
        grid=(g // gps,),
        in_specs=[
            pl.BlockSpec((ngrp_in, t, gps * hc, LANES), lambda i: (0, 0, i, 0)),
            grp((t, 4 * n_state)), grp((t, 4 * n_state)), grp((hc, 4 * n_state)), grp((hc, 4 * n_state)),
            grp((2 * hc, 2 * n_state)), grp((hc, 4 * n_state)), grp((hc, 4 * n_state)),
            grp((t, 4 * n_state)), grp((t, 4 * n_state)), grp((ap.shape[1], LANES)), grp((4 * SUBLANES, LANES)),
        ],
        out_specs=pl.BlockSpec((nl // LANES, t, gps * hc, LANES), lambda i: (0, 0, i, 0)),
        scratch_shapes=[pltpu.VMEM((gps, th, 4 * n_state), F32), pltpu.VMEM((gps, th, 4 * n_state), BF16),
                        pltpu.VMEM((gps, th, th), BF16), pltpu.VMEM((gps, th, 4 * n_state), BF16),
                        pltpu.VMEM((gps, 4, pad + rbp + pad, LANES), F32),
                        pltpu.VMEM((gps, 4, cpad + nv + cpad, LANES), F32),
                        pltpu.VMEM((gps, 4 * n_state, nl), BF16)],
        compiler_params=_cparams(("arbitrary",)),
        name="s5",
    )(ut, *params)


def _merge_kernel(y_ref, u_ref, p_ref, zs_ref, x_ref, gate_ref, d_ref, gw_ref, gb_ref, ow_ref, fg_ref, o_ref):
    sw = y_ref.shape[-1]
    pw = p_ref.shape[1] * p_ref.shape[-1]
    y = _gelu_tanh(y_ref[0].astype(F32) + d_ref[...] * u_ref[0].astype(F32)).astype(BF16)
    yy = jnp.dot(y, gw_ref[...], preferred_element_type=F32) + gb_ref[...]
    ssm_out = yy[:, :sw] * _sigmoid(yy[:, sw:])
    pool = jnp.concatenate([p_ref[0, g] for g in range(p_ref.shape[1])], axis=1)
    br_pool = pool * zs_ref[0, :, 0:pw]
    br_ssm = (ssm_out * zs_ref[0, :, pw:pw + sw].astype(F32)).astype(BF16)
    mix = jnp.dot(jnp.concatenate([br_pool, br_ssm], axis=1), ow_ref[...], preferred_element_type=F32)
    xo = x_ref[0] + gate_ref[0] * mix
    ms = jnp.mean(xo * xo, axis=-1, keepdims=True)
    o_ref[0] = (xo * lax.rsqrt(ms + EPS) * fg_ref[...]).astype(o_ref.dtype)


def _merge_call(y_ssm, u_ssm, pool_out, zs, x, gate, d_skip, glu_w, glu_b, out_w, final_g, tm):
    b, l, d = x.shape
    sw = y_ssm.shape[-1]
    ng, pc = pool_out.shape[1], pool_out.shape[-1]
    mixw = zs.shape[-1]
    tok = lambda wd: pl.BlockSpec((1, tm, wd), lambda i, j: (i, j, 0))
    return pl.pallas_call(
        _merge_kernel,
        out_shape=jax.ShapeDtypeStruct((b, l, d), x.dtype),
        grid=(b, l // tm),
        in_specs=[
            tok(sw), tok(sw), pl.BlockSpec((1, ng, tm, pc), lambda i, j: (i, 0, j, 0)), tok(mixw), tok(d),
            pl.BlockSpec((1, 1, d), lambda i, j: (i, 0, 0)),
            _const_spec((1, sw)), _const_spec((sw, 2 * sw)), _const_spec((1, 2 * sw)), _const_spec((mixw, d)),
            _const_spec((1, d)),
        ],
        out_specs=tok(d),
        compiler_params=_cparams(("arbitrary", "arbitrary")),
        name="merge",
    )(y_ssm, u_ssm, pool_out, zs, x, gate, d_skip.reshape(1, sw), glu_w.astype(BF16), glu_b.reshape(1, 2 * sw),
      out_w.astype(BF16), final_g.reshape(1, d))


def kernel(x, c, ctx, c_ctx, ada_w, ada_b, norm_g, in_w, pool_w, pool_scale, s5_lam_re, s5_lam_im, s5_log_dt,
           s5_b_re, s5_b_im, s5_c_re, s5_c_im, s5_d, glu_w, glu_b, out_w, final_g):
    assert ada_w.shape[0] == 1, "single-layer block"
    bsz, seq, d = x.shape
    cl = ctx.shape[1]
    mixw = in_w.shape[2] // 2
    poolw = pool_scale.shape[-1]
    ssmw = s5_d.shape[-1]
    n_grp, n_state = s5_lam_re.shape[2], s5_lam_re.shape[3]
    hc = ssmw // n_grp
    t = CHUNK_T
    assert poolw + ssmw == mixw and seq % t == 0 and cl % t == 0 and 2 * n_state == LANES and LANES % hc == 0

    mod = _ada_call(jnp.concatenate([c, c_ctx[None]], axis=0), ada_w[0], ada_b[0])
    shift, scale, gate = mod[:, :d], mod[:, d:2 * d], mod[:, 2 * d:]
    g1 = norm_g[0].reshape(1, d)
    in_w16 = in_w[0].astype(BF16)

    tm = min(512, seq)
    u_pool, u_ssm, zs = _inproj_call(x, scale[:bsz, None], shift[:bsz, None], g1, in_w16,
                                     (poolw, ssmw, mixw), (False, False, True), min(1024, seq), "inproj",
                                     groups=(len(POOL_WINDOWS), 1, 1))
    sc_c = jnp.broadcast_to(scale[bsz][None, None], (bsz, 1, d))
    sh_c = jnp.broadcast_to(shift[bsz][None, None], (bsz, 1, d))
    assert poolw % ssmw == 0
    (uc_ssm,) = _inproj_call(ctx, sc_c, sh_c, g1, in_w16, (ssmw,), (False,), min(256, cl), "inproj_ctx",
                             wcol=poolw // ssmw)

    pool_out = _pool_call(u_pool, pool_w[0], pool_scale[0])

    ncl, ncc = seq // t, cl // t
    nsteps = max(1, (ncc + ncl - 1).bit_length())
    params = _s5_params(s5_lam_re[0], s5_lam_im[0], s5_log_dt[0], s5_b_re[0], s5_b_im[0], s5_c_re[0], s5_c_im[0],
                        t, nsteps)
    ut = _pack_call(u_ssm, uc_ssm, t)
    yt = _s5_call(ut, params, t, bsz, ncl, ncc, nsteps, n_state, hc)
    y_ssm = _unpack_call(yt, bsz, seq)

    return _merge_call(y_ssm, u_ssm, pool_out, zs, x, gate[:bsz, None], s5_d[0], glu_w[0], glu_b[0], out_w[0],
                       final_g, tm)
```

```python
import functools
import math

import jax
import jax.numpy as jnp
import numpy as np
from jax import lax
from jax.experimental import pallas as pl
from jax.experimental.pallas import tpu as pltpu

GRID_W = 64
POOL_WINDOWS = (2, 4, 8, 16)
EPS = 1e-6
CHUNK_T = 32
S5_GROUPS_PER_STEP = 4
LANES = 128
SUBLANES = 8
VMEM_LIMIT = 56 * 1024 * 1024

F32 = jnp.float32
BF16 = jnp.bfloat16
TN_DIMS = (((0,), (0,)), ((), ()))
NT_DIMS = (((1,), (1,)), ((), ()))


def _cparams(sem):
    return pltpu.CompilerParams(dimension_semantics=sem, vmem_limit_bytes=VMEM_LIMIT)


def _const_spec(shape):
    return pl.BlockSpec(shape, lambda *_: (0,) * len(shape), pipeline_mode=pl.Buffered(1))


def _sigmoid(v):
    return 1.0 / (1.0 + jnp.exp(-v))


def _gelu_tanh(v):
    return 0.5 * v * (1.0 + jnp.tanh(math.sqrt(2.0 / math.pi) * (v + 0.044715 * (v * v * v))))


def _modulated_norm(x, g, scale, shift):
    ms = jnp.mean(x * x, axis=-1, keepdims=True)
    return (x * lax.rsqrt(ms + EPS) * g) * (1.0 + scale) + shift


ADA_LANE_TILES = 6


def _ada_kernel(ct_ref, w_ref, b_ref, o_ref, sb_ref, acc_ref, *, nrows):
    tk, n = w_ref.shape
    step = pl.program_id(0)

    @pl.when(step == 0)
    def _():
        acc_ref[...] = jnp.zeros(acc_ref.shape, F32)

    cv = ct_ref[...]
    s = cv * _sigmoid(cv)
    for r in range(nrows):
        sb_ref[r] = jnp.broadcast_to(s[:, r:r + 1], (tk, LANES))

    group = ADA_LANE_TILES * LANES
    for g0 in range(0, n, group):
        def body(i, accs, g0=g0):
            k0 = pl.multiple_of(i * SUBLANES, SUBLANES)
            sb = [sb_ref[r, pl.ds(k0, SUBLANES), :] for r in range(nrows)]
            out = []
            for lt in range(ADA_LANE_TILES):
                w = w_ref[pl.ds(k0, SUBLANES), g0 + lt * LANES:g0 + (lt + 1) * LANES]
                out.append(tuple(accs[lt][r] + sb[r] * w for r in range(nrows)))
            return tuple(out)

        init = tuple(tuple(jnp.zeros((SUBLANES, LANES), F32) for _ in range(nrows)) for _ in range(ADA_LANE_TILES))
        accs = lax.fori_loop(0, tk // SUBLANES, body, init, unroll=8)
        for lt in range(ADA_LANE_TILES):
            lanes = slice(g0 + lt * LANES, g0 + (lt + 1) * LANES)
            for r in range(nrows):
                acc_ref[r, :, lanes] = acc_ref[r, :, lanes] + accs[lt][r]

    @pl.when(step == pl.num_programs(0) - 1)
    def _():
        o_ref[...] = jnp.zeros(o_ref.shape, F32)
        for r in range(nrows):
            o_ref[r:r + 1, :] = jnp.sum(acc_ref[r], axis=0, keepdims=True) + b_ref[...]


def _ada_call(cvecs, ada_w, ada_b):
    nrows, d = cvecs.shape
    n = ada_w.shape[1]
    tk = 256 if d % 256 == 0 else d
    assert n % (ADA_LANE_TILES * LANES) == 0
    ct = jnp.zeros((d, SUBLANES), F32).at[:, :nrows].set(cvecs.T)
    out = pl.pallas_call(
        functools.partial(_ada_kernel, nrows=nrows),
        out_shape=jax.ShapeDtypeStruct((SUBLANES, n), F32),
        grid=(d // tk,),
        in_specs=[
            pl.BlockSpec((tk, SUBLANES), lambda j: (j, 0)),
            pl.BlockSpec((tk, n), lambda j: (j, 0)),
            _const_spec((1, n)),
        ],
        out_specs=pl.BlockSpec((SUBLANES, n), lambda j: (0, 0)),
        scratch_shapes=[pltpu.VMEM((nrows, tk, LANES), F32), pltpu.VMEM((nrows, SUBLANES, n), F32)],
        compiler_params=_cparams(("arbitrary",)),
        name="ada",
    )(ct, ada_w, ada_b.reshape(1, n))
    return out[:nrows]


def _inproj_kernel(x_ref, sc_ref, sh_ref, g_ref, w_ref, *o_refs, nchunk, silu):
    h = _modulated_norm(x_ref[0], g_ref[...], sc_ref[0], sh_ref[0]).astype(BF16)
    col = 0
    for o_ref, gate in zip(o_refs, silu):
        grouped = len(o_ref.shape) == 4
        cw = o_ref.shape[-1]
        width = cw * o_ref.shape[1] if grouped else cw
        for n0 in range(0, width, nchunk):
            acc = jnp.dot(h, w_ref[:, col + n0:col + n0 + nchunk], preferred_element_type=F32)
            if gate:
                acc = acc * _sigmoid(acc)
            if grouped:
                for c0 in range(0, nchunk, cw):
                    o_ref[0, (n0 + c0) // cw] = acc[:, c0:c0 + cw].astype(o_ref.dtype)
            else:
                o_ref[0, :, n0:n0 + nchunk] = acc.astype(o_ref.dtype)
        col += width


def _inproj_call(x, scale, shift, g, w, widths, silu, tm, name, wcol=0, groups=None):
    b, l, d = x.shape
    nout = sum(widths)
    assert w.shape[1] % nout == 0 and l % tm == 0
    nchunk = min(1024, min(widths))
    assert all(wd % nchunk == 0 for wd in widths)
    groups = groups or (1,) * len(widths)
    shape = lambda wd, ng, rows: (b, rows, wd) if ng == 1 else (b, ng, rows, wd // ng)
    spec = lambda wd, ng: (pl.BlockSpec((1, tm, wd), lambda i, j: (i, j, 0)) if ng == 1 else
                           pl.BlockSpec((1, ng, tm, wd // ng), lambda i, j: (i, 0, j, 0)))
    return pl.pallas_call(
        functools.partial(_inproj_kernel, nchunk=nchunk, silu=silu),
        out_shape=[jax.ShapeDtypeStruct(shape(wd, ng, l), BF16) for wd, ng in zip(widths, groups)],
        grid=(b, l // tm),
        in_specs=[
            pl.BlockSpec((1, tm, d), lambda i, j: (i, j, 0)),
            pl.BlockSpec((1, 1, d), lambda i, j: (i, 0, 0)),
            pl.BlockSpec((1, 1, d), lambda i, j: (i, 0, 0)),
            _const_spec((1, d)),
            pl.BlockSpec((d, nout), lambda i, j: (0, wcol), pipeline_mode=pl.Buffered(1)),
        ],
        out_specs=[spec(wd, ng) for wd, ng in zip(widths, groups)],
        compiler_params=_cparams(("arbitrary", "arbitrary")),
        name=name,
    )(x, scale, shift, g, w)


def _chunk_perm(t):
    i = np.arange(SUBLANES * t)
    src = (i % SUBLANES) * t + i // SUBLANES
    return jnp.asarray(src[:, None] == i[None, :], BF16)


def _pack_kernel(u_ref, uc_ref, perm_ref, ut_ref, scr_ref, *, t, nblk):
    j = pl.program_id(0)
    s = scr_ref.shape[2]
    rows = SUBLANES * t
    ctok = uc_ref.shape[0]

    def permute(src, blk):
        pm = jnp.dot(perm_ref[...], src, preferred_element_type=F32)
        for ti in range(t):
            scr_ref[ti, pl.ds(pl.multiple_of(blk * SUBLANES, SUBLANES), SUBLANES), :] = pm[ti * SUBLANES:(ti + 1) * SUBLANES]

    @pl.when(j < nblk)
    def _():
        def body(blk, carry):
            permute(u_ref[pl.ds(pl.multiple_of(blk * rows, rows), rows), :], blk)
            return carry

        lax.fori_loop(0, LANES // SUBLANES, body, 0, unroll=2)

    @pl.when(j == nblk)
    def _():
        for blk in range(ctok // rows):
            permute(uc_ref[blk * rows:(blk + 1) * rows, :], blk)
        scr_ref[:, ctok // t:LANES, :] = jnp.zeros((t, LANES - ctok // t, s), F32)

    def transpose(ti, carry):
        for k in range(s // LANES):
            ut_ref[0, ti, LANES * k:LANES * (k + 1), :] = scr_ref[ti, :, LANES * k:LANES * (k + 1)].T.astype(BF16)
        return carry

    lax.fori_loop(0, t, transpose, 0, unroll=2)


def _pack_call(u_ssm, uc_ssm, t):
    b, l, s = u_ssm.shape
    tok = LANES * t
    rows = SUBLANES * t
    ntok = b * l
    ctok = b * uc_ssm.shape[1]
    assert ntok % tok == 0 and ctok <= tok and ctok % rows == 0 and s % LANES == 0
    nblk = ntok // tok
    return pl.pallas_call(
        functools.partial(_pack_kernel, t=t, nblk=nblk),
        out_shape=jax.ShapeDtypeStruct((nblk + 1, t, s, LANES), BF16),
        grid=(nblk + 1,),
        in_specs=[pl.BlockSpec((tok, s), lambda j: (jnp.minimum(j, nblk - 1), 0)),
                  _const_spec((ctok, s)), _const_spec((rows, rows))],
        out_specs=pl.BlockSpec((1, t, s, LANES), lambda j: (j, 0, 0, 0)),
        scratch_shapes=[pltpu.VMEM((t, LANES, s), F32)],
        compiler_params=_cparams(("arbitrary",)),
        name="pack",
    )(u_ssm.reshape(ntok, s), uc_ssm.reshape(ctok, s), _chunk_perm(t))


def _unpack_kernel(yt_ref, perm_ref, y_ref, scr_ref, *, t):
    s = scr_ref.shape[2]
    rows = SUBLANES * t

    def transpose(ti, carry):
        for k in range(s // LANES):
            scr_ref[ti, :, LANES * k:LANES * (k + 1)] = yt_ref[0, ti, LANES * k:LANES * (k + 1), :].astype(F32).T
        return carry

    lax.fori_loop(0, t, transpose, 0, unroll=2)

    def body(blk, carry):
        c0 = pl.multiple_of(blk * SUBLANES, SUBLANES)
        src = jnp.concatenate([scr_ref[ti, pl.ds(c0, SUBLANES), :] for ti in range(t)], axis=0).astype(BF16)
        out = jnp.dot(perm_ref[...], src, preferred_element_type=F32)
        y_ref[pl.ds(pl.multiple_of(blk * rows, rows), rows), :] = out.astype(y_ref.dtype)
        return carry

    lax.fori_loop(0, LANES // SUBLANES, body, 0, unroll=2)


def _unpack_call(yt, b, l):
    ngrp, t, s, _ = yt.shape
    nl = ngrp * LANES
    tok = LANES * t
    rows = SUBLANES * t
    y = pl.pallas_call(
        functools.partial(_unpack_kernel, t=t),
        out_shape=jax.ShapeDtypeStruct((nl * t, s), BF16),
        grid=(ngrp,),
        in_specs=[pl.BlockSpec((1, t, s, LANES), lambda j: (j, 0, 0, 0)), _const_spec((rows, rows))],
        out_specs=pl.BlockSpec((tok, s), lambda j: (j, 0)),
        scratch_shapes=[pltpu.VMEM((t, LANES, s), F32)],
        compiler_params=_cparams(("arbitrary",)),
        name="unpack",
    )(yt, _chunk_perm(t).T)
    return y.reshape(b, l, s)


POOL_BLOCK_ROWS = 8
POOL_BLOCKS_PER_STEP = 4


def _pool_kernel(u_ref, *refs, tb, pad):
    nwin = len(POOL_WINDOWS)
    band_refs = refs[:nwin]
    inv_ref, pw_ref, ps_ref, wa_ref, wb_ref, o_ref, wa16_ref, wb16_ref, pad_ref = refs[nwin:]
    wa16_ref[...] = wa_ref[...].astype(wa16_ref.dtype)
    wb16_ref[...] = wb_ref[...].astype(wb16_ref.dtype)
    grp = pl.program_id(1)
    l = u_ref.shape[2]
    c = u_ref.shape[3]
    nblk = l // tb
    pad_ref[0:pad, :] = jnp.zeros((pad, c), pad_ref.dtype)
    pad_ref[pad + l:pad + l + pad, :] = jnp.zeros((pad, c), pad_ref.dtype)

    def copy(i, carry):
        r0 = pl.multiple_of(i * tb, tb)
        pad_ref[pl.ds(pl.multiple_of(pad + r0, GRID_W), tb), :] = u_ref[0, 0, pl.ds(r0, tb), :]
        return carry

    lax.fori_loop(0, nblk, copy, 0)

    for gi, w in enumerate(POOL_WINDOWS):
        half = w // 2
        span = band_refs[gi].shape[1]

        @pl.when(grp == gi)
        def _(half=half, span=span, band_ref=band_refs[gi]):
            def blk(i, carry):
                first = POOL_BLOCKS_PER_STEP * i
                r0 = pl.multiple_of(first * tb, POOL_BLOCKS_PER_STEP * tb)
                wins = [pad_ref[pl.ds(pl.multiple_of(pad + r0 + k * tb - half * GRID_W, GRID_W), span), :]
                        for k in range(POOL_BLOCKS_PER_STEP)]
                box = jnp.dot(band_ref[...], jnp.concatenate(wins, axis=1), preferred_element_type=F32)
                dlts = []
                for k in range(POOL_BLOCKS_PER_STEP):
                    j = first + k
                    variant = jnp.where(j == 0, 0, jnp.where(j == nblk - 1, 2, 1))
                    inv = jnp.concatenate([inv_ref[0, variant]] * (c // LANES), axis=1)
                    u_blk = u_ref[0, 0, pl.ds(r0 + k * tb, tb), :].astype(F32)
                    dlts.append((box[:, k * c:(k + 1) * c] * inv - u_blk).astype(BF16))
                o = jnp.dot(jnp.concatenate(dlts, axis=0), pw_ref[0], preferred_element_type=F32) * ps_ref[...]
                o_ref[0, 0, pl.ds(r0, POOL_BLOCKS_PER_STEP * tb), :] = o.astype(o_ref.dtype)
                return carry

            lax.fori_loop(0, nblk // POOL_BLOCKS_PER_STEP, blk, 0)


def _pool_tables(rows, tb):
    br = tb // GRID_W
    i = np.arange(tb)
    ri, ci = i // GRID_W, i % GRID_W
    bands, invs = [], []
    for w in POOL_WINDOWS:
        half = w // 2
        j = np.arange((br + w) * GRID_W)
        dr = (j // GRID_W)[None, :] - ri[:, None]
        dc = (j % GRID_W)[None, :] - ci[:, None]
        bands.append(jnp.asarray((dr >= 0) & (dr < w) & (dc >= -half) & (dc < w - half), BF16))
        cnt_c = np.minimum(ci + w - half, GRID_W) - np.maximum(ci - half, 0)
        per_variant = []
        for r_first in (0, br, rows - br):
            r = r_first + ri
            cnt_r = np.minimum(r + w - half, rows) - np.maximum(r - half, 0)
            per_variant.append(np.float32(1.0) / (cnt_r * cnt_c).astype(np.float32))
        invs.append(np.stack(per_variant))
    inv = np.broadcast_to(np.stack(invs)[..., None], (len(POOL_WINDOWS), 3, tb, LANES))
    return bands, jnp.asarray(inv, F32)


def _pool_call(u_pool, pool_w, pool_scale, wa, wb):
    b, ng, l, c = u_pool.shape
    nstep = b * ng
    assert wa.shape[0] % (nstep * 16) == 0 and wb.shape[0] % (nstep * 16) == 0
    rows_of = lambda w: pl.BlockSpec((w.shape[0] // nstep, w.shape[1]), lambda i, j: (i * ng + j, 0))
    assert ng == len(POOL_WINDOWS)
    p = ng * c
    rows = l // GRID_W
    tb = POOL_BLOCK_ROWS * GRID_W
    pad = (max(POOL_WINDOWS) // 2) * GRID_W
    assert l % (tb * POOL_BLOCKS_PER_STEP) == 0 and c % LANES == 0
    assert rows >= 3 * POOL_BLOCK_ROWS and max(POOL_WINDOWS) <= 2 * POOL_BLOCK_ROWS
    bands, inv = _pool_tables(rows, tb)
    return pl.pallas_call(
        functools.partial(_pool_kernel, tb=tb, pad=pad),
        out_shape=[jax.ShapeDtypeStruct((b, ng, l, c), BF16), jax.ShapeDtypeStruct(wa.shape, BF16),
                   jax.ShapeDtypeStruct(wb.shape, BF16)],
        grid=(b, ng),
        in_specs=[pl.BlockSpec((1, 1, l, c), lambda i, j: (i, j, 0, 0))]
        + [_const_spec(bd.shape) for bd in bands]
        + [
            pl.BlockSpec((1, 3, tb, LANES), lambda i, j: (j, 0, 0, 0)),
            pl.BlockSpec((1, c, c), lambda i, j: (j, 0, 0)),
            pl.BlockSpec((1, c), lambda i, j: (0, j)),
            rows_of(wa), rows_of(wb),
        ],
        out_specs=[pl.BlockSpec((1, 1, l, c), lambda i, j: (i, j, 0, 0)), rows_of(wa), rows_of(wb)],
        scratch_shapes=[pltpu.VMEM((pad + l + pad, c), BF16)],
        compiler_params=_cparams(("arbitrary", "arbitrary")),
        name="pool",
    )(u_pool, *bands, inv, pool_w.astype(BF16), pool_scale.reshape(1, p), wa, wb)


def _dot_hi_nt(a, b):
    a_hi = a.astype(BF16)
    a_lo = (a - a_hi.astype(F32)).astype(BF16)
    b_hi = b.astype(BF16)
    b_lo = (b - b_hi.astype(F32)).astype(BF16)
    dot = lambda p, q: lax.dot_general(p, q, NT_DIMS, preferred_element_type=F32)
    return dot(a_hi, b_hi) + dot(a_hi, b_lo) + dot(a_lo, b_hi)


def _s5_kernel(u_ref, ea_ref, eb_ref, ba_ref, bm_ref, cc_ref, ca_ref, cb_ref, oa_ref, ob_ref, ap_ref, a8_ref, y_ref,
               et_ref, win_ref, mt_ref, wout_ref, buf_ref, cbuf_ref, hin_ref, *, gps, hc, **static):
    @pl.when(pl.program_id(0) == 0)
    def _():
        buf_ref[...] = jnp.zeros(buf_ref.shape, F32)
        cbuf_ref[...] = jnp.zeros(cbuf_ref.shape, F32)

    for gg in range(gps):
        rows = slice(gg * hc, (gg + 1) * hc)
        _s5_group(u_ref.at[:, :, rows, :], ea_ref.at[gg], eb_ref.at[gg], ba_ref.at[gg], bm_ref.at[gg], cc_ref.at[gg],
                  ca_ref.at[gg], cb_ref.at[gg], oa_ref.at[gg], ob_ref.at[gg], ap_ref.at[gg], a8_ref.at[gg],
                  y_ref.at[:, :, rows, :], et_ref.at[gg], win_ref.at[gg], mt_ref.at[gg], wout_ref.at[gg], buf_ref.at[gg],
                  cbuf_ref.at[gg], hin_ref.at[gg],
                  hc=hc, **static)


def _s5_group(u_ref, ea_ref, eb_ref, ba_ref, bm_ref, cc_ref, ca_ref, cb_ref, oa_ref, ob_ref, ap_ref, a8_ref, y_ref,
              et_ref, win_ref, mt_ref, wout_ref, buf_ref, cbuf_ref, hin_ref, *, t, hc, nb, ncl, ncc, pad, nsteps, nstate):
    n = nstate
    th = t * hc
    nl = nb * ncl
    rb = ncc + ncl
    ntile = th // LANES

    ba = ba_ref[...]
    bm = bm_ref[...]
    ca = ca_ref[...]
    cb = cb_ref[...]
    for ti in range(t):
        rows = slice(ti * hc, (ti + 1) * hc)
        e_rows = ba * ea_ref[ti:ti + 1, :] + bm * eb_ref[ti:ti + 1, :]
        et_ref[rows, :] = e_rows
        win_ref[rows, :] = e_rows.astype(BF16)
        wout_ref[rows, :] = (ca * oa_ref[ti:ti + 1, :] + cb * ob_ref[ti:ti + 1, :]).astype(BF16)
    cc = cc_ref[...]
    q_f = _dot_hi_nt(cc[0:hc], et_ref[:, 0:2 * n])
    q_b = _dot_hi_nt(cc[hc:2 * hc], et_ref[:, 2 * n:4 * n])
    zeros = jnp.zeros((hc, th), F32)
    line_b = jnp.concatenate([zeros, q_b], axis=1)
    line_f = jnp.concatenate([q_f, zeros], axis=1)
    per_tile = LANES // hc
    rolled_b = [line_b if r == 0 else pltpu.roll(line_b, hc * r, axis=1) for r in range(per_tile)]
    rolled_f = [line_f if r == per_tile - 1 else pltpu.roll(line_f, 2 * th - hc * (per_tile - 1 - r), axis=1)
                for r in range(per_tile)]
    for ti in range(t):
        a, r = divmod(ti, per_tile)
        row_b = rolled_b[r][:, th - LANES * a:2 * th - LANES * a]
        off_f = LANES * (ntile - 1 - a)
        row_f = rolled_f[r][:, off_f:off_f + th]
        mt_ref[ti * hc:(ti + 1) * hc, :] = (row_b + row_f).astype(BF16)

    u = jnp.concatenate([jnp.concatenate([u_ref[cg, ti] for cg in range(u_ref.shape[0])], axis=1)
                         for ti in range(t)], axis=0)
    st_all = lax.dot_general(win_ref[...], u, TN_DIMS, preferred_element_type=F32)
    first_half = lax.broadcasted_iota(jnp.int32, (LANES, LANES), 1) < n
    for q in range(4):
        rows_q = st_all[q * n:(q + 1) * n]
        lat = jnp.concatenate([rows_q[:, b * ncl:(b + 1) * ncl] for b in range(nb)], axis=0).T
        cblk = jnp.concatenate([rows_q[:, nl:nl + LANES]] * nb, axis=0).T
        ctx = jnp.where(first_half, cblk, jnp.concatenate([cblk[ncc:], cblk[:ncc]], axis=0))[0:ncc]
        if q < 2:
            buf_ref[q, pad:pad + ncc, :] = ctx
            buf_ref[q, pad + ncc:pad + rb, :] = lat
        else:
            buf_ref[q, pad:pad + ncl, :] = lat
            buf_ref[q, pad + ncl:pad + rb, :] = ctx
    def step(ref, d, k, sgn, lo, rows, keep=None):
        ar = ap_ref[4 * k + 2 * d:4 * k + 2 * d + 1, :]
        ai = ap_ref[4 * k + 2 * d + 1:4 * k + 2 * d + 2, :]
        pr = ref[2 * d, lo + sgn:lo + sgn + rows, :]
        pi = ref[2 * d + 1, lo + sgn:lo + sgn + rows, :]
        if keep is not None:
            pr = jnp.where(keep, pr, 0.0)
            pi = jnp.where(keep, pi, 0.0)
        xr = ref[2 * d, lo:lo + rows, :]
        xi = ref[2 * d + 1, lo:lo + rows, :]
        ref[2 * d, lo:lo + rows, :] = xr + (ar * pr - ai * pi)
        ref[2 * d + 1, lo:lo + rows, :] = xi + (ar * pi + ai * pr)

    rbp = buf_ref.shape[1] - 2 * pad
    nv = rbp // SUBLANES
    cpad = (cbuf_ref.shape[1] - nv) // 2
    in_block = lax.broadcasted_iota(jnp.int32, (rbp, LANES), 0) % SUBLANES
    nlocal = SUBLANES.bit_length() - 1
    nchunk = 3 if nv % 3 == 0 else 1
    crow = rbp // nchunk
    blk = in_block[0:crow]
    for d in range(2):
        for ch in range(nchunk):
            rows = slice(pad + ch * crow, pad + (ch + 1) * crow)
            xr = buf_ref[2 * d, rows, :]
            xi = buf_ref[2 * d + 1, rows, :]
            for k in range(nlocal):
                st = 1 << k
                ar = ap_ref[4 * k + 2 * d:4 * k + 2 * d + 1, :]
                ai = ap_ref[4 * k + 2 * d + 1:4 * k + 2 * d + 2, :]
                keep = blk >= st if d == 0 else blk < SUBLANES - st
                shift = st if d == 0 else SUBLANES - st
                rot = lambda v: pltpu.roll(v.reshape(crow // SUBLANES, SUBLANES, LANES), shift, axis=1).reshape(crow, LANES)
                pr = jnp.where(keep, rot(xr), 0.0)
                pi = jnp.where(keep, rot(xi), 0.0)
                xr, xi = xr + (ar * pr - ai * pi), xi + (ar * pi + ai * pr)
            buf_ref[2 * d, rows, :] = xr
            buf_ref[2 * d + 1, rows, :] = xi
    for d in range(2):
        edge = SUBLANES - 1 if d == 0 else 0
        for q in (2 * d, 2 * d + 1):
            cbuf_ref[q, cpad:cpad + nv, :] = buf_ref[q, pl.ds(pad + edge, nv, stride=SUBLANES), :]
    for m in range(max(0, (nv - 1).bit_length())):
        st = 1 << m
        step(cbuf_ref, 0, nlocal + m, -st, cpad, nv)
        step(cbuf_ref, 1, nlocal + m, st, cpad, nv)
    for d in range(2):
        a8r = a8_ref[2 * SUBLANES * d:2 * SUBLANES * d + SUBLANES, :]
        a8i = a8_ref[2 * SUBLANES * d + SUBLANES:2 * SUBLANES * (d + 1), :]
        for j in range(nv):
            src = j - 1 if d == 0 else j + 1
            if 0 <= src < nv:
                cr = jnp.broadcast_to(cbuf_ref[2 * d, cpad + src:cpad + src + 1, :], (SUBLANES, LANES))
                ci = jnp.broadcast_to(cbuf_ref[2 * d + 1, cpad + src:cpad + src + 1, :], (SUBLANES, LANES))
                rows = slice(pad + j * SUBLANES, pad + (j + 1) * SUBLANES)
                buf_ref[2 * d, rows, :] = buf_ref[2 * d, rows, :] + (a8r * cr - a8i * ci)
                buf_ref[2 * d + 1, rows, :] = buf_ref[2 * d + 1, rows, :] + (a8r * ci + a8i * cr)
    for q in range(4):
        lo = pad + ncc - 1 if q < 2 else pad + 1
        ent = buf_ref[q, lo:lo + ncl, :].T.astype(BF16)
        for b in range(nb):
            hin_ref[q * n:(q + 1) * n, b * ncl:(b + 1) * ncl] = ent[b * n:(b + 1) * n]

    y = jnp.dot(mt_ref[...], u[:, 0:nl], preferred_element_type=F32)
    y = y + jnp.dot(wout_ref[...], hin_ref[...], preferred_element_type=F32)
    for ti in range(t):
        for cg in range(y_ref.shape[0]):
            y_ref[cg, ti] = y[ti * hc:(ti + 1) * hc, cg * LANES:(cg + 1) * LANES].astype(y_ref.dtype)


def _s5_params(lam_re, lam_im, log_dt, b_re, b_im, c_re, c_im, t, nsteps):
    _, g, n = lam_re.shape
    h = b_re.shape[-1]
    lam_re, lam_im = lam_re.astype(F32), lam_im.astype(F32)
    dt = jnp.exp(log_dt.astype(F32))[..., None]
    lr, li = lam_re * dt, lam_im * dt

    def power(d, k):
        kk = k[None, :, None]
        mag = jnp.exp(lr[d][:, None, :] * kk)
        return mag * jnp.cos(li[d][:, None, :] * kk), mag * jnp.sin(li[d][:, None, :] * kk)

    a_re, a_im = jnp.exp(lr) * jnp.cos(li), jnp.exp(lr) * jnp.sin(li)
    den = lam_re * lam_re + lam_im * lam_im
    f_re = ((a_re - 1.0) * lam_re + a_im * lam_im) / den
    f_im = (a_im * lam_re - (a_re - 1.0) * lam_im) / den
    bt_re = (f_re[..., None] * b_re - f_im[..., None] * b_im).transpose(0, 1, 3, 2)
    bt_im = (f_re[..., None] * b_im + f_im[..., None] * b_re).transpose(0, 1, 3, 2)
    lanes4 = lambda f0, f1, b0, b1: jnp.concatenate([f0, f1, b0, b1], axis=-1)

    strides = t * (2.0 ** jnp.arange(nsteps, dtype=F32))
    exps = jnp.concatenate([jnp.arange(t + 1, dtype=F32), strides, t * jnp.arange(1, SUBLANES + 1, dtype=F32)])
    (allf_re, allf_im), (allb_re, allb_im) = power(0, exps), power(1, exps)
    pf_re, pf_im = allf_re[:, t - 1::-1], allf_im[:, t - 1::-1]
    pb_re, pb_im = allb_re[:, 0:t], allb_im[:, 0:t]
    ea = lanes4(pf_re, pf_re, pb_re, pb_re)
    eb = lanes4(pf_im, pf_im, pb_im, pb_im)
    ba = lanes4(bt_re[0], bt_im[0], bt_re[1], bt_im[1])
    bm = lanes4(-bt_im[0], bt_re[0], -bt_im[1], bt_re[1])
    cc = jnp.concatenate([jnp.concatenate([c_re[0], -c_im[0]], axis=-1),
                          jnp.concatenate([c_re[1], -c_im[1]], axis=-1)], axis=1)
    rf_re, rf_im = allf_re[:, 1:t + 1], allf_im[:, 1:t + 1]
    rb_re, rb_im = allb_re[:, t:0:-1], allb_im[:, t:0:-1]
    oa = lanes4(rf_re, rf_im, rb_re, rb_im)
    ob = lanes4(rf_im, rf_re, rb_im, rb_re)
    ca = lanes4(c_re[0], -c_re[0], c_re[1], -c_re[1])
    cb = lanes4(-c_im[0], -c_im[0], -c_im[1], -c_im[1])
    st0, st1 = t + 1, t + 1 + nsteps
    apf_re, apf_im, apb_re, apb_im = (v[:, st0:st1] for v in (allf_re, allf_im, allb_re, allb_im))
    ap = jnp.stack([apf_re, apf_im, apb_re, apb_im], axis=2).reshape(g, 4 * nsteps, n)
    ap = jnp.concatenate([ap] * (LANES // n), axis=-1)
    rows = -(-4 * nsteps // SUBLANES) * SUBLANES
    ap = jnp.pad(ap, ((0, 0), (0, rows - 4 * nsteps), (0, 0)))
    f8_re, f8_im = allf_re[:, st1:], allf_im[:, st1:]
    b8_re, b8_im = allb_re[:, :st1 - 1:-1], allb_im[:, :st1 - 1:-1]
    a8 = jnp.concatenate([f8_re, f8_im, b8_re, b8_im], axis=1)
    a8 = jnp.concatenate([a8] * (LANES // n), axis=-1)
    return tuple(v.astype(F32) for v in (ea, eb, ba, bm, cc, ca, cb, oa, ob, ap, a8))


def _s5_call(ut, params, t, nb, ncl, ncc, nsteps, n_state, hc):
    ap = params[-2]
    ngrp_in, _, sw, _ = ut.shape
    nl = nb * ncl
    g = sw // hc
    th = t * hc
    rbp = -(-(ncc + ncl) // SUBLANES) * SUBLANES
    nv = rbp // SUBLANES
    pad = SUBLANES
    cpad = max(SUBLANES, 1 << max(0, (nv - 1).bit_length() - 1))
    assert nb * n_state == LANES and nb == 2 and 4 * nsteps <= ap.shape[1]
    gps = S5_GROUPS_PER_STEP
    assert g % gps == 0
    grp = lambda shp: pl.BlockSpec((gps,) + shp, lambda i: (i, 0, 0))
    return pl.pallas_call(
        functools.partial(_s5_kernel, gps=gps, t=t, hc=hc, nb=nb, ncl=ncl, ncc=ncc, pad=pad, nsteps=nsteps,
                          nstate=n_state),
        out_shape=jax.ShapeDtypeStruct((nl // LANES, t, sw, LANES), BF16),
        grid=(g // gps,),
        in_specs=[
            pl.BlockSpec((ngrp_in, t, gps * hc, LANES), lambda i: (0, 0, i, 0)),
            grp((t, 4 * n_state)), grp((t, 4 * n_state)), grp((hc, 4 * n_state)), grp((hc, 4 * n_state)),
            grp((2 * hc, 2 * n_state)), grp((hc, 4 * n_state)), grp((hc, 4 * n_state)),
            grp((t, 4 * n_state)), grp((t, 4 * n_state)), grp((ap.shape[1], LANES)), grp((4 * SUBLANES, LANES)),
        ],
        out_specs=pl.BlockSpec((nl // LANES, t, gps * hc, LANES), lambda i: (0, 0, i, 0)),
        scratch_shapes=[pltpu.VMEM((gps, th, 4 * n_state), F32), pltpu.VMEM((gps, th, 4 * n_state), BF16),
                        pltpu.VMEM((gps, th, th), BF16), pltpu.VMEM((gps, th, 4 * n_state), BF16),
                        pltpu.VMEM((gps, 4, pad + rbp + pad, LANES), F32),
                        pltpu.VMEM((gps, 4, cpad + nv + cpad, LANES), F32),
                        pltpu.VMEM((gps, 4 * n_state, nl), BF16)],
        compiler_params=_cparams(("arbitrary",)),
        name="s5",
    )(ut, *params)


def _merge_kernel(y_ref, u_ref, p_ref, zs_ref, x_ref, gate_ref, d_ref, gw_ref, gb_ref, ow_ref, fg_ref, o_ref):
    sw = y_ref.shape[-1]
    pw = p_ref.shape[1] * p_ref.shape[-1]
    y = _gelu_tanh(y_ref[0].astype(F32) + d_ref[...] * u_ref[0].astype(F32)).astype(BF16)
    yy = jnp.dot(y, gw_ref[...], preferred_element_type=F32) + gb_ref[...]
    ssm_out = yy[:, :sw] * _sigmoid(yy[:, sw:])
    pool = jnp.concatenate([p_ref[0, g] for g in range(p_ref.shape[1])], axis=1)
    br_pool = pool * zs_ref[0, :, 0:pw]
    br_ssm = (ssm_out * zs_ref[0, :, pw:pw + sw].astype(F32)).astype(BF16)
    mix = jnp.dot(jnp.concatenate([br_pool, br_ssm], axis=1), ow_ref[...], preferred_element_type=F32)
    xo = x_ref[0] + gate_ref[0] * mix
    ms = jnp.mean(xo * xo, axis=-1, keepdims=True)
    o_ref[0] = (xo * lax.rsqrt(ms + EPS) * fg_ref[...]).astype(o_ref.dtype)


def _merge_call(y_ssm, u_ssm, pool_out, zs, x, gate, d_skip, glu_w, glu_b, out_w, final_g, tm):
    b, l, d = x.shape
    sw = y_ssm.shape[-1]
    ng, pc = pool_out.shape[1], pool_out.shape[-1]
    mixw = zs.shape[-1]
    tok = lambda wd: pl.BlockSpec((1, tm, wd), lambda i, j: (i, j, 0))
    return pl.pallas_call(
        _merge_kernel,
        out_shape=jax.ShapeDtypeStruct((b, l, d), x.dtype),
        grid=(b, l // tm),
        in_specs=[
            tok(sw), tok(sw), pl.BlockSpec((1, ng, tm, pc), lambda i, j: (i, 0, j, 0)), tok(mixw), tok(d),
            pl.BlockSpec((1, 1, d), lambda i, j: (i, 0, 0)),
            _const_spec((1, sw)), _const_spec((sw, 2 * sw)), _const_spec((1, 2 * sw)), _const_spec((mixw, d)),
            _const_spec((1, d)),
        ],
        out_specs=tok(d),
        compiler_params=_cparams(("arbitrary", "arbitrary")),
        name="merge",
    )(y_ssm, u_ssm, pool_out, zs, x, gate, d_skip.reshape(1, sw), glu_w, glu_b.reshape(1, 2 * sw), out_w,
      final_g.reshape(1, d))


def kernel(x, c, ctx, c_ctx, ada_w, ada_b, norm_g, in_w, pool_w, pool_scale, s5_lam_re, s5_lam_im, s5_log_dt,
           s5_b_re, s5_b_im, s5_c_re, s5_c_im, s5_d, glu_w, glu_b, out_w, final_g):
    assert ada_w.shape[0] == 1, "single-layer block"
    bsz, seq, d = x.shape
    cl = ctx.shape[1]
    mixw = in_w.shape[2] // 2
    poolw = pool_scale.shape[-1]
    ssmw = s5_d.shape[-1]
    n_grp, n_state = s5_lam_re.shape[2], s5_lam_re.shape[3]
    hc = ssmw // n_grp
    t = CHUNK_T
    assert poolw + ssmw == mixw and seq % t == 0 and cl % t == 0 and 2 * n_state == LANES and LANES % hc == 0

    mod = _ada_call(jnp.concatenate([c, c_ctx[None]], axis=0), ada_w[0], ada_b[0])
    shift, scale, gate = mod[:, :d], mod[:, d:2 * d], mod[:, 2 * d:]
    g1 = norm_g[0].reshape(1, d)
    in_w16 = in_w[0].astype(BF16)

    tm = min(512, seq)
    u_pool, u_ssm, zs = _inproj_call(x, scale[:bsz, None], shift[:bsz, None], g1, in_w16,
                                     (poolw, ssmw, mixw), (False, False, True), min(1024, seq), "inproj",
                                     groups=(len(POOL_WINDOWS), 1, 1))
    sc_c = jnp.broadcast_to(scale[bsz][None, None], (bsz, 1, d))
    sh_c = jnp.broadcast_to(shift[bsz][None, None], (bsz, 1, d))
    assert poolw % ssmw == 0
    (uc_ssm,) = _inproj_call(ctx, sc_c, sh_c, g1, in_w16, (ssmw,), (False,), min(256, cl), "inproj_ctx",
                             wcol=poolw // ssmw)

    pool_out, out_w16, glu_w16 = _pool_call(u_pool, pool_w[0], pool_scale[0], out_w[0], glu_w[0])

    ncl, ncc = seq // t, cl // t
    nsteps = max(1, (ncc + ncl - 1).bit_length())
    params = _s5_params(s5_lam_re[0], s5_lam_im[0], s5_log_dt[0], s5_b_re[0], s5_b_im[0], s5_c_re[0], s5_c_im[0],
                        t, nsteps)
    ut = _pack_call(u_ssm, uc_ssm, t)
    yt = _s5_call(ut, params, t, bsz, ncl, ncc, nsteps, n_state, hc)
    y_ssm = _unpack_call(yt, bsz, seq)

    return _merge_call(y_ssm, u_ssm, pool_out, zs, x, gate[:bsz, None], s5_d[0], glu_w16, glu_b[0], out_w16,
                       final_g, tm)
```

```python
import functools
import math

import jax
import jax.numpy as jnp
import numpy as np
from jax import lax
from jax.experimental import pallas as pl
from jax.experimental.pallas import tpu as pltpu

GRID_W = 64
POOL_WINDOWS = (2, 4, 8, 16)
EPS = 1e-6
CHUNK_T = 32
S5_GROUPS_PER_STEP = 4
LANES = 128
SUBLANES = 8
VMEM_LIMIT = 56 * 1024 * 1024

F32 = jnp.float32
BF16 = jnp.bfloat16
TN_DIMS = (((0,), (0,)), ((), ()))
NT_DIMS = (((1,), (1,)), ((), ()))


def _cparams(sem):
    return pltpu.CompilerParams(dimension_semantics=sem, vmem_limit_bytes=VMEM_LIMIT)


def _const_spec(shape):
    return pl.BlockSpec(shape, lambda *_: (0,) * len(shape), pipeline_mode=pl.Buffered(1))


def _sigmoid(v):
    return 1.0 / (1.0 + jnp.exp(-v))


def _gelu_tanh(v):
    return 0.5 * v * (1.0 + jnp.tanh(math.sqrt(2.0 / math.pi) * (v + 0.044715 * (v * v * v))))


def _modulated_norm(x, g, scale, shift):
    ms = jnp.mean(x * x, axis=-1, keepdims=True)
    return (x * lax.rsqrt(ms + EPS) * g) * (1.0 + scale) + shift


ADA_LANE_TILES = 6


def _ada_kernel(ct_ref, w_ref, b_ref, cw_ref, o_ref, cw16_ref, sb_ref, acc_ref, *, nrows):
    tk, n = w_ref.shape
    step = pl.program_id(0)
    cw16_ref[...] = cw_ref[...].astype(cw16_ref.dtype)

    @pl.when(step == 0)
    def _():
        acc_ref[...] = jnp.zeros(acc_ref.shape, F32)

    cv = ct_ref[...]
    s = cv * _sigmoid(cv)
    for r in range(nrows):
        sb_ref[r] = jnp.broadcast_to(s[:, r:r + 1], (tk, LANES))

    group = ADA_LANE_TILES * LANES
    for g0 in range(0, n, group):
        def body(i, accs, g0=g0):
            k0 = pl.multiple_of(i * SUBLANES, SUBLANES)
            sb = [sb_ref[r, pl.ds(k0, SUBLANES), :] for r in range(nrows)]
            out = []
            for lt in range(ADA_LANE_TILES):
                w = w_ref[pl.ds(k0, SUBLANES), g0 + lt * LANES:g0 + (lt + 1) * LANES]
                out.append(tuple(accs[lt][r] + sb[r] * w for r in range(nrows)))
            return tuple(out)

        init = tuple(tuple(jnp.zeros((SUBLANES, LANES), F32) for _ in range(nrows)) for _ in range(ADA_LANE_TILES))
        accs = lax.fori_loop(0, tk // SUBLANES, body, init, unroll=8)
        for lt in range(ADA_LANE_TILES):
            lanes = slice(g0 + lt * LANES, g0 + (lt + 1) * LANES)
            for r in range(nrows):
                acc_ref[r, :, lanes] = acc_ref[r, :, lanes] + accs[lt][r]

    @pl.when(step == pl.num_programs(0) - 1)
    def _():
        o_ref[...] = jnp.zeros(o_ref.shape, F32)
        for r in range(nrows):
            o_ref[r:r + 1, :] = jnp.sum(acc_ref[r], axis=0, keepdims=True) + b_ref[...]


def _ada_call(cvecs, ada_w, ada_b, cw):
    nrows, d = cvecs.shape
    n = ada_w.shape[1]
    tk = 256 if d % 256 == 0 else d
    assert n % (ADA_LANE_TILES * LANES) == 0
    ct = jnp.zeros((d, SUBLANES), F32).at[:, :nrows].set(cvecs.T)
    nstep = d // tk
    assert cw.shape[0] % (nstep * 16) == 0
    cw_spec = pl.BlockSpec((cw.shape[0] // nstep, cw.shape[1]), lambda j: (j, 0))
    out, cw16 = pl.pallas_call(
        functools.partial(_ada_kernel, nrows=nrows),
        out_shape=[jax.ShapeDtypeStruct((SUBLANES, n), F32), jax.ShapeDtypeStruct(cw.shape, BF16)],
        grid=(nstep,),
        in_specs=[
            pl.BlockSpec((tk, SUBLANES), lambda j: (j, 0)),
            pl.BlockSpec((tk, n), lambda j: (j, 0)),
            _const_spec((1, n)),
            cw_spec,
        ],
        out_specs=[pl.BlockSpec((SUBLANES, n), lambda j: (0, 0)), cw_spec],
        scratch_shapes=[pltpu.VMEM((nrows, tk, LANES), F32), pltpu.VMEM((nrows, SUBLANES, n), F32)],
        compiler_params=_cparams(("arbitrary",)),
        name="ada",
    )(ct, ada_w, ada_b.reshape(1, n), cw)
    return out[:nrows], cw16


def _inproj_kernel(x_ref, sc_ref, sh_ref, g_ref, w_ref, *o_refs, nchunk, silu):
    h = _modulated_norm(x_ref[0], g_ref[...], sc_ref[0], sh_ref[0]).astype(BF16)
    col = 0
    for o_ref, gate in zip(o_refs, silu):
        grouped = len(o_ref.shape) == 4
        cw = o_ref.shape[-1]
        width = cw * o_ref.shape[1] if grouped else cw
        for n0 in range(0, width, nchunk):
            acc = jnp.dot(h, w_ref[:, col + n0:col + n0 + nchunk], preferred_element_type=F32)
            if gate:
                acc = acc * _sigmoid(acc)
            if grouped:
                for c0 in range(0, nchunk, cw):
                    o_ref[0, (n0 + c0) // cw] = acc[:, c0:c0 + cw].astype(o_ref.dtype)
            else:
                o_ref[0, :, n0:n0 + nchunk] = acc.astype(o_ref.dtype)
        col += width


def _inproj_call(x, scale, shift, g, w, widths, silu, tm, name, wcol=0, groups=None):
    b, l, d = x.shape
    nout = sum(widths)
    assert w.shape[1] % nout == 0 and l % tm == 0
    nchunk = min(1024, min(widths))
    assert all(wd % nchunk == 0 for wd in widths)
    groups = groups or (1,) * len(widths)
    shape = lambda wd, ng, rows: (b, rows, wd) if ng == 1 else (b, ng, rows, wd // ng)
    spec = lambda wd, ng: (pl.BlockSpec((1, tm, wd), lambda i, j: (i, j, 0)) if ng == 1 else
                           pl.BlockSpec((1, ng, tm, wd // ng), lambda i, j: (i, 0, j, 0)))
    return pl.pallas_call(
        functools.partial(_inproj_kernel, nchunk=nchunk, silu=silu),
        out_shape=[jax.ShapeDtypeStruct(shape(wd, ng, l), BF16) for wd, ng in zip(widths, groups)],
        grid=(b, l // tm),
        in_specs=[
            pl.BlockSpec((1, tm, d), lambda i, j: (i, j, 0)),
            pl.BlockSpec((1, 1, d), lambda i, j: (i, 0, 0)),
            pl.BlockSpec((1, 1, d), lambda i, j: (i, 0, 0)),
            _const_spec((1, d)),
            pl.BlockSpec((d, nout), lambda i, j: (0, wcol), pipeline_mode=pl.Buffered(1)),
        ],
        out_specs=[spec(wd, ng) for wd, ng in zip(widths, groups)],
        compiler_params=_cparams(("arbitrary", "arbitrary")),
        name=name,
    )(x, scale, shift, g, w)


def _chunk_perm(t):
    i = np.arange(SUBLANES * t)
    src = (i % SUBLANES) * t + i // SUBLANES
    return jnp.asarray(src[:, None] == i[None, :], BF16)


def _pack_kernel(u_ref, uc_ref, perm_ref, ut_ref, scr_ref, *, t, nblk):
    j = pl.program_id(0)
    s = scr_ref.shape[2]
    rows = SUBLANES * t
    ctok = uc_ref.shape[0]

    def permute(src, blk):
        pm = jnp.dot(perm_ref[...], src, preferred_element_type=F32)
        for ti in range(t):
            scr_ref[ti, pl.ds(pl.multiple_of(blk * SUBLANES, SUBLANES), SUBLANES), :] = pm[ti * SUBLANES:(ti + 1) * SUBLANES]

    @pl.when(j < nblk)
    def _():
        def body(blk, carry):
            permute(u_ref[pl.ds(pl.multiple_of(blk * rows, rows), rows), :], blk)
            return carry

        lax.fori_loop(0, LANES // SUBLANES, body, 0, unroll=2)

    @pl.when(j == nblk)
    def _():
        for blk in range(ctok // rows):
            permute(uc_ref[blk * rows:(blk + 1) * rows, :], blk)
        scr_ref[:, ctok // t:LANES, :] = jnp.zeros((t, LANES - ctok // t, s), F32)

    def transpose(ti, carry):
        for k in range(s // LANES):
            ut_ref[0, ti, LANES * k:LANES * (k + 1), :] = scr_ref[ti, :, LANES * k:LANES * (k + 1)].T.astype(BF16)
        return carry

    lax.fori_loop(0, t, transpose, 0, unroll=2)


def _pack_call(u_ssm, uc_ssm, t):
    b, l, s = u_ssm.shape
    tok = LANES * t
    rows = SUBLANES * t
    ntok = b * l
    ctok = b * uc_ssm.shape[1]
    assert ntok % tok == 0 and ctok <= tok and ctok % rows == 0 and s % LANES == 0
    nblk = ntok // tok
    return pl.pallas_call(
        functools.partial(_pack_kernel, t=t, nblk=nblk),
        out_shape=jax.ShapeDtypeStruct((nblk + 1, t, s, LANES), BF16),
        grid=(nblk + 1,),
        in_specs=[pl.BlockSpec((tok, s), lambda j: (jnp.minimum(j, nblk - 1), 0)),
                  _const_spec((ctok, s)), _const_spec((rows, rows))],
        out_specs=pl.BlockSpec((1, t, s, LANES), lambda j: (j, 0, 0, 0)),
        scratch_shapes=[pltpu.VMEM((t, LANES, s), F32)],
        compiler_params=_cparams(("arbitrary",)),
        name="pack",
    )(u_ssm.reshape(ntok, s), uc_ssm.reshape(ctok, s), _chunk_perm(t))


def _unpack_kernel(yt_ref, perm_ref, y_ref, scr_ref, *, t):
    s = scr_ref.shape[2]
    rows = SUBLANES * t

    def transpose(ti, carry):
        for k in range(s // LANES):
            scr_ref[ti, :, LANES * k:LANES * (k + 1)] = yt_ref[0, ti, LANES * k:LANES * (k + 1), :].astype(F32).T
        return carry

    lax.fori_loop(0, t, transpose, 0, unroll=2)

    def body(blk, carry):
        c0 = pl.multiple_of(blk * SUBLANES, SUBLANES)
        src = jnp.concatenate([scr_ref[ti, pl.ds(c0, SUBLANES), :] for ti in range(t)], axis=0).astype(BF16)
        out = jnp.dot(perm_ref[...], src, preferred_element_type=F32)
        y_ref[pl.ds(pl.multiple_of(blk * rows, rows), rows), :] = out.astype(y_ref.dtype)
        return carry

    lax.fori_loop(0, LANES // SUBLANES, body, 0, unroll=2)


def _unpack_call(yt, b, l):
    ngrp, t, s, _ = yt.shape
    nl = ngrp * LANES
    tok = LANES * t
    rows = SUBLANES * t
    y = pl.pallas_call(
        functools.partial(_unpack_kernel, t=t),
        out_shape=jax.ShapeDtypeStruct((nl * t, s), BF16),
        grid=(ngrp,),
        in_specs=[pl.BlockSpec((1, t, s, LANES), lambda j: (j, 0, 0, 0)), _const_spec((rows, rows))],
        out_specs=pl.BlockSpec((tok, s), lambda j: (j, 0)),
        scratch_shapes=[pltpu.VMEM((t, LANES, s), F32)],
        compiler_params=_cparams(("arbitrary",)),
        name="unpack",
    )(yt, _chunk_perm(t).T)
    return y.reshape(b, l, s)


POOL_BLOCK_ROWS = 8
POOL_BLOCKS_PER_STEP = 4


def _pool_kernel(u_ref, *refs, tb, pad):
    nwin = len(POOL_WINDOWS)
    band_refs = refs[:nwin]
    inv_ref, pw_ref, ps_ref, wa_ref, wb_ref, o_ref, wa16_ref, wb16_ref, pad_ref = refs[nwin:]
    wa16_ref[...] = wa_ref[...].astype(wa16_ref.dtype)
    wb16_ref[...] = wb_ref[...].astype(wb16_ref.dtype)
    grp = pl.program_id(1)
    l = u_ref.shape[2]
    c = u_ref.shape[3]
    nblk = l // tb
    pad_ref[0:pad, :] = jnp.zeros((pad, c), pad_ref.dtype)
    pad_ref[pad + l:pad + l + pad, :] = jnp.zeros((pad, c), pad_ref.dtype)

    def copy(i, carry):
        r0 = pl.multiple_of(i * tb, tb)
        pad_ref[pl.ds(pl.multiple_of(pad + r0, GRID_W), tb), :] = u_ref[0, 0, pl.ds(r0, tb), :]
        return carry

    lax.fori_loop(0, nblk, copy, 0)

    for gi, w in enumerate(POOL_WINDOWS):
        half = w // 2
        span = band_refs[gi].shape[1]

        @pl.when(grp == gi)
        def _(half=half, span=span, band_ref=band_refs[gi]):
            def blk(i, carry):
                first = POOL_BLOCKS_PER_STEP * i
                r0 = pl.multiple_of(first * tb, POOL_BLOCKS_PER_STEP * tb)
                wins = [pad_ref[pl.ds(pl.multiple_of(pad + r0 + k * tb - half * GRID_W, GRID_W), span), :]
                        for k in range(POOL_BLOCKS_PER_STEP)]
                box = jnp.dot(band_ref[...], jnp.concatenate(wins, axis=1), preferred_element_type=F32)
                dlts = []
                for k in range(POOL_BLOCKS_PER_STEP):
                    j = first + k
                    variant = jnp.where(j == 0, 0, jnp.where(j == nblk - 1, 2, 1))
                    inv = jnp.concatenate([inv_ref[0, variant]] * (c // LANES), axis=1)
                    u_blk = u_ref[0, 0, pl.ds(r0 + k * tb, tb), :].astype(F32)
                    dlts.append((box[:, k * c:(k + 1) * c] * inv - u_blk).astype(BF16))
                o = jnp.dot(jnp.concatenate(dlts, axis=0), pw_ref[0], preferred_element_type=F32) * ps_ref[...]
                o_ref[0, 0, pl.ds(r0, POOL_BLOCKS_PER_STEP * tb), :] = o.astype(o_ref.dtype)
                return carry

            lax.fori_loop(0, nblk // POOL_BLOCKS_PER_STEP, blk, 0)


def _pool_tables(rows, tb):
    br = tb // GRID_W
    i = np.arange(tb)
    ri, ci = i // GRID_W, i % GRID_W
    bands, invs = [], []
    for w in POOL_WINDOWS:
        half = w // 2
        j = np.arange((br + w) * GRID_W)
        dr = (j // GRID_W)[None, :] - ri[:, None]
        dc = (j % GRID_W)[None, :] - ci[:, None]
        bands.append(jnp.asarray((dr >= 0) & (dr < w) & (dc >= -half) & (dc < w - half), BF16))
        cnt_c = np.minimum(ci + w - half, GRID_W) - np.maximum(ci - half, 0)
        per_variant = []
        for r_first in (0, br, rows - br):
            r = r_first + ri
            cnt_r = np.minimum(r + w - half, rows) - np.maximum(r - half, 0)
            per_variant.append(np.float32(1.0) / (cnt_r * cnt_c).astype(np.float32))
        invs.append(np.stack(per_variant))
    inv = np.broadcast_to(np.stack(invs)[..., None], (len(POOL_WINDOWS), 3, tb, LANES))
    return bands, jnp.asarray(inv, F32)


def _pool_call(u_pool, pool_w, pool_scale, wa, wb):
    b, ng, l, c = u_pool.shape
    nstep = b * ng
    assert wa.shape[0] % (nstep * 16) == 0 and wb.shape[0] % (nstep * 16) == 0
    rows_of = lambda w: pl.BlockSpec((w.shape[0] // nstep, w.shape[1]), lambda i, j: (i * ng + j, 0))
    assert ng == len(POOL_WINDOWS)
    p = ng * c
    rows = l // GRID_W
    tb = POOL_BLOCK_ROWS * GRID_W
    pad = (max(POOL_WINDOWS) // 2) * GRID_W
    assert l % (tb * POOL_BLOCKS_PER_STEP) == 0 and c % LANES == 0
    assert rows >= 3 * POOL_BLOCK_ROWS and max(POOL_WINDOWS) <= 2 * POOL_BLOCK_ROWS
    bands, inv = _pool_tables(rows, tb)
    return pl.pallas_call(
        functools.partial(_pool_kernel, tb=tb, pad=pad),
        out_shape=[jax.ShapeDtypeStruct((b, ng, l, c), BF16), jax.ShapeDtypeStruct(wa.shape, BF16),
                   jax.ShapeDtypeStruct(wb.shape, BF16)],
        grid=(b, ng),
        in_specs=[pl.BlockSpec((1, 1, l, c), lambda i, j: (i, j, 0, 0))]
        + [_const_spec(bd.shape) for bd in bands]
        + [
            pl.BlockSpec((1, 3, tb, LANES), lambda i, j: (j, 0, 0, 0)),
            pl.BlockSpec((1, c, c), lambda i, j: (j, 0, 0)),
            pl.BlockSpec((1, c), lambda i, j: (0, j)),
            rows_of(wa), rows_of(wb),
        ],
        out_specs=[pl.BlockSpec((1, 1, l, c), lambda i, j: (i, j, 0, 0)), rows_of(wa), rows_of(wb)],
        scratch_shapes=[pltpu.VMEM((pad + l + pad, c), BF16)],
        compiler_params=_cparams(("arbitrary", "arbitrary")),
        name="pool",
    )(u_pool, *bands, inv, pool_w.astype(BF16), pool_scale.reshape(1, p), wa, wb)


def _dot_hi_nt(a, b):
    a_hi = a.astype(BF16)
    a_lo = (a - a_hi.astype(F32)).astype(BF16)
    b_hi = b.astype(BF16)
    b_lo = (b - b_hi.astype(F32)).astype(BF16)
    dot = lambda p, q: lax.dot_general(p, q, NT_DIMS, preferred_element_type=F32)
    return dot(a_hi, b_hi) + dot(a_hi, b_lo) + dot(a_lo, b_hi)


def _s5_kernel(u_ref, ea_ref, eb_ref, ba_ref, bm_ref, cc_ref, ca_ref, cb_ref, oa_ref, ob_ref, ap_ref, a8_ref, y_ref,
               et_ref, win_ref, mt_ref, wout_ref, buf_ref, cbuf_ref, hin_ref, *, gps, hc, **static):
    @pl.when(pl.program_id(0) == 0)
    def _():
        buf_ref[...] = jnp.zeros(buf_ref.shape, F32)
        cbuf_ref[...] = jnp.zeros(cbuf_ref.shape, F32)

    for gg in range(gps):
        rows = slice(gg * hc, (gg + 1) * hc)
        _s5_group(u_ref.at[:, :, rows, :], ea_ref.at[gg], eb_ref.at[gg], ba_ref.at[gg], bm_ref.at[gg], cc_ref.at[gg],
                  ca_ref.at[gg], cb_ref.at[gg], oa_ref.at[gg], ob_ref.at[gg], ap_ref.at[gg], a8_ref.at[gg],
                  y_ref.at[:, :, rows, :], et_ref.at[gg], win_ref.at[gg], mt_ref.at[gg], wout_ref.at[gg], buf_ref.at[gg],
                  cbuf_ref.at[gg], hin_ref.at[gg],
                  hc=hc, **static)


def _s5_group(u_ref, ea_ref, eb_ref, ba_ref, bm_ref, cc_ref, ca_ref, cb_ref, oa_ref, ob_ref, ap_ref, a8_ref, y_ref,
              et_ref, win_ref, mt_ref, wout_ref, buf_ref, cbuf_ref, hin_ref, *, t, hc, nb, ncl, ncc, pad, nsteps, nstate):
    n = nstate
    th = t * hc
    nl = nb * ncl
    rb = ncc + ncl
    ntile = th // LANES

    ba = ba_ref[...]
    bm = bm_ref[...]
    ca = ca_ref[...]
    cb = cb_ref[...]
    for ti in range(t):
        rows = slice(ti * hc, (ti + 1) * hc)
        e_rows = ba * ea_ref[ti:ti + 1, :] + bm * eb_ref[ti:ti + 1, :]
        et_ref[rows, :] = e_rows
        win_ref[rows, :] = e_rows.astype(BF16)
        wout_ref[rows, :] = (ca * oa_ref[ti:ti + 1, :] + cb * ob_ref[ti:ti + 1, :]).astype(BF16)
    cc = cc_ref[...]
    q_f = _dot_hi_nt(cc[0:hc], et_ref[:, 0:2 * n])
    q_b = _dot_hi_nt(cc[hc:2 * hc], et_ref[:, 2 * n:4 * n])
    zeros = jnp.zeros((hc, th), F32)
    line_b = jnp.concatenate([zeros, q_b], axis=1)
    line_f = jnp.concatenate([q_f, zeros], axis=1)
    per_tile = LANES // hc
    rolled_b = [line_b if r == 0 else pltpu.roll(line_b, hc * r, axis=1) for r in range(per_tile)]
    rolled_f = [line_f if r == per_tile - 1 else pltpu.roll(line_f, 2 * th - hc * (per_tile - 1 - r), axis=1)
                for r in range(per_tile)]
    for ti in range(t):
        a, r = divmod(ti, per_tile)
        row_b = rolled_b[r][:, th - LANES * a:2 * th - LANES * a]
        off_f = LANES * (ntile - 1 - a)
        row_f = rolled_f[r][:, off_f:off_f + th]
        mt_ref[ti * hc:(ti + 1) * hc, :] = (row_b + row_f).astype(BF16)

    u = jnp.concatenate([jnp.concatenate([u_ref[cg, ti] for cg in range(u_ref.shape[0])], axis=1)
                         for ti in range(t)], axis=0)
    st_all = lax.dot_general(win_ref[...], u, TN_DIMS, preferred_element_type=F32)
    first_half = lax.broadcasted_iota(jnp.int32, (LANES, LANES), 1) < n
    for q in range(4):
        rows_q = st_all[q * n:(q + 1) * n]
        lat = jnp.concatenate([rows_q[:, b * ncl:(b + 1) * ncl] for b in range(nb)], axis=0).T
        cblk = jnp.concatenate([rows_q[:, nl:nl + LANES]] * nb, axis=0).T
        ctx = jnp.where(first_half, cblk, jnp.concatenate([cblk[ncc:], cblk[:ncc]], axis=0))[0:ncc]
        if q < 2:
            buf_ref[q, pad:pad + ncc, :] = ctx
            buf_ref[q, pad + ncc:pad + rb, :] = lat
        else:
            buf_ref[q, pad:pad + ncl, :] = lat
            buf_ref[q, pad + ncl:pad + rb, :] = ctx
    def step(ref, d, k, sgn, lo, rows, keep=None):
        ar = ap_ref[4 * k + 2 * d:4 * k + 2 * d + 1, :]
        ai = ap_ref[4 * k + 2 * d + 1:4 * k + 2 * d + 2, :]
        pr = ref[2 * d, lo + sgn:lo + sgn + rows, :]
        pi = ref[2 * d + 1, lo + sgn:lo + sgn + rows, :]
        if keep is not None:
            pr = jnp.where(keep, pr, 0.0)
            pi = jnp.where(keep, pi, 0.0)
        xr = ref[2 * d, lo:lo + rows, :]
        xi = ref[2 * d + 1, lo:lo + rows, :]
        ref[2 * d, lo:lo + rows, :] = xr + (ar * pr - ai * pi)
        ref[2 * d + 1, lo:lo + rows, :] = xi + (ar * pi + ai * pr)

    rbp = buf_ref.shape[1] - 2 * pad
    nv = rbp // SUBLANES
    cpad = (cbuf_ref.shape[1] - nv) // 2
    in_block = lax.broadcasted_iota(jnp.int32, (rbp, LANES), 0) % SUBLANES
    nlocal = SUBLANES.bit_length() - 1
    nchunk = 3 if nv % 3 == 0 else 1
    crow = rbp // nchunk
    blk = in_block[0:crow]
    for d in range(2):
        for ch in range(nchunk):
            rows = slice(pad + ch * crow, pad + (ch + 1) * crow)
            xr = buf_ref[2 * d, rows, :]
            xi = buf_ref[2 * d + 1, rows, :]
            for k in range(nlocal):
                st = 1 << k
                ar = ap_ref[4 * k + 2 * d:4 * k + 2 * d + 1, :]
                ai = ap_ref[4 * k + 2 * d + 1:4 * k + 2 * d + 2, :]
                keep = blk >= st if d == 0 else blk < SUBLANES - st
                shift = st if d == 0 else SUBLANES - st
                rot = lambda v: pltpu.roll(v.reshape(crow // SUBLANES, SUBLANES, LANES), shift, axis=1).reshape(crow, LANES)
                pr = jnp.where(keep, rot(xr), 0.0)
                pi = jnp.where(keep, rot(xi), 0.0)
                xr, xi = xr + (ar * pr - ai * pi), xi + (ar * pi + ai * pr)
            buf_ref[2 * d, rows, :] = xr
            buf_ref[2 * d + 1, rows, :] = xi
    for d in range(2):
        edge = SUBLANES - 1 if d == 0 else 0
        for q in (2 * d, 2 * d + 1):
            cbuf_ref[q, cpad:cpad + nv, :] = buf_ref[q, pl.ds(pad + edge, nv, stride=SUBLANES), :]
    for m in range(max(0, (nv - 1).bit_length())):
        st = 1 << m
        step(cbuf_ref, 0, nlocal + m, -st, cpad, nv)
        step(cbuf_ref, 1, nlocal + m, st, cpad, nv)
    for d in range(2):
        a8r = a8_ref[2 * SUBLANES * d:2 * SUBLANES * d + SUBLANES, :]
        a8i = a8_ref[2 * SUBLANES * d + SUBLANES:2 * SUBLANES * (d + 1), :]
        for j in range(nv):
            src = j - 1 if d == 0 else j + 1
            if 0 <= src < nv:
                cr = jnp.broadcast_to(cbuf_ref[2 * d, cpad + src:cpad + src + 1, :], (SUBLANES, LANES))
                ci = jnp.broadcast_to(cbuf_ref[2 * d + 1, cpad + src:cpad + src + 1, :], (SUBLANES, LANES))
                rows = slice(pad + j * SUBLANES, pad + (j + 1) * SUBLANES)
                buf_ref[2 * d, rows, :] = buf_ref[2 * d, rows, :] + (a8r * cr - a8i * ci)
                buf_ref[2 * d + 1, rows, :] = buf_ref[2 * d + 1, rows, :] + (a8r * ci + a8i * cr)
    for q in range(4):
        lo = pad + ncc - 1 if q < 2 else pad + 1
        ent = buf_ref[q, lo:lo + ncl, :].T.astype(BF16)
        for b in range(nb):
            hin_ref[q * n:(q + 1) * n, b * ncl:(b + 1) * ncl] = ent[b * n:(b + 1) * n]

    y = jnp.dot(mt_ref[...], u[:, 0:nl], preferred_element_type=F32)
    y = y + jnp.dot(wout_ref[...], hin_ref[...], preferred_element_type=F32)
    for ti in range(t):
        for cg in range(y_ref.shape[0]):
            y_ref[cg, ti] = y[ti * hc:(ti + 1) * hc, cg * LANES:(cg + 1) * LANES].astype(y_ref.dtype)


def _s5_params(lam_re, lam_im, log_dt, b_re, b_im, c_re, c_im, t, nsteps):
    _, g, n = lam_re.shape
    h = b_re.shape[-1]
    lam_re, lam_im = lam_re.astype(F32), lam_im.astype(F32)
    dt = jnp.exp(log_dt.astype(F32))[..., None]
    lr, li = lam_re * dt, lam_im * dt

    def power(d, k):
        kk = k[None, :, None]
        mag = jnp.exp(lr[d][:, None, :] * kk)
        return mag * jnp.cos(li[d][:, None, :] * kk), mag * jnp.sin(li[d][:, None, :] * kk)

    a_re, a_im = jnp.exp(lr) * jnp.cos(li), jnp.exp(lr) * jnp.sin(li)
    den = lam_re * lam_re + lam_im * lam_im
    f_re = ((a_re - 1.0) * lam_re + a_im * lam_im) / den
    f_im = (a_im * lam_re - (a_re - 1.0) * lam_im) / den
    bt_re = (f_re[..., None] * b_re - f_im[..., None] * b_im).transpose(0, 1, 3, 2)
    bt_im = (f_re[..., None] * b_im + f_im[..., None] * b_re).transpose(0, 1, 3, 2)
    lanes4 = lambda f0, f1, b0, b1: jnp.concatenate([f0, f1, b0, b1], axis=-1)

    strides = t * (2.0 ** jnp.arange(nsteps, dtype=F32))
    exps = jnp.concatenate([jnp.arange(t + 1, dtype=F32), strides, t * jnp.arange(1, SUBLANES + 1, dtype=F32)])
    (allf_re, allf_im), (allb_re, allb_im) = power(0, exps), power(1, exps)
    pf_re, pf_im = allf_re[:, t - 1::-1], allf_im[:, t - 1::-1]
    pb_re, pb_im = allb_re[:, 0:t], allb_im[:, 0:t]
    ea = lanes4(pf_re, pf_re, pb_re, pb_re)
    eb = lanes4(pf_im, pf_im, pb_im, pb_im)
    ba = lanes4(bt_re[0], bt_im[0], bt_re[1], bt_im[1])
    bm = lanes4(-bt_im[0], bt_re[0], -bt_im[1], bt_re[1])
    cc = jnp.concatenate([jnp.concatenate([c_re[0], -c_im[0]], axis=-1),
                          jnp.concatenate([c_re[1], -c_im[1]], axis=-1)], axis=1)
    rf_re, rf_im = allf_re[:, 1:t + 1], allf_im[:, 1:t + 1]
    rb_re, rb_im = allb_re[:, t:0:-1], allb_im[:, t:0:-1]
    oa = lanes4(rf_re, rf_im, rb_re, rb_im)
    ob = lanes4(rf_im, rf_re, rb_im, rb_re)
    ca = lanes4(c_re[0], -c_re[0], c_re[1], -c_re[1])
    cb = lanes4(-c_im[0], -c_im[0], -c_im[1], -c_im[1])
    st0, st1 = t + 1, t + 1 + nsteps
    apf_re, apf_im, apb_re, apb_im = (v[:, st0:st1] for v in (allf_re, allf_im, allb_re, allb_im))
    ap = jnp.stack([apf_re, apf_im, apb_re, apb_im], axis=2).reshape(g, 4 * nsteps, n)
    ap = jnp.concatenate([ap] * (LANES // n), axis=-1)
    rows = -(-4 * nsteps // SUBLANES) * SUBLANES
    ap = jnp.pad(ap, ((0, 0), (0, rows - 4 * nsteps), (0, 0)))
    f8_re, f8_im = allf_re[:, st1:], allf_im[:, st1:]
    b8_re, b8_im = allb_re[:, :st1 - 1:-1], allb_im[:, :st1 - 1:-1]
    a8 = jnp.concatenate([f8_re, f8_im, b8_re, b8_im], axis=1)
    a8 = jnp.concatenate([a8] * (LANES // n), axis=-1)
    return tuple(v.astype(F32) for v in (ea, eb, ba, bm, cc, ca, cb, oa, ob, ap, a8))


def _s5_call(ut, params, t, nb, ncl, ncc, nsteps, n_state, hc):
    ap = params[-2]
    ngrp_in, _, sw, _ = ut.shape
    nl = nb * ncl
    g = sw // hc
    th = t * hc
    rbp = -(-(ncc + ncl) // SUBLANES) * SUBLANES
    nv = rbp // SUBLANES
    pad = SUBLANES
    cpad = max(SUBLANES, 1 << max(0, (nv - 1).bit_length() - 1))
    assert nb * n_state == LANES and nb == 2 and 4 * nsteps <= ap.shape[1]
    gps = S5_GROUPS_PER_STEP
    assert g % gps == 0
    grp = lambda shp: pl.BlockSpec((gps,) + shp, lambda i: (i, 0, 0))
    return pl.pallas_call(
        functools.partial(_s5_kernel, gps=gps, t=t, hc=hc, nb=nb, ncl=ncl, ncc=ncc, pad=pad, nsteps=nsteps,
                          nstate=n_state),
        out_shape=jax.ShapeDtypeStruct((nl // LANES, t, sw, LANES), BF16),
        grid=(g // gps,),
        in_specs=[
            pl.BlockSpec((ngrp_in, t, gps * hc, LANES), lambda i: (0, 0, i, 0)),
            grp((t, 4 * n_state)), grp((t, 4 * n_state)), grp((hc, 4 * n_state)), grp((hc, 4 * n_state)),
            grp((2 * hc, 2 * n_state)), grp((hc, 4 * n_state)), grp((hc, 4 * n_state)),
            grp((t, 4 * n_state)), grp((t, 4 * n_state)), grp((ap.shape[1], LANES)), grp((4 * SUBLANES, LANES)),
        ],
        out_specs=pl.BlockSpec((nl // LANES, t, gps * hc, LANES), lambda i: (0, 0, i, 0)),
        scratch_shapes=[pltpu.VMEM((gps, th, 4 * n_state), F32), pltpu.VMEM((gps, th, 4 * n_state), BF16),
                        pltpu.VMEM((gps, th, th), BF16), pltpu.VMEM((gps, th, 4 * n_state), BF16),
                        pltpu.VMEM((gps, 4, pad + rbp + pad, LANES), F32),
                        pltpu.VMEM((gps, 4, cpad + nv + cpad, LANES), F32),
                        pltpu.VMEM((gps, 4 * n_state, nl), BF16)],
        compiler_params=_cparams(("arbitrary",)),
        name="s5",
    )(ut, *params)


def _merge_kernel(y_ref, u_ref, p_ref, zs_ref, x_ref, gate_ref, d_ref, gw_ref, gb_ref, ow_ref, fg_ref, o_ref):
    sw = y_ref.shape[-1]
    pw = p_ref.shape[1] * p_ref.shape[-1]
    y = _gelu_tanh(y_ref[0].astype(F32) + d_ref[...] * u_ref[0].astype(F32)).astype(BF16)
    yy = jnp.dot(y, gw_ref[...], preferred_element_type=F32) + gb_ref[...]
    ssm_out = yy[:, :sw] * _sigmoid(yy[:, sw:])
    pool = jnp.concatenate([p_ref[0, g] for g in range(p_ref.shape[1])], axis=1)
    br_pool = pool * zs_ref[0, :, 0:pw]
    br_ssm = (ssm_out * zs_ref[0, :, pw:pw + sw].astype(F32)).astype(BF16)
    mix = jnp.dot(jnp.concatenate([br_pool, br_ssm], axis=1), ow_ref[...], preferred_element_type=F32)
    xo = x_ref[0] + gate_ref[0] * mix
    ms = jnp.mean(xo * xo, axis=-1, keepdims=True)
    o_ref[0] = (xo * lax.rsqrt(ms + EPS) * fg_ref[...]).astype(o_ref.dtype)


def _merge_call(y_ssm, u_ssm, pool_out, zs, x, gate, d_skip, glu_w, glu_b, out_w, final_g, tm):
    b, l, d = x.shape
    sw = y_ssm.shape[-1]
    ng, pc = pool_out.shape[1], pool_out.shape[-1]
    mixw = zs.shape[-1]
    tok = lambda wd: pl.BlockSpec((1, tm, wd), lambda i, j: (i, j, 0))
    return pl.pallas_call(
        _merge_kernel,
        out_shape=jax.ShapeDtypeStruct((b, l, d), x.dtype),
        grid=(b, l // tm),
        in_specs=[
            tok(sw), tok(sw), pl.BlockSpec((1, ng, tm, pc), lambda i, j: (i, 0, j, 0)), tok(mixw), tok(d),
            pl.BlockSpec((1, 1, d), lambda i, j: (i, 0, 0)),
            _const_spec((1, sw)), _const_spec((sw, 2 * sw)), _const_spec((1, 2 * sw)), _const_spec((mixw, d)),
            _const_spec((1, d)),
        ],
        out_specs=tok(d),
        compiler_params=_cparams(("arbitrary", "arbitrary")),
        name="merge",
    )(y_ssm, u_ssm, pool_out, zs, x, gate, d_skip.reshape(1, sw), glu_w, glu_b.reshape(1, 2 * sw), out_w,
      final_g.reshape(1, d))


def kernel(x, c, ctx, c_ctx, ada_w, ada_b, norm_g, in_w, pool_w, pool_scale, s5_lam_re, s5_lam_im, s5_log_dt,
           s5_b_re, s5_b_im, s5_c_re, s5_c_im, s5_d, glu_w, glu_b, out_w, final_g):
    assert ada_w.shape[0] == 1, "single-layer block"
    bsz, seq, d = x.shape
    cl = ctx.shape[1]
    mixw = in_w.shape[2] // 2
    poolw = pool_scale.shape[-1]
    ssmw = s5_d.shape[-1]
    n_grp, n_state = s5_lam_re.shape[2], s5_lam_re.shape[3]
    hc = ssmw // n_grp
    t = CHUNK_T
    assert poolw + ssmw == mixw and seq % t == 0 and cl % t == 0 and 2 * n_state == LANES and LANES % hc == 0

    mod, in_w16 = _ada_call(jnp.concatenate([c, c_ctx[None]], axis=0), ada_w[0], ada_b[0], in_w[0])
    shift, scale, gate = mod[:, :d], mod[:, d:2 * d], mod[:, 2 * d:]
    g1 = norm_g[0].reshape(1, d)

    tm = min(512, seq)
    u_pool, u_ssm, zs = _inproj_call(x, scale[:bsz, None], shift[:bsz, None], g1, in_w16,
                                     (poolw, ssmw, mixw), (False, False, True), min(1024, seq), "inproj",
                                     groups=(len(POOL_WINDOWS), 1, 1))
    sc_c = jnp.broadcast_to(scale[bsz][None, None], (bsz, 1, d))
    sh_c = jnp.broadcast_to(shift[bsz][None, None], (bsz, 1, d))
    assert poolw % ssmw == 0
    (uc_ssm,) = _inproj_call(ctx, sc_c, sh_c, g1, in_w16, (ssmw,), (False,), min(256, cl), "inproj_ctx",
                             wcol=poolw // ssmw)

    pool_out, out_w16, glu_w16 = _pool_call(u_pool, pool_w[0], pool_scale[0], out_w[0], glu_w[0])

    ncl, ncc = seq // t, cl // t
    nsteps = max(1, (ncc + ncl - 1).bit_length())
    params = _s5_params(s5_lam_re[0], s5_lam_im[0], s5_log_dt[0], s5_b_re[0], s5_b_im[0], s5_c_re[0], s5_c_im[0],
                        t, nsteps)
    ut = _pack_call(u_ssm, uc_ssm, t)
    yt = _s5_call(ut, params, t, bsz, ncl, ncc, nsteps, n_state, hc)
    y_ssm = _unpack_call(yt, bsz, seq)

    return _merge_call(y_ssm, u_ssm, pool_out, zs, x, gate[:bsz, None], s5_d[0], glu_w16, glu_b[0], out_w16,
                       final_g, tm)
```

```python
import functools
import math

import jax
import jax.numpy as jnp
import numpy as np
from jax import lax
from jax.experimental import pallas as pl
from jax.experimental.pallas import tpu as pltpu

GRID_W = 64
POOL_WINDOWS = (2, 4, 8, 16)
EPS = 1e-6
CHUNK_T = 32
S5_GROUPS_PER_STEP = 4
LANES = 128
SUBLANES = 8
VMEM_LIMIT = 56 * 1024 * 1024

F32 = jnp.float32
BF16 = jnp.bfloat16
TN_DIMS = (((0,), (0,)), ((), ()))
NT_DIMS = (((1,), (1,)), ((), ()))


def _cparams(sem):
    return pltpu.CompilerParams(dimension_semantics=sem, vmem_limit_bytes=VMEM_LIMIT)


def _const_spec(shape):
    return pl.BlockSpec(shape, lambda *_: (0,) * len(shape), pipeline_mode=pl.Buffered(1))


def _sigmoid(v):
    return 1.0 / (1.0 + jnp.exp(-v))


def _gelu_tanh(v):
    return 0.5 * v * (1.0 + jnp.tanh(math.sqrt(2.0 / math.pi) * (v + 0.044715 * (v * v * v))))


def _modulated_norm(x, g, scale, shift):
    ms = jnp.mean(x * x, axis=-1, keepdims=True)
    return (x * lax.rsqrt(ms + EPS) * g) * (1.0 + scale) + shift


ADA_LANE_TILES = 6


def _ada_kernel(ct_ref, w_ref, b_ref, o_ref, sb_ref, acc_ref, *, nrows):
    tk, n = w_ref.shape
    step = pl.program_id(0)

    @pl.when(step == 0)
    def _():
        acc_ref[...] = jnp.zeros(acc_ref.shape, F32)

    cv = ct_ref[...]
    s = cv * _sigmoid(cv)
    for r in range(nrows):
        sb_ref[r] = jnp.broadcast_to(s[:, r:r + 1], (tk, LANES))

    group = ADA_LANE_TILES * LANES
    for g0 in range(0, n, group):
        def body(i, accs, g0=g0):
            k0 = pl.multiple_of(i * SUBLANES, SUBLANES)
            sb = [sb_ref[r, pl.ds(k0, SUBLANES), :] for r in range(nrows)]
            out = []
            for lt in range(ADA_LANE_TILES):
                w = w_ref[pl.ds(k0, SUBLANES), g0 + lt * LANES:g0 + (lt + 1) * LANES]
                out.append(tuple(accs[lt][r] + sb[r] * w for r in range(nrows)))
            return tuple(out)

        init = tuple(tuple(jnp.zeros((SUBLANES, LANES), F32) for _ in range(nrows)) for _ in range(ADA_LANE_TILES))
        accs = lax.fori_loop(0, tk // SUBLANES, body, init, unroll=8)
        for lt in range(ADA_LANE_TILES):
            lanes = slice(g0 + lt * LANES, g0 + (lt + 1) * LANES)
            for r in range(nrows):
                acc_ref[r, :, lanes] = acc_ref[r, :, lanes] + accs[lt][r]

    @pl.when(step == pl.num_programs(0) - 1)
    def _():
        o_ref[...] = jnp.zeros(o_ref.shape, F32)
        for r in range(nrows):
            o_ref[r:r + 1, :] = jnp.sum(acc_ref[r], axis=0, keepdims=True) + b_ref[...]


def _ada_call(cvecs, ada_w, ada_b):
    nrows, d = cvecs.shape
    n = ada_w.shape[1]
    tk = 256 if d % 256 == 0 else d
    assert n % (ADA_LANE_TILES * LANES) == 0
    ct = jnp.zeros((d, SUBLANES), F32).at[:, :nrows].set(cvecs.T)
    out = pl.pallas_call(
        functools.partial(_ada_kernel, nrows=nrows),
        out_shape=jax.ShapeDtypeStruct((SUBLANES, n), F32),
        grid=(d // tk,),
        in_specs=[
            pl.BlockSpec((tk, SUBLANES), lambda j: (j, 0)),
            pl.BlockSpec((tk, n), lambda j: (j, 0)),
            _const_spec((1, n)),
        ],
        out_specs=pl.BlockSpec((SUBLANES, n), lambda j: (0, 0)),
        scratch_shapes=[pltpu.VMEM((nrows, tk, LANES), F32), pltpu.VMEM((nrows, SUBLANES, n), F32)],
        compiler_params=_cparams(("arbitrary",)),
        name="ada",
    )(ct, ada_w, ada_b.reshape(1, n))
    return out[:nrows]


def _inproj_kernel(x_ref, sc_ref, sh_ref, g_ref, w_ref, *o_refs, nchunk, silu):
    h = _modulated_norm(x_ref[0], g_ref[...], sc_ref[0], sh_ref[0]).astype(BF16)
    col = 0
    for o_ref, gate in zip(o_refs, silu):
        grouped = len(o_ref.shape) == 4
        cw = o_ref.shape[-1]
        width = cw * o_ref.shape[1] if grouped else cw
        for n0 in range(0, width, nchunk):
            acc = jnp.dot(h, w_ref[:, col + n0:col + n0 + nchunk], preferred_element_type=F32)
            if gate:
                acc = acc * _sigmoid(acc)
            if grouped:
                for c0 in range(0, nchunk, cw):
                    o_ref[0, (n0 + c0) // cw] = acc[:, c0:c0 + cw].astype(o_ref.dtype)
            else:
                o_ref[0, :, n0:n0 + nchunk] = acc.astype(o_ref.dtype)
        col += width


def _inproj_call(x, scale, shift, g, w, widths, silu, tm, name, wcol=0, groups=None):
    b, l, d = x.shape
    nout = sum(widths)
    assert w.shape[1] % nout == 0 and l % tm == 0
    nchunk = min(1024, min(widths))
    assert all(wd % nchunk == 0 for wd in widths)
    groups = groups or (1,) * len(widths)
    shape = lambda wd, ng, rows: (b, rows, wd) if ng == 1 else (b, ng, rows, wd // ng)
    spec = lambda wd, ng: (pl.BlockSpec((1, tm, wd), lambda i, j: (i, j, 0)) if ng == 1 else
                           pl.BlockSpec((1, ng, tm, wd // ng), lambda i, j: (i, 0, j, 0)))
    return pl.pallas_call(
        functools.partial(_inproj_kernel, nchunk=nchunk, silu=silu),
        out_shape=[jax.ShapeDtypeStruct(shape(wd, ng, l), BF16) for wd, ng in zip(widths, groups)],
        grid=(b, l // tm),
        in_specs=[
            pl.BlockSpec((1, tm, d), lambda i, j: (i, j, 0)),
            pl.BlockSpec((1, 1, d), lambda i, j: (i, 0, 0)),
            pl.BlockSpec((1, 1, d), lambda i, j: (i, 0, 0)),
            _const_spec((1, d)),
            pl.BlockSpec((d, nout), lambda i, j: (0, wcol), pipeline_mode=pl.Buffered(1)),
        ],
        out_specs=[spec(wd, ng) for wd, ng in zip(widths, groups)],
        compiler_params=_cparams(("arbitrary", "arbitrary")),
        name=name,
    )(x, scale, shift, g, w)


def _chunk_perm(t):
    i = np.arange(SUBLANES * t)
    src = (i % SUBLANES) * t + i // SUBLANES
    return jnp.asarray(src[:, None] == i[None, :], BF16)


def _pack_kernel(u_ref, uc_ref, perm_ref, ut_ref, scr_ref, *, t, nblk):
    j = pl.program_id(0)
    s = scr_ref.shape[2]
    rows = SUBLANES * t
    ctok = uc_ref.shape[0]

    def permute(src, blk):
        pm = jnp.dot(perm_ref[...], src, preferred_element_type=F32)
        for ti in range(t):
            scr_ref[ti, pl.ds(pl.multiple_of(blk * SUBLANES, SUBLANES), SUBLANES), :] = pm[ti * SUBLANES:(ti + 1) * SUBLANES]

    @pl.when(j < nblk)
    def _():
        def body(blk, carry):
            permute(u_ref[pl.ds(pl.multiple_of(blk * rows, rows), rows), :], blk)
            return carry

        lax.fori_loop(0, LANES // SUBLANES, body, 0, unroll=2)

    @pl.when(j == nblk)
    def _():
        for blk in range(ctok // rows):
            permute(uc_ref[blk * rows:(blk + 1) * rows, :], blk)
        scr_ref[:, ctok // t:LANES, :] = jnp.zeros((t, LANES - ctok // t, s), F32)

    def transpose(ti, carry):
        for k in range(s // LANES):
            ut_ref[0, ti, LANES * k:LANES * (k + 1), :] = scr_ref[ti, :, LANES * k:LANES * (k + 1)].T.astype(BF16)
        return carry

    lax.fori_loop(0, t, transpose, 0, unroll=2)


def _pack_call(u_ssm, uc_ssm, t):
    b, l, s = u_ssm.shape
    tok = LANES * t
    rows = SUBLANES * t
    ntok = b * l
    ctok = b * uc_ssm.shape[1]
    assert ntok % tok == 0 and ctok <= tok and ctok % rows == 0 and s % LANES == 0
    nblk = ntok // tok
    return pl.pallas_call(
        functools.partial(_pack_kernel, t=t, nblk=nblk),
        out_shape=jax.ShapeDtypeStruct((nblk + 1, t, s, LANES), BF16),
        grid=(nblk + 1,),
        in_specs=[pl.BlockSpec((tok, s), lambda j: (jnp.minimum(j, nblk - 1), 0)),
                  _const_spec((ctok, s)), _const_spec((rows, rows))],
        out_specs=pl.BlockSpec((1, t, s, LANES), lambda j: (j, 0, 0, 0)),
        scratch_shapes=[pltpu.VMEM((t, LANES, s), F32)],
        compiler_params=_cparams(("arbitrary",)),
        name="pack",
    )(u_ssm.reshape(ntok, s), uc_ssm.reshape(ctok, s), _chunk_perm(t))


def _unpack_kernel(yt_ref, perm_ref, y_ref, scr_ref, *, t):
    s = scr_ref.shape[2]
    rows = SUBLANES * t

    def transpose(ti, carry):
        for k in range(s // LANES):
            scr_ref[ti, :, LANES * k:LANES * (k + 1)] = yt_ref[0, ti, LANES * k:LANES * (k + 1), :].astype(F32).T
        return carry

    lax.fori_loop(0, t, transpose, 0, unroll=2)

    def body(blk, carry):
        c0 = pl.multiple_of(blk * SUBLANES, SUBLANES)
        src = jnp.concatenate([scr_ref[ti, pl.ds(c0, SUBLANES), :] for ti in range(t)], axis=0).astype(BF16)
        out = jnp.dot(perm_ref[...], src, preferred_element_type=F32)
        y_ref[pl.ds(pl.multiple_of(blk * rows, rows), rows), :] = out.astype(y_ref.dtype)
        return carry

    lax.fori_loop(0, LANES // SUBLANES, body, 0, unroll=2)


def _unpack_call(yt, b, l):
    ngrp, t, s, _ = yt.shape
    nl = ngrp * LANES
    tok = LANES * t
    rows = SUBLANES * t
    y = pl.pallas_call(
        functools.partial(_unpack_kernel, t=t),
        out_shape=jax.ShapeDtypeStruct((nl * t, s), BF16),
        grid=(ngrp,),
        in_specs=[pl.BlockSpec((1, t, s, LANES), lambda j: (j, 0, 0, 0)), _const_spec((rows, rows))],
        out_specs=pl.BlockSpec((tok, s), lambda j: (j, 0)),
        scratch_shapes=[pltpu.VMEM((t, LANES, s), F32)],
        compiler_params=_cparams(("arbitrary",)),
        name="unpack",
    )(yt, _chunk_perm(t).T)
    return y.reshape(b, l, s)


POOL_BLOCK_ROWS = 8
POOL_BLOCKS_PER_STEP = 8


def _pool_kernel(u_ref, *refs, tb, pad):
    nwin = len(POOL_WINDOWS)
    band_refs = refs[:nwin]
    inv_ref, pw_ref, ps_ref, wa_ref, wb_ref, o_ref, wa16_ref, wb16_ref, pad_ref = refs[nwin:]
    wa16_ref[...] = wa_ref[...].astype(wa16_ref.dtype)
    wb16_ref[...] = wb_ref[...].astype(wb16_ref.dtype)
    grp = pl.program_id(1)
    l = u_ref.shape[2]
    c = u_ref.shape[3]
    nblk = l // tb
    pad_ref[0:pad, :] = jnp.zeros((pad, c), pad_ref.dtype)
    pad_ref[pad + l:pad + l + pad, :] = jnp.zeros((pad, c), pad_ref.dtype)

    def copy(i, carry):
        r0 = pl.multiple_of(i * tb, tb)
        pad_ref[pl.ds(pl.multiple_of(pad + r0, GRID_W), tb), :] = u_ref[0, 0, pl.ds(r0, tb), :]
        return carry

    lax.fori_loop(0, nblk, copy, 0)

    for gi, w in enumerate(POOL_WINDOWS):
        half = w // 2
        span = band_refs[gi].shape[1]

        @pl.when(grp == gi)
        def _(half=half, span=span, band_ref=band_refs[gi]):
            def blk(i, carry):
                first = POOL_BLOCKS_PER_STEP * i
                r0 = pl.multiple_of(first * tb, POOL_BLOCKS_PER_STEP * tb)
                wins = [pad_ref[pl.ds(pl.multiple_of(pad + r0 + k * tb - half * GRID_W, GRID_W), span), :]
                        for k in range(POOL_BLOCKS_PER_STEP)]
                box = jnp.dot(band_ref[...], jnp.concatenate(wins, axis=1), preferred_element_type=F32)
                dlts = []
                for k in range(POOL_BLOCKS_PER_STEP):
                    j = first + k
                    variant = jnp.where(j == 0, 0, jnp.where(j == nblk - 1, 2, 1))
                    inv = jnp.concatenate([inv_ref[0, variant]] * (c // LANES), axis=1)
                    u_blk = u_ref[0, 0, pl.ds(r0 + k * tb, tb), :].astype(F32)
                    dlts.append((box[:, k * c:(k + 1) * c] * inv - u_blk).astype(BF16))
                o = jnp.dot(jnp.concatenate(dlts, axis=0), pw_ref[0], preferred_element_type=F32) * ps_ref[...]
                o_ref[0, 0, pl.ds(r0, POOL_BLOCKS_PER_STEP * tb), :] = o.astype(o_ref.dtype)
                return carry

            lax.fori_loop(0, nblk // POOL_BLOCKS_PER_STEP, blk, 0)


def _pool_tables(rows, tb):
    br = tb // GRID_W
    i = np.arange(tb)
    ri, ci = i // GRID_W, i % GRID_W
    bands, invs = [], []
    for w in POOL_WINDOWS:
        half = w // 2
        j = np.arange((br + w) * GRID_W)
        dr = (j // GRID_W)[None, :] - ri[:, None]
        dc = (j % GRID_W)[None, :] - ci[:, None]
        bands.append(jnp.asarray((dr >= 0) & (dr < w) & (dc >= -half) & (dc < w - half), BF16))
        cnt_c = np.minimum(ci + w - half, GRID_W) - np.maximum(ci - half, 0)
        per_variant = []
        for r_first in (0, br, rows - br):
            r = r_first + ri
            cnt_r = np.minimum(r + w - half, rows) - np.maximum(r - half, 0)
            per_variant.append(np.float32(1.0) / (cnt_r * cnt_c).astype(np.float32))
        invs.append(np.stack(per_variant))
    inv = np.broadcast_to(np.stack(invs)[..., None], (len(POOL_WINDOWS), 3, tb, LANES))
    return bands, jnp.asarray(inv, F32)


def _pool_call(u_pool, pool_w, pool_scale, wa, wb):
    b, ng, l, c = u_pool.shape
    nstep = b * ng
    assert wa.shape[0] % (nstep * 16) == 0 and wb.shape[0] % (nstep * 16) == 0
    rows_of = lambda w: pl.BlockSpec((w.shape[0] // nstep, w.shape[1]), lambda i, j: (i * ng + j, 0))
    assert ng == len(POOL_WINDOWS)
    p = ng * c
    rows = l // GRID_W
    tb = POOL_BLOCK_ROWS * GRID_W
    pad = (max(POOL_WINDOWS) // 2) * GRID_W
    assert l % (tb * POOL_BLOCKS_PER_STEP) == 0 and c % LANES == 0
    assert rows >= 3 * POOL_BLOCK_ROWS and max(POOL_WINDOWS) <= 2 * POOL_BLOCK_ROWS
    bands, inv = _pool_tables(rows, tb)
    return pl.pallas_call(
        functools.partial(_pool_kernel, tb=tb, pad=pad),
        out_shape=[jax.ShapeDtypeStruct((b, ng, l, c), BF16), jax.ShapeDtypeStruct(wa.shape, BF16),
                   jax.ShapeDtypeStruct(wb.shape, BF16)],
        grid=(b, ng),
        in_specs=[pl.BlockSpec((1, 1, l, c), lambda i, j: (i, j, 0, 0))]
        + [_const_spec(bd.shape) for bd in bands]
        + [
            pl.BlockSpec((1, 3, tb, LANES), lambda i, j: (j, 0, 0, 0)),
            pl.BlockSpec((1, c, c), lambda i, j: (j, 0, 0)),
            pl.BlockSpec((1, c), lambda i, j: (0, j)),
            rows_of(wa), rows_of(wb),
        ],
        out_specs=[pl.BlockSpec((1, 1, l, c), lambda i, j: (i, j, 0, 0)), rows_of(wa), rows_of(wb)],
        scratch_shapes=[pltpu.VMEM((pad + l + pad, c), BF16)],
        compiler_params=_cparams(("arbitrary", "arbitrary")),
        name="pool",
    )(u_pool, *bands, inv, pool_w.astype(BF16), pool_scale.reshape(1, p), wa, wb)


def _dot_hi_nt(a, b):
    a_hi = a.astype(BF16)
    a_lo = (a - a_hi.astype(F32)).astype(BF16)
    b_hi = b.astype(BF16)
    b_lo = (b - b_hi.astype(F32)).astype(BF16)
    dot = lambda p, q: lax.dot_general(p, q, NT_DIMS, preferred_element_type=F32)
    return dot(a_hi, b_hi) + dot(a_hi, b_lo) + dot(a_lo, b_hi)


def _s5_kernel(u_ref, ea_ref, eb_ref, ba_ref, bm_ref, cc_ref, ca_ref, cb_ref, oa_ref, ob_ref, ap_ref, a8_ref, y_ref,
               et_ref, win_ref, mt_ref, wout_ref, buf_ref, cbuf_ref, hin_ref, *, gps, hc, **static):
    @pl.when(pl.program_id(0) == 0)
    def _():
        buf_ref[...] = jnp.zeros(buf_ref.shape, F32)
        cbuf_ref[...] = jnp.zeros(cbuf_ref.shape, F32)

    for gg in range(gps):
        rows = slice(gg * hc, (gg + 1) * hc)
        _s5_group(u_ref.at[:, :, rows, :], ea_ref.at[gg], eb_ref.at[gg], ba_ref.at[gg], bm_ref.at[gg], cc_ref.at[gg],
                  ca_ref.at[gg], cb_ref.at[gg], oa_ref.at[gg], ob_ref.at[gg], ap_ref.at[gg], a8_ref.at[gg],
                  y_ref.at[:, :, rows, :], et_ref.at[gg], win_ref.at[gg], mt_ref.at[gg], wout_ref.at[gg], buf_ref.at[gg],
                  cbuf_ref.at[gg], hin_ref.at[gg],
                  hc=hc, **static)


def _s5_group(u_ref, ea_ref, eb_ref, ba_ref, bm_ref, cc_ref, ca_ref, cb_ref, oa_ref, ob_ref, ap_ref, a8_ref, y_ref,
              et_ref, win_ref, mt_ref, wout_ref, buf_ref, cbuf_ref, hin_ref, *, t, hc, nb, ncl, ncc, pad, nsteps, nstate):
    n = nstate
    th = t * hc
    nl = nb * ncl
    rb = ncc + ncl
    ntile = th // LANES

    ba = ba_ref[...]
    bm = bm_ref[...]
    ca = ca_ref[...]
    cb = cb_ref[...]
    for ti in range(t):
        rows = slice(ti * hc, (ti + 1) * hc)
        e_rows = ba * ea_ref[ti:ti + 1, :] + bm * eb_ref[ti:ti + 1, :]
        et_ref[rows, :] = e_rows
        win_ref[rows, :] = e_rows.astype(BF16)
        wout_ref[rows, :] = (ca * oa_ref[ti:ti + 1, :] + cb * ob_ref[ti:ti + 1, :]).astype(BF16)
    cc = cc_ref[...]
    q_f = _dot_hi_nt(cc[0:hc], et_ref[:, 0:2 * n])
    q_b = _dot_hi_nt(cc[hc:2 * hc], et_ref[:, 2 * n:4 * n])
    zeros = jnp.zeros((hc, th), F32)
    line_b = jnp.concatenate([zeros, q_b], axis=1)
    line_f = jnp.concatenate([q_f, zeros], axis=1)
    per_tile = LANES // hc
    rolled_b = [line_b if r == 0 else pltpu.roll(line_b, hc * r, axis=1) for r in range(per_tile)]
    rolled_f = [line_f if r == per_tile - 1 else pltpu.roll(line_f, 2 * th - hc * (per_tile - 1 - r), axis=1)
                for r in range(per_tile)]
    for ti in range(t):
        a, r = divmod(ti, per_tile)
        row_b = rolled_b[r][:, th - LANES * a:2 * th - LANES * a]
        off_f = LANES * (ntile - 1 - a)
        row_f = rolled_f[r][:, off_f:off_f + th]
        mt_ref[ti * hc:(ti + 1) * hc, :] = (row_b + row_f).astype(BF16)

    u = jnp.concatenate([jnp.concatenate([u_ref[cg, ti] for cg in range(u_ref.shape[0])], axis=1)
                         for ti in range(t)], axis=0)
    st_all = lax.dot_general(win_ref[...], u, TN_DIMS, preferred_element_type=F32)
    first_half = lax.broadcasted_iota(jnp.int32, (LANES, LANES), 1) < n
    for q in range(4):
        rows_q = st_all[q * n:(q + 1) * n]
        lat = jnp.concatenate([rows_q[:, b * ncl:(b + 1) * ncl] for b in range(nb)], axis=0).T
        cblk = jnp.concatenate([rows_q[:, nl:nl + LANES]] * nb, axis=0).T
        ctx = jnp.where(first_half, cblk, jnp.concatenate([cblk[ncc:], cblk[:ncc]], axis=0))[0:ncc]
        if q < 2:
            buf_ref[q, pad:pad + ncc, :] = ctx
            buf_ref[q, pad + ncc:pad + rb, :] = lat
        else:
            buf_ref[q, pad:pad + ncl, :] = lat
            buf_ref[q, pad + ncl:pad + rb, :] = ctx
    def step(ref, d, k, sgn, lo, rows, keep=None):
        ar = ap_ref[4 * k + 2 * d:4 * k + 2 * d + 1, :]
        ai = ap_ref[4 * k + 2 * d + 1:4 * k + 2 * d + 2, :]
        pr = ref[2 * d, lo + sgn:lo + sgn + rows, :]
        pi = ref[2 * d + 1, lo + sgn:lo + sgn + rows, :]
        if keep is not None:
            pr = jnp.where(keep, pr, 0.0)
            pi = jnp.where(keep, pi, 0.0)
        xr = ref[2 * d, lo:lo + rows, :]
        xi = ref[2 * d + 1, lo:lo + rows, :]
        ref[2 * d, lo:lo + rows, :] = xr + (ar * pr - ai * pi)
        ref[2 * d + 1, lo:lo + rows, :] = xi + (ar * pi + ai * pr)

    rbp = buf_ref.shape[1] - 2 * pad
    nv = rbp // SUBLANES
    cpad = (cbuf_ref.shape[1] - nv) // 2
    in_block = lax.broadcasted_iota(jnp.int32, (rbp, LANES), 0) % SUBLANES
    nlocal = SUBLANES.bit_length() - 1
    nchunk = 3 if nv % 3 == 0 else 1
    crow = rbp // nchunk
    blk = in_block[0:crow]
    for d in range(2):
        for ch in range(nchunk):
            rows = slice(pad + ch * crow, pad + (ch + 1) * crow)
            xr = buf_ref[2 * d, rows, :]
            xi = buf_ref[2 * d + 1, rows, :]
            for k in range(nlocal):
                st = 1 << k
                ar = ap_ref[4 * k + 2 * d:4 * k + 2 * d + 1, :]
                ai = ap_ref[4 * k + 2 * d + 1:4 * k + 2 * d + 2, :]
                keep = blk >= st if d == 0 else blk < SUBLANES - st
                shift = st if d == 0 else SUBLANES - st
                rot = lambda v: pltpu.roll(v.reshape(crow // SUBLANES, SUBLANES, LANES), shift, axis=1).reshape(crow, LANES)
                pr = jnp.where(keep, rot(xr), 0.0)
                pi = jnp.where(keep, rot(xi), 0.0)
                xr, xi = xr + (ar * pr - ai * pi), xi + (ar * pi + ai * pr)
            buf_ref[2 * d, rows, :] = xr
            buf_ref[2 * d + 1, rows, :] = xi
    for d in range(2):
        edge = SUBLANES - 1 if d == 0 else 0
        for q in (2 * d, 2 * d + 1):
            cbuf_ref[q, cpad:cpad + nv, :] = buf_ref[q, pl.ds(pad + edge, nv, stride=SUBLANES), :]
    for m in range(max(0, (nv - 1).bit_length())):
        st = 1 << m
        step(cbuf_ref, 0, nlocal + m, -st, cpad, nv)
        step(cbuf_ref, 1, nlocal + m, st, cpad, nv)
    for d in range(2):
        a8r = a8_ref[2 * SUBLANES * d:2 * SUBLANES * d + SUBLANES, :]
        a8i = a8_ref[2 * SUBLANES * d + SUBLANES:2 * SUBLANES * (d + 1), :]
        for j in range(nv):
            src = j - 1 if d == 0 else j + 1
            if 0 <= src < nv:
                cr = jnp.broadcast_to(cbuf_ref[2 * d, cpad + src:cpad + src + 1, :], (SUBLANES, LANES))
                ci = jnp.broadcast_to(cbuf_ref[2 * d + 1, cpad + src:cpad + src + 1, :], (SUBLANES, LANES))
                rows = slice(pad + j * SUBLANES, pad + (j + 1) * SUBLANES)
                buf_ref[2 * d, rows, :] = buf_ref[2 * d, rows, :] + (a8r * cr - a8i * ci)
                buf_ref[2 * d + 1, rows, :] = buf_ref[2 * d + 1, rows, :] + (a8r * ci + a8i * cr)
    for q in range(4):
        lo = pad + ncc - 1 if q < 2 else pad + 1
        ent = buf_ref[q, lo:lo + ncl, :].T.astype(BF16)
        for b in range(nb):
            hin_ref[q * n:(q + 1) * n, b * ncl:(b + 1) * ncl] = ent[b * n:(b + 1) * n]

    y = jnp.dot(mt_ref[...], u[:, 0:nl], preferred_element_type=F32)
    y = y + jnp.dot(wout_ref[...], hin_ref[...], preferred_element_type=F32)
    for ti in range(t):
        for cg in range(y_ref.shape[0]):
            y_ref[cg, ti] = y[ti * hc:(ti + 1) * hc, cg * LANES:(cg + 1) * LANES].astype(y_ref.dtype)


def _s5_params(lam_re, lam_im, log_dt, b_re, b_im, c_re, c_im, t, nsteps):
    _, g, n = lam_re.shape
    h = b_re.shape[-1]
    lam_re, lam_im = lam_re.astype(F32), lam_im.astype(F32)
    dt = jnp.exp(log_dt.astype(F32))[..., None]
    lr, li = lam_re * dt, lam_im * dt

    def power(d, k):
        kk = k[None, :, None]
        mag = jnp.exp(lr[d][:, None, :] * kk)
        return mag * jnp.cos(li[d][:, None, :] * kk), mag * jnp.sin(li[d][:, None, :] * kk)

    a_re, a_im = jnp.exp(lr) * jnp.cos(li), jnp.exp(lr) * jnp.sin(li)
    den = lam_re * lam_re + lam_im * lam_im
    f_re = ((a_re - 1.0) * lam_re + a_im * lam_im) / den
    f_im = (a_im * lam_re - (a_re - 1.0) * lam_im) / den
    bt_re = (f_re[..., None] * b_re - f_im[..., None] * b_im).transpose(0, 1, 3, 2)
    bt_im = (f_re[..., None] * b_im + f_im[..., None] * b_re).transpose(0, 1, 3, 2)
    lanes4 = lambda f0, f1, b0, b1: jnp.concatenate([f0, f1, b0, b1], axis=-1)

    strides = t * (2.0 ** jnp.arange(nsteps, dtype=F32))
    exps = jnp.concatenate([jnp.arange(t + 1, dtype=F32), strides, t * jnp.arange(1, SUBLANES + 1, dtype=F32)])
    (allf_re, allf_im), (allb_re, allb_im) = power(0, exps), power(1, exps)
    pf_re, pf_im = allf_re[:, t - 1::-1], allf_im[:, t - 1::-1]
    pb_re, pb_im = allb_re[:, 0:t], allb_im[:, 0:t]
    ea = lanes4(pf_re, pf_re, pb_re, pb_re)
    eb = lanes4(pf_im, pf_im, pb_im, pb_im)
    ba = lanes4(bt_re[0], bt_im[0], bt_re[1], bt_im[1])
    bm = lanes4(-bt_im[0], bt_re[0], -bt_im[1], bt_re[1])
    cc = jnp.concatenate([jnp.concatenate([c_re[0], -c_im[0]], axis=-1),
                          jnp.concatenate([c_re[1], -c_im[1]], axis=-1)], axis=1)
    rf_re, rf_im = allf_re[:, 1:t + 1], allf_im[:, 1:t + 1]
    rb_re, rb_im = allb_re[:, t:0:-1], allb_im[:, t:0:-1]
    oa = lanes4(rf_re, rf_im, rb_re, rb_im)
    ob = lanes4(rf_im, rf_re, rb_im, rb_re)
    ca = lanes4(c_re[0], -c_re[0], c_re[1], -c_re[1])
    cb = lanes4(-c_im[0], -c_im[0], -c_im[1], -c_im[1])
    st0, st1 = t + 1, t + 1 + nsteps
    apf_re, apf_im, apb_re, apb_im = (v[:, st0:st1] for v in (allf_re, allf_im, allb_re, allb_im))
    ap = jnp.stack([apf_re, apf_im, apb_re, apb_im], axis=2).reshape(g, 4 * nsteps, n)
    ap = jnp.concatenate([ap] * (LANES // n), axis=-1)
    rows = -(-4 * nsteps // SUBLANES) * SUBLANES
    ap = jnp.pad(ap, ((0, 0), (0, rows - 4 * nsteps), (0, 0)))
    f8_re, f8_im = allf_re[:, st1:], allf_im[:, st1:]
    b8_re, b8_im = allb_re[:, :st1 - 1:-1], allb_im[:, :st1 - 1:-1]
    a8 = jnp.concatenate([f8_re, f8_im, b8_re, b8_im], axis=1)
    a8 = jnp.concatenate([a8] * (LANES // n), axis=-1)
    return tuple(v.astype(F32) for v in (ea, eb, ba, bm, cc, ca, cb, oa, ob, ap, a8))


def _s5_call(ut, params, t, nb, ncl, ncc, nsteps, n_state, hc):
    ap = params[-2]
    ngrp_in, _, sw, _ = ut.shape
    nl = nb * ncl
    g = sw // hc
    th = t * hc
    rbp = -(-(ncc + ncl) // SUBLANES) * SUBLANES
    nv = rbp // SUBLANES
    pad = SUBLANES
    cpad = max(SUBLANES, 1 << max(0, (nv - 1).bit_length() - 1))
    assert nb * n_state == LANES and nb == 2 and 4 * nsteps <= ap.shape[1]
    gps = S5_GROUPS_PER_STEP
    assert g % gps == 0
    grp = lambda shp: pl.BlockSpec((gps,) + shp, lambda i: (i, 0, 0))
    return pl.pallas_call(
        functools.partial(_s5_kernel, gps=gps, t=t, hc=hc, nb=nb, ncl=ncl, ncc=ncc, pad=pad, nsteps=nsteps,
                          nstate=n_state),
        out_shape=jax.ShapeDtypeStruct((nl // LANES, t, sw, LANES), BF16),
        grid=(g // gps,),
        in_specs=[
            pl.BlockSpec((ngrp_in, t, gps * hc, LANES), lambda i: (0, 0, i, 0)),
            grp((t, 4 * n_state)), grp((t, 4 * n_state)), grp((hc, 4 * n_state)), grp((hc, 4 * n_state)),
            grp((2 * hc, 2 * n_state)), grp((hc, 4 * n_state)), grp((hc, 4 * n_state)),
            grp((t, 4 * n_state)), grp((t, 4 * n_state)), grp((ap.shape[1], LANES)), grp((4 * SUBLANES, LANES)),
        ],
        out_specs=pl.BlockSpec((nl // LANES, t, gps * hc, LANES), lambda i: (0, 0, i, 0)),
        scratch_shapes=[pltpu.VMEM((gps, th, 4 * n_state), F32), pltpu.VMEM((gps, th, 4 * n_state), BF16),
                        pltpu.VMEM((gps, th, th), BF16), pltpu.VMEM((gps, th, 4 * n_state), BF16),
                        pltpu.VMEM((gps, 4, pad + rbp + pad, LANES), F32),
                        pltpu.VMEM((gps, 4, cpad + nv + cpad, LANES), F32),
                        pltpu.VMEM((gps, 4 * n_state, nl), BF16)],
        compiler_params=_cparams(("arbitrary",)),
        name="s5",
    )(ut, *params)


def _merge_kernel(y_ref, u_ref, p_ref, zs_ref, x_ref, gate_ref, d_ref, gw_ref, gb_ref, ow_ref, fg_ref, o_ref):
    sw = y_ref.shape[-1]
    pw = p_ref.shape[1] * p_ref.shape[-1]
    y = _gelu_tanh(y_ref[0].astype(F32) + d_ref[...] * u_ref[0].astype(F32)).astype(BF16)
    yy = jnp.dot(y, gw_ref[...], preferred_element_type=F32) + gb_ref[...]
    ssm_out = yy[:, :sw] * _sigmoid(yy[:, sw:])
    pool = jnp.concatenate([p_ref[0, g] for g in range(p_ref.shape[1])], axis=1)
    br_pool = pool * zs_ref[0, :, 0:pw]
    br_ssm = (ssm_out * zs_ref[0, :, pw:pw + sw].astype(F32)).astype(BF16)
    mix = jnp.dot(jnp.concatenate([br_pool, br_ssm], axis=1), ow_ref[...], preferred_element_type=F32)
    xo = x_ref[0] + gate_ref[0] * mix
    ms = jnp.mean(xo * xo, axis=-1, keepdims=True)
    o_ref[0] = (xo * lax.rsqrt(ms + EPS) * fg_ref[...]).astype(o_ref.dtype)


def _merge_call(y_ssm, u_ssm, pool_out, zs, x, gate, d_skip, glu_w, glu_b, out_w, final_g, tm):
    b, l, d = x.shape
    sw = y_ssm.shape[-1]
    ng, pc = pool_out.shape[1], pool_out.shape[-1]
    mixw = zs.shape[-1]
    tok = lambda wd: pl.BlockSpec((1, tm, wd), lambda i, j: (i, j, 0))
    return pl.pallas_call(
        _merge_kernel,
        out_shape=jax.ShapeDtypeStruct((b, l, d), x.dtype),
        grid=(b, l // tm),
        in_specs=[
            tok(sw), tok(sw), pl.BlockSpec((1, ng, tm, pc), lambda i, j: (i, 0, j, 0)), tok(mixw), tok(d),
            pl.BlockSpec((1, 1, d), lambda i, j: (i, 0, 0)),
            _const_spec((1, sw)), _const_spec((sw, 2 * sw)), _const_spec((1, 2 * sw)), _const_spec((mixw, d)),
            _const_spec((1, d)),
        ],
        out_specs=tok(d),
        compiler_params=_cparams(("arbitrary", "arbitrary")),
        name="merge",
    )(y_ssm, u_ssm, pool_out, zs, x, gate, d_skip.reshape(1, sw), glu_w, glu_b.reshape(1, 2 * sw), out_w,
      final_g.reshape(1, d))


def kernel(x, c, ctx, c_ctx, ada_w, ada_b, norm_g, in_w, pool_w, pool_scale, s5_lam_re, s5_lam_im, s5_log_dt,
           s5_b_re, s5_b_im, s5_c_re, s5_c_im, s5_d, glu_w, glu_b, out_w, final_g):
    assert ada_w.shape[0] == 1, "single-layer block"
    bsz, seq, d = x.shape
    cl = ctx.shape[1]
    mixw = in_w.shape[2] // 2
    poolw = pool_scale.shape[-1]
    ssmw = s5_d.shape[-1]
    n_grp, n_state = s5_lam_re.shape[2], s5_lam_re.shape[3]
    hc = ssmw // n_grp
    t = CHUNK_T
    assert poolw + ssmw == mixw and seq % t == 0 and cl % t == 0 and 2 * n_state == LANES and LANES % hc == 0

    mod = _ada_call(jnp.concatenate([c, c_ctx[None]], axis=0), ada_w[0], ada_b[0])
    shift, scale, gate = mod[:, :d], mod[:, d:2 * d], mod[:, 2 * d:]
    g1 = norm_g[0].reshape(1, d)
    in_w16 = in_w[0].astype(BF16)

    tm = min(512, seq)
    u_pool, u_ssm, zs = _inproj_call(x, scale[:bsz, None], shift[:bsz, None], g1, in_w16,
                                     (poolw, ssmw, mixw), (False, False, True), min(1024, seq), "inproj",
                                     groups=(len(POOL_WINDOWS), 1, 1))
    sc_c = jnp.broadcast_to(scale[bsz][None, None], (bsz, 1, d))
    sh_c = jnp.broadcast_to(shift[bsz][None, None], (bsz, 1, d))
    assert poolw % ssmw == 0
    (uc_ssm,) = _inproj_call(ctx, sc_c, sh_c, g1, in_w16, (ssmw,), (False,), min(256, cl), "inproj_ctx",
                             wcol=poolw // ssmw)

    pool_out, out_w16, glu_w16 = _pool_call(u_pool, pool_w[0], pool_scale[0], out_w[0], glu_w[0])

    ncl, ncc = seq // t, cl // t
    nsteps = max(1, (ncc + ncl - 1).bit_length())
    params = _s5_params(s5_lam_re[0], s5_lam_im[0], s5_log_dt[0], s5_b_re[0], s5_b_im[0], s5_c_re[0], s5_c_im[0],
                        t, nsteps)
    ut = _pack_call(u_ssm, uc_ssm, t)
    yt = _s5_call(ut, params, t, bsz, ncl, ncc, nsteps, n_state, hc)
    y_ssm = _unpack_call(yt, bsz, seq)

    return _merge_call(y_ssm, u_ssm, pool_out, zs, x, gate[:bsz, None], s5_d[0], glu_w16, glu_b[0], out_w16,
                       final_g, tm)
```

```python
import functools
import math

import jax
import jax.numpy as jnp
import numpy as np
from jax import lax
from jax.experimental import pallas as pl
from jax.experimental.pallas import tpu as pltpu

GRID_W = 64
POOL_WINDOWS = (2, 4, 8, 16)
EPS = 1e-6
CHUNK_T = 32
S5_GROUPS_PER_STEP = 4
LANES = 128
SUBLANES = 8
VMEM_LIMIT = 56 * 1024 * 1024

F32 = jnp.float32
BF16 = jnp.bfloat16
TN_DIMS = (((0,), (0,)), ((), ()))
NT_DIMS = (((1,), (1,)), ((), ()))


def _cparams(sem):
    return pltpu.CompilerParams(dimension_semantics=sem, vmem_limit_bytes=VMEM_LIMIT)


def _const_spec(shape):
    return pl.BlockSpec(shape, lambda *_: (0,) * len(shape), pipeline_mode=pl.Buffered(1))


def _sigmoid(v):
    return 1.0 / (1.0 + jnp.exp(-v))


def _gelu_tanh(v):
    return 0.5 * v * (1.0 + jnp.tanh(math.sqrt(2.0 / math.pi) * (v + 0.044715 * (v * v * v))))


def _modulated_norm(x, g, scale, shift):
    ms = jnp.mean(x * x, axis=-1, keepdims=True)
    return (x * lax.rsqrt(ms + EPS) * g) * (1.0 + scale) + shift


ADA_LANE_TILES = 6


def _ada_kernel(ct_ref, w_ref, b_ref, o_ref, sb_ref, acc_ref, *, nrows):
    tk, n = w_ref.shape
    step = pl.program_id(0)

    @pl.when(step == 0)
    def _():
        acc_ref[...] = jnp.zeros(acc_ref.shape, F32)

    cv = ct_ref[...]
    s = cv * _sigmoid(cv)
    for r in range(nrows):
        sb_ref[r] = jnp.broadcast_to(s[:, r:r + 1], (tk, LANES))

    group = ADA_LANE_TILES * LANES
    for g0 in range(0, n, group):
        def body(i, accs, g0=g0):
            k0 = pl.multiple_of(i * SUBLANES, SUBLANES)
            sb = [sb_ref[r, pl.ds(k0, SUBLANES), :] for r in range(nrows)]
            out = []
            for lt in range(ADA_LANE_TILES):
                w = w_ref[pl.ds(k0, SUBLANES), g0 + lt * LANES:g0 + (lt + 1) * LANES]
                out.append(tuple(accs[lt][r] + sb[r] * w for r in range(nrows)))
            return tuple(out)

        init = tuple(tuple(jnp.zeros((SUBLANES, LANES), F32) for _ in range(nrows)) for _ in range(ADA_LANE_TILES))
        accs = lax.fori_loop(0, tk // SUBLANES, body, init, unroll=8)
        for lt in range(ADA_LANE_TILES):
            lanes = slice(g0 + lt * LANES, g0 + (lt + 1) * LANES)
            for r in range(nrows):
                acc_ref[r, :, lanes] = acc_ref[r, :, lanes] + accs[lt][r]

    @pl.when(step == pl.num_programs(0) - 1)
    def _():
        o_ref[...] = jnp.zeros(o_ref.shape, F32)
        for r in range(nrows):
            o_ref[r:r + 1, :] = jnp.sum(acc_ref[r], axis=0, keepdims=True) + b_ref[...]


def _ada_call(cvecs, ada_w, ada_b):
    nrows, d = cvecs.shape
    n = ada_w.shape[1]
    tk = 256 if d % 256 == 0 else d
    assert n % (ADA_LANE_TILES * LANES) == 0
    ct = jnp.zeros((d, SUBLANES), F32).at[:, :nrows].set(cvecs.T)
    out = pl.pallas_call(
        functools.partial(_ada_kernel, nrows=nrows),
        out_shape=jax.ShapeDtypeStruct((SUBLANES, n), F32),
        grid=(d // tk,),
        in_specs=[
            pl.BlockSpec((tk, SUBLANES), lambda j: (j, 0)),
            pl.BlockSpec((tk, n), lambda j: (j, 0)),
            _const_spec((1, n)),
        ],
        out_specs=pl.BlockSpec((SUBLANES, n), lambda j: (0, 0)),
        scratch_shapes=[pltpu.VMEM((nrows, tk, LANES), F32), pltpu.VMEM((nrows, SUBLANES, n), F32)],
        compiler_params=_cparams(("arbitrary",)),
        name="ada",
    )(ct, ada_w, ada_b.reshape(1, n))
    return out[:nrows]


def _inproj_kernel(x_ref, sc_ref, sh_ref, g_ref, w_hbm, *refs, nchunk, silu, wcol0):
    o_refs, (w_ref, sem) = refs[:-2], refs[-2:]
    first = jnp.logical_and(pl.program_id(0) == 0, pl.program_id(1) == 0)
    nchunks = w_ref.shape[1] // nchunk

    def chunk_copy(k):
        return pltpu.make_async_copy(w_hbm.at[:, pl.ds(wcol0 + k * nchunk, nchunk)],
                                     w_ref.at[:, pl.ds(k * nchunk, nchunk)], sem.at[k])

    @pl.when(first)
    def _():
        for k in range(nchunks):
            chunk_copy(k).start()

    h = _modulated_norm(x_ref[0], g_ref[...], sc_ref[0], sh_ref[0]).astype(BF16)
    col = 0
    for o_ref, gate in zip(o_refs, silu):
        grouped = len(o_ref.shape) == 4
        cw = o_ref.shape[-1]
        width = cw * o_ref.shape[1] if grouped else cw
        for n0 in range(0, width, nchunk):
            @pl.when(first)
            def _(k=(col + n0) // nchunk):
                chunk_copy(k).wait()

            acc = jnp.dot(h, w_ref[:, col + n0:col + n0 + nchunk], preferred_element_type=F32)
            if gate:
                acc = acc * _sigmoid(acc)
            if grouped:
                for c0 in range(0, nchunk, cw):
                    o_ref[0, (n0 + c0) // cw] = acc[:, c0:c0 + cw].astype(o_ref.dtype)
            else:
                o_ref[0, :, n0:n0 + nchunk] = acc.astype(o_ref.dtype)
        col += width


def _inproj_call(x, scale, shift, g, w, widths, silu, tm, name, wcol=0, groups=None):
    b, l, d = x.shape
    nout = sum(widths)
    assert w.shape[1] % nout == 0 and l % tm == 0
    nchunk = min(1024, min(widths))
    assert all(wd % nchunk == 0 for wd in widths)
    groups = groups or (1,) * len(widths)
    shape = lambda wd, ng, rows: (b, rows, wd) if ng == 1 else (b, ng, rows, wd // ng)
    spec = lambda wd, ng: (pl.BlockSpec((1, tm, wd), lambda i, j: (i, j, 0)) if ng == 1 else
                           pl.BlockSpec((1, ng, tm, wd // ng), lambda i, j: (i, 0, j, 0)))
    return pl.pallas_call(
        functools.partial(_inproj_kernel, nchunk=nchunk, silu=silu, wcol0=wcol * nout),
        out_shape=[jax.ShapeDtypeStruct(shape(wd, ng, l), BF16) for wd, ng in zip(widths, groups)],
        grid=(b, l // tm),
        in_specs=[
            pl.BlockSpec((1, tm, d), lambda i, j: (i, j, 0)),
            pl.BlockSpec((1, 1, d), lambda i, j: (i, 0, 0)),
            pl.BlockSpec((1, 1, d), lambda i, j: (i, 0, 0)),
            _const_spec((1, d)),
            pl.BlockSpec(memory_space=pl.ANY),
        ],
        out_specs=[spec(wd, ng) for wd, ng in zip(widths, groups)],
        scratch_shapes=[pltpu.VMEM((d, nout), BF16), pltpu.SemaphoreType.DMA((nout // nchunk,))],
        compiler_params=_cparams(("arbitrary", "arbitrary")),
        name=name,
    )(x, scale, shift, g, w)


def _chunk_perm(t):
    i = np.arange(SUBLANES * t)
    src = (i % SUBLANES) * t + i // SUBLANES
    return jnp.asarray(src[:, None] == i[None, :], BF16)


def _pack_kernel(u_ref, uc_ref, perm_ref, ut_ref, scr_ref, *, t, nblk):
    j = pl.program_id(0)
    s = scr_ref.shape[2]
    rows = SUBLANES * t
    ctok = uc_ref.shape[0]

    def permute(src, blk):
        pm = jnp.dot(perm_ref[...], src, preferred_element_type=F32)
        for ti in range(t):
            scr_ref[ti, pl.ds(pl.multiple_of(blk * SUBLANES, SUBLANES), SUBLANES), :] = pm[ti * SUBLANES:(ti + 1) * SUBLANES]

    @pl.when(j < nblk)
    def _():
        def body(blk, carry):
            permute(u_ref[pl.ds(pl.multiple_of(blk * rows, rows), rows), :], blk)
            return carry

        lax.fori_loop(0, LANES // SUBLANES, body, 0, unroll=2)

    @pl.when(j == nblk)
    def _():
        for blk in range(ctok // rows):
            permute(uc_ref[blk * rows:(blk + 1) * rows, :], blk)
        scr_ref[:, ctok // t:LANES, :] = jnp.zeros((t, LANES - ctok // t, s), F32)

    def transpose(ti, carry):
        for k in range(s // LANES):
            ut_ref[0, ti, LANES * k:LANES * (k + 1), :] = scr_ref[ti, :, LANES * k:LANES * (k + 1)].T.astype(BF16)
        return carry

    lax.fori_loop(0, t, transpose, 0, unroll=2)


def _pack_call(u_ssm, uc_ssm, t):
    b, l, s = u_ssm.shape
    tok = LANES * t
    rows = SUBLANES * t
    ntok = b * l
    ctok = b * uc_ssm.shape[1]
    assert ntok % tok == 0 and ctok <= tok and ctok % rows == 0 and s % LANES == 0
    nblk = ntok // tok
    return pl.pallas_call(
        functools.partial(_pack_kernel, t=t, nblk=nblk),
        out_shape=jax.ShapeDtypeStruct((nblk + 1, t, s, LANES), BF16),
        grid=(nblk + 1,),
        in_specs=[pl.BlockSpec((tok, s), lambda j: (jnp.minimum(j, nblk - 1), 0)),
                  _const_spec((ctok, s)), _const_spec((rows, rows))],
        out_specs=pl.BlockSpec((1, t, s, LANES), lambda j: (j, 0, 0, 0)),
        scratch_shapes=[pltpu.VMEM((t, LANES, s), F32)],
        compiler_params=_cparams(("arbitrary",)),
        name="pack",
    )(u_ssm.reshape(ntok, s), uc_ssm.reshape(ctok, s), _chunk_perm(t))


def _unpack_kernel(yt_ref, perm_ref, y_ref, scr_ref, *, t):
    s = scr_ref.shape[2]
    rows = SUBLANES * t

    def transpose(ti, carry):
        for k in range(s // LANES):
            scr_ref[ti, :, LANES * k:LANES * (k + 1)] = yt_ref[0, ti, LANES * k:LANES * (k + 1), :].astype(F32).T
        return carry

    lax.fori_loop(0, t, transpose, 0, unroll=2)

    def body(blk, carry):
        c0 = pl.multiple_of(blk * SUBLANES, SUBLANES)
        src = jnp.concatenate([scr_ref[ti, pl.ds(c0, SUBLANES), :] for ti in range(t)], axis=0).astype(BF16)
        out = jnp.dot(perm_ref[...], src, preferred_element_type=F32)
        y_ref[pl.ds(pl.multiple_of(blk * rows, rows), rows), :] = out.astype(y_ref.dtype)
        return carry

    lax.fori_loop(0, LANES // SUBLANES, body, 0, unroll=2)


def _unpack_call(yt, b, l):
    ngrp, t, s, _ = yt.shape
    nl = ngrp * LANES
    tok = LANES * t
    rows = SUBLANES * t
    y = pl.pallas_call(
        functools.partial(_unpack_kernel, t=t),
        out_shape=jax.ShapeDtypeStruct((nl * t, s), BF16),
        grid=(ngrp,),
        in_specs=[pl.BlockSpec((1, t, s, LANES), lambda j: (j, 0, 0, 0)), _const_spec((rows, rows))],
        out_specs=pl.BlockSpec((tok, s), lambda j: (j, 0)),
        scratch_shapes=[pltpu.VMEM((t, LANES, s), F32)],
        compiler_params=_cparams(("arbitrary",)),
        name="unpack",
    )(yt, _chunk_perm(t).T)
    return y.reshape(b, l, s)


POOL_BLOCK_ROWS = 8
POOL_BLOCKS_PER_STEP = 8


def _pool_kernel(u_ref, *refs, tb, pad):
    nwin = len(POOL_WINDOWS)
    band_refs = refs[:nwin]
    inv_ref, pw_ref, ps_ref, wa_ref, wb_ref, o_ref, wa16_ref, wb16_ref, pad_ref = refs[nwin:]
    wa16_ref[...] = wa_ref[...].astype(wa16_ref.dtype)
    wb16_ref[...] = wb_ref[...].astype(wb16_ref.dtype)
    grp = pl.program_id(1)
    l = u_ref.shape[2]
    c = u_ref.shape[3]
    nblk = l // tb
    pad_ref[0:pad, :] = jnp.zeros((pad, c), pad_ref.dtype)
    pad_ref[pad + l:pad + l + pad, :] = jnp.zeros((pad, c), pad_ref.dtype)

    def copy(i, carry):
        r0 = pl.multiple_of(i * tb, tb)
        pad_ref[pl.ds(pl.multiple_of(pad + r0, GRID_W), tb), :] = u_ref[0, 0, pl.ds(r0, tb), :]
        return carry

    lax.fori_loop(0, nblk, copy, 0)

    for gi, w in enumerate(POOL_WINDOWS):
        half = w // 2
        span = band_refs[gi].shape[1]

        @pl.when(grp == gi)
        def _(half=half, span=span, band_ref=band_refs[gi]):
            def blk(i, carry):
                first = POOL_BLOCKS_PER_STEP * i
                r0 = pl.multiple_of(first * tb, POOL_BLOCKS_PER_STEP * tb)
                wins = [pad_ref[pl.ds(pl.multiple_of(pad + r0 + k * tb - half * GRID_W, GRID_W), span), :]
                        for k in range(POOL_BLOCKS_PER_STEP)]
                box = jnp.dot(band_ref[...], jnp.concatenate(wins, axis=1), preferred_element_type=F32)
                dlts = []
                for k in range(POOL_BLOCKS_PER_STEP):
                    j = first + k
                    variant = jnp.where(j == 0, 0, jnp.where(j == nblk - 1, 2, 1))
                    inv = jnp.concatenate([inv_ref[0, variant]] * (c // LANES), axis=1)
                    u_blk = u_ref[0, 0, pl.ds(r0 + k * tb, tb), :].astype(F32)
                    dlts.append((box[:, k * c:(k + 1) * c] * inv - u_blk).astype(BF16))
                o = jnp.dot(jnp.concatenate(dlts, axis=0), pw_ref[0], preferred_element_type=F32) * ps_ref[...]
                o_ref[0, 0, pl.ds(r0, POOL_BLOCKS_PER_STEP * tb), :] = o.astype(o_ref.dtype)
                return carry

            lax.fori_loop(0, nblk // POOL_BLOCKS_PER_STEP, blk, 0)


def _pool_tables(rows, tb):
    br = tb // GRID_W
    i = np.arange(tb)
    ri, ci = i // GRID_W, i % GRID_W
    bands, invs = [], []
    for w in POOL_WINDOWS:
        half = w // 2
        j = np.arange((br + w) * GRID_W)
        dr = (j // GRID_W)[None, :] - ri[:, None]
        dc = (j % GRID_W)[None, :] - ci[:, None]
        bands.append(jnp.asarray((dr >= 0) & (dr < w) & (dc >= -half) & (dc < w - half), BF16))
        cnt_c = np.minimum(ci + w - half, GRID_W) - np.maximum(ci - half, 0)
        per_variant = []
        for r_first in (0, br, rows - br):
            r = r_first + ri
            cnt_r = np.minimum(r + w - half, rows) - np.maximum(r - half, 0)
            per_variant.append(np.float32(1.0) / (cnt_r * cnt_c).astype(np.float32))
        invs.append(np.stack(per_variant))
    inv = np.broadcast_to(np.stack(invs)[..., None], (len(POOL_WINDOWS), 3, tb, LANES))
    return bands, jnp.asarray(inv, F32)


def _pool_call(u_pool, pool_w, pool_scale, wa, wb):
    b, ng, l, c = u_pool.shape
    nstep = b * ng
    assert wa.shape[0] % (nstep * 16) == 0 and wb.shape[0] % (nstep * 16) == 0
    rows_of = lambda w: pl.BlockSpec((w.shape[0] // nstep, w.shape[1]), lambda i, j: (i * ng + j, 0))
    assert ng == len(POOL_WINDOWS)
    p = ng * c
    rows = l // GRID_W
    tb = POOL_BLOCK_ROWS * GRID_W
    pad = (max(POOL_WINDOWS) // 2) * GRID_W
    assert l % (tb * POOL_BLOCKS_PER_STEP) == 0 and c % LANES == 0
    assert rows >= 3 * POOL_BLOCK_ROWS and max(POOL_WINDOWS) <= 2 * POOL_BLOCK_ROWS
    bands, inv = _pool_tables(rows, tb)
    return pl.pallas_call(
        functools.partial(_pool_kernel, tb=tb, pad=pad),
        out_shape=[jax.ShapeDtypeStruct((b, ng, l, c), BF16), jax.ShapeDtypeStruct(wa.shape, BF16),
                   jax.ShapeDtypeStruct(wb.shape, BF16)],
        grid=(b, ng),
        in_specs=[pl.BlockSpec((1, 1, l, c), lambda i, j: (i, j, 0, 0))]
        + [_const_spec(bd.shape) for bd in bands]
        + [
            pl.BlockSpec((1, 3, tb, LANES), lambda i, j: (j, 0, 0, 0)),
            pl.BlockSpec((1, c, c), lambda i, j: (j, 0, 0)),
            pl.BlockSpec((1, c), lambda i, j: (0, j)),
            rows_of(wa), rows_of(wb),
        ],
        out_specs=[pl.BlockSpec((1, 1, l, c), lambda i, j: (i, j, 0, 0)), rows_of(wa), rows_of(wb)],
        scratch_shapes=[pltpu.VMEM((pad + l + pad, c), BF16)],
        compiler_params=_cparams(("arbitrary", "arbitrary")),
        name="pool",
    )(u_pool, *bands, inv, pool_w.astype(BF16), pool_scale.reshape(1, p), wa, wb)


def _dot_hi_nt(a, b):
    a_hi = a.astype(BF16)
    a_lo = (a - a_hi.astype(F32)).astype(BF16)
    b_hi = b.astype(BF16)
    b_lo = (b - b_hi.astype(F32)).astype(BF16)
    dot = lambda p, q: lax.dot_general(p, q, NT_DIMS, preferred_element_type=F32)
    return dot(a_hi, b_hi) + dot(a_hi, b_lo) + dot(a_lo, b_hi)


def _s5_kernel(u_ref, ea_ref, eb_ref, ba_ref, bm_ref, cc_ref, ca_ref, cb_ref, oa_ref, ob_ref, ap_ref, a8_ref, y_ref,
               et_ref, win_ref, mt_ref, wout_ref, buf_ref, cbuf_ref, hin_ref, *, gps, hc, **static):
    @pl.when(pl.program_id(0) == 0)
    def _():
        buf_ref[...] = jnp.zeros(buf_ref.shape, F32)
        cbuf_ref[...] = jnp.zeros(cbuf_ref.shape, F32)

    for gg in range(gps):
        rows = slice(gg * hc, (gg + 1) * hc)
        _s5_group(u_ref.at[:, :, rows, :], ea_ref.at[gg], eb_ref.at[gg], ba_ref.at[gg], bm_ref.at[gg], cc_ref.at[gg],
                  ca_ref.at[gg], cb_ref.at[gg], oa_ref.at[gg], ob_ref.at[gg], ap_ref.at[gg], a8_ref.at[gg],
                  y_ref.at[:, :, rows, :], et_ref.at[gg], win_ref.at[gg], mt_ref.at[gg], wout_ref.at[gg], buf_ref.at[gg],
                  cbuf_ref.at[gg], hin_ref.at[gg],
                  hc=hc, **static)


def _s5_group(u_ref, ea_ref, eb_ref, ba_ref, bm_ref, cc_ref, ca_ref, cb_ref, oa_ref, ob_ref, ap_ref, a8_ref, y_ref,
              et_ref, win_ref, mt_ref, wout_ref, buf_ref, cbuf_ref, hin_ref, *, t, hc, nb, ncl, ncc, pad, nsteps, nstate):
    n = nstate
    th = t * hc
    nl = nb * ncl
    rb = ncc + ncl
    ntile = th // LANES

    ba = ba_ref[...]
    bm = bm_ref[...]
    ca = ca_ref[...]
    cb = cb_ref[...]
    for ti in range(t):
        rows = slice(ti * hc, (ti + 1) * hc)
        e_rows = ba * ea_ref[ti:ti + 1, :] + bm * eb_ref[ti:ti + 1, :]
        et_ref[rows, :] = e_rows
        win_ref[rows, :] = e_rows.astype(BF16)
        wout_ref[rows, :] = (ca * oa_ref[ti:ti + 1, :] + cb * ob_ref[ti:ti + 1, :]).astype(BF16)
    cc = cc_ref[...]
    q_f = _dot_hi_nt(cc[0:hc], et_ref[:, 0:2 * n])
    q_b = _dot_hi_nt(cc[hc:2 * hc], et_ref[:, 2 * n:4 * n])
    zeros = jnp.zeros((hc, th), F32)
    line_b = jnp.concatenate([zeros, q_b], axis=1)
    line_f = jnp.concatenate([q_f, zeros], axis=1)
    per_tile = LANES // hc
    rolled_b = [line_b if r == 0 else pltpu.roll(line_b, hc * r, axis=1) for r in range(per_tile)]
    rolled_f = [line_f if r == per_tile - 1 else pltpu.roll(line_f, 2 * th - hc * (per_tile - 1 - r), axis=1)
                for r in range(per_tile)]
    for ti in range(t):
        a, r = divmod(ti, per_tile)
        row_b = rolled_b[r][:, th - LANES * a:2 * th - LANES * a]
        off_f = LANES * (ntile - 1 - a)
        row_f = rolled_f[r][:, off_f:off_f + th]
        mt_ref[ti * hc:(ti + 1) * hc, :] = (row_b + row_f).astype(BF16)

    u = jnp.concatenate([jnp.concatenate([u_ref[cg, ti] for cg in range(u_ref.shape[0])], axis=1)
                         for ti in range(t)], axis=0)
    st_all = lax.dot_general(win_ref[...], u, TN_DIMS, preferred_element_type=F32)
    first_half = lax.broadcasted_iota(jnp.int32, (LANES, LANES), 1) < n
    for q in range(4):
        rows_q = st_all[q * n:(q + 1) * n]
        lat = jnp.concatenate([rows_q[:, b * ncl:(b + 1) * ncl] for b in range(nb)], axis=0).T
        cblk = jnp.concatenate([rows_q[:, nl:nl + LANES]] * nb, axis=0).T
        ctx = jnp.where(first_half, cblk, jnp.concatenate([cblk[ncc:], cblk[:ncc]], axis=0))[0:ncc]
        if q < 2:
            buf_ref[q, pad:pad + ncc, :] = ctx
            buf_ref[q, pad + ncc:pad + rb, :] = lat
        else:
            buf_ref[q, pad:pad + ncl, :] = lat
            buf_ref[q, pad + ncl:pad + rb, :] = ctx
    def step(ref, d, k, sgn, lo, rows, keep=None):
        ar = ap_ref[4 * k + 2 * d:4 * k + 2 * d + 1, :]
        ai = ap_ref[4 * k + 2 * d + 1:4 * k + 2 * d + 2, :]
        pr = ref[2 * d, lo + sgn:lo + sgn + rows, :]
        pi = ref[2 * d + 1, lo + sgn:lo + sgn + rows, :]
        if keep is not None:
            pr = jnp.where(keep, pr, 0.0)
            pi = jnp.where(keep, pi, 0.0)
        xr = ref[2 * d, lo:lo + rows, :]
        xi = ref[2 * d + 1, lo:lo + rows, :]
        ref[2 * d, lo:lo + rows, :] = xr + (ar * pr - ai * pi)
        ref[2 * d + 1, lo:lo + rows, :] = xi + (ar * pi + ai * pr)

    rbp = buf_ref.shape[1] - 2 * pad
    nv = rbp // SUBLANES
    cpad = (cbuf_ref.shape[1] - nv) // 2
    in_block = lax.broadcasted_iota(jnp.int32, (rbp, LANES), 0) % SUBLANES
    nlocal = SUBLANES.bit_length() - 1
    nchunk = 3 if nv % 3 == 0 else 1
    crow = rbp // nchunk
    blk = in_block[0:crow]
    for d in range(2):
        for ch in range(nchunk):
            rows = slice(pad + ch * crow, pad + (ch + 1) * crow)
            xr = buf_ref[2 * d, rows, :]
            xi = buf_ref[2 * d + 1, rows, :]
            for k in range(nlocal):
                st = 1 << k
                ar = ap_ref[4 * k + 2 * d:4 * k + 2 * d + 1, :]
                ai = ap_ref[4 * k + 2 * d + 1:4 * k + 2 * d + 2, :]
                keep = blk >= st if d == 0 else blk < SUBLANES - st
                shift = st if d == 0 else SUBLANES - st
                rot = lambda v: pltpu.roll(v.reshape(crow // SUBLANES, SUBLANES, LANES), shift, axis=1).reshape(crow, LANES)
                pr = jnp.where(keep, rot(xr), 0.0)
                pi = jnp.where(keep, rot(xi), 0.0)
                xr, xi = xr + (ar * pr - ai * pi), xi + (ar * pi + ai * pr)
            buf_ref[2 * d, rows, :] = xr
            buf_ref[2 * d + 1, rows, :] = xi
    for d in range(2):
        edge = SUBLANES - 1 if d == 0 else 0
        for q in (2 * d, 2 * d + 1):
            cbuf_ref[q, cpad:cpad + nv, :] = buf_ref[q, pl.ds(pad + edge, nv, stride=SUBLANES), :]
    for m in range(max(0, (nv - 1).bit_length())):
        st = 1 << m
        step(cbuf_ref, 0, nlocal + m, -st, cpad, nv)
        step(cbuf_ref, 1, nlocal + m, st, cpad, nv)
    for d in range(2):
        a8r = a8_ref[2 * SUBLANES * d:2 * SUBLANES * d + SUBLANES, :]
        a8i = a8_ref[2 * SUBLANES * d + SUBLANES:2 * SUBLANES * (d + 1), :]
        for j in range(nv):
            src = j - 1 if d == 0 else j + 1
            if 0 <= src < nv:
                cr = jnp.broadcast_to(cbuf_ref[2 * d, cpad + src:cpad + src + 1, :], (SUBLANES, LANES))
                ci = jnp.broadcast_to(cbuf_ref[2 * d + 1, cpad + src:cpad + src + 1, :], (SUBLANES, LANES))
                rows = slice(pad + j * SUBLANES, pad + (j + 1) * SUBLANES)
                buf_ref[2 * d, rows, :] = buf_ref[2 * d, rows, :] + (a8r * cr - a8i * ci)
                buf_ref[2 * d + 1, rows, :] = buf_ref[2 * d + 1, rows, :] + (a8r * ci + a8i * cr)
    for q in range(4):
        lo = pad + ncc - 1 if q < 2 else pad + 1
        ent = buf_ref[q, lo:lo + ncl, :].T.astype(BF16)
        for b in range(nb):
            hin_ref[q * n:(q + 1) * n, b * ncl:(b + 1) * ncl] = ent[b * n:(b + 1) * n]

    y = jnp.dot(mt_ref[...], u[:, 0:nl], preferred_element_type=F32)
    y = y + jnp.dot(wout_ref[...], hin_ref[...], preferred_element_type=F32)
    for ti in range(t):
        for cg in range(y_ref.shape[0]):
            y_ref[cg, ti] = y[ti * hc:(ti + 1) * hc, cg * LANES:(cg + 1) * LANES].astype(y_ref.dtype)


def _s5_params(lam_re, lam_im, log_dt, b_re, b_im, c_re, c_im, t, nsteps):
    _, g, n = lam_re.shape
    h = b_re.shape[-1]
    lam_re, lam_im = lam_re.astype(F32), lam_im.astype(F32)
    dt = jnp.exp(log_dt.astype(F32))[..., None]
    lr, li = lam_re * dt, lam_im * dt

    def power(d, k):
        kk = k[None, :, None]
        mag = jnp.exp(lr[d][:, None, :] * kk)
        return mag * jnp.cos(li[d][:, None, :] * kk), mag * jnp.sin(li[d][:, None, :] * kk)

    a_re, a_im = jnp.exp(lr) * jnp.cos(li), jnp.exp(lr) * jnp.sin(li)
    den = lam_re * lam_re + lam_im * lam_im
    f_re = ((a_re - 1.0) * lam_re + a_im * lam_im) / den
    f_im = (a_im * lam_re - (a_re - 1.0) * lam_im) / den
    bt_re = (f_re[..., None] * b_re - f_im[..., None] * b_im).transpose(0, 1, 3, 2)
    bt_im = (f_re[..., None] * b_im + f_im[..., None] * b_re).transpose(0, 1, 3, 2)
    lanes4 = lambda f0, f1, b0, b1: jnp.concatenate([f0, f1, b0, b1], axis=-1)

    strides = t * (2.0 ** jnp.arange(nsteps, dtype=F32))
    exps = jnp.concatenate([jnp.arange(t + 1, dtype=F32), strides, t * jnp.arange(1, SUBLANES + 1, dtype=F32)])
    (allf_re, allf_im), (allb_re, allb_im) = power(0, exps), power(1, exps)
    pf_re, pf_im = allf_re[:, t - 1::-1], allf_im[:, t - 1::-1]
    pb_re, pb_im = allb_re[:, 0:t], allb_im[:, 0:t]
    ea = lanes4(pf_re, pf_re, pb_re, pb_re)
    eb = lanes4(pf_im, pf_im, pb_im, pb_im)
    ba = lanes4(bt_re[0], bt_im[0], bt_re[1], bt_im[1])
    bm = lanes4(-bt_im[0], bt_re[0], -bt_im[1], bt_re[1])
    cc = jnp.concatenate([jnp.concatenate([c_re[0], -c_im[0]], axis=-1),
                          jnp.concatenate([c_re[1], -c_im[1]], axis=-1)], axis=1)
    rf_re, rf_im = allf_re[:, 1:t + 1], allf_im[:, 1:t + 1]
    rb_re, rb_im = allb_re[:, t:0:-1], allb_im[:, t:0:-1]
    oa = lanes4(rf_re, rf_im, rb_re, rb_im)
    ob = lanes4(rf_im, rf_re, rb_im, rb_re)
    ca = lanes4(c_re[0], -c_re[0], c_re[1], -c_re[1])
    cb = lanes4(-c_im[0], -c_im[0], -c_im[1], -c_im[1])
    st0, st1 = t + 1, t + 1 + nsteps
    apf_re, apf_im, apb_re, apb_im = (v[:, st0:st1] for v in (allf_re, allf_im, allb_re, allb_im))
    ap = jnp.stack([apf_re, apf_im, apb_re, apb_im], axis=2).reshape(g, 4 * nsteps, n)
    ap = jnp.concatenate([ap] * (LANES // n), axis=-1)
    rows = -(-4 * nsteps // SUBLANES) * SUBLANES
    ap = jnp.pad(ap, ((0, 0), (0, rows - 4 * nsteps), (0, 0)))
    f8_re, f8_im = allf_re[:, st1:], allf_im[:, st1:]
    b8_re, b8_im = allb_re[:, :st1 - 1:-1], allb_im[:, :st1 - 1:-1]
    a8 = jnp.concatenate([f8_re, f8_im, b8_re, b8_im], axis=1)
    a8 = jnp.concatenate([a8] * (LANES // n), axis=-1)
    return tuple(v.astype(F32) for v in (ea, eb, ba, bm, cc, ca, cb, oa, ob, ap, a8))


def _s5_call(ut, params, t, nb, ncl, ncc, nsteps, n_state, hc):
    ap = params[-2]
    ngrp_in, _, sw, _ = ut.shape
    nl = nb * ncl
    g = sw // hc
    th = t * hc
    rbp = -(-(ncc + ncl) // SUBLANES) * SUBLANES
    nv = rbp // SUBLANES
    pad = SUBLANES
    cpad = max(SUBLANES, 1 << max(0, (nv - 1).bit_length() - 1))
    assert nb * n_state == LANES and nb == 2 and 4 * nsteps <= ap.shape[1]
    gps = S5_GROUPS_PER_STEP
    assert g % gps == 0
    grp = lambda shp: pl.BlockSpec((gps,) + shp, lambda i: (i, 0, 0))
    return pl.pallas_call(
        functools.partial(_s5_kernel, gps=gps, t=t, hc=hc, nb=nb, ncl=ncl, ncc=ncc, pad=pad, nsteps=nsteps,
                          nstate=n_state),
        out_shape=jax.ShapeDtypeStruct((nl // LANES, t, sw, LANES), BF16),
        grid=(g // gps,),
        in_specs=[
            pl.BlockSpec((ngrp_in, t, gps * hc, LANES), lambda i: (0, 0, i, 0)),
            grp((t, 4 * n_state)), grp((t, 4 * n_state)), grp((hc, 4 * n_state)), grp((hc, 4 * n_state)),
            grp((2 * hc, 2 * n_state)), grp((hc, 4 * n_state)), grp((hc, 4 * n_state)),
            grp((t, 4 * n_state)), grp((t, 4 * n_state)), grp((ap.shape[1], LANES)), grp((4 * SUBLANES, LANES)),
        ],
        out_specs=pl.BlockSpec((nl // LANES, t, gps * hc, LANES), lambda i: (0, 0, i, 0)),
        scratch_shapes=[pltpu.VMEM((gps, th, 4 * n_state), F32), pltpu.VMEM((gps, th, 4 * n_state), BF16),
                        pltpu.VMEM((gps, th, th), BF16), pltpu.VMEM((gps, th, 4 * n_state), BF16),
                        pltpu.VMEM((gps, 4, pad + rbp + pad, LANES), F32),
                        pltpu.VMEM((gps, 4, cpad + nv + cpad, LANES), F32),
                        pltpu.VMEM((gps, 4 * n_state, nl), BF16)],
        compiler_params=_cparams(("arbitrary",)),
        name="s5",
    )(ut, *params)


def _merge_kernel(y_ref, u_ref, p_ref, zs_ref, x_ref, gate_ref, d_ref, gw_ref, gb_ref, ow_ref, fg_ref, o_ref):
    sw = y_ref.shape[-1]
    pw = p_ref.shape[1] * p_ref.shape[-1]
    y = _gelu_tanh(y_ref[0].astype(F32) + d_ref[...] * u_ref[0].astype(F32)).astype(BF16)
    yy = jnp.dot(y, gw_ref[...], preferred_element_type=F32) + gb_ref[...]
    ssm_out = yy[:, :sw] * _sigmoid(yy[:, sw:])
    pool = jnp.concatenate([p_ref[0, g] for g in range(p_ref.shape[1])], axis=1)
    br_pool = pool * zs_ref[0, :, 0:pw]
    br_ssm = (ssm_out * zs_ref[0, :, pw:pw + sw].astype(F32)).astype(BF16)
    mix = jnp.dot(jnp.concatenate([br_pool, br_ssm], axis=1), ow_ref[...], preferred_element_type=F32)
    xo = x_ref[0] + gate_ref[0] * mix
    ms = jnp.mean(xo * xo, axis=-1, keepdims=True)
    o_ref[0] = (xo * lax.rsqrt(ms + EPS) * fg_ref[...]).astype(o_ref.dtype)


def _merge_call(y_ssm, u_ssm, pool_out, zs, x, gate, d_skip, glu_w, glu_b, out_w, final_g, tm):
    b, l, d = x.shape
    sw = y_ssm.shape[-1]
    ng, pc = pool_out.shape[1], pool_out.shape[-1]
    mixw = zs.shape[-1]
    tok = lambda wd: pl.BlockSpec((1, tm, wd), lambda i, j: (i, j, 0))
    return pl.pallas_call(
        _merge_kernel,
        out_shape=jax.ShapeDtypeStruct((b, l, d), x.dtype),
        grid=(b, l // tm),
        in_specs=[
            tok(sw), tok(sw), pl.BlockSpec((1, ng, tm, pc), lambda i, j: (i, 0, j, 0)), tok(mixw), tok(d),
            pl.BlockSpec((1, 1, d), lambda i, j: (i, 0, 0)),
            _const_spec((1, sw)), _const_spec((sw, 2 * sw)), _const_spec((1, 2 * sw)), _const_spec((mixw, d)),
            _const_spec((1, d)),
        ],
        out_specs=tok(d),
        compiler_params=_cparams(("arbitrary", "arbitrary")),
        name="merge",
    )(y_ssm, u_ssm, pool_out, zs, x, gate, d_skip.reshape(1, sw), glu_w, glu_b.reshape(1, 2 * sw), out_w,
      final_g.reshape(1, d))


def kernel(x, c, ctx, c_ctx, ada_w, ada_b, norm_g, in_w, pool_w, pool_scale, s5_lam_re, s5_lam_im, s5_log_dt,
           s5_b_re, s5_b_im, s5_c_re, s5_c_im, s5_d, glu_w, glu_b, out_w, final_g):
    assert ada_w.shape[0] == 1, "single-layer block"
    bsz, seq, d = x.shape
    cl = ctx.shape[1]
    mixw = in_w.shape[2] // 2
    poolw = pool_scale.shape[-1]
    ssmw = s5_d.shape[-1]
    n_grp, n_state = s5_lam_re.shape[2], s5_lam_re.shape[3]
    hc = ssmw // n_grp
    t = CHUNK_T
    assert poolw + ssmw == mixw and seq % t == 0 and cl % t == 0 and 2 * n_state == LANES and LANES % hc == 0

    mod = _ada_call(jnp.concatenate([c, c_ctx[None]], axis=0), ada_w[0], ada_b[0])
    shift, scale, gate = mod[:, :d], mod[:, d:2 * d], mod[:, 2 * d:]
    g1 = norm_g[0].reshape(1, d)
    in_w16 = in_w[0].astype(BF16)

    tm = min(512, seq)
    u_pool, u_ssm, zs = _inproj_call(x, scale[:bsz, None], shift[:bsz, None], g1, in_w16,
                                     (poolw, ssmw, mixw), (False, False, True), min(1024, seq), "inproj",
                                     groups=(len(POOL_WINDOWS), 1, 1))
    sc_c = jnp.broadcast_to(scale[bsz][None, None], (bsz, 1, d))
    sh_c = jnp.broadcast_to(shift[bsz][None, None], (bsz, 1, d))
    assert poolw % ssmw == 0
    (uc_ssm,) = _inproj_call(ctx, sc_c, sh_c, g1, in_w16, (ssmw,), (False,), min(256, cl), "inproj_ctx",
                             wcol=poolw // ssmw)

    pool_out, out_w16, glu_w16 = _pool_call(u_pool, pool_w[0], pool_scale[0], out_w[0], glu_w[0])

    ncl, ncc = seq // t, cl // t
    nsteps = max(1, (ncc + ncl - 1).bit_length())
    params = _s5_params(s5_lam_re[0], s5_lam_im[0], s5_log_dt[0], s5_b_re[0], s5_b_im[0], s5_c_re[0], s5_c_im[0],
                        t, nsteps)
    ut = _pack_call(u_ssm, uc_ssm, t)
    yt = _s5_call(ut, params, t, bsz, ncl, ncc, nsteps, n_state, hc)
    y_ssm = _unpack_call(yt, bsz, seq)

    return _merge_call(y_ssm, u_ssm, pool_out, zs, x, gate[:bsz, None], s5_d[0], glu_w16, glu_b[0], out_w16,
                       final_g, tm)
```

```python
import functools
import math

import jax
import jax.numpy as jnp
import numpy as np
from jax import lax
from jax.experimental import pallas as pl
from jax.experimental.pallas import tpu as pltpu

GRID_W = 64
POOL_WINDOWS = (2, 4, 8, 16)
EPS = 1e-6
CHUNK_T = 32
S5_GROUPS_PER_STEP = 4
LANES = 128
SUBLANES = 8
VMEM_LIMIT = 56 * 1024 * 1024

F32 = jnp.float32
BF16 = jnp.bfloat16
TN_DIMS = (((0,), (0,)), ((), ()))
NT_DIMS = (((1,), (1,)), ((), ()))


def _cparams(sem):
    return pltpu.CompilerParams(dimension_semantics=sem, vmem_limit_bytes=VMEM_LIMIT)


def _const_spec(shape):
    return pl.BlockSpec(shape, lambda *_: (0,) * len(shape), pipeline_mode=pl.Buffered(1))


def _sigmoid(v):
    return 1.0 / (1.0 + jnp.exp(-v))


def _gelu_tanh(v):
    return 0.5 * v * (1.0 + jnp.tanh(math.sqrt(2.0 / math.pi) * (v + 0.044715 * (v * v * v))))


def _modulated_norm(x, g, scale, shift):
    ms = jnp.mean(x * x, axis=-1, keepdims=True)
    return (x * lax.rsqrt(ms + EPS) * g) * (1.0 + scale) + shift


ADA_LANE_TILES = 6


def _ada_kernel(ct_ref, w_ref, b_ref, o_ref, sb_ref, acc_ref, *, nrows):
    tk, n = w_ref.shape
    step = pl.program_id(0)

    @pl.when(step == 0)
    def _():
        acc_ref[...] = jnp.zeros(acc_ref.shape, F32)

    cv = ct_ref[...]
    s = cv * _sigmoid(cv)
    for r in range(nrows):
        sb_ref[r] = jnp.broadcast_to(s[:, r:r + 1], (tk, LANES))

    group = ADA_LANE_TILES * LANES
    for g0 in range(0, n, group):
        def body(i, accs, g0=g0):
            k0 = pl.multiple_of(i * SUBLANES, SUBLANES)
            sb = [sb_ref[r, pl.ds(k0, SUBLANES), :] for r in range(nrows)]
            out = []
            for lt in range(ADA_LANE_TILES):
                w = w_ref[pl.ds(k0, SUBLANES), g0 + lt * LANES:g0 + (lt + 1) * LANES]
                out.append(tuple(accs[lt][r] + sb[r] * w for r in range(nrows)))
            return tuple(out)

        init = tuple(tuple(jnp.zeros((SUBLANES, LANES), F32) for _ in range(nrows)) for _ in range(ADA_LANE_TILES))
        accs = lax.fori_loop(0, tk // SUBLANES, body, init, unroll=8)
        for lt in range(ADA_LANE_TILES):
            lanes = slice(g0 + lt * LANES, g0 + (lt + 1) * LANES)
            for r in range(nrows):
                acc_ref[r, :, lanes] = acc_ref[r, :, lanes] + accs[lt][r]

    @pl.when(step == pl.num_programs(0) - 1)
    def _():
        o_ref[...] = jnp.zeros(o_ref.shape, F32)
        for r in range(nrows):
            o_ref[r:r + 1, :] = jnp.sum(acc_ref[r], axis=0, keepdims=True) + b_ref[...]


def _ada_call(cvecs, ada_w, ada_b):
    nrows, d = cvecs.shape
    n = ada_w.shape[1]
    tk = 256 if d % 256 == 0 else d
    assert n % (ADA_LANE_TILES * LANES) == 0
    ct = jnp.zeros((d, SUBLANES), F32).at[:, :nrows].set(cvecs.T)
    out = pl.pallas_call(
        functools.partial(_ada_kernel, nrows=nrows),
        out_shape=jax.ShapeDtypeStruct((SUBLANES, n), F32),
        grid=(d // tk,),
        in_specs=[
            pl.BlockSpec((tk, SUBLANES), lambda j: (j, 0)),
            pl.BlockSpec((tk, n), lambda j: (j, 0)),
            _const_spec((1, n)),
        ],
        out_specs=pl.BlockSpec((SUBLANES, n), lambda j: (0, 0)),
        scratch_shapes=[pltpu.VMEM((nrows, tk, LANES), F32), pltpu.VMEM((nrows, SUBLANES, n), F32)],
        compiler_params=_cparams(("arbitrary",)),
        name="ada",
    )(ct, ada_w, ada_b.reshape(1, n))
    return out[:nrows]


def _inproj_kernel(x_ref, sc_ref, sh_ref, g_ref, w_ref, *o_refs, nchunk, silu):
    h = _modulated_norm(x_ref[0], g_ref[...], sc_ref[0], sh_ref[0]).astype(BF16)
    col = 0
    for o_ref, gate in zip(o_refs, silu):
        grouped = len(o_ref.shape) == 4
        cw = o_ref.shape[-1]
        width = cw * o_ref.shape[1] if grouped else cw
        for n0 in range(0, width, nchunk):
            acc = jnp.dot(h, w_ref[:, col + n0:col + n0 + nchunk], preferred_element_type=F32)
            if gate:
                acc = acc * _sigmoid(acc)
            if grouped:
                for c0 in range(0, nchunk, cw):
                    o_ref[0, (n0 + c0) // cw] = acc[:, c0:c0 + cw].astype(o_ref.dtype)
            else:
                o_ref[0, :, n0:n0 + nchunk] = acc.astype(o_ref.dtype)
        col += width


def _inproj_call(x, scale, shift, g, w, widths, silu, tm, name, wcol=0, groups=None):
    b, l, d = x.shape
    nout = sum(widths)
    assert w.shape[1] % nout == 0 and l % tm == 0
    nchunk = min(1024, min(widths))
    assert all(wd % nchunk == 0 for wd in widths)
    groups = groups or (1,) * len(widths)
    shape = lambda wd, ng, rows: (b, rows, wd) if ng == 1 else (b, ng, rows, wd // ng)
    spec = lambda wd, ng: (pl.BlockSpec((1, tm, wd), lambda i, j: (i, j, 0)) if ng == 1 else
                           pl.BlockSpec((1, ng, tm, wd // ng), lambda i, j: (i, 0, j, 0)))
    return pl.pallas_call(
        functools.partial(_inproj_kernel, nchunk=nchunk, silu=silu),
        out_shape=[jax.ShapeDtypeStruct(shape(wd, ng, l), BF16) for wd, ng in zip(widths, groups)],
        grid=(b, l // tm),
        in_specs=[
            pl.BlockSpec((1, tm, d), lambda i, j: (i, j, 0)),
            pl.BlockSpec((1, 1, d), lambda i, j: (i, 0, 0)),
            pl.BlockSpec((1, 1, d), lambda i, j: (i, 0, 0)),
            _const_spec((1, d)),
            pl.BlockSpec((d, nout), lambda i, j: (0, wcol), pipeline_mode=pl.Buffered(1)),
        ],
        out_specs=[spec(wd, ng) for wd, ng in zip(widths, groups)],
        compiler_params=_cparams(("arbitrary", "arbitrary")),
        name=name,
    )(x, scale, shift, g, w)


def _chunk_perm(t):
    i = np.arange(SUBLANES * t)
    src = (i % SUBLANES) * t + i // SUBLANES
    return jnp.asarray(src[:, None] == i[None, :], BF16)


def _pack_kernel(u_ref, uc_ref, perm_ref, ut_ref, scr_ref, *, t, nblk):
    j = pl.program_id(0)
    s = scr_ref.shape[2]
    rows = SUBLANES * t
    ctok = uc_ref.shape[0]

    def permute(src, blk):
        pm = jnp.dot(perm_ref[...], src, preferred_element_type=F32)
        for ti in range(t):
            scr_ref[ti, pl.ds(pl.multiple_of(blk * SUBLANES, SUBLANES), SUBLANES), :] = pm[ti * SUBLANES:(ti + 1) * SUBLANES]

    @pl.when(j < nblk)
    def _():
        def body(blk, carry):
            permute(u_ref[pl.ds(pl.multiple_of(blk * rows, rows), rows), :], blk)
            return carry

        lax.fori_loop(0, LANES // SUBLANES, body, 0, unroll=2)

    @pl.when(j == nblk)
    def _():
        for blk in range(ctok // rows):
            permute(uc_ref[blk * rows:(blk + 1) * rows, :], blk)
        scr_ref[:, ctok // t:LANES, :] = jnp.zeros((t, LANES - ctok // t, s), F32)

    def transpose(ti, carry):
        for k in range(s // LANES):
            ut_ref[0, ti, LANES * k:LANES * (k + 1), :] = scr_ref[ti, :, LANES * k:LANES * (k + 1)].T.astype(BF16)
        return carry

    lax.fori_loop(0, t, transpose, 0, unroll=2)


def _pack_call(u_ssm, uc_ssm, t):
    b, l, s = u_ssm.shape
    tok = LANES * t
    rows = SUBLANES * t
    ntok = b * l
    ctok = b * uc_ssm.shape[1]
    assert ntok % tok == 0 and ctok <= tok and ctok % rows == 0 and s % LANES == 0
    nblk = ntok // tok
    return pl.pallas_call(
        functools.partial(_pack_kernel, t=t, nblk=nblk),
        out_shape=jax.ShapeDtypeStruct((nblk + 1, t, s, LANES), BF16),
        grid=(nblk + 1,),
        in_specs=[pl.BlockSpec((tok, s), lambda j: (jnp.minimum(j, nblk - 1), 0)),
                  _const_spec((ctok, s)), _const_spec((rows, rows))],
        out_specs=pl.BlockSpec((1, t, s, LANES), lambda j: (j, 0, 0, 0)),
        scratch_shapes=[pltpu.VMEM((t, LANES, s), F32)],
        compiler_params=_cparams(("arbitrary",)),
        name="pack",
    )(u_ssm.reshape(ntok, s), uc_ssm.reshape(ctok, s), _chunk_perm(t))


def _unpack_kernel(yt_ref, perm_ref, y_ref, scr_ref, *, t):
    s = scr_ref.shape[2]
    rows = SUBLANES * t

    def transpose(ti, carry):
        for k in range(s // LANES):
            scr_ref[ti, :, LANES * k:LANES * (k + 1)] = yt_ref[0, ti, LANES * k:LANES * (k + 1), :].astype(F32).T
        return carry

    lax.fori_loop(0, t, transpose, 0, unroll=2)

    def body(blk, carry):
        c0 = pl.multiple_of(blk * SUBLANES, SUBLANES)
        src = jnp.concatenate([scr_ref[ti, pl.ds(c0, SUBLANES), :] for ti in range(t)], axis=0).astype(BF16)
        out = jnp.dot(perm_ref[...], src, preferred_element_type=F32)
        y_ref[pl.ds(pl.multiple_of(blk * rows, rows), rows), :] = out.astype(y_ref.dtype)
        return carry

    lax.fori_loop(0, LANES // SUBLANES, body, 0, unroll=2)


def _unpack_call(yt, b, l):
    ngrp, t, s, _ = yt.shape
    nl = ngrp * LANES
    tok = LANES * t
    rows = SUBLANES * t
    y = pl.pallas_call(
        functools.partial(_unpack_kernel, t=t),
        out_shape=jax.ShapeDtypeStruct((nl * t, s), BF16),
        grid=(ngrp,),
        in_specs=[pl.BlockSpec((1, t, s, LANES), lambda j: (j, 0, 0, 0)), _const_spec((rows, rows))],
        out_specs=pl.BlockSpec((tok, s), lambda j: (j, 0)),
        scratch_shapes=[pltpu.VMEM((t, LANES, s), F32)],
        compiler_params=_cparams(("arbitrary",)),
        name="unpack",
    )(yt, _chunk_perm(t).T)
    return y.reshape(b, l, s)


POOL_BLOCK_ROWS = 8
POOL_BLOCKS_PER_STEP = 8


def _pool_kernel(u_ref, *refs, tb, pad):
    nwin = len(POOL_WINDOWS)
    band_refs = refs[:nwin]
    inv_ref, pw_ref, ps_ref, wa_ref, wb_ref, o_ref, wa16_ref, wb16_ref, pad_ref = refs[nwin:]
    wa16_ref[...] = wa_ref[...].astype(wa16_ref.dtype)
    wb16_ref[...] = wb_ref[...].astype(wb16_ref.dtype)
    grp = pl.program_id(1)
    l = u_ref.shape[2]
    c = u_ref.shape[3]
    nblk = l // tb
    pad_ref[0:pad, :] = jnp.zeros((pad, c), pad_ref.dtype)
    pad_ref[pad + l:pad + l + pad, :] = jnp.zeros((pad, c), pad_ref.dtype)

    def copy(i, carry):
        r0 = pl.multiple_of(i * tb, tb)
        pad_ref[pl.ds(pl.multiple_of(pad + r0, GRID_W), tb), :] = u_ref[0, 0, pl.ds(r0, tb), :]
        return carry

    lax.fori_loop(0, nblk, copy, 0)

    for gi, w in enumerate(POOL_WINDOWS):
        half = w // 2
        span = band_refs[gi].shape[1]

        @pl.when(grp == gi)
        def _(half=half, span=span, band_ref=band_refs[gi]):
            def blk(i, carry):
                first = POOL_BLOCKS_PER_STEP * i
                r0 = pl.multiple_of(first * tb, POOL_BLOCKS_PER_STEP * tb)
                wins = [pad_ref[pl.ds(pl.multiple_of(pad + r0 + k * tb - half * GRID_W, GRID_W), span), :]
                        for k in range(POOL_BLOCKS_PER_STEP)]
                box = jnp.dot(band_ref[...], jnp.concatenate(wins, axis=1), preferred_element_type=F32)
                dlts = []
                for k in range(POOL_BLOCKS_PER_STEP):
                    j = first + k
                    variant = jnp.where(j == 0, 0, jnp.where(j == nblk - 1, 2, 1))
                    inv = jnp.concatenate([inv_ref[0, variant]] * (c // LANES), axis=1)
                    u_blk = u_ref[0, 0, pl.ds(r0 + k * tb, tb), :].astype(F32)
                    dlts.append((box[:, k * c:(k + 1) * c] * inv - u_blk).astype(BF16))
                o = jnp.dot(jnp.concatenate(dlts, axis=0), pw_ref[0], preferred_element_type=F32) * ps_ref[...]
                o_ref[0, 0, pl.ds(r0, POOL_BLOCKS_PER_STEP * tb), :] = o.astype(o_ref.dtype)
                return carry

            lax.fori_loop(0, nblk // POOL_BLOCKS_PER_STEP, blk, 0)


def _pool_tables(rows, tb):
    br = tb // GRID_W
    i = np.arange(tb)
    ri, ci = i // GRID_W, i % GRID_W
    bands, invs = [], []
    for w in POOL_WINDOWS:
        half = w // 2
        j = np.arange((br + w) * GRID_W)
        dr = (j // GRID_W)[None, :] - ri[:, None]
        dc = (j % GRID_W)[None, :] - ci[:, None]
        bands.append(jnp.asarray((dr >= 0) & (dr < w) & (dc >= -half) & (dc < w - half), BF16))
        cnt_c = np.minimum(ci + w - half, GRID_W) - np.maximum(ci - half, 0)
        per_variant = []
        for r_first in (0, br, rows - br):
            r = r_first + ri
            cnt_r = np.minimum(r + w - half, rows) - np.maximum(r - half, 0)
            per_variant.append(np.float32(1.0) / (cnt_r * cnt_c).astype(np.float32))
        invs.append(np.stack(per_variant))
    inv = np.broadcast_to(np.stack(invs)[..., None], (len(POOL_WINDOWS), 3, tb, LANES))
    return bands, jnp.asarray(inv, F32)


def _pool_call(u_pool, pool_w, pool_scale, wa, wb):
    b, ng, l, c = u_pool.shape
    nstep = b * ng
    assert wa.shape[0] % (nstep * 16) == 0 and wb.shape[0] % (nstep * 16) == 0
    rows_of = lambda w: pl.BlockSpec((w.shape[0] // nstep, w.shape[1]), lambda i, j: (i * ng + j, 0))
    assert ng == len(POOL_WINDOWS)
    p = ng * c
    rows = l // GRID_W
    tb = POOL_BLOCK_ROWS * GRID_W
    pad = (max(POOL_WINDOWS) // 2) * GRID_W
    assert l % (tb * POOL_BLOCKS_PER_STEP) == 0 and c % LANES == 0
    assert rows >= 3 * POOL_BLOCK_ROWS and max(POOL_WINDOWS) <= 2 * POOL_BLOCK_ROWS
    bands, inv = _pool_tables(rows, tb)
    return pl.pallas_call(
        functools.partial(_pool_kernel, tb=tb, pad=pad),
        out_shape=[jax.ShapeDtypeStruct((b, ng, l, c), BF16), jax.ShapeDtypeStruct(wa.shape, BF16),
                   jax.ShapeDtypeStruct(wb.shape, BF16)],
        grid=(b, ng),
        in_specs=[pl.BlockSpec((1, 1, l, c), lambda i, j: (i, j, 0, 0))]
        + [_const_spec(bd.shape) for bd in bands]
        + [
            pl.BlockSpec((1, 3, tb, LANES), lambda i, j: (j, 0, 0, 0)),
            pl.BlockSpec((1, c, c), lambda i, j: (j, 0, 0)),
            pl.BlockSpec((1, c), lambda i, j: (0, j)),
            rows_of(wa), rows_of(wb),
        ],
        out_specs=[pl.BlockSpec((1, 1, l, c), lambda i, j: (i, j, 0, 0)), rows_of(wa), rows_of(wb)],
        scratch_shapes=[pltpu.VMEM((pad + l + pad, c), BF16)],
        compiler_params=_cparams(("arbitrary", "arbitrary")),
        name="pool",
    )(u_pool, *bands, inv, pool_w.astype(BF16), pool_scale.reshape(1, p), wa, wb)


def _dot_hi_nt(a, b):
    a_hi = a.astype(BF16)
    a_lo = (a - a_hi.astype(F32)).astype(BF16)
    b_hi = b.astype(BF16)
    b_lo = (b - b_hi.astype(F32)).astype(BF16)
    dot = lambda p, q: lax.dot_general(p, q, NT_DIMS, preferred_element_type=F32)
    return dot(a_hi, b_hi) + dot(a_hi, b_lo) + dot(a_lo, b_hi)


def _s5_kernel(u_ref, ea_ref, eb_ref, ba_ref, bm_ref, cc_ref, ca_ref, cb_ref, oa_ref, ob_ref, ap_ref, a8_ref, y_ref,
               et_ref, win_ref, mt_ref, wout_ref, buf_ref, cbuf_ref, hin_ref, *, gps, hc, **static):
    @pl.when(pl.program_id(0) == 0)
    def _():
        buf_ref[...] = jnp.zeros(buf_ref.shape, F32)
        cbuf_ref[...] = jnp.zeros(cbuf_ref.shape, F32)

    for gg in range(gps):
        rows = slice(gg * hc, (gg + 1) * hc)
        _s5_group(u_ref.at[:, :, rows, :], ea_ref.at[gg], eb_ref.at[gg], ba_ref.at[gg], bm_ref.at[gg], cc_ref.at[gg],
                  ca_ref.at[gg], cb_ref.at[gg], oa_ref.at[gg], ob_ref.at[gg], ap_ref.at[gg], a8_ref.at[gg],
                  y_ref.at[:, :, rows, :], et_ref.at[gg], win_ref.at[gg], mt_ref.at[gg], wout_ref.at[gg], buf_ref.at[gg],
                  cbuf_ref.at[gg], hin_ref.at[gg],
                  hc=hc, **static)


def _s5_group(u_ref, ea_ref, eb_ref, ba_ref, bm_ref, cc_ref, ca_ref, cb_ref, oa_ref, ob_ref, ap_ref, a8_ref, y_ref,
              et_ref, win_ref, mt_ref, wout_ref, buf_ref, cbuf_ref, hin_ref, *, t, hc, nb, ncl, ncc, pad, nsteps, nstate):
    n = nstate
    th = t * hc
    nl = nb * ncl
    rb = ncc + ncl
    ntile = th // LANES

    ba = ba_ref[...]
    bm = bm_ref[...]
    ca = ca_ref[...]
    cb = cb_ref[...]
    for ti in range(t):
        rows = slice(ti * hc, (ti + 1) * hc)
        e_rows = ba * ea_ref[ti:ti + 1, :] + bm * eb_ref[ti:ti + 1, :]
        et_ref[rows, :] = e_rows
        win_ref[rows, :] = e_rows.astype(BF16)
        wout_ref[rows, :] = (ca * oa_ref[ti:ti + 1, :] + cb * ob_ref[ti:ti + 1, :]).astype(BF16)
    cc = cc_ref[...]
    q_f = _dot_hi_nt(cc[0:hc], et_ref[:, 0:2 * n])
    q_b = _dot_hi_nt(cc[hc:2 * hc], et_ref[:, 2 * n:4 * n])
    zeros = jnp.zeros((hc, th), F32)
    line_b = jnp.concatenate([zeros, q_b], axis=1)
    line_f = jnp.concatenate([q_f, zeros], axis=1)
    per_tile = LANES // hc
    rolled_b = [line_b if r == 0 else pltpu.roll(line_b, hc * r, axis=1) for r in range(per_tile)]
    rolled_f = [line_f if r == per_tile - 1 else pltpu.roll(line_f, 2 * th - hc * (per_tile - 1 - r), axis=1)
                for r in range(per_tile)]
    for ti in range(t):
        a, r = divmod(ti, per_tile)
        row_b = rolled_b[r][:, th - LANES * a:2 * th - LANES * a]
        off_f = LANES * (ntile - 1 - a)
        row_f = rolled_f[r][:, off_f:off_f + th]
        mt_ref[ti * hc:(ti + 1) * hc, :] = (row_b + row_f).astype(BF16)

    u = jnp.concatenate([jnp.concatenate([u_ref[cg, ti] for cg in range(u_ref.shape[0])], axis=1)
                         for ti in range(t)], axis=0)
    st_all = lax.dot_general(win_ref[...], u, TN_DIMS, preferred_element_type=F32)
    first_half = lax.broadcasted_iota(jnp.int32, (LANES, LANES), 1) < n
    for q in range(4):
        rows_q = st_all[q * n:(q + 1) * n]
        lat = jnp.concatenate([rows_q[:, b * ncl:(b + 1) * ncl] for b in range(nb)], axis=0).T
        cblk = jnp.concatenate([rows_q[:, nl:nl + LANES]] * nb, axis=0).T
        ctx = jnp.where(first_half, cblk, jnp.concatenate([cblk[ncc:], cblk[:ncc]], axis=0))[0:ncc]
        if q < 2:
            buf_ref[q, pad:pad + ncc, :] = ctx
            buf_ref[q, pad + ncc:pad + rb, :] = lat
        else:
            buf_ref[q, pad:pad + ncl, :] = lat
            buf_ref[q, pad + ncl:pad + rb, :] = ctx
    def step(ref, d, k, sgn, lo, rows, keep=None):
        ar = ap_ref[4 * k + 2 * d:4 * k + 2 * d + 1, :]
        ai = ap_ref[4 * k + 2 * d + 1:4 * k + 2 * d + 2, :]
        pr = ref[2 * d, lo + sgn:lo + sgn + rows, :]
        pi = ref[2 * d + 1, lo + sgn:lo + sgn + rows, :]
        if keep is not None:
            pr = jnp.where(keep, pr, 0.0)
            pi = jnp.where(keep, pi, 0.0)
        xr = ref[2 * d, lo:lo + rows, :]
        xi = ref[2 * d + 1, lo:lo + rows, :]
        ref[2 * d, lo:lo + rows, :] = xr + (ar * pr - ai * pi)
        ref[2 * d + 1, lo:lo + rows, :] = xi + (ar * pi + ai * pr)

    rbp = buf_ref.shape[1] - 2 * pad
    nv = rbp // SUBLANES
    cpad = (cbuf_ref.shape[1] - nv) // 2
    in_block = lax.broadcasted_iota(jnp.int32, (rbp, LANES), 0) % SUBLANES
    nlocal = SUBLANES.bit_length() - 1
    nchunk = 3 if nv % 3 == 0 else 1
    crow = rbp // nchunk
    blk = in_block[0:crow]
    for d in range(2):
        for ch in range(nchunk):
            rows = slice(pad + ch * crow, pad + (ch + 1) * crow)
            xr = buf_ref[2 * d, rows, :]
            xi = buf_ref[2 * d + 1, rows, :]
            for k in range(nlocal):
                st = 1 << k
                ar = ap_ref[4 * k + 2 * d:4 * k + 2 * d + 1, :]
                ai = ap_ref[4 * k + 2 * d + 1:4 * k + 2 * d + 2, :]
                keep = blk >= st if d == 0 else blk < SUBLANES - st
                shift = st if d == 0 else SUBLANES - st
                rot = lambda v: pltpu.roll(v.reshape(crow // SUBLANES, SUBLANES, LANES), shift, axis=1).reshape(crow, LANES)
                pr = jnp.where(keep, rot(xr), 0.0)
                pi = jnp.where(keep, rot(xi), 0.0)
                xr, xi = xr + (ar * pr - ai * pi), xi + (ar * pi + ai * pr)
            buf_ref[2 * d, rows, :] = xr
            buf_ref[2 * d + 1, rows, :] = xi
    for d in range(2):
        edge = SUBLANES - 1 if d == 0 else 0
        for q in (2 * d, 2 * d + 1):
            cbuf_ref[q, cpad:cpad + nv, :] = buf_ref[q, pl.ds(pad + edge, nv, stride=SUBLANES), :]
    for m in range(max(0, (nv - 1).bit_length())):
        st = 1 << m
        step(cbuf_ref, 0, nlocal + m, -st, cpad, nv)
        step(cbuf_ref, 1, nlocal + m, st, cpad, nv)
    for d in range(2):
        a8r = a8_ref[2 * SUBLANES * d:2 * SUBLANES * d + SUBLANES, :]
        a8i = a8_ref[2 * SUBLANES * d + SUBLANES:2 * SUBLANES * (d + 1), :]
        for j in range(nv):
            src = j - 1 if d == 0 else j + 1
            if 0 <= src < nv:
                cr = jnp.broadcast_to(cbuf_ref[2 * d, cpad + src:cpad + src + 1, :], (SUBLANES, LANES))
                ci = jnp.broadcast_to(cbuf_ref[2 * d + 1, cpad + src:cpad + src + 1, :], (SUBLANES, LANES))
                rows = slice(pad + j * SUBLANES, pad + (j + 1) * SUBLANES)
                buf_ref[2 * d, rows, :] = buf_ref[2 * d, rows, :] + (a8r * cr - a8i * ci)
                buf_ref[2 * d + 1, rows, :] = buf_ref[2 * d + 1, rows, :] + (a8r * ci + a8i * cr)
    for q in range(4):
        lo = pad + ncc - 1 if q < 2 else pad + 1
        ent = buf_ref[q, lo:lo + ncl, :].T.astype(BF16)
        for b in range(nb):
            hin_ref[q * n:(q + 1) * n, b * ncl:(b + 1) * ncl] = ent[b * n:(b + 1) * n]

    y = jnp.dot(mt_ref[...], u[:, 0:nl], preferred_element_type=F32)
    y = y + jnp.dot(wout_ref[...], hin_ref[...], preferred_element_type=F32)
    for ti in range(t):
        for cg in range(y_ref.shape[0]):
            y_ref[cg, ti] = y[ti * hc:(ti + 1) * hc, cg * LANES:(cg + 1) * LANES].astype(y_ref.dtype)


def _s5_params(lam_re, lam_im, log_dt, b_re, b_im, c_re, c_im, t, nsteps):
    _, g, n = lam_re.shape
    h = b_re.shape[-1]
    lam_re, lam_im = lam_re.astype(F32), lam_im.astype(F32)
    dt = jnp.exp(log_dt.astype(F32))[..., None]
    lr, li = lam_re * dt, lam_im * dt

    def power(d, k):
        kk = k[None, :, None]
        mag = jnp.exp(lr[d][:, None, :] * kk)
        return mag * jnp.cos(li[d][:, None, :] * kk), mag * jnp.sin(li[d][:, None, :] * kk)

    a_re, a_im = jnp.exp(lr) * jnp.cos(li), jnp.exp(lr) * jnp.sin(li)
    den = lam_re * lam_re + lam_im * lam_im
    f_re = ((a_re - 1.0) * lam_re + a_im * lam_im) / den
    f_im = (a_im * lam_re - (a_re - 1.0) * lam_im) / den
    bt_re = (f_re[..., None] * b_re - f_im[..., None] * b_im).transpose(0, 1, 3, 2)
    bt_im = (f_re[..., None] * b_im + f_im[..., None] * b_re).transpose(0, 1, 3, 2)
    lanes4 = lambda f0, f1, b0, b1: jnp.concatenate([f0, f1, b0, b1], axis=-1)

    strides = t * (2.0 ** jnp.arange(nsteps, dtype=F32))
    exps = jnp.concatenate([jnp.arange(t + 1, dtype=F32), strides, t * jnp.arange(1, SUBLANES + 1, dtype=F32)])
    (allf_re, allf_im), (allb_re, allb_im) = power(0, exps), power(1, exps)
    pf_re, pf_im = allf_re[:, t - 1::-1], allf_im[:, t - 1::-1]
    pb_re, pb_im = allb_re[:, 0:t], allb_im[:, 0:t]
    ea = lanes4(pf_re, pf_re, pb_re, pb_re)
    eb = lanes4(pf_im, pf_im, pb_im, pb_im)
    ba = lanes4(bt_re[0], bt_im[0], bt_re[1], bt_im[1])
    bm = lanes4(-bt_im[0], bt_re[0], -bt_im[1], bt_re[1])
    cc = jnp.concatenate([jnp.concatenate([c_re[0], -c_im[0]], axis=-1),
                          jnp.concatenate([c_re[1], -c_im[1]], axis=-1)], axis=1)
    rf_re, rf_im = allf_re[:, 1:t + 1], allf_im[:, 1:t + 1]
    rb_re, rb_im = allb_re[:, t:0:-1], allb_im[:, t:0:-1]
    oa = lanes4(rf_re, rf_im, rb_re, rb_im)
    ob = lanes4(rf_im, rf_re, rb_im, rb_re)
    ca = lanes4(c_re[0], -c_re[0], c_re[1], -c_re[1])
    cb = lanes4(-c_im[0], -c_im[0], -c_im[1], -c_im[1])
    st0, st1 = t + 1, t + 1 + nsteps
    apf_re, apf_im, apb_re, apb_im = (v[:, st0:st1] for v in (allf_re, allf_im, allb_re, allb_im))
    ap = jnp.stack([apf_re, apf_im, apb_re, apb_im], axis=2).reshape(g, 4 * nsteps, n)
    ap = jnp.concatenate([ap] * (LANES // n), axis=-1)
    rows = -(-4 * nsteps // SUBLANES) * SUBLANES
    ap = jnp.pad(ap, ((0, 0), (0, rows - 4 * nsteps), (0, 0)))
    f8_re, f8_im = allf_re[:, st1:], allf_im[:, st1:]
    b8_re, b8_im = allb_re[:, :st1 - 1:-1], allb_im[:, :st1 - 1:-1]
    a8 = jnp.concatenate([f8_re, f8_im, b8_re, b8_im], axis=1)
    a8 = jnp.concatenate([a8] * (LANES // n), axis=-1)
    return tuple(v.astype(F32) for v in (ea, eb, ba, bm, cc, ca, cb, oa, ob, ap, a8))


def _s5_call(ut, params, t, nb, ncl, ncc, nsteps, n_state, hc):
    ap = params[-2]
    ngrp_in, _, sw, _ = ut.shape
    nl = nb * ncl
    g = sw // hc
    th = t * hc
    rbp = -(-(ncc + ncl) // SUBLANES) * SUBLANES
    nv = rbp // SUBLANES
    pad = SUBLANES
    cpad = max(SUBLANES, 1 << max(0, (nv - 1).bit_length() - 1))
    assert nb * n_state == LANES and nb == 2 and 4 * nsteps <= ap.shape[1]
    gps = S5_GROUPS_PER_STEP
    assert g % gps == 0
    grp = lambda shp: pl.BlockSpec((gps,) + shp, lambda i: (i, 0, 0))
    return pl.pallas_call(
        functools.partial(_s5_kernel, gps=gps, t=t, hc=hc, nb=nb, ncl=ncl, ncc=ncc, pad=pad, nsteps=nsteps,
                          nstate=n_state),
        out_shape=jax.ShapeDtypeStruct((nl // LANES, t, sw, LANES), BF16),
        grid=(g // gps,),
        in_specs=[
            pl.BlockSpec((ngrp_in, t, gps * hc, LANES), lambda i: (0, 0, i, 0)),
            grp((t, 4 * n_state)), grp((t, 4 * n_state)), grp((hc, 4 * n_state)), grp((hc, 4 * n_state)),
            grp((2 * hc, 2 * n_state)), grp((hc, 4 * n_state)), grp((hc, 4 * n_state)),
            grp((t, 4 * n_state)), grp((t, 4 * n_state)), grp((ap.shape[1], LANES)), grp((4 * SUBLANES, LANES)),
        ],
        out_specs=pl.BlockSpec((nl // LANES, t, gps * hc, LANES), lambda i: (0, 0, i, 0)),
        scratch_shapes=[pltpu.VMEM((gps, th, 4 * n_state), F32), pltpu.VMEM((gps, th, 4 * n_state), BF16),
                        pltpu.VMEM((gps, th, th), BF16), pltpu.VMEM((gps, th, 4 * n_state), BF16),
                        pltpu.VMEM((gps, 4, pad + rbp + pad, LANES), F32),
                        pltpu.VMEM((gps, 4, cpad + nv + cpad, LANES), F32),
                        pltpu.VMEM((gps, 4 * n_state, nl), BF16)],
        compiler_params=_cparams(("arbitrary",)),
        name="s5",
    )(ut, *params)


def _merge_kernel(y_ref, u_ref, p_ref, zs_ref, x_ref, gate_ref, d_ref, gw_ref, gb_ref, ow_ref, fg_ref, o_ref):
    sw = y_ref.shape[-1]
    pw = p_ref.shape[1] * p_ref.shape[-1]
    y = _gelu_tanh(y_ref[0].astype(F32) + d_ref[...] * u_ref[0].astype(F32)).astype(BF16)
    yy = jnp.dot(y, gw_ref[...], preferred_element_type=F32) + gb_ref[...]
    ssm_out = yy[:, :sw] * _sigmoid(yy[:, sw:])
    pool = jnp.concatenate([p_ref[0, g] for g in range(p_ref.shape[1])], axis=1)
    br_pool = pool * zs_ref[0, :, 0:pw]
    br_ssm = (ssm_out * zs_ref[0, :, pw:pw + sw].astype(F32)).astype(BF16)
    mix = jnp.dot(jnp.concatenate([br_pool, br_ssm], axis=1), ow_ref[...], preferred_element_type=F32)
    half = mix.shape[0] // 2
    for r0 in (0, half):
        rows = slice(r0, r0 + half)
        xo = x_ref[0, rows, :] + gate_ref[0] * mix[rows]
        ms = jnp.mean(xo * xo, axis=-1, keepdims=True)
        o_ref[0, rows, :] = (xo * lax.rsqrt(ms + EPS) * fg_ref[...]).astype(o_ref.dtype)


def _merge_call(y_ssm, u_ssm, pool_out, zs, x, gate, d_skip, glu_w, glu_b, out_w, final_g, tm):
    b, l, d = x.shape
    sw = y_ssm.shape[-1]
    ng, pc = pool_out.shape[1], pool_out.shape[-1]
    mixw = zs.shape[-1]
    tok = lambda wd: pl.BlockSpec((1, tm, wd), lambda i, j: (i, j, 0))
    return pl.pallas_call(
        _merge_kernel,
        out_shape=jax.ShapeDtypeStruct((b, l, d), x.dtype),
        grid=(b, l // tm),
        in_specs=[
            tok(sw), tok(sw), pl.BlockSpec((1, ng, tm, pc), lambda i, j: (i, 0, j, 0)), tok(mixw), tok(d),
            pl.BlockSpec((1, 1, d), lambda i, j: (i, 0, 0)),
            _const_spec((1, sw)), _const_spec((sw, 2 * sw)), _const_spec((1, 2 * sw)), _const_spec((mixw, d)),
            _const_spec((1, d)),
        ],
        out_specs=tok(d),
        compiler_params=_cparams(("arbitrary", "arbitrary")),
        name="merge",
    )(y_ssm, u_ssm, pool_out, zs, x, gate, d_skip.reshape(1, sw), glu_w, glu_b.reshape(1, 2 * sw), out_w,
      final_g.reshape(1, d))


def kernel(x, c, ctx, c_ctx, ada_w, ada_b, norm_g, in_w, pool_w, pool_scale, s5_lam_re, s5_lam_im, s5_log_dt,
           s5_b_re, s5_b_im, s5_c_re, s5_c_im, s5_d, glu_w, glu_b, out_w, final_g):
    assert ada_w.shape[0] == 1, "single-layer block"
    bsz, seq, d = x.shape
    cl = ctx.shape[1]
    mixw = in_w.shape[2] // 2
    poolw = pool_scale.shape[-1]
    ssmw = s5_d.shape[-1]
    n_grp, n_state = s5_lam_re.shape[2], s5_lam_re.shape[3]
    hc = ssmw // n_grp
    t = CHUNK_T
    assert poolw + ssmw == mixw and seq % t == 0 and cl % t == 0 and 2 * n_state == LANES and LANES % hc == 0

    mod = _ada_call(jnp.concatenate([c, c_ctx[None]], axis=0), ada_w[0], ada_b[0])
    shift, scale, gate = mod[:, :d], mod[:, d:2 * d], mod[:, 2 * d:]
    g1 = norm_g[0].reshape(1, d)
    in_w16 = in_w[0].astype(BF16)

    tm = min(512, seq)
    u_pool, u_ssm, zs = _inproj_call(x, scale[:bsz, None], shift[:bsz, None], g1, in_w16,
                                     (poolw, ssmw, mixw), (False, False, True), min(1024, seq), "inproj",
                                     groups=(len(POOL_WINDOWS), 1, 1))
    sc_c = jnp.broadcast_to(scale[bsz][None, None], (bsz, 1, d))
    sh_c = jnp.broadcast_to(shift[bsz][None, None], (bsz, 1, d))
    assert poolw % ssmw == 0
    (uc_ssm,) = _inproj_call(ctx, sc_c, sh_c, g1, in_w16, (ssmw,), (False,), min(256, cl), "inproj_ctx",
                             wcol=poolw // ssmw)

    pool_out, out_w16, glu_w16 = _pool_call(u_pool, pool_w[0], pool_scale[0], out_w[0], glu_w[0])

    ncl, ncc = seq // t, cl // t
    nsteps = max(1, (ncc + ncl - 1).bit_length())
    params = _s5_params(s5_lam_re[0], s5_lam_im[0], s5_log_dt[0], s5_b_re[0], s5_b_im[0], s5_c_re[0], s5_c_im[0],
                        t, nsteps)
    ut = _pack_call(u_ssm, uc_ssm, t)
    yt = _s5_call(ut, params, t, bsz, ncl, ncc, nsteps, n_state, hc)
    y_ssm = _unpack_call(yt, bsz, seq)

    return _merge_call(y_ssm, u_ssm, pool_out, zs, x, gate[:bsz, None], s5_d[0], glu_w16, glu_b[0], out_w16,
                       final_g, tm)
```
